```python
import functools
import jax, jax.numpy as jnp
from jax import lax
import numpy as np

D_MODEL = 2048
BATCH = 4
SEQ = 2048
DEPTH = 1
DEC_BATCH = 128
DEC_SEQ = 8
PAST_LEN = 16384
PAGE_SIZE = 128

N_META = 16
CHUNK = 128
M_HEADS = 4
M_DK = 256
M_DV = 256
M_WIDTH = M_HEADS * M_DV
C_WIDTH = 1024
CONV_W = 3
D_FF = 4 * D_MODEL
EPS = 1e-6
D_IN = 2 * M_HEADS * M_DK + 2 * M_WIDTH + 2 * M_HEADS + 3 * C_WIDTH + 2 * D_MODEL

kernel_name = 'hybrid_mlstm_shortconv_decode_step'


def _split_points():
    sizes = (M_HEADS * M_DK, M_HEADS * M_DK, M_WIDTH, M_HEADS, M_HEADS, M_WIDTH,
             C_WIDTH, C_WIDTH, C_WIDTH, D_MODEL, D_MODEL)
    pts, acc = [], 0
    for s in sizes[:-1]:
        acc += s
        pts.append(acc)
    return pts


def rmsnorm(x, g):
    xf = x.astype(jnp.float32)
    y = xf * lax.rsqrt(jnp.mean(xf * xf, axis=-1, keepdims=True) + EPS)
    return (y * g.astype(jnp.float32)).astype(x.dtype)


def mlstm_chunk(q, k, v, ig, lf, C, n, m):
    f32 = jnp.float32
    q, k, v, ig, lf = (a.astype(f32) for a in (q, k, v, ig, lf))
    C, n, m = C.astype(f32), n.astype(f32), m.astype(f32)
    L = q.shape[2]
    b = jnp.cumsum(lf, axis=-1)
    causal = jnp.tril(jnp.ones((L, L), dtype=bool))
    logw = jnp.where(causal, b[..., :, None] - b[..., None, :] + ig[..., None, :], -jnp.inf)
    inter = b + m[..., None]
    m_t = jnp.maximum(inter, jnp.max(logw, axis=-1))
    w_inter = jnp.exp(inter - m_t)
    s = jnp.einsum('bhtd,bhsd->bhts', q, k) * jnp.exp(logw - m_t[..., None])
    num = w_inter[..., None] * jnp.einsum('bhtd,bhde->bhte', q, C) + jnp.einsum('bhts,bhse->bhte', s, v)
    den = w_inter * jnp.einsum('bhtd,bhd->bht', q, n) + jnp.sum(s, axis=-1)
    h = num / jnp.maximum(jnp.abs(den), jnp.exp(-m_t))[..., None]
    m_new = m_t[..., -1]
    decay = jnp.exp(b[..., -1] + m - m_new)
    wk = jnp.exp(b[..., -1:] - b + ig - m_new[..., None])
    C_new = decay[..., None, None] * C + jnp.einsum('bhs,bhsd,bhse->bhde', wk, k, v)
    n_new = decay[..., None] * n + jnp.einsum('bhs,bhsd->bhd', wk, k)
    return h, (C_new, n_new, m_new)


def mlstm_prompt(q, k, v, ig, lf):
    bsz, t = q.shape[0], q.shape[2]
    f32 = jnp.float32
    C0 = jnp.zeros((bsz, M_HEADS, M_DK, M_DV), f32)
    n0 = jnp.zeros((bsz, M_HEADS, M_DK), f32)
    m0 = jnp.zeros((bsz, M_HEADS), f32)
    h0, st = mlstm_chunk(q[:, :, :N_META], k[:, :, :N_META], v[:, :, :N_META],
                         ig[:, :, :N_META], lf[:, :, :N_META], C0, n0, m0)
    nc = (t - N_META) // CHUNK

    def to_chunks(a):
        a = a[:, :, N_META:]
        a = a.reshape(a.shape[:2] + (nc, CHUNK) + a.shape[3:])
        return jnp.moveaxis(a, 2, 0)

    def step(carry, xs):
        h, carry = mlstm_chunk(*xs, *carry)
        return carry, h

    st, hs = lax.scan(step, st, tuple(to_chunks(a) for a in (q, k, v, ig, lf)))
    hs = jnp.moveaxis(hs, 0, 2).reshape(bsz, M_HEADS, nc * CHUNK, M_DV)
    return jnp.concatenate([h0, hs], axis=2), st


def causal_conv(zp, w):
    L = zp.shape[1] - (CONV_W - 1)
    y = w[0] * zp[:, 0:L]
    for j in range(1, CONV_W):
        y = y + w[j] * zp[:, j:j + L]
    return y


def conv_prompt(z, w):
    zp = jnp.pad(z, ((0, 0), (CONV_W - 1, 0), (0, 0)))
    return causal_conv(zp, w), z[:, -(CONV_W - 1):]


def conv_step(z, w, buf):
    zp = jnp.concatenate([buf.astype(z.dtype), z], axis=1)
    return causal_conv(zp, w), zp[:, -(CONV_W - 1):]


def trunk_layer(x, lw, mlstm_fn, conv_fn):
    g_mix, w_in, b_in, w_conv, p_a, p_b, w_o, g_ffn, w_up, w_down = lw
    bsz, t = x.shape[0], x.shape[1]
    xn = rmsnorm(x, g_mix)
    z = xn @ w_in + b_in
    zq, zk, zv, zi, zf, zo, zu, zc, zb, zga, zgb = jnp.split(z, _split_points(), axis=-1)
    heads = lambda a, d: a.reshape(bsz, t, M_HEADS, d).transpose(0, 2, 1, 3)
    q = heads(zq, M_DK)
    k = heads(zk, M_DK) * (M_DK ** -0.5)
    v = heads(zv, M_DV)
    ig = zi.astype(jnp.float32).transpose(0, 2, 1)
    lf = jax.nn.log_sigmoid(zf.astype(jnp.float32)).transpose(0, 2, 1)
    h, m_state = mlstm_fn(q, k, v, ig, lf)
    h = h.transpose(0, 2, 1, 3).reshape(bsz, t, M_WIDTH).astype(x.dtype)
    h_a = jax.nn.sigmoid(zo) * h
    conv_out, conv_state = conv_fn(zc * zu, w_conv)
    y_b = zb * conv_out
    merged = jax.nn.sigmoid(zga) * (h_a @ p_a) + jax.nn.sigmoid(zgb) * (y_b @ p_b)
    x = x + merged @ w_o
    u = jax.nn.relu(rmsnorm(x, g_ffn) @ w_up)
    x = x + (u * u) @ w_down
    return x, m_state, conv_state


def setup_inputs(seed: int = 0) -> dict:
    key = jax.random.key(seed)
    ks = jax.random.split(key, 20)
    nrm = lambda kk, shape, scale: jax.random.normal(kk, shape, jnp.float32) * scale
    f_off = 2 * M_HEADS * M_DK + M_WIDTH + M_HEADS
    b_in = nrm(ks[0], (DEPTH, D_IN), 0.02)
    b_in = b_in.at[:, f_off:f_off + M_HEADS].add(jnp.linspace(3.0, 6.0, M_HEADS))
    return {
        'x_prompt': nrm(ks[1], (BATCH, SEQ, D_MODEL), 1.0),
        'x_sample': nrm(ks[2], (DEC_BATCH, DEC_SEQ, D_MODEL), 1.0),
        'state_mlstm_C': nrm(ks[3], (DEPTH, DEC_BATCH, M_HEADS, M_DK, M_DV), 0.5),
        'state_mlstm_n': nrm(ks[4], (DEPTH, DEC_BATCH, M_HEADS, M_DK), 0.5),
        'state_mlstm_m': nrm(ks[5], (DEPTH, DEC_BATCH, M_HEADS), 1.0),
        'state_conv': nrm(ks[6], (DEPTH, DEC_BATCH, CONV_W - 1, C_WIDTH), 1.0),
        'meta_tokens': nrm(ks[7], (N_META, D_MODEL), 1.0),
        'g_mix': 1.0 + nrm(ks[8], (DEPTH, D_MODEL), 0.02),
        'w_in': nrm(ks[9], (DEPTH, D_MODEL, D_IN), D_MODEL ** -0.5),
        'b_in': b_in,
        'w_conv': nrm(ks[10], (DEPTH, CONV_W, C_WIDTH), CONV_W ** -0.5),
        'p_a': nrm(ks[11], (DEPTH, M_WIDTH, D_MODEL), M_WIDTH ** -0.5),
        'p_b': nrm(ks[12], (DEPTH, C_WIDTH, D_MODEL), C_WIDTH ** -0.5),
        'w_o': nrm(ks[13], (DEPTH, D_MODEL, D_MODEL), D_MODEL ** -0.5),
        'g_ffn': 1.0 + nrm(ks[14], (DEPTH, D_MODEL), 0.02),
        'w_up': nrm(ks[15], (DEPTH, D_MODEL, D_FF), D_MODEL ** -0.5),
        'w_down': nrm(ks[16], (DEPTH, D_FF, D_MODEL), D_FF ** -0.5),
        'g_final': 1.0 + nrm(ks[17], (D_MODEL,), 0.02),
    }


def reference(x_prompt, x_sample, state_mlstm_C, state_mlstm_n, state_mlstm_m, state_conv,
              meta_tokens, g_mix, w_in, b_in, w_conv, p_a, p_b, w_o, g_ffn, w_up, w_down, g_final):
    bsz = x_prompt.shape[0]
    meta = jnp.broadcast_to(meta_tokens.astype(x_prompt.dtype)[None], (bsz, N_META, D_MODEL))
    xp = jnp.concatenate([meta, x_prompt], axis=1)
    xs = x_sample
    Cp, n_p, mp, cvp, Cs, n_s, ms, cvs = [], [], [], [], [], [], [], []
    for l in range(DEPTH):
        lw = (g_mix[l], w_in[l], b_in[l], w_conv[l], p_a[l], p_b[l], w_o[l], g_ffn[l], w_up[l], w_down[l])
        xp, (c1, n1, m1), cv1 = trunk_layer(xp, lw, mlstm_prompt, conv_prompt)
        mfn = functools.partial(mlstm_chunk, C=state_mlstm_C[l], n=state_mlstm_n[l], m=state_mlstm_m[l])
        cfn = functools.partial(conv_step, buf=state_conv[l])
        xs, (c2, n2, m2), cv2 = trunk_layer(xs, lw, mfn, cfn)
        Cp.append(c1); n_p.append(n1); mp.append(m1); cvp.append(cv1)
        Cs.append(c2); n_s.append(n2); ms.append(m2); cvs.append(cv2)
    y_prompt = rmsnorm(xp, g_final)[:, N_META:]
    y_sample = rmsnorm(xs, g_final)
    st = lambda lst, ref: jnp.stack(lst).astype(ref.dtype)
    return (y_prompt, y_sample,
            st(Cp, state_mlstm_C), st(n_p, state_mlstm_n), st(mp, state_mlstm_m), st(cvp, state_conv),
            st(Cs, state_mlstm_C), st(n_s, state_mlstm_n), st(ms, state_mlstm_m), st(cvs, state_conv))
```

```python
import functools

import jax
import jax.numpy as jnp
from jax import lax
from jax.experimental import pallas as pl
from jax.experimental.pallas import tpu as pltpu

F32 = jnp.float32
BF16 = jnp.bfloat16

D_MODEL = 2048
N_META = 16
CHUNK = 128
M_HEADS = 4
M_DK = 256
M_DV = 256
M_WIDTH = M_HEADS * M_DV
C_WIDTH = 1024
CONV_W = 3
D_FF = 4 * D_MODEL
EPS = 1e-6
GATE_OFF = 2 * M_HEADS * M_DK + M_WIDTH
N_GATES = 2 * M_HEADS

LANES = 128
SUBLANES = 8
VMEM_LIMIT_BYTES = 60000 * 1024

TN = 1024
J_Q, J_K, J_V, J_O, J_U, J_C, J_B, J_GA0, J_GA1, J_GB0, J_GB1 = range(11)
Z_Q, Z_K, Z_V, Z_SO, Z_GA, Z_GB, Z_YB = 0, 1, 2, 3, 4, 6, 8
Z_COLS = 9 * TN

NEG = -1e30


def _params(semantics):
    return pltpu.CompilerParams(dimension_semantics=semantics, vmem_limit_bytes=VMEM_LIMIT_BYTES)


def _rmsnorm(x, g):
    y = x * lax.rsqrt(jnp.mean(x * x, axis=-1, keepdims=True) + EPS)
    return y * g


def _log_sigmoid(x):
    return jnp.minimum(x, 0.0) - jnp.log1p(jnp.exp(-jnp.abs(x)))


def _rms_kernel(n_prompt_tiles, xp_ref, xs_ref, g_ref, o_ref):
    i = pl.program_id(0)

    @pl.when(i < n_prompt_tiles)
    def _():
        o_ref[...] = _rmsnorm(xp_ref[...], g_ref[...]).astype(o_ref.dtype)

    @pl.when(i >= n_prompt_tiles)
    def _():
        o_ref[...] = _rmsnorm(xs_ref[...], g_ref[...]).astype(o_ref.dtype)


def _rms_call(xp, xs, g, tm):
    tp, ts = xp.shape[0], xs.shape[0]
    npt, nst = tp // tm, ts // tm
    return pl.pallas_call(
        functools.partial(_rms_kernel, npt),
        grid=(npt + nst,),
        in_specs=[
            pl.BlockSpec((tm, D_MODEL), lambda i: (jnp.minimum(i, npt - 1), 0)),
            pl.BlockSpec((tm, D_MODEL), lambda i: (jnp.maximum(i - npt, 0), 0)),
            pl.BlockSpec((1, D_MODEL), lambda i: (0, 0)),
        ],
        out_specs=pl.BlockSpec((tm, D_MODEL), lambda i: (i, 0)),
        out_shape=jax.ShapeDtypeStruct((tp + ts, D_MODEL), BF16),
        compiler_params=_params(("arbitrary",)),
        name="rms",
    )(xp, xs, g)


def _meta_kernel(x_ref, g_ref, w_ref, b_ref, wg_ref, bg_ref, z_ref, gate_ref, cu_ref, xn_ref, zu_ref):
    j = pl.program_id(0)

    @pl.when(j == 0)
    def _():
        xn_ref[...] = _rmsnorm(x_ref[...], g_ref[...]).astype(BF16)
        gate_ref[...] = jnp.dot(xn_ref[...], wg_ref[...], preferred_element_type=F32) + bg_ref[...]

    z = jnp.dot(xn_ref[...], w_ref[...], preferred_element_type=F32) + b_ref[...]

    @pl.when((j == 0) | (j == 2))
    def _():
        z_ref[...] = z.astype(BF16)

    @pl.when(j == 1)
    def _():
        z_ref[...] = (z * (M_DK ** -0.5)).astype(BF16)

    @pl.when(j == 3)
    def _():
        zu_ref[...] = z

    @pl.when(j == 4)
    def _():
        cu_ref[...] = z * zu_ref[...]


def _meta_call(x_meta, g, w1, b1, wg, bg):
    rows = x_meta.shape[0]
    col = lambda j: j + (j >= 3).astype(jnp.int32)
    return pl.pallas_call(
        _meta_kernel,
        grid=(5,),
        in_specs=[
            pl.BlockSpec((rows, D_MODEL), lambda j: (0, 0)),
            pl.BlockSpec((1, D_MODEL), lambda j: (0, 0)),
            pl.BlockSpec((D_MODEL, TN), lambda j: (0, col(j))),
            pl.BlockSpec((1, TN), lambda j: (0, col(j))),
            pl.BlockSpec((D_MODEL, LANES), lambda j: (0, 0)),
            pl.BlockSpec((1, LANES), lambda j: (0, 0)),
        ],
        out_specs=[
            pl.BlockSpec((rows, TN), lambda j: (0, jnp.minimum(j, 2))),
            pl.BlockSpec((rows, LANES), lambda j: (0, 0)),
            pl.BlockSpec((rows, C_WIDTH), lambda j: (0, 0)),
        ],
        out_shape=[
            jax.ShapeDtypeStruct((rows, 3 * TN), BF16),
            jax.ShapeDtypeStruct((rows, LANES), F32),
            jax.ShapeDtypeStruct((rows, C_WIDTH), F32),
        ],
        scratch_shapes=[pltpu.VMEM((rows, D_MODEL), BF16), pltpu.VMEM((rows, C_WIDTH), F32)],
        compiler_params=_params(("arbitrary",)),
        name="meta_inproj",
    )(x_meta, g, w1, b1, wg, bg)


def _inproj_kernel(n_prompt_tiles, tiles_per_seq, xn_ref, w_ref, b_ref, wg_ref, bg_ref, wconv_ref,
                   mtail_ref, sconv_ref, z_ref, gate_ref, cu6_ref, cu7_ref, zu_ref, conv_ref, carry_ref):
    i = pl.program_id(0)
    j = pl.program_id(1)
    tm = xn_ref.shape[0]
    n_seq = tm // SUBLANES

    xn = xn_ref[...]
    z = jnp.dot(xn, w_ref[...], preferred_element_type=F32) + b_ref[...]

    @pl.when(j == J_Q)
    def _():
        gate_ref[...] = jnp.dot(xn, wg_ref[...], preferred_element_type=F32) + bg_ref[...]

    @pl.when((j == J_Q) | (j == J_V))
    def _():
        z_ref[...] = z.astype(BF16)

    @pl.when(j == J_K)
    def _():
        z_ref[...] = (z * (M_DK ** -0.5)).astype(BF16)

    @pl.when((j == J_O) | (j >= J_GA0))
    def _():
        z_ref[...] = jax.nn.sigmoid(z).astype(BF16)

    slabs = [(c, slice(c * LANES, (c + 1) * LANES)) for c in range(C_WIDTH // LANES)]
    seq_row = lambda r: pl.ds(r, n_seq, stride=SUBLANES)

    @pl.when(j == J_U)
    def _():
        for c, cs in slabs:
            zu_ref[c] = z[:, cs]

    @pl.when(j == J_C)
    def _():
        is_prompt = i < n_prompt_tiles
        first = (i % tiles_per_seq) == 0
        for c, cs in slabs:
            cu = z[:, cs] * zu_ref[c]
            zu_ref[c] = cu
            w0, w1, w2 = wconv_ref[0:1, cs], wconv_ref[1:2, cs], wconv_ref[2:3, cs]
            conv_ref[c] = (w0 * pltpu.roll(cu, 2, 0) + w1 * pltpu.roll(cu, 1, 0)) + w2 * cu
            cu6_ref[:, cs] = zu_ref[c, seq_row(6), :]
            cu7_ref[:, cs] = zu_ref[c, seq_row(7), :]

            @pl.when(is_prompt)
            def _():
                p6 = jnp.where(first, mtail_ref[6:7, cs], carry_ref[6:7, cs])
                p7 = jnp.where(first, mtail_ref[7:8, cs], carry_ref[7:8, cs])
                c0, c1 = cu[0:1, :], cu[1:2, :]
                conv_ref[c, 0:1, :] = (w0 * p6 + w1 * p7) + w2 * c0
                conv_ref[c, 1:2, :] = (w0 * p7 + w1 * c0) + w2 * c1

            @pl.when(jnp.logical_not(is_prompt))
            def _():
                s0 = sconv_ref[:, cs]
                s1 = sconv_ref[:, C_WIDTH + c * LANES:C_WIDTH + (c + 1) * LANES]
                c0 = zu_ref[c, seq_row(0), :]
                c1 = zu_ref[c, seq_row(1), :]
                conv_ref[c, seq_row(0), :] = (w0 * s0 + w1 * s1) + w2 * c0
                conv_ref[c, seq_row(1), :] = (w0 * s1 + w1 * c0) + w2 * c1

            carry_ref[:, cs] = cu[tm - SUBLANES:tm, :]

    @pl.when(j == J_B)
    def _():
        for c, cs in slabs:
            z_ref[:, cs] = (z[:, cs] * conv_ref[c]).astype(BF16)


def _z_block(j):
    return jnp.where(j <= J_O, j, jnp.where(j <= J_B, Z_YB, j - 3))


def _inproj_call(xn, w1, b1, wg, bg, wconv, mtail, sconv, n_prompt_tokens, seq_len, tm):
    t = xn.shape[0]
    npt = n_prompt_tokens // tm
    n_tiles = t // tm
    n_seq = tm // SUBLANES
    return pl.pallas_call(
        functools.partial(_inproj_kernel, npt, seq_len // tm),
        grid=(n_tiles, 11),
        in_specs=[
            pl.BlockSpec((tm, D_MODEL), lambda i, j: (i, 0)),
            pl.BlockSpec((D_MODEL, TN), lambda i, j: (0, j)),
            pl.BlockSpec((1, TN), lambda i, j: (0, j)),
            pl.BlockSpec((D_MODEL, LANES), lambda i, j: (0, 0)),
            pl.BlockSpec((1, LANES), lambda i, j: (0, 0)),
            pl.BlockSpec((CONV_W, C_WIDTH), lambda i, j: (0, 0)),
            pl.BlockSpec((SUBLANES, C_WIDTH), lambda i, j: (1, 0)),
            pl.BlockSpec((n_seq, 2 * C_WIDTH), lambda i, j: (jnp.maximum(i - npt, 0), 0)),
        ],
        out_specs=[
            pl.BlockSpec((tm, TN), lambda i, j: (i, _z_block(j))),
            pl.BlockSpec((tm, LANES), lambda i, j: (i, 0)),
            pl.BlockSpec((n_seq, C_WIDTH), lambda i, j: (i, 0)),
            pl.BlockSpec((n_seq, C_WIDTH), lambda i, j: (i, 0)),
        ],
        out_shape=[
            jax.ShapeDtypeStruct((t, Z_COLS), BF16),
            jax.ShapeDtypeStruct((t, LANES), F32),
            jax.ShapeDtypeStruct((t // SUBLANES, C_WIDTH), F32),
            jax.ShapeDtypeStruct((t // SUBLANES, C_WIDTH), F32),
        ],
        scratch_shapes=[
            pltpu.VMEM((C_WIDTH // LANES, tm, LANES), F32),
            pltpu.VMEM((C_WIDTH // LANES, tm, LANES), F32),
            pltpu.VMEM((SUBLANES, C_WIDTH), F32),
        ],
        compiler_params=_params(("arbitrary", "arbitrary")),
        name="inproj",
    )(xn, w1, b1, wg, bg, wconv, mtail, sconv)


def _gate_layouts(a):
    r = a.shape[0]
    if r < LANES:
        a = jnp.concatenate([a, jnp.zeros((LANES - r, LANES), F32)], axis=0)
    at8 = a.T[0:SUBLANES, :]
    lane = lax.broadcasted_iota(jnp.int32, (SUBLANES, LANES), 1)
    b8 = at8
    shift = 1
    while shift < LANES:
        b8 = b8 + jnp.where(lane >= shift, pltpu.roll(b8, shift, 1), 0.0)
        shift *= 2
    bc = jnp.concatenate([b8, jnp.zeros((LANES - SUBLANES, LANES), F32)], axis=0).T
    return at8, b8, bc


def _mlstm_chunk(q, k, v, ig_row, b_row, bq_col, bk_col, igk_col, c_st, n_st, m_st):
    lq, l = q.shape[0], k.shape[0]
    t_idx = lax.broadcasted_iota(jnp.int32, (lq, l), 0)
    s_idx = lax.broadcasted_iota(jnp.int32, (lq, l), 1)
    logw = jnp.where(s_idx <= t_idx, bq_col - b_row + ig_row, NEG)
    inter = bq_col + m_st
    m_t = jnp.maximum(inter, jnp.max(logw, axis=-1, keepdims=True))
    w_inter = jnp.exp(inter - m_t)
    s = lax.dot_general(q, k, (((1,), (1,)), ((), ())), preferred_element_type=F32) * jnp.exp(logw - m_t)
    num = (w_inter * jnp.dot(q, c_st.astype(BF16), preferred_element_type=F32)
           + jnp.dot(s.astype(BF16), v, preferred_element_type=F32))
    den = (w_inter * jnp.sum(q.astype(F32) * n_st, axis=-1, keepdims=True)
           + jnp.sum(s, axis=-1, keepdims=True))
    h = num * (1.0 / jnp.maximum(jnp.abs(den), jnp.exp(-m_t)))
    m_new = m_t[lq - 1:lq, :]
    b_last = bq_col[lq - 1:lq, :]
    decay = jnp.exp(b_last + m_st - m_new)
    kw = k.astype(F32) * jnp.exp(b_last - bk_col + igk_col - m_new)
    c_new = decay * c_st + lax.dot_general(kw.astype(BF16), v, (((0,), (0,)), ((), ())),
                                           preferred_element_type=F32)
    n_new = decay * n_st + jnp.sum(kw, axis=0, keepdims=True)
    return h, c_new, n_new, m_new


def _gate_tile(g, n_valid):
    lane = lax.broadcasted_iota(jnp.int32, g.shape, 1)
    a = jnp.where(lane < M_HEADS, g, _log_sigmoid(g))
    if n_valid < g.shape[0]:
        row = lax.broadcasted_iota(jnp.int32, g.shape, 0)
        a = jnp.where(row < n_valid, a, jnp.where(lane < M_HEADS, NEG, 0.0))
    return a


def _mlstm_prompt_kernel(qm_ref, km_ref, vm_ref, gm_ref, q_ref, k_ref, v_ref, g_ref,
                         h_ref, c_ref, n_ref, m_ref):
    c = pl.program_id(1)

    def run_chunk(qr, kr, vr, gr, n_valid, write_h):
        a = _gate_tile(gr[...], n_valid)
        at8, b8, bc = _gate_layouts(a)
        for hd in range(M_HEADS):
            sl = slice(hd * M_DK, (hd + 1) * M_DK)
            b_col = bc[:, M_HEADS + hd:M_HEADS + hd + 1]
            h, c_new, n_new, m_new = _mlstm_chunk(
                qr[:, sl], kr[:, sl], vr[:, sl],
                at8[hd:hd + 1, :], b8[M_HEADS + hd:M_HEADS + hd + 1, :], b_col, b_col, a[:, hd:hd + 1],
                c_ref[0, hd], n_ref[0, hd:hd + 1, :], m_ref[0, hd:hd + 1, 0:1])
            c_ref[0, hd] = c_new
            n_ref[0, hd:hd + 1, :] = n_new
            m_ref[0, hd:hd + 1, :] = jnp.broadcast_to(m_new, (1, LANES))
            if write_h:
                h_ref[:, sl] = h.astype(h_ref.dtype)

    @pl.when(c == 0)
    def _():
        c_ref[...] = jnp.zeros_like(c_ref)
        n_ref[...] = jnp.zeros_like(n_ref)
        m_ref[...] = jnp.zeros_like(m_ref)
        run_chunk(qm_ref, km_ref, vm_ref, gm_ref, N_META, False)

    run_chunk(q_ref, k_ref, v_ref, g_ref, CHUNK, True)


def _mlstm_prompt_call(zm, gm, z, gates, batch, seq):
    nc = seq // CHUNK
    row = lambda b, c: b * nc + c
    return pl.pallas_call(
        _mlstm_prompt_kernel,
        grid=(batch, nc),
        in_specs=[
            pl.BlockSpec((CHUNK, M_WIDTH), lambda b, c: (0, Z_Q)),
            pl.BlockSpec((CHUNK, M_WIDTH), lambda b, c: (0, Z_K)),
            pl.BlockSpec((CHUNK, M_WIDTH), lambda b, c: (0, Z_V)),
            pl.BlockSpec((CHUNK, LANES), lambda b, c: (0, 0)),
            pl.BlockSpec((CHUNK, M_WIDTH), lambda b, c: (row(b, c), Z_Q)),
            pl.BlockSpec((CHUNK, M_WIDTH), lambda b, c: (row(b, c), Z_K)),
            pl.BlockSpec((CHUNK, M_WIDTH), lambda b, c: (row(b, c), Z_V)),
            pl.BlockSpec((CHUNK, LANES), lambda b, c: (row(b, c), 0)),
        ],
        out_specs=[
            pl.BlockSpec((CHUNK, M_WIDTH), lambda b, c: (row(b, c), 0)),
            pl.BlockSpec((1, M_HEADS, M_DK, M_DV), lambda b, c: (b, 0, 0, 0)),
            pl.BlockSpec((1, M_HEADS, M_DK), lambda b, c: (b, 0, 0)),
            pl.BlockSpec((1, M_HEADS, LANES), lambda b, c: (b, 0, 0)),
        ],
        out_shape=[
            jax.ShapeDtypeStruct((batch * seq, M_WIDTH), BF16),
            jax.ShapeDtypeStruct((batch, M_HEADS, M_DK, M_DV), F32),
            jax.ShapeDtypeStruct((batch, M_HEADS, M_DK), F32),
            jax.ShapeDtypeStruct((batch, M_HEADS, LANES), F32),
        ],
        compiler_params=_params(("arbitrary", "arbitrary")),
        name="mlstm_prompt",
    )(zm, zm, zm, gm, z, z, z, gates)


def _mlstm_sample_kernel(bb, dec_seq, q_ref, k_ref, v_ref, g_ref, cin_ref, nin_ref, min_ref,
                         h_ref, c_ref, n_ref, m_ref):
    qf = q_ref[...].astype(F32)
    kf = k_ref[...].astype(F32)
    vf = v_ref[...].astype(F32)
    a_all = _gate_tile(g_ref[...], g_ref.shape[0])
    zpad = jnp.zeros((CHUNK - dec_seq, M_DK), F32)
    gpad = jnp.zeros((CHUNK - dec_seq, LANES), F32)
    for bi in range(bb):
        rows = slice(bi * dec_seq, (bi + 1) * dec_seq)
        a8 = a_all[rows, :]
        at8, b8, bc = _gate_layouts(a8)
        a_pad = jnp.concatenate([a8, gpad], axis=0)
        for hd in range(M_HEADS):
            sl = slice(hd * M_DK, (hd + 1) * M_DK)
            kp = jnp.concatenate([kf[rows, sl], zpad], axis=0).astype(BF16)
            vp = jnp.concatenate([vf[rows, sl], zpad], axis=0).astype(BF16)
            b_col = bc[:, M_HEADS + hd:M_HEADS + hd + 1]
            h, c_new, n_new, m_new = _mlstm_chunk(
                qf[rows, sl].astype(BF16), kp, vp,
                at8[hd:hd + 1, :], b8[M_HEADS + hd:M_HEADS + hd + 1, :],
                b_col[0:dec_seq, :], b_col, a_pad[:, hd:hd + 1],
                cin_ref[bi, hd], nin_ref[bi, hd:hd + 1, :], min_ref[bi, hd:hd + 1, 0:1])
            c_ref[bi, hd] = c_new
            n_ref[bi, hd:hd + 1, :] = n_new
            m_ref[bi, hd:hd + 1, :] = jnp.broadcast_to(m_new, (1, LANES))
            h_ref[rows, sl] = h


def _mlstm_sample_call(z, gates, c_in, n_in, m_in, row0, dec_seq, bb):
    nb = c_in.shape[0]
    rb = bb * dec_seq
    blk0 = row0 // rb
    return pl.pallas_call(
        functools.partial(_mlstm_sample_kernel, bb, dec_seq),
        grid=(nb // bb,),
        in_specs=[
            pl.BlockSpec((rb, M_WIDTH), lambda i: (blk0 + i, Z_Q)),
            pl.BlockSpec((rb, M_WIDTH), lambda i: (blk0 + i, Z_K)),
            pl.BlockSpec((rb, M_WIDTH), lambda i: (blk0 + i, Z_V)),
            pl.BlockSpec((rb, LANES), lambda i: (blk0 + i, 0)),
            pl.BlockSpec((bb, M_HEADS, M_DK, M_DV), lambda i: (i, 0, 0, 0)),
            pl.BlockSpec((bb, M_HEADS, M_DK), lambda i: (i, 0, 0)),
            pl.BlockSpec((bb, M_HEADS, LANES), lambda i: (i, 0, 0)),
        ],
        out_specs=[
            pl.BlockSpec((rb, M_WIDTH), lambda i: (i, 0)),
            pl.BlockSpec((bb, M_HEADS, M_DK, M_DV), lambda i: (i, 0, 0, 0)),
            pl.BlockSpec((bb, M_HEADS, M_DK), lambda i: (i, 0, 0)),
            pl.BlockSpec((bb, M_HEADS, LANES), lambda i: (i, 0, 0)),
        ],
        out_shape=[
            jax.ShapeDtypeStruct((nb * dec_seq, M_WIDTH), F32),
            jax.ShapeDtypeStruct(c_in.shape, F32),
            jax.ShapeDtypeStruct(n_in.shape, F32),
            jax.ShapeDtypeStruct(m_in.shape, F32),
        ],
        compiler_params=_params(("arbitrary",)),
        name="mlstm_sample",
    )(z, z, z, gates, c_in, n_in, m_in)


def _mix_kernel(n_prompt_tiles, hp_ref, hs_ref, so_ref, yb_ref, ga_ref, gb_ref, xp_ref, xs_ref,
                pa_ref, pb_ref, wo_ref, o_ref):
    i = pl.program_id(0)

    def body(h_ref, x_ref):
        h_a = (so_ref[...].astype(F32) * h_ref[...].astype(F32)).astype(BF16)
        a = jnp.dot(h_a, pa_ref[...], preferred_element_type=F32)
        b = jnp.dot(yb_ref[...], pb_ref[...], preferred_element_type=F32)
        merged = ga_ref[...].astype(F32) * a + gb_ref[...].astype(F32) * b
        o_ref[...] = x_ref[...] + jnp.dot(merged.astype(BF16), wo_ref[...], preferred_element_type=F32)

    @pl.when(i < n_prompt_tiles)
    def _():
        body(hp_ref, xp_ref)

    @pl.when(i >= n_prompt_tiles)
    def _():
        body(hs_ref, xs_ref)


def _mix_call(hp, hs, z, xp, xs, pa, pb, wo, tm):
    tp, ts = xp.shape[0], xs.shape[0]
    npt, nst = tp // tm, ts // tm
    pi = lambda i: jnp.minimum(i, npt - 1)
    si = lambda i: jnp.maximum(i - npt, 0)
    const = lambda i: (0, 0)
    return pl.pallas_call(
        functools.partial(_mix_kernel, npt),
        grid=(npt + nst,),
        in_specs=[
            pl.BlockSpec((tm, M_WIDTH), lambda i: (pi(i), 0)),
            pl.BlockSpec((tm, M_WIDTH), lambda i: (si(i), 0)),
            pl.BlockSpec((tm, M_WIDTH), lambda i: (i, Z_SO)),
            pl.BlockSpec((tm, C_WIDTH), lambda i: (i, Z_YB)),
            pl.BlockSpec((tm, D_MODEL), lambda i: (i, Z_GA // 2)),
            pl.BlockSpec((tm, D_MODEL), lambda i: (i, Z_GB // 2)),
            pl.BlockSpec((tm, D_MODEL), lambda i: (pi(i), 0)),
            pl.BlockSpec((tm, D_MODEL), lambda i: (si(i), 0)),
            pl.BlockSpec((M_WIDTH, D_MODEL), const, pipeline_mode=pl.Buffered(1)),
            pl.BlockSpec((C_WIDTH, D_MODEL), const, pipeline_mode=pl.Buffered(1)),
            pl.BlockSpec((D_MODEL, D_MODEL), const, pipeline_mode=pl.Buffered(1)),
        ],
        out_specs=pl.BlockSpec((tm, D_MODEL), lambda i: (i, 0)),
        out_shape=jax.ShapeDtypeStruct((tp + ts, D_MODEL), F32),
        compiler_params=_params(("arbitrary",)),
        name="mix",
    )(hp, hs, z, z, z, z, xp, xs, pa, pb, wo)


def _ffn_kernel(n_prompt_tiles, x1_ref, gffn_ref, wup_ref, wdn_ref, gfin_ref, yp_ref, ys_ref,
                xn_ref, acc_ref):
    i = pl.program_id(0)
    f = pl.program_id(1)

    @pl.when(f == 0)
    def _():
        xn_ref[...] = _rmsnorm(x1_ref[...], gffn_ref[...]).astype(BF16)

    u = jnp.maximum(jnp.dot(xn_ref[...], wup_ref[...], preferred_element_type=F32), 0.0)
    part = jnp.dot((u * u).astype(BF16), wdn_ref[...], preferred_element_type=F32)

    @pl.when(f == 0)
    def _():
        acc_ref[...] = part

    @pl.when(f > 0)
    def _():
        acc_ref[...] += part

    @pl.when(f == pl.num_programs(1) - 1)
    def _():
        y = _rmsnorm(x1_ref[...] + acc_ref[...], gfin_ref[...])

        @pl.when(i < n_prompt_tiles)
        def _():
            yp_ref[...] = y

        @pl.when(i >= n_prompt_tiles)
        def _():
            ys_ref[...] = y


def _ffn_call(x1, gffn, wup, wdn, gfin, n_prompt_tokens, tm, tf):
    t = x1.shape[0]
    npt = n_prompt_tokens // tm
    n_tiles = t // tm
    return pl.pallas_call(
        functools.partial(_ffn_kernel, npt),
        grid=(n_tiles, D_FF // tf),
        in_specs=[
            pl.BlockSpec((tm, D_MODEL), lambda i, f: (i, 0)),
            pl.BlockSpec((1, D_MODEL), lambda i, f: (0, 0)),
            pl.BlockSpec((D_MODEL, tf), lambda i, f: (0, f)),
            pl.BlockSpec((tf, D_MODEL), lambda i, f: (f, 0)),
            pl.BlockSpec((1, D_MODEL), lambda i, f: (0, 0)),
        ],
        out_specs=[
            pl.BlockSpec((tm, D_MODEL), lambda i, f: (jnp.minimum(i, npt - 1), 0)),
            pl.BlockSpec((tm, D_MODEL), lambda i, f: (jnp.maximum(i - npt, 0), 0)),
        ],
        out_shape=[
            jax.ShapeDtypeStruct((n_prompt_tokens, D_MODEL), F32),
            jax.ShapeDtypeStruct((t - n_prompt_tokens, D_MODEL), F32),
        ],
        scratch_shapes=[pltpu.VMEM((tm, D_MODEL), BF16), pltpu.VMEM((tm, D_MODEL), F32)],
        compiler_params=_params(("arbitrary", "arbitrary")),
        name="ffn",
    )(x1, gffn, wup, wdn, gfin)


def kernel(x_prompt, x_sample, state_mlstm_C, state_mlstm_n, state_mlstm_m, state_conv, meta_tokens,
           g_mix, w_in, b_in, w_conv, p_a, p_b, w_o, g_ffn, w_up, w_down, g_final):
    assert w_in.shape[0] == 1, "single-layer trunk"
    batch, seq, _ = x_prompt.shape
    dec_batch, dec_seq, _ = x_sample.shape
    assert dec_seq == SUBLANES and seq % 1024 == 0
    n_p, n_s = batch * seq, dec_batch * dec_seq

    w = w_in[0]
    bias = b_in[0]
    w1 = jnp.concatenate([w[:, :GATE_OFF], w[:, GATE_OFF + N_GATES:]], axis=1).astype(BF16)
    b1 = jnp.concatenate([bias[:GATE_OFF], bias[GATE_OFF + N_GATES:]])[None, :]
    wg = jnp.pad(w[:, GATE_OFF:GATE_OFF + N_GATES], ((0, 0), (0, LANES - N_GATES))).astype(BF16)
    bg = jnp.pad(bias[GATE_OFF:GATE_OFF + N_GATES], (0, LANES - N_GATES))[None, :]
    gmix = g_mix[0][None, :]

    xp = x_prompt.reshape(n_p, D_MODEL)
    xs = x_sample.reshape(n_s, D_MODEL)
    x_meta = jnp.pad(meta_tokens.astype(F32), ((0, CHUNK - N_META), (0, 0)))

    xn = _rms_call(xp, xs, gmix, 512)
    zm, gm, cu_m = _meta_call(x_meta, gmix, w1, b1, wg, bg)
    sconv = state_conv[0].reshape(dec_batch, (CONV_W - 1) * C_WIDTH)
    z, gates, cu6, cu7 = _inproj_call(xn, w1, b1, wg, bg, w_conv[0], cu_m, sconv, n_p, seq, 1024)

    hp, c_p, n_pr, m_p = _mlstm_prompt_call(zm, gm, z, gates, batch, seq)
    m_in = jnp.broadcast_to(state_mlstm_m[0][:, :, None], (dec_batch, M_HEADS, LANES))
    hs, c_s, n_sm, m_s = _mlstm_sample_call(z, gates, state_mlstm_C[0], state_mlstm_n[0], m_in,
                                            n_p, dec_seq, 4)

    x1 = _mix_call(hp, hs, z, xp, xs, p_a[0].astype(BF16), p_b[0].astype(BF16), w_o[0].astype(BF16), 256)
    y_p, y_s = _ffn_call(x1, g_ffn[0][None, :], w_up[0].astype(BF16), w_down[0].astype(BF16),
                         g_final[None, :], n_p, 512, 1024)

    last_p = (jnp.arange(batch) + 1) * (seq // SUBLANES) - 1
    cv_p = jnp.stack([cu6[last_p], cu7[last_p]], axis=1)
    first_s = n_p // SUBLANES
    cv_s = jnp.stack([cu6[first_s:], cu7[first_s:]], axis=1)
    return (y_p.reshape(batch, seq, D_MODEL), y_s.reshape(dec_batch, dec_seq, D_MODEL),
            c_p[None], n_pr[None], m_p[None, :, :, 0], cv_p[None],
            c_s[None], n_sm[None], m_s[None, :, :, 0], cv_s[None])
```

```python
import functools

import jax
import jax.numpy as jnp
from jax import lax
from jax.experimental import pallas as pl
from jax.experimental.pallas import tpu as pltpu

F32 = jnp.float32
BF16 = jnp.bfloat16

D_MODEL = 2048
N_META = 16
CHUNK = 128
M_HEADS = 4
M_DK = 256
M_DV = 256
M_WIDTH = M_HEADS * M_DV
C_WIDTH = 1024
CONV_W = 3
D_FF = 4 * D_MODEL
EPS = 1e-6
N_GATES = 2 * M_HEADS
OFF_Q = 0
OFF_K = OFF_Q + M_HEADS * M_DK
OFF_V = OFF_K + M_HEADS * M_DK
OFF_GATES = OFF_V + M_WIDTH
OFF_O = OFF_GATES + N_GATES
OFF_U = OFF_O + M_WIDTH
OFF_C = OFF_U + C_WIDTH
OFF_B = OFF_C + C_WIDTH
OFF_GA = OFF_B + C_WIDTH
OFF_GB = OFF_GA + D_MODEL

LANES = 128
SUBLANES = 8
VMEM_LIMIT_BYTES = 60000 * 1024

TN = 1024
Z_Q, Z_K, Z_V, Z_SO, Z_GA, Z_GB = 0, 1, 2, 3, 4, 6
Z_COLS = 8 * TN
CQ = 256

NEG = -1e30
NT_DIMS = (((1,), (1,)), ((), ()))


def _params(semantics):
    return pltpu.CompilerParams(dimension_semantics=semantics, vmem_limit_bytes=VMEM_LIMIT_BYTES)


def _rmsnorm(x, g):
    y = x * lax.rsqrt(jnp.mean(x * x, axis=-1, keepdims=True) + EPS)
    return y * g


def _log_sigmoid(x):
    return jnp.minimum(x, 0.0) - jnp.log1p(jnp.exp(-jnp.abs(x)))


def _dot_nt(a, b):
    return lax.dot_general(a, b, NT_DIMS, preferred_element_type=F32)


def _gate_weights(wg_ref):
    wg = wg_ref[...].astype(BF16)
    return jnp.concatenate([wg, jnp.zeros((LANES - N_GATES, wg.shape[1]), BF16)], axis=0)


def _rms_kernel(n_prompt_tiles, xp_ref, xs_ref, g_ref, wg_ref, bg_ref, o_ref, gate_ref):
    i = pl.program_id(0)

    def run(x_ref):
        xn = _rmsnorm(x_ref[...], g_ref[...]).astype(BF16)
        o_ref[...] = xn
        gate_ref[...] = _dot_nt(xn, _gate_weights(wg_ref)) + bg_ref[...]

    @pl.when(i < n_prompt_tiles)
    def _():
        run(xp_ref)

    @pl.when(i >= n_prompt_tiles)
    def _():
        run(xs_ref)


def _rms_call(xp, xs, g, wt, bg, tm):
    tp, ts = xp.shape[0], xs.shape[0]
    npt, nst = tp // tm, ts // tm
    return pl.pallas_call(
        functools.partial(_rms_kernel, npt),
        grid=(npt + nst,),
        in_specs=[
            pl.BlockSpec((tm, D_MODEL), lambda i: (jnp.minimum(i, npt - 1), 0)),
            pl.BlockSpec((tm, D_MODEL), lambda i: (jnp.maximum(i - npt, 0), 0)),
            pl.BlockSpec((1, D_MODEL), lambda i: (0, 0)),
            pl.BlockSpec((N_GATES, D_MODEL), lambda i: (OFF_GATES // N_GATES, 0)),
            pl.BlockSpec((1, LANES), lambda i: (0, 0)),
        ],
        out_specs=[
            pl.BlockSpec((tm, D_MODEL), lambda i: (i, 0)),
            pl.BlockSpec((tm, LANES), lambda i: (i, 0)),
        ],
        out_shape=[
            jax.ShapeDtypeStruct((tp + ts, D_MODEL), BF16),
            jax.ShapeDtypeStruct((tp + ts, LANES), F32),
        ],
        compiler_params=_params(("arbitrary",)),
        name="rms",
    )(xp, xs, g, wt, bg)


def _meta_kernel(x_ref, g_ref, wt_ref, b_ref, wg_ref, bg_ref, z_ref, gate_ref, cu_ref, xn_ref, zu_ref):
    j = pl.program_id(0)

    @pl.when(j == 0)
    def _():
        xn_ref[...] = _rmsnorm(x_ref[...], g_ref[...]).astype(BF16)
        gate_ref[...] = _dot_nt(xn_ref[...], _gate_weights(wg_ref)) + bg_ref[...]

    z = _dot_nt(xn_ref[...], wt_ref[...].astype(BF16)) + b_ref[...]

    @pl.when((j == 0) | (j == 2))
    def _():
        z_ref[...] = z.astype(BF16)

    @pl.when(j == 1)
    def _():
        z_ref[...] = (z * (M_DK ** -0.5)).astype(BF16)

    @pl.when(j == 3)
    def _():
        zu_ref[...] = z

    @pl.when(j == 4)
    def _():
        cu_ref[...] = z * zu_ref[...]


def _meta_call(x_meta, g, wt, b1, bg):
    rows = x_meta.shape[0]
    w_off = lambda j: jnp.where(j < 3, j * TN, OFF_U + (j - 3) * TN)
    b_blk = lambda j: j + (j >= 3).astype(jnp.int32)
    return pl.pallas_call(
        _meta_kernel,
        grid=(5,),
        in_specs=[
            pl.BlockSpec((rows, D_MODEL), lambda j: (0, 0)),
            pl.BlockSpec((1, D_MODEL), lambda j: (0, 0)),
            pl.BlockSpec((pl.Element(TN), pl.Element(D_MODEL)), lambda j: (pl.multiple_of(w_off(j), SUBLANES), 0)),
            pl.BlockSpec((1, TN), lambda j: (0, b_blk(j))),
            pl.BlockSpec((N_GATES, D_MODEL), lambda j: (OFF_GATES // N_GATES, 0)),
            pl.BlockSpec((1, LANES), lambda j: (0, 0)),
        ],
        out_specs=[
            pl.BlockSpec((rows, TN), lambda j: (0, jnp.minimum(j, 2))),
            pl.BlockSpec((rows, LANES), lambda j: (0, 0)),
            pl.BlockSpec((rows, C_WIDTH), lambda j: (0, 0)),
        ],
        out_shape=[
            jax.ShapeDtypeStruct((rows, 3 * TN), BF16),
            jax.ShapeDtypeStruct((rows, LANES), F32),
            jax.ShapeDtypeStruct((rows, C_WIDTH), F32),
        ],
        scratch_shapes=[pltpu.VMEM((rows, D_MODEL), BF16), pltpu.VMEM((rows, C_WIDTH), F32)],
        compiler_params=_params(("arbitrary",)),
        name="meta_inproj",
    )(x_meta, g, wt, b1, wt, bg)


S_Q, S_K, S_V, S_O = 0, 1, 2, 3


def _inproj_kernel(xn_ref, wt_ref, b_ref, z_ref, w_ref):
    s = pl.program_id(0)

    @pl.when(pl.program_id(1) == 0)
    def _():
        w_ref[...] = wt_ref[...].astype(BF16)

    def z():
        return _dot_nt(xn_ref[...], w_ref[...]) + b_ref[...]

    @pl.when((s == S_Q) | (s == S_V))
    def _():
        z_ref[...] = z().astype(BF16)

    @pl.when(s == S_K)
    def _():
        z_ref[...] = (z() * (M_DK ** -0.5)).astype(BF16)

    @pl.when(s >= S_O)
    def _():
        z_ref[...] = jax.nn.sigmoid(z()).astype(BF16)


def _inproj_call(xn, wt, b1, tm):
    t = xn.shape[0]
    w_off = lambda s: jnp.where(s < S_O, s * TN, jnp.where(s == S_O, OFF_O, OFF_GA + (s - 4) * TN))
    b_blk = lambda s: jnp.where(s <= S_O, s, s + 3)
    return pl.pallas_call(
        _inproj_kernel,
        grid=(8, t // tm),
        in_specs=[
            pl.BlockSpec((tm, D_MODEL), lambda s, m: (m, 0)),
            pl.BlockSpec((pl.Element(TN), pl.Element(D_MODEL)), lambda s, m: (pl.multiple_of(w_off(s), SUBLANES), 0)),
            pl.BlockSpec((1, TN), lambda s, m: (0, b_blk(s))),
        ],
        out_specs=pl.BlockSpec((tm, TN), lambda s, m: (m, s)),
        out_shape=jax.ShapeDtypeStruct((t, Z_COLS), BF16),
        scratch_shapes=[pltpu.VMEM((TN, D_MODEL), BF16)],
        compiler_params=_params(("arbitrary", "arbitrary")),
        name="inproj",
    )(xn, wt, b1)


def _conv_kernel(n_prompt_tiles, tiles_per_seq, xn_ref, wu_ref, wc_ref, wb_ref, bu_ref, bc_ref, bb_ref,
                 wconv_ref, mtail_ref, s0_ref, s1_ref, yb_ref, cu6_ref, cu7_ref,
                 w3_ref, cu_ref, h1_ref, h2_ref, carry_ref):
    m = pl.program_id(1)
    tm, cq = yb_ref.shape
    n_seq = tm // SUBLANES
    slabs = [(k, slice(k * LANES, (k + 1) * LANES)) for k in range(cq // LANES)]
    seq_row = lambda r: pl.ds(r, n_seq, stride=SUBLANES)
    is_prompt = m < n_prompt_tiles

    @pl.when(m == 0)
    def _():
        w3_ref[0] = wu_ref[...].astype(BF16)
        w3_ref[1] = wc_ref[...].astype(BF16)
        w3_ref[2] = wb_ref[...].astype(BF16)
        h1_ref[...] = jnp.zeros_like(h1_ref)
        h2_ref[...] = jnp.zeros_like(h2_ref)

    @pl.when(is_prompt)
    def _():
        first = (m % tiles_per_seq) == 0
        p6 = jnp.where(first, mtail_ref[6:7, :], carry_ref[6:7, :])
        p7 = jnp.where(first, mtail_ref[7:8, :], carry_ref[7:8, :])
        for k, ks in slabs:
            h2_ref[k, 0:1, :] = p6[:, ks]
            h2_ref[k, 1:2, :] = p7[:, ks]
            h1_ref[k, 0:1, :] = p7[:, ks]

    @pl.when(jnp.logical_not(is_prompt))
    def _():
        for k, ks in slabs:
            h2_ref[k, seq_row(0), :] = s0_ref[:, ks]
            h2_ref[k, seq_row(1), :] = s1_ref[:, ks]
            h1_ref[k, seq_row(0), :] = s1_ref[:, ks]

    xn = xn_ref[...]
    zu = _dot_nt(xn, w3_ref[0]) + bu_ref[...]
    zc = _dot_nt(xn, w3_ref[1]) + bc_ref[...]
    zb = _dot_nt(xn, w3_ref[2]) + bb_ref[...]
    cu = zc * zu
    pos = lax.broadcasted_iota(jnp.int32, (tm, cq), 0) & jnp.where(is_prompt, tm - 1, SUBLANES - 1)
    h1 = jnp.concatenate([h1_ref[k] for k, _ in slabs], axis=1)
    h2 = jnp.concatenate([h2_ref[k] for k, _ in slabs], axis=1)
    x1 = jnp.where(pos >= 1, pltpu.roll(cu, 1, 0), h1)
    x2 = jnp.where(pos >= 2, pltpu.roll(cu, 2, 0), h2)
    w0, w1, w2 = wconv_ref[0:1, :], wconv_ref[1:2, :], wconv_ref[2:3, :]
    yb_ref[...] = (zb * ((w0 * x2 + w1 * x1) + w2 * cu)).astype(BF16)

    carry_ref[...] = cu[tm - SUBLANES:tm, :]
    for k, ks in slabs:
        cu_ref[k] = cu[:, ks]
        cu6_ref[:, ks] = cu_ref[k, seq_row(6), :]
        cu7_ref[:, ks] = cu_ref[k, seq_row(7), :]


def _conv_call(xn, wt, b1, wconv, mtail, s0, s1, n_prompt_tokens, seq_len, tm):
    t = xn.shape[0]
    assert tm & (tm - 1) == 0 and seq_len % tm == 0
    npt = n_prompt_tokens // tm
    n_seq = tm // SUBLANES
    n_slab = CQ // LANES
    w_spec = lambda off: pl.BlockSpec((pl.Element(CQ), pl.Element(D_MODEL)),
                                      lambda c, m: (pl.multiple_of(off + c * CQ, SUBLANES), 0))
    b_spec = lambda off: pl.BlockSpec((1, CQ), lambda c, m: (0, (off - N_GATES) // CQ + c))
    s_spec = pl.BlockSpec((n_seq, CQ), lambda c, m: (jnp.maximum(m - npt, 0), c))
    return pl.pallas_call(
        functools.partial(_conv_kernel, npt, seq_len // tm),
        grid=(C_WIDTH // CQ, t // tm),
        in_specs=[
            pl.BlockSpec((tm, D_MODEL), lambda c, m: (m, 0)),
            w_spec(OFF_U), w_spec(OFF_C), w_spec(OFF_B),
            b_spec(OFF_U), b_spec(OFF_C), b_spec(OFF_B),
            pl.BlockSpec((CONV_W, CQ), lambda c, m: (0, c)),
            pl.BlockSpec((SUBLANES, CQ), lambda c, m: (1, c)),
            s_spec, s_spec,
        ],
        out_specs=[
            pl.BlockSpec((tm, CQ), lambda c, m: (m, c)),
            pl.BlockSpec((n_seq, CQ), lambda c, m: (m, c)),
            pl.BlockSpec((n_seq, CQ), lambda c, m: (m, c)),
        ],
        out_shape=[
            jax.ShapeDtypeStruct((t, C_WIDTH), BF16),
            jax.ShapeDtypeStruct((t // SUBLANES, C_WIDTH), F32),
            jax.ShapeDtypeStruct((t // SUBLANES, C_WIDTH), F32),
        ],
        scratch_shapes=[
            pltpu.VMEM((3, CQ, D_MODEL), BF16),
            pltpu.VMEM((n_slab, tm, LANES), F32),
            pltpu.VMEM((n_slab, tm, LANES), F32),
            pltpu.VMEM((n_slab, tm, LANES), F32),
            pltpu.VMEM((SUBLANES, CQ), F32),
        ],
        compiler_params=_params(("arbitrary", "arbitrary")),
        name="inproj_conv",
    )(xn, wt, wt, wt, b1, b1, b1, wconv, mtail, s0, s1)


def _gate_layouts(a):
    r = a.shape[0]
    if r < LANES:
        a = jnp.concatenate([a, jnp.zeros((LANES - r, LANES), F32)], axis=0)
    at8 = a.T[0:SUBLANES, :]
    lane = lax.broadcasted_iota(jnp.int32, (SUBLANES, LANES), 1)
    b8 = at8
    shift = 1
    while shift < LANES:
        b8 = b8 + jnp.where(lane >= shift, pltpu.roll(b8, shift, 1), 0.0)
        shift *= 2
    bc = jnp.concatenate([b8, jnp.zeros((LANES - SUBLANES, LANES), F32)], axis=0).T
    return at8, b8, bc


def _mlstm_chunk(q, k, v, ig_row, b_row, bq_col, bk_col, igk_col, c_st, n_st, m_st):
    lq, l = q.shape[0], k.shape[0]
    t_idx = lax.broadcasted_iota(jnp.int32, (lq, l), 0)
    s_idx = lax.broadcasted_iota(jnp.int32, (lq, l), 1)
    logw = jnp.where(s_idx <= t_idx, bq_col - b_row + ig_row, NEG)
    inter = bq_col + m_st
    m_t = jnp.maximum(inter, jnp.max(logw, axis=-1, keepdims=True))
    w_inter = jnp.exp(inter - m_t)
    s = _dot_nt(q, k) * jnp.exp(logw - m_t)
    num = (w_inter * jnp.dot(q, c_st.astype(BF16), preferred_element_type=F32)
           + jnp.dot(s.astype(BF16), v, preferred_element_type=F32))
    den = (w_inter * jnp.sum(q.astype(F32) * n_st, axis=-1, keepdims=True)
           + jnp.sum(s, axis=-1, keepdims=True))
    h = num * (1.0 / jnp.maximum(jnp.abs(den), jnp.exp(-m_t)))
    m_new = m_t[lq - 1:lq, :]
    b_last = bq_col[lq - 1:lq, :]
    decay = jnp.exp(b_last + m_st - m_new)
    kw = k.astype(F32) * jnp.exp(b_last - bk_col + igk_col - m_new)
    c_new = decay * c_st + lax.dot_general(kw.astype(BF16), v, (((0,), (0,)), ((), ())),
                                           preferred_element_type=F32)
    n_new = decay * n_st + jnp.sum(kw, axis=0, keepdims=True)
    return h, c_new, n_new, m_new


def _gate_tile(g, n_valid):
    lane = lax.broadcasted_iota(jnp.int32, g.shape, 1)
    a = jnp.where(lane < M_HEADS, g, _log_sigmoid(g))
    if n_valid < g.shape[0]:
        row = lax.broadcasted_iota(jnp.int32, g.shape, 0)
        a = jnp.where(row < n_valid, a, jnp.where(lane < M_HEADS, NEG, 0.0))
    return a


def _mlstm_prompt_kernel(qm_ref, km_ref, vm_ref, gm_ref, q_ref, k_ref, v_ref, g_ref,
                         h_ref, c_ref, n_ref, m_ref):
    c = pl.program_id(1)

    def run_chunk(qr, kr, vr, gr, n_valid, write_h):
        a = _gate_tile(gr[...], n_valid)
        at8, b8, bc = _gate_layouts(a)
        for hd in range(M_HEADS):
            sl = slice(hd * M_DK, (hd + 1) * M_DK)
            b_col = bc[:, M_HEADS + hd:M_HEADS + hd + 1]
            h, c_new, n_new, m_new = _mlstm_chunk(
                qr[:, sl], kr[:, sl], vr[:, sl],
                at8[hd:hd + 1, :], b8[M_HEADS + hd:M_HEADS + hd + 1, :], b_col, b_col, a[:, hd:hd + 1],
                c_ref[0, hd], n_ref[0, hd:hd + 1, :], m_ref[0, hd:hd + 1, 0:1])
            c_ref[0, hd] = c_new
            n_ref[0, hd:hd + 1, :] = n_new
            m_ref[0, hd:hd + 1, :] = jnp.broadcast_to(m_new, (1, LANES))
            if write_h:
                h_ref[:, sl] = h.astype(h_ref.dtype)

    @pl.when(c == 0)
    def _():
        c_ref[...] = jnp.zeros_like(c_ref)
        n_ref[...] = jnp.zeros_like(n_ref)
        m_ref[...] = jnp.zeros_like(m_ref)
        run_chunk(qm_ref, km_ref, vm_ref, gm_ref, N_META, False)

    run_chunk(q_ref, k_ref, v_ref, g_ref, CHUNK, True)


def _mlstm_prompt_call(zm, gm, z, gates, batch, seq):
    nc = seq // CHUNK
    row = lambda b, c: b * nc + c
    return pl.pallas_call(
        _mlstm_prompt_kernel,
        grid=(batch, nc),
        in_specs=[
            pl.BlockSpec((CHUNK, M_WIDTH), lambda b, c: (0, Z_Q)),
            pl.BlockSpec((CHUNK, M_WIDTH), lambda b, c: (0, Z_K)),
            pl.BlockSpec((CHUNK, M_WIDTH), lambda b, c: (0, Z_V)),
            pl.BlockSpec((CHUNK, LANES), lambda b, c: (0, 0)),
            pl.BlockSpec((CHUNK, M_WIDTH), lambda b, c: (row(b, c), Z_Q)),
            pl.BlockSpec((CHUNK, M_WIDTH), lambda b, c: (row(b, c), Z_K)),
            pl.BlockSpec((CHUNK, M_WIDTH), lambda b, c: (row(b, c), Z_V)),
            pl.BlockSpec((CHUNK, LANES), lambda b, c: (row(b, c), 0)),
        ],
        out_specs=[
            pl.BlockSpec((CHUNK, M_WIDTH), lambda b, c: (row(b, c), 0)),
            pl.BlockSpec((1, M_HEADS, M_DK, M_DV), lambda b, c: (b, 0, 0, 0)),
            pl.BlockSpec((1, M_HEADS, M_DK), lambda b, c: (b, 0, 0)),
            pl.BlockSpec((1, M_HEADS, LANES), lambda b, c: (b, 0, 0)),
        ],
        out_shape=[
            jax.ShapeDtypeStruct((batch * seq, M_WIDTH), BF16),
            jax.ShapeDtypeStruct((batch, M_HEADS, M_DK, M_DV), F32),
            jax.ShapeDtypeStruct((batch, M_HEADS, M_DK), F32),
            jax.ShapeDtypeStruct((batch, M_HEADS, LANES), F32),
        ],
        compiler_params=_params(("arbitrary", "arbitrary")),
        name="mlstm_prompt",
    )(zm, zm, zm, gm, z, z, z, gates)


def _mlstm_sample_kernel(bb, dec_seq, q_ref, k_ref, v_ref, g_ref, cin_ref, nin_ref, min_ref,
                         h_ref, c_ref, n_ref, m_ref):
    qf = q_ref[...].astype(F32)
    kf = k_ref[...].astype(F32)
    vf = v_ref[...].astype(F32)
    a_all = _gate_tile(g_ref[...], g_ref.shape[0])
    zpad = jnp.zeros((CHUNK - dec_seq, M_DK), F32)
    gpad = jnp.zeros((CHUNK - dec_seq, LANES), F32)
    for bi in range(bb):
        rows = slice(bi * dec_seq, (bi + 1) * dec_seq)
        a8 = a_all[rows, :]
        at8, b8, bc = _gate_layouts(a8)
        a_pad = jnp.concatenate([a8, gpad], axis=0)
        for hd in range(M_HEADS):
            sl = slice(hd * M_DK, (hd + 1) * M_DK)
            kp = jnp.concatenate([kf[rows, sl], zpad], axis=0).astype(BF16)
            vp = jnp.concatenate([vf[rows, sl], zpad], axis=0).astype(BF16)
            b_col = bc[:, M_HEADS + hd:M_HEADS + hd + 1]
            h, c_new, n_new, m_new = _mlstm_chunk(
                qf[rows, sl].astype(BF16), kp, vp,
                at8[hd:hd + 1, :], b8[M_HEADS + hd:M_HEADS + hd + 1, :],
                b_col[0:dec_seq, :], b_col, a_pad[:, hd:hd + 1],
                cin_ref[bi, hd], nin_ref[bi, hd:hd + 1, :], min_ref[bi, hd:hd + 1, 0:1])
            c_ref[bi, hd] = c_new
            n_ref[bi, hd:hd + 1, :] = n_new
            m_ref[bi, hd:hd + 1, :] = jnp.broadcast_to(m_new, (1, LANES))
            h_ref[rows, sl] = h


def _mlstm_sample_call(z, gates, c_in, n_in, m_in, row0, dec_seq, bb):
    nb = c_in.shape[0]
    rb = bb * dec_seq
    blk0 = row0 // rb
    return pl.pallas_call(
        functools.partial(_mlstm_sample_kernel, bb, dec_seq),
        grid=(nb // bb,),
        in_specs=[
            pl.BlockSpec((rb, M_WIDTH), lambda i: (blk0 + i, Z_Q)),
            pl.BlockSpec((rb, M_WIDTH), lambda i: (blk0 + i, Z_K)),
            pl.BlockSpec((rb, M_WIDTH), lambda i: (blk0 + i, Z_V)),
            pl.BlockSpec((rb, LANES), lambda i: (blk0 + i, 0)),
            pl.BlockSpec((bb, M_HEADS, M_DK, M_DV), lambda i: (i, 0, 0, 0)),
            pl.BlockSpec((bb, M_HEADS, M_DK), lambda i: (i, 0, 0)),
            pl.BlockSpec((bb, M_HEADS, LANES), lambda i: (i, 0, 0)),
        ],
        out_specs=[
            pl.BlockSpec((rb, M_WIDTH), lambda i: (i, 0)),
            pl.BlockSpec((bb, M_HEADS, M_DK, M_DV), lambda i: (i, 0, 0, 0)),
            pl.BlockSpec((bb, M_HEADS, M_DK), lambda i: (i, 0, 0)),
            pl.BlockSpec((bb, M_HEADS, LANES), lambda i: (i, 0, 0)),
        ],
        out_shape=[
            jax.ShapeDtypeStruct((nb * dec_seq, M_WIDTH), F32),
            jax.ShapeDtypeStruct(c_in.shape, F32),
            jax.ShapeDtypeStruct(n_in.shape, F32),
            jax.ShapeDtypeStruct(m_in.shape, F32),
        ],
        compiler_params=_params(("arbitrary",)),
        name="mlstm_sample",
    )(z, z, z, gates, c_in, n_in, m_in)


def _mix_kernel(n_prompt_tiles, hp_ref, hs_ref, so_ref, yb_ref, ga_ref, gb_ref, xp_ref, xs_ref,
                pa_ref, pb_ref, wo_ref, o_ref):
    i = pl.program_id(0)

    def body(h_ref, x_ref):
        h_a = (so_ref[...].astype(F32) * h_ref[...].astype(F32)).astype(BF16)
        a = jnp.dot(h_a, pa_ref[...], preferred_element_type=F32)
        b = jnp.dot(yb_ref[...], pb_ref[...], preferred_element_type=F32)
        merged = ga_ref[...].astype(F32) * a + gb_ref[...].astype(F32) * b
        o_ref[...] = x_ref[...] + jnp.dot(merged.astype(BF16), wo_ref[...], preferred_element_type=F32)

    @pl.when(i < n_prompt_tiles)
    def _():
        body(hp_ref, xp_ref)

    @pl.when(i >= n_prompt_tiles)
    def _():
        body(hs_ref, xs_ref)


def _mix_call(hp, hs, z, yb, xp, xs, pa, pb, wo, tm):
    tp, ts = xp.shape[0], xs.shape[0]
    npt, nst = tp // tm, ts // tm
    pi = lambda i: jnp.minimum(i, npt - 1)
    si = lambda i: jnp.maximum(i - npt, 0)
    const = lambda i: (0, 0)
    return pl.pallas_call(
        functools.partial(_mix_kernel, npt),
        grid=(npt + nst,),
        in_specs=[
            pl.BlockSpec((tm, M_WIDTH), lambda i: (pi(i), 0)),
            pl.BlockSpec((tm, M_WIDTH), lambda i: (si(i), 0)),
            pl.BlockSpec((tm, M_WIDTH), lambda i: (i, Z_SO)),
            pl.BlockSpec((tm, C_WIDTH), lambda i: (i, 0)),
            pl.BlockSpec((tm, D_MODEL), lambda i: (i, Z_GA // 2)),
            pl.BlockSpec((tm, D_MODEL), lambda i: (i, Z_GB // 2)),
            pl.BlockSpec((tm, D_MODEL), lambda i: (pi(i), 0)),
            pl.BlockSpec((tm, D_MODEL), lambda i: (si(i), 0)),
            pl.BlockSpec((M_WIDTH, D_MODEL), const, pipeline_mode=pl.Buffered(1)),
            pl.BlockSpec((C_WIDTH, D_MODEL), const, pipeline_mode=pl.Buffered(1)),
            pl.BlockSpec((D_MODEL, D_MODEL), const, pipeline_mode=pl.Buffered(1)),
        ],
        out_specs=pl.BlockSpec((tm, D_MODEL), lambda i: (i, 0)),
        out_shape=jax.ShapeDtypeStruct((tp + ts, D_MODEL), F32),
        compiler_params=_params(("arbitrary",)),
        name="mix",
    )(hp, hs, z, yb, z, z, xp, xs, pa, pb, wo)


def _ffn_kernel(n_prompt_tiles, x1_ref, gffn_ref, wup_ref, wdn_ref, gfin_ref, yp_ref, ys_ref,
                xn_ref, acc_ref):
    i = pl.program_id(0)
    f = pl.program_id(1)

    @pl.when(f == 0)
    def _():
        x1 = x1_ref[...]
        xn_ref[...] = _rmsnorm(x1, gffn_ref[...]).astype(BF16)
        acc_ref[...] = x1

    u = jnp.maximum(jnp.dot(xn_ref[...], wup_ref[...], preferred_element_type=F32), 0.0)
    acc_ref[...] += jnp.dot((u * u).astype(BF16), wdn_ref[...], preferred_element_type=F32)

    @pl.when(f == pl.num_programs(1) - 1)
    def _():
        y = _rmsnorm(acc_ref[...], gfin_ref[...])

        @pl.when(i < n_prompt_tiles)
        def _():
            yp_ref[...] = y

        @pl.when(i >= n_prompt_tiles)
        def _():
            ys_ref[...] = y


def _ffn_call(x1, gffn, wup, wdn, gfin, n_prompt_tokens, tm, tf):
    t = x1.shape[0]
    npt = n_prompt_tokens // tm
    n_tiles = t // tm
    return pl.pallas_call(
        functools.partial(_ffn_kernel, npt),
        grid=(n_tiles, D_FF // tf),
        in_specs=[
            pl.BlockSpec((tm, D_MODEL), lambda i, f: (i, 0)),
            pl.BlockSpec((1, D_MODEL), lambda i, f: (0, 0)),
            pl.BlockSpec((D_MODEL, tf), lambda i, f: (0, f)),
            pl.BlockSpec((tf, D_MODEL), lambda i, f: (f, 0)),
            pl.BlockSpec((1, D_MODEL), lambda i, f: (0, 0)),
        ],
        out_specs=[
            pl.BlockSpec((tm, D_MODEL), lambda i, f: (jnp.minimum(i, npt - 1), 0)),
            pl.BlockSpec((tm, D_MODEL), lambda i, f: (jnp.maximum(i - npt, 0), 0)),
        ],
        out_shape=[
            jax.ShapeDtypeStruct((n_prompt_tokens, D_MODEL), F32),
            jax.ShapeDtypeStruct((t - n_prompt_tokens, D_MODEL), F32),
        ],
        scratch_shapes=[pltpu.VMEM((tm, D_MODEL), BF16), pltpu.VMEM((tm, D_MODEL), F32)],
        compiler_params=_params(("arbitrary", "arbitrary")),
        name="ffn",
    )(x1, gffn, wup, wdn, gfin)


def kernel(x_prompt, x_sample, state_mlstm_C, state_mlstm_n, state_mlstm_m, state_conv, meta_tokens,
           g_mix, w_in, b_in, w_conv, p_a, p_b, w_o, g_ffn, w_up, w_down, g_final):
    assert w_in.shape[0] == 1, "single-layer trunk"
    batch, seq, _ = x_prompt.shape
    dec_batch, dec_seq, _ = x_sample.shape
    assert dec_seq == SUBLANES and seq % 1024 == 0
    n_p, n_s = batch * seq, dec_batch * dec_seq

    wt = w_in[0].T
    bias = b_in[0]
    b1 = jnp.concatenate([bias[:OFF_GATES], bias[OFF_O:]])[None, :]
    bg = jnp.pad(bias[OFF_GATES:OFF_O], (0, LANES - N_GATES))[None, :]
    gmix = g_mix[0][None, :]

    xp = x_prompt.reshape(n_p, D_MODEL)
    xs = x_sample.reshape(n_s, D_MODEL)
    x_meta = jnp.pad(meta_tokens.astype(F32), ((0, CHUNK - N_META), (0, 0)))

    xn, gates = _rms_call(xp, xs, gmix, wt, bg, 512)
    zm, gm, cu_m = _meta_call(x_meta, gmix, wt, b1, bg)
    z = _inproj_call(xn, wt, b1, 1024)
    yb, cu6, cu7 = _conv_call(xn, wt, b1, w_conv[0], cu_m, state_conv[0, :, 0, :], state_conv[0, :, 1, :],
                              n_p, seq, 1024)

    hp, c_p, n_pr, m_p = _mlstm_prompt_call(zm, gm, z, gates, batch, seq)
    m_in = jnp.broadcast_to(state_mlstm_m[0][:, :, None], (dec_batch, M_HEADS, LANES))
    hs, c_s, n_sm, m_s = _mlstm_sample_call(z, gates, state_mlstm_C[0], state_mlstm_n[0], m_in,
                                            n_p, dec_seq, 4)

    x1 = _mix_call(hp, hs, z, yb, xp, xs, p_a[0].astype(BF16), p_b[0].astype(BF16), w_o[0].astype(BF16), 256)
    y_p, y_s = _ffn_call(x1, g_ffn[0][None, :], w_up[0].astype(BF16), w_down[0].astype(BF16),
                         g_final[None, :], n_p, 512, 1024)

    last_p = (jnp.arange(batch) + 1) * (seq // SUBLANES) - 1
    cv_p = jnp.stack([cu6[last_p], cu7[last_p]], axis=1)
    first_s = n_p // SUBLANES
    cv_s = jnp.stack([cu6[first_s:], cu7[first_s:]], axis=1)
    return (y_p.reshape(batch, seq, D_MODEL), y_s.reshape(dec_batch, dec_seq, D_MODEL),
            c_p[None], n_pr[None], m_p[None, :, :, 0], cv_p[None],
            c_s[None], n_sm[None], m_s[None, :, :, 0], cv_s[None])
```

```python
import functools

import jax
import jax.numpy as jnp
from jax import lax
from jax.experimental import pallas as pl
from jax.experimental.pallas import tpu as pltpu

F32 = jnp.float32
BF16 = jnp.bfloat16

D_MODEL = 2048
N_META = 16
CHUNK = 128
M_HEADS = 4
M_DK = 256
M_DV = 256
M_WIDTH = M_HEADS * M_DV
C_WIDTH = 1024
CONV_W = 3
D_FF = 4 * D_MODEL
EPS = 1e-6
N_GATES = 2 * M_HEADS
OFF_Q = 0
OFF_K = OFF_Q + M_HEADS * M_DK
OFF_V = OFF_K + M_HEADS * M_DK
OFF_GATES = OFF_V + M_WIDTH
OFF_O = OFF_GATES + N_GATES
OFF_U = OFF_O + M_WIDTH
OFF_C = OFF_U + C_WIDTH
OFF_B = OFF_C + C_WIDTH
OFF_GA = OFF_B + C_WIDTH
OFF_GB = OFF_GA + D_MODEL

LANES = 128
SUBLANES = 8
VMEM_LIMIT_BYTES = 60000 * 1024

TN = 1024
Z_Q, Z_V, Z_GA, Z_GB, Z_SO = 0, 1, 2, 4, 6
Z_COLS = 7 * TN
CQ = 256
AUG = M_DV + LANES
SEQ_PER_STEP = CHUNK // SUBLANES
G_IG, G_B, G_R = 0, 1, 2

NEG = -1e30
NT_DIMS = (((1,), (1,)), ((), ()))


def _params(semantics):
    return pltpu.CompilerParams(dimension_semantics=semantics, vmem_limit_bytes=VMEM_LIMIT_BYTES)


def _rmsnorm(x, g):
    y = x * lax.rsqrt(jnp.mean(x * x, axis=-1, keepdims=True) + EPS)
    return y * g


def _log_sigmoid(x):
    return jnp.minimum(x, 0.0) - jnp.log1p(jnp.exp(-jnp.abs(x)))


def _dot(a, b):
    return jnp.dot(a, b, preferred_element_type=F32)


def _dot_nt(a, b):
    return lax.dot_general(a, b, NT_DIMS, preferred_element_type=F32)


def _gate_prep(xn, wg_ref, bg_ref, blk, n_valid, gch_ref, grh_ref):
    tm = xn.shape[0]
    wg = wg_ref[...].astype(BF16)
    wg = jnp.concatenate([wg, jnp.zeros((LANES - N_GATES, wg.shape[1]), BF16)], axis=0)
    g = _dot_nt(wg, xn)[0:SUBLANES, :] + bg_ref[...]
    row = lax.broadcasted_iota(jnp.int32, (SUBLANES, tm), 0)
    lane = lax.broadcasted_iota(jnp.int32, (SUBLANES, tm), 1)
    a = jnp.where(row < M_HEADS, g, _log_sigmoid(g))
    if n_valid < tm:
        a = jnp.where(lane < n_valid, a, jnp.where(row < M_HEADS, NEG, 0.0))
    pos = lane & (blk - 1)
    pre, suf = a, a
    shift = 1
    while shift < blk:
        pre = pre + jnp.where(pos >= shift, pltpu.roll(pre, shift, 1), 0.0)
        suf = suf + jnp.where(pos < blk - shift, pltpu.roll(suf, tm - shift, 1), 0.0)
        shift *= 2
    suf = suf - a
    zeros = jnp.zeros((LANES - SUBLANES, LANES), F32)
    for h in range(M_HEADS):
        grh = jnp.where(
            row == G_IG, pltpu.roll(a, (G_IG - h) % SUBLANES, 0),
            jnp.where(row == G_B, pltpu.roll(pre, (G_B - M_HEADS - h) % SUBLANES, 0),
                      jnp.where(row == G_R, pltpu.roll(suf, (G_R - M_HEADS - h) % SUBLANES, 0), 0.0)))
        grh_ref[h] = grh
        for c in range(tm // LANES):
            cs = slice(c * LANES, (c + 1) * LANES)
            gch_ref[h, cs, :] = jnp.concatenate([grh[:, cs], zeros], axis=0).T


def _rms_kernel(n_prompt_tiles, dec_seq, xp_ref, xs_ref, g_ref, wg_ref, bg_ref, o_ref, gch_ref, grh_ref):
    i = pl.program_id(0)

    def run(x_ref, blk):
        xn = _rmsnorm(x_ref[...], g_ref[...]).astype(BF16)
        o_ref[...] = xn
        _gate_prep(xn, wg_ref, bg_ref, blk, xn.shape[0], gch_ref, grh_ref)

    @pl.when(i < n_prompt_tiles)
    def _():
        run(xp_ref, CHUNK)

    @pl.when(i >= n_prompt_tiles)
    def _():
        run(xs_ref, dec_seq)


def _rms_call(xp, xs, g, wt, bg, dec_seq, tm):
    tp, ts = xp.shape[0], xs.shape[0]
    npt, nst = tp // tm, ts // tm
    t = tp + ts
    return pl.pallas_call(
        functools.partial(_rms_kernel, npt, dec_seq),
        grid=(npt + nst,),
        in_specs=[
            pl.BlockSpec((tm, D_MODEL), lambda i: (jnp.minimum(i, npt - 1), 0)),
            pl.BlockSpec((tm, D_MODEL), lambda i: (jnp.maximum(i - npt, 0), 0)),
            pl.BlockSpec((1, D_MODEL), lambda i: (0, 0)),
            pl.BlockSpec((N_GATES, D_MODEL), lambda i: (OFF_GATES // N_GATES, 0)),
            pl.BlockSpec((N_GATES, 1), lambda i: (0, 0)),
        ],
        out_specs=[
            pl.BlockSpec((tm, D_MODEL), lambda i: (i, 0)),
            pl.BlockSpec((M_HEADS, tm, LANES), lambda i: (0, i, 0)),
            pl.BlockSpec((M_HEADS, SUBLANES, tm), lambda i: (0, 0, i)),
        ],
        out_shape=[
            jax.ShapeDtypeStruct((t, D_MODEL), BF16),
            jax.ShapeDtypeStruct((M_HEADS, t, LANES), F32),
            jax.ShapeDtypeStruct((M_HEADS, SUBLANES, t), F32),
        ],
        compiler_params=_params(("arbitrary",)),
        name="rms",
    )(xp, xs, g, wt, bg)


def _meta_kernel(x_ref, g_ref, wt_ref, b_ref, bcol_ref, wg_ref, bg_ref,
                 z_ref, kt_ref, gch_ref, grh_ref, cu_ref, xn_ref, zu_ref):
    j = pl.program_id(0)

    @pl.when(j == 0)
    def _():
        xn_ref[...] = _rmsnorm(x_ref[...], g_ref[...]).astype(BF16)
        _gate_prep(xn_ref[...], wg_ref, bg_ref, CHUNK, N_META, gch_ref, grh_ref)

    w = wt_ref[...].astype(BF16)

    @pl.when(j == 1)
    def _():
        kt_ref[...] = ((_dot_nt(w, xn_ref[...]) + bcol_ref[...]) * (M_DK ** -0.5)).astype(BF16)

    @pl.when(j != 1)
    def _():
        z = _dot_nt(xn_ref[...], w) + b_ref[...]

        @pl.when(j <= 2)
        def _():
            z_ref[...] = z.astype(BF16)

        @pl.when(j == 3)
        def _():
            zu_ref[...] = z

        @pl.when(j == 4)
        def _():
            cu_ref[...] = z * zu_ref[...]


def _meta_call(x_meta, g, wt, b1, b1col, bg):
    rows = x_meta.shape[0]
    w_off = lambda j: jnp.where(j < 3, j * TN, OFF_U + (j - 3) * TN)
    b_blk = lambda j: j + (j >= 3).astype(jnp.int32)
    return pl.pallas_call(
        _meta_kernel,
        grid=(5,),
        in_specs=[
            pl.BlockSpec((rows, D_MODEL), lambda j: (0, 0)),
            pl.BlockSpec((1, D_MODEL), lambda j: (0, 0)),
            pl.BlockSpec((pl.Element(TN), pl.Element(D_MODEL)),
                         lambda j: (pl.multiple_of(w_off(j), SUBLANES), 0)),
            pl.BlockSpec((1, TN), lambda j: (0, b_blk(j))),
            pl.BlockSpec((TN, 1), lambda j: (b_blk(j), 0)),
            pl.BlockSpec((N_GATES, D_MODEL), lambda j: (OFF_GATES // N_GATES, 0)),
            pl.BlockSpec((N_GATES, 1), lambda j: (0, 0)),
        ],
        out_specs=[
            pl.BlockSpec((rows, TN), lambda j: (0, (j >= 2).astype(jnp.int32))),
            pl.BlockSpec((TN, rows), lambda j: (0, 0)),
            pl.BlockSpec((M_HEADS, rows, LANES), lambda j: (0, 0, 0)),
            pl.BlockSpec((M_HEADS, SUBLANES, rows), lambda j: (0, 0, 0)),
            pl.BlockSpec((rows, C_WIDTH), lambda j: (0, 0)),
        ],
        out_shape=[
            jax.ShapeDtypeStruct((rows, 2 * TN), BF16),
            jax.ShapeDtypeStruct((TN, rows), BF16),
            jax.ShapeDtypeStruct((M_HEADS, rows, LANES), F32),
            jax.ShapeDtypeStruct((M_HEADS, SUBLANES, rows), F32),
            jax.ShapeDtypeStruct((rows, C_WIDTH), F32),
        ],
        scratch_shapes=[pltpu.VMEM((rows, D_MODEL), BF16), pltpu.VMEM((rows, C_WIDTH), F32)],
        compiler_params=_params(("arbitrary",)),
        name="meta_inproj",
    )(x_meta, g, wt, b1, b1col, wt, bg)


S_Q, S_K, S_V, S_O = 0, 1, 2, 3


def _inproj_kernel(xn_ref, wt_ref, b_ref, bcol_ref, z_ref, kt_ref, w_ref):
    s = pl.program_id(0)

    @pl.when(pl.program_id(1) == 0)
    def _():
        w_ref[...] = wt_ref[...].astype(BF16)

    def z():
        return _dot_nt(xn_ref[...], w_ref[...]) + b_ref[...]

    @pl.when((s == S_Q) | (s == S_V))
    def _():
        z_ref[...] = z().astype(BF16)

    @pl.when(s == S_K)
    def _():
        kt_ref[...] = ((_dot_nt(w_ref[...], xn_ref[...]) + bcol_ref[...]) * (M_DK ** -0.5)).astype(BF16)

    @pl.when(s >= S_O)
    def _():
        z_ref[...] = jax.nn.sigmoid(z()).astype(BF16)


def _inproj_call(xn, wt, b1, b1col, tm):
    t = xn.shape[0]
    n_m = t // tm
    w_off = lambda s: jnp.where(s < S_O, s * TN, jnp.where(s == S_O, OFF_O, OFF_GA + (s - 4) * TN))
    b_blk = lambda s: jnp.where(s <= S_O, s, s + 3)
    z_col = lambda s: jnp.where(s <= S_K, Z_Q, jnp.where(s == S_V, Z_V, jnp.where(s == S_O, Z_SO, s - 2)))
    z_row = lambda s, m: jnp.where(s == S_K, n_m - 1, m)
    kt_blk = lambda s, m: jnp.where(s == S_K, m, jnp.where(s < S_K, 0, n_m - 1))
    return pl.pallas_call(
        _inproj_kernel,
        grid=(8, n_m),
        in_specs=[
            pl.BlockSpec((tm, D_MODEL), lambda s, m: (m, 0)),
            pl.BlockSpec((pl.Element(TN), pl.Element(D_MODEL)),
                         lambda s, m: (pl.multiple_of(w_off(s), SUBLANES), 0)),
            pl.BlockSpec((1, TN), lambda s, m: (0, b_blk(s))),
            pl.BlockSpec((TN, 1), lambda s, m: (b_blk(s), 0)),
        ],
        out_specs=[
            pl.BlockSpec((tm, TN), lambda s, m: (z_row(s, m), z_col(s))),
            pl.BlockSpec((TN, tm), lambda s, m: (0, kt_blk(s, m))),
        ],
        out_shape=[
            jax.ShapeDtypeStruct((t, Z_COLS), BF16),
            jax.ShapeDtypeStruct((TN, t), BF16),
        ],
        scratch_shapes=[pltpu.VMEM((TN, D_MODEL), BF16)],
        compiler_params=_params(("arbitrary", "arbitrary")),
        name="inproj",
    )(xn, wt, b1, b1col)


def _conv_kernel(n_prompt_tiles, tiles_per_seq, xn_ref, wu_ref, wc_ref, wb_ref, bu_ref, bc_ref, bb_ref,
                 wconv_ref, mtail_ref, s0_ref, s1_ref, yb_ref, cu6_ref, cu7_ref,
                 w3_ref, cu_ref, h1_ref, h2_ref, carry_ref):
    m = pl.program_id(1)
    tm, cq = yb_ref.shape
    n_seq = tm // SUBLANES
    slabs = [(k, slice(k * LANES, (k + 1) * LANES)) for k in range(cq // LANES)]
    seq_row = lambda r: pl.ds(r, n_seq, stride=SUBLANES)
    is_prompt = m < n_prompt_tiles

    @pl.when(m == 0)
    def _():
        w3_ref[0] = wu_ref[...].astype(BF16)
        w3_ref[1] = wc_ref[...].astype(BF16)
        w3_ref[2] = wb_ref[...].astype(BF16)
        h1_ref[...] = jnp.zeros_like(h1_ref)
        h2_ref[...] = jnp.zeros_like(h2_ref)

    @pl.when(is_prompt)
    def _():
        first = (m % tiles_per_seq) == 0
        p6 = jnp.where(first, mtail_ref[6:7, :], carry_ref[6:7, :])
        p7 = jnp.where(first, mtail_ref[7:8, :], carry_ref[7:8, :])
        for k, ks in slabs:
            h2_ref[k, 0:1, :] = p6[:, ks]
            h2_ref[k, 1:2, :] = p7[:, ks]
            h1_ref[k, 0:1, :] = p7[:, ks]

    @pl.when(jnp.logical_not(is_prompt))
    def _():
        for k, ks in slabs:
            h2_ref[k, seq_row(0), :] = s0_ref[:, ks]
            h2_ref[k, seq_row(1), :] = s1_ref[:, ks]
            h1_ref[k, seq_row(0), :] = s1_ref[:, ks]

    xn = xn_ref[...]
    zu = _dot_nt(xn, w3_ref[0]) + bu_ref[...]
    zc = _dot_nt(xn, w3_ref[1]) + bc_ref[...]
    zb = _dot_nt(xn, w3_ref[2]) + bb_ref[...]
    cu = zc * zu
    pos = lax.broadcasted_iota(jnp.int32, (tm, cq), 0) & jnp.where(is_prompt, tm - 1, SUBLANES - 1)
    h1 = jnp.concatenate([h1_ref[k] for k, _ in slabs], axis=1)
    h2 = jnp.concatenate([h2_ref[k] for k, _ in slabs], axis=1)
    x1 = jnp.where(pos >= 1, pltpu.roll(cu, 1, 0), h1)
    x2 = jnp.where(pos >= 2, pltpu.roll(cu, 2, 0), h2)
    w0, w1, w2 = wconv_ref[0:1, :], wconv_ref[1:2, :], wconv_ref[2:3, :]
    yb_ref[...] = (zb * ((w0 * x2 + w1 * x1) + w2 * cu)).astype(BF16)

    carry_ref[...] = cu[tm - SUBLANES:tm, :]
    for k, ks in slabs:
        cu_ref[k] = cu[:, ks]
        cu6_ref[:, ks] = cu_ref[k, seq_row(6), :]
        cu7_ref[:, ks] = cu_ref[k, seq_row(7), :]


def _conv_call(xn, wt, b1, wconv, mtail, s0, s1, n_prompt_tokens, seq_len, tm):
    t = xn.shape[0]
    assert tm & (tm - 1) == 0 and seq_len % tm == 0
    npt = n_prompt_tokens // tm
    n_seq = tm // SUBLANES
    n_slab = CQ // LANES
    w_spec = lambda off: pl.BlockSpec((pl.Element(CQ), pl.Element(D_MODEL)),
                                      lambda c, m: (pl.multiple_of(off + c * CQ, SUBLANES), 0))
    b_spec = lambda off: pl.BlockSpec((1, CQ), lambda c, m: (0, (off - N_GATES) // CQ + c))
    s_spec = pl.BlockSpec((n_seq, CQ), lambda c, m: (jnp.maximum(m - npt, 0), c))
    return pl.pallas_call(
        functools.partial(_conv_kernel, npt, seq_len // tm),
        grid=(C_WIDTH // CQ, t // tm),
        in_specs=[
            pl.BlockSpec((tm, D_MODEL), lambda c, m: (m, 0)),
            w_spec(OFF_U), w_spec(OFF_C), w_spec(OFF_B),
            b_spec(OFF_U), b_spec(OFF_C), b_spec(OFF_B),
            pl.BlockSpec((CONV_W, CQ), lambda c, m: (0, c)),
            pl.BlockSpec((SUBLANES, CQ), lambda c, m: (1, c)),
            s_spec, s_spec,
        ],
        out_specs=[
            pl.BlockSpec((tm, CQ), lambda c, m: (m, c)),
            pl.BlockSpec((n_seq, CQ), lambda c, m: (m, c)),
            pl.BlockSpec((n_seq, CQ), lambda c, m: (m, c)),
        ],
        out_shape=[
            jax.ShapeDtypeStruct((t, C_WIDTH), BF16),
            jax.ShapeDtypeStruct((t // SUBLANES, C_WIDTH), F32),
            jax.ShapeDtypeStruct((t // SUBLANES, C_WIDTH), F32),
        ],
        scratch_shapes=[
            pltpu.VMEM((3, CQ, D_MODEL), BF16),
            pltpu.VMEM((n_slab, tm, LANES), F32),
            pltpu.VMEM((n_slab, tm, LANES), F32),
            pltpu.VMEM((n_slab, tm, LANES), F32),
            pltpu.VMEM((SUBLANES, CQ), F32),
        ],
        compiler_params=_params(("arbitrary", "arbitrary")),
        name="inproj_conv",
    )(xn, wt, wt, wt, b1, b1, b1, wconv, mtail, s0, s1)


def _prompt_head(q, kt, v, gr, gc, ca, m_st):
    l = q.shape[0]
    ig_row, b_row, r_row = gr[G_IG:G_IG + 1, :], gr[G_B:G_B + 1, :], gr[G_R:G_R + 1, :]
    b_col = gc[:, G_B:G_B + 1]
    lane = lax.broadcasted_iota(jnp.int32, (l, LANES), 1)
    v_aug = jnp.concatenate([v, jnp.where(lane == 0, 1.0, 0.0).astype(BF16)], axis=1)
    t_idx = lax.broadcasted_iota(jnp.int32, (l, l), 0)
    s_idx = lax.broadcasted_iota(jnp.int32, (l, l), 1)
    logw = jnp.where(s_idx <= t_idx, b_col - b_row + ig_row, NEG)
    inter = b_col + m_st
    m_t = jnp.maximum(inter, jnp.max(logw, axis=-1, keepdims=True))
    w_inter = jnp.exp(inter - m_t)
    s = _dot(q, kt) * jnp.exp(logw - m_t)
    nd = w_inter * _dot(q, ca.astype(BF16)) + _dot(s.astype(BF16), v_aug)
    h = nd[:, 0:M_DV] * (1.0 / jnp.maximum(jnp.abs(nd[:, M_DV:M_DV + 1]), jnp.exp(-m_t)))
    m_new = m_t[l - 1:l, :]
    decay = jnp.exp(b_col[l - 1:l, :] + m_st - m_new)
    kwt = (kt.astype(F32) * jnp.exp(r_row + ig_row - m_new)).astype(BF16)
    return h, decay * ca + _dot(kwt, v_aug), m_new


def _mlstm_prompt_kernel(qm_ref, ktm_ref, vm_ref, gcm_ref, grm_ref, q_ref, kt_ref, v_ref, gc_ref, gr_ref,
                         h_ref, ca_ref, m_ref):
    c = pl.program_id(1)

    def run_chunk(qr, ktr, vr, gcr, grr, write_h):
        for hd in range(M_HEADS):
            sl = slice(hd * M_DK, (hd + 1) * M_DK)
            h, ca_new, m_new = _prompt_head(qr[:, sl], ktr[sl, :], vr[:, sl], grr[hd], gcr[hd],
                                            ca_ref[0, hd], m_ref[0, hd:hd + 1, 0:1])
            ca_ref[0, hd] = ca_new
            m_ref[0, hd:hd + 1, :] = jnp.broadcast_to(m_new, (1, LANES))
            if write_h:
                h_ref[:, sl] = h.astype(h_ref.dtype)

    @pl.when(c == 0)
    def _():
        ca_ref[...] = jnp.zeros_like(ca_ref)
        m_ref[...] = jnp.zeros_like(m_ref)
        run_chunk(qm_ref, ktm_ref, vm_ref, gcm_ref, grm_ref, False)

    run_chunk(q_ref, kt_ref, v_ref, gc_ref, gr_ref, True)


def _mlstm_prompt_call(zm, ktm, gcm, grm, z, kt, gch, grh, batch, seq):
    nc = seq // CHUNK
    row = lambda b, c: b * nc + c
    return pl.pallas_call(
        _mlstm_prompt_kernel,
        grid=(batch, nc),
        in_specs=[
            pl.BlockSpec((CHUNK, M_WIDTH), lambda b, c: (0, 0)),
            pl.BlockSpec((M_WIDTH, CHUNK), lambda b, c: (0, 0)),
            pl.BlockSpec((CHUNK, M_WIDTH), lambda b, c: (0, 1)),
            pl.BlockSpec((M_HEADS, CHUNK, LANES), lambda b, c: (0, 0, 0)),
            pl.BlockSpec((M_HEADS, SUBLANES, CHUNK), lambda b, c: (0, 0, 0)),
            pl.BlockSpec((CHUNK, M_WIDTH), lambda b, c: (row(b, c), Z_Q)),
            pl.BlockSpec((M_WIDTH, CHUNK), lambda b, c: (0, row(b, c))),
            pl.BlockSpec((CHUNK, M_WIDTH), lambda b, c: (row(b, c), Z_V)),
            pl.BlockSpec((M_HEADS, CHUNK, LANES), lambda b, c: (0, row(b, c), 0)),
            pl.BlockSpec((M_HEADS, SUBLANES, CHUNK), lambda b, c: (0, 0, row(b, c))),
        ],
        out_specs=[
            pl.BlockSpec((CHUNK, M_WIDTH), lambda b, c: (row(b, c), 0)),
            pl.BlockSpec((1, M_HEADS, M_DK, AUG), lambda b, c: (b, 0, 0, 0)),
            pl.BlockSpec((1, M_HEADS, LANES), lambda b, c: (b, 0, 0)),
        ],
        out_shape=[
            jax.ShapeDtypeStruct((batch * seq, M_WIDTH), BF16),
            jax.ShapeDtypeStruct((batch, M_HEADS, M_DK, AUG), F32),
            jax.ShapeDtypeStruct((batch, M_HEADS, LANES), F32),
        ],
        compiler_params=_params(("arbitrary", "arbitrary")),
        name="mlstm_prompt",
    )(zm, ktm, zm, gcm, grm, z, kt, z, gch, grh)


def _group_max(x, size):
    n = x.shape[-1]
    lane = lax.broadcasted_iota(jnp.int32, x.shape, x.ndim - 1)
    k = 1
    while k < size:
        partner = jnp.where((lane & k) == 0, pltpu.roll(x, n - k, x.ndim - 1), pltpu.roll(x, k, x.ndim - 1))
        x = jnp.maximum(x, partner)
        k *= 2
    return x


def _mlstm_sample_kernel(dec_seq, q_ref, kt_ref, v_ref, gc_ref, gr_ref, mcol_ref, mrow_ref, cin_ref, nin_ref,
                         h_ref, c_ref, n_ref, m_ref):
    l = q_ref.shape[0]
    n_seq = l // dec_seq
    shift = dec_seq.bit_length() - 1
    q, kt, v = q_ref[...], kt_ref[...], v_ref[...]
    gr, gc = gr_ref[0], gc_ref[0]
    ig_row, b_row, r_row = gr[G_IG:G_IG + 1, :], gr[G_B:G_B + 1, :], gr[G_R:G_R + 1, :]
    b_col = gc[:, G_B:G_B + 1]
    m_col, m_row = mcol_ref[0], mrow_ref[0]

    t_idx = lax.broadcasted_iota(jnp.int32, (l, l), 0)
    s_idx = lax.broadcasted_iota(jnp.int32, (l, l), 1)
    t_seq = t_idx >> shift
    logw = jnp.where((t_seq == (s_idx >> shift)) & (s_idx <= t_idx), b_col - b_row + ig_row, NEG)
    inter = b_col + m_col
    m_t = jnp.maximum(inter, jnp.max(logw, axis=-1, keepdims=True))
    w_inter = jnp.exp(inter - m_t)
    s = _dot(q, kt) * jnp.exp(logw - m_t)

    e_row = r_row + ig_row
    b_last = b_row + r_row
    m_new = jnp.maximum(b_last + m_row, _group_max(e_row, dec_seq))
    decay = jnp.exp(b_last + m_row - m_new)
    kwt = (kt.astype(F32) * jnp.exp(e_row - m_new)).astype(BF16)

    seq_lane = lax.broadcasted_iota(jnp.int32, (l, LANES), 1) == t_seq[:, 0:LANES]
    v_aug = jnp.concatenate([v, jnp.where(seq_lane, 1.0, 0.0).astype(BF16)], axis=1)
    pa = _dot(s.astype(BF16), v_aug)

    n_rows = jnp.concatenate([nin_ref[...], jnp.zeros((LANES - n_seq, M_DK), F32)], axis=0)
    n_t = jnp.concatenate([n_rows[:, 0:LANES].T, n_rows[:, LANES:2 * LANES].T], axis=0)
    lane_k = lax.broadcasted_iota(jnp.int32, (M_DK, LANES), 1)
    qf = q.astype(F32)
    qca_rows = []
    n_t_new = jnp.zeros((M_DK, LANES), F32)
    for i in range(n_seq):
        ca = jnp.concatenate([cin_ref[i, 0], jnp.where(lane_k == i, n_t, 0.0)], axis=1)
        rows = slice(i * dec_seq, (i + 1) * dec_seq)
        qca_rows.append(_dot(qf[rows, :].astype(BF16), ca.astype(BF16)))
        kw_i = jnp.where((lane_k >> shift) == i, kwt, jnp.zeros_like(kwt))
        ca_new = decay[:, i * dec_seq:i * dec_seq + 1] * ca + _dot(kw_i, v_aug)
        c_ref[i, 0] = ca_new[:, 0:M_DV]
        n_t_new = n_t_new + ca_new[:, M_DV:AUG]
    nd = w_inter * jnp.concatenate(qca_rows, axis=0) + pa
    den = jnp.sum(jnp.where(seq_lane, nd[:, M_DV:AUG], 0.0), axis=-1, keepdims=True)
    h_ref[...] = (nd[:, 0:M_DV] * (1.0 / jnp.maximum(jnp.abs(den), jnp.exp(-m_t)))).astype(h_ref.dtype)
    n_ref[...] = jnp.concatenate([n_t_new[0:LANES, :].T, n_t_new[LANES:2 * LANES, :].T], axis=1)[0:n_seq, :]
    m_ref[0] = jnp.broadcast_to(m_new, (SUBLANES, l))


def _mlstm_sample_call(z, kt, gch, grh, m_col, m_row, c_in, n_in, row0, dec_seq):
    nb = c_in.shape[0]
    l = SEQ_PER_STEP * dec_seq
    blk0 = row0 // l
    return pl.pallas_call(
        functools.partial(_mlstm_sample_kernel, dec_seq),
        grid=(nb // SEQ_PER_STEP, M_HEADS),
        in_specs=[
            pl.BlockSpec((l, M_DK), lambda i, h: (blk0 + i, Z_Q * M_HEADS + h)),
            pl.BlockSpec((M_DK, l), lambda i, h: (h, blk0 + i)),
            pl.BlockSpec((l, M_DV), lambda i, h: (blk0 + i, Z_V * M_HEADS + h)),
            pl.BlockSpec((1, l, LANES), lambda i, h: (h, blk0 + i, 0)),
            pl.BlockSpec((1, SUBLANES, l), lambda i, h: (h, 0, blk0 + i)),
            pl.BlockSpec((1, l, 1), lambda i, h: (h, i, 0)),
            pl.BlockSpec((1, 1, l), lambda i, h: (h, 0, i)),
            pl.BlockSpec((SEQ_PER_STEP, 1, M_DK, M_DV), lambda i, h: (i, h, 0, 0)),
            pl.BlockSpec((SEQ_PER_STEP, M_DK), lambda i, h: (i, h)),
        ],
        out_specs=[
            pl.BlockSpec((l, M_DV), lambda i, h: (i, h)),
            pl.BlockSpec((SEQ_PER_STEP, 1, M_DK, M_DV), lambda i, h: (i, h, 0, 0)),
            pl.BlockSpec((SEQ_PER_STEP, M_DK), lambda i, h: (i, h)),
            pl.BlockSpec((1, SUBLANES, l), lambda i, h: (h, 0, i)),
        ],
        out_shape=[
            jax.ShapeDtypeStruct((nb * dec_seq, M_WIDTH), BF16),
            jax.ShapeDtypeStruct(c_in.shape, F32),
            jax.ShapeDtypeStruct(n_in.shape, F32),
            jax.ShapeDtypeStruct((M_HEADS, SUBLANES, nb * dec_seq), F32),
        ],
        compiler_params=_params(("arbitrary", "arbitrary")),
        name="mlstm_sample",
    )(z, kt, z, gch, grh, m_col, m_row, c_in, n_in)


def _mix_kernel(n_prompt_tiles, hp_ref, hs_ref, so_ref, yb_ref, ga_ref, gb_ref, xp_ref, xs_ref,
                pa_ref, pb_ref, wo_ref, o_ref):
    i = pl.program_id(0)

    def body(h_ref, x_ref):
        h_a = (so_ref[...].astype(F32) * h_ref[...].astype(F32)).astype(BF16)
        a = _dot(h_a, pa_ref[...])
        b = _dot(yb_ref[...], pb_ref[...])
        merged = ga_ref[...].astype(F32) * a + gb_ref[...].astype(F32) * b
        o_ref[...] = x_ref[...] + _dot(merged.astype(BF16), wo_ref[...])

    @pl.when(i < n_prompt_tiles)
    def _():
        body(hp_ref, xp_ref)

    @pl.when(i >= n_prompt_tiles)
    def _():
        body(hs_ref, xs_ref)


def _mix_call(hp, hs, z, yb, xp, xs, pa, pb, wo, tm):
    tp, ts = xp.shape[0], xs.shape[0]
    npt, nst = tp // tm, ts // tm
    pi = lambda i: jnp.minimum(i, npt - 1)
    si = lambda i: jnp.maximum(i - npt, 0)
    const = lambda i: (0, 0)
    return pl.pallas_call(
        functools.partial(_mix_kernel, npt),
        grid=(npt + nst,),
        in_specs=[
            pl.BlockSpec((tm, M_WIDTH), lambda i: (pi(i), 0)),
            pl.BlockSpec((tm, M_WIDTH), lambda i: (si(i), 0)),
            pl.BlockSpec((tm, M_WIDTH), lambda i: (i, Z_SO)),
            pl.BlockSpec((tm, C_WIDTH), lambda i: (i, 0)),
            pl.BlockSpec((tm, D_MODEL), lambda i: (i, Z_GA // 2)),
            pl.BlockSpec((tm, D_MODEL), lambda i: (i, Z_GB // 2)),
            pl.BlockSpec((tm, D_MODEL), lambda i: (pi(i), 0)),
            pl.BlockSpec((tm, D_MODEL), lambda i: (si(i), 0)),
            pl.BlockSpec((M_WIDTH, D_MODEL), const, pipeline_mode=pl.Buffered(1)),
            pl.BlockSpec((C_WIDTH, D_MODEL), const, pipeline_mode=pl.Buffered(1)),
            pl.BlockSpec((D_MODEL, D_MODEL), const, pipeline_mode=pl.Buffered(1)),
        ],
        out_specs=pl.BlockSpec((tm, D_MODEL), lambda i: (i, 0)),
        out_shape=jax.ShapeDtypeStruct((tp + ts, D_MODEL), F32),
        compiler_params=_params(("arbitrary",)),
        name="mix",
    )(hp, hs, z, yb, z, z, xp, xs, pa, pb, wo)


def _ffn_kernel(n_prompt_tiles, x1_ref, gffn_ref, wup_ref, wdn_ref, gfin_ref, yp_ref, ys_ref,
                xn_ref, acc_ref):
    i = pl.program_id(0)
    f = pl.program_id(1)

    @pl.when(f == 0)
    def _():
        x1 = x1_ref[...]
        xn_ref[...] = _rmsnorm(x1, gffn_ref[...]).astype(BF16)
        acc_ref[...] = x1

    u = jnp.maximum(_dot(xn_ref[...], wup_ref[...]), 0.0)
    acc_ref[...] += _dot((u * u).astype(BF16), wdn_ref[...])

    @pl.when(f == pl.num_programs(1) - 1)
    def _():
        y = _rmsnorm(acc_ref[...], gfin_ref[...])

        @pl.when(i < n_prompt_tiles)
        def _():
            yp_ref[...] = y

        @pl.when(i >= n_prompt_tiles)
        def _():
            ys_ref[...] = y


def _ffn_call(x1, gffn, wup, wdn, gfin, n_prompt_tokens, tm, tf):
    t = x1.shape[0]
    npt = n_prompt_tokens // tm
    n_tiles = t // tm
    return pl.pallas_call(
        functools.partial(_ffn_kernel, npt),
        grid=(n_tiles, D_FF // tf),
        in_specs=[
            pl.BlockSpec((tm, D_MODEL), lambda i, f: (i, 0)),
            pl.BlockSpec((1, D_MODEL), lambda i, f: (0, 0)),
            pl.BlockSpec((D_MODEL, tf), lambda i, f: (0, f)),
            pl.BlockSpec((tf, D_MODEL), lambda i, f: (f, 0)),
            pl.BlockSpec((1, D_MODEL), lambda i, f: (0, 0)),
        ],
        out_specs=[
            pl.BlockSpec((tm, D_MODEL), lambda i, f: (jnp.minimum(i, npt - 1), 0)),
            pl.BlockSpec((tm, D_MODEL), lambda i, f: (jnp.maximum(i - npt, 0), 0)),
        ],
        out_shape=[
            jax.ShapeDtypeStruct((n_prompt_tokens, D_MODEL), F32),
            jax.ShapeDtypeStruct((t - n_prompt_tokens, D_MODEL), F32),
        ],
        scratch_shapes=[pltpu.VMEM((tm, D_MODEL), BF16), pltpu.VMEM((tm, D_MODEL), F32)],
        compiler_params=_params(("arbitrary", "arbitrary")),
        name="ffn",
    )(x1, gffn, wup, wdn, gfin)


def kernel(x_prompt, x_sample, state_mlstm_C, state_mlstm_n, state_mlstm_m, state_conv, meta_tokens,
           g_mix, w_in, b_in, w_conv, p_a, p_b, w_o, g_ffn, w_up, w_down, g_final):
    assert w_in.shape[0] == 1, "single-layer trunk"
    batch, seq, _ = x_prompt.shape
    dec_batch, dec_seq, _ = x_sample.shape
    assert dec_seq == SUBLANES and seq % 1024 == 0 and dec_batch % SEQ_PER_STEP == 0
    n_p, n_s = batch * seq, dec_batch * dec_seq

    wt = w_in[0].T
    bias = b_in[0]
    b1 = jnp.concatenate([bias[:OFF_GATES], bias[OFF_O:]])
    bg = bias[OFF_GATES:OFF_O][:, None]
    gmix = g_mix[0][None, :]

    xp = x_prompt.reshape(n_p, D_MODEL)
    xs = x_sample.reshape(n_s, D_MODEL)
    x_meta = jnp.pad(meta_tokens.astype(F32), ((0, CHUNK - N_META), (0, 0)))

    xn, gch, grh = _rms_call(xp, xs, gmix, wt, bg, dec_seq, 512)
    zm, ktm, gcm, grm, cu_m = _meta_call(x_meta, gmix, wt, b1[None, :], b1[:, None], bg)
    z, kt = _inproj_call(xn, wt, b1[None, :], b1[:, None], 1024)
    yb, cu6, cu7 = _conv_call(xn, wt, b1[None, :], w_conv[0], cu_m, state_conv[0, :, 0, :],
                              state_conv[0, :, 1, :], n_p, seq, 1024)

    hp, ca_p, m_p = _mlstm_prompt_call(zm, ktm, gcm, grm, z, kt, gch, grh, batch, seq)
    m_tok = jnp.repeat(state_mlstm_m[0], dec_seq, axis=0).T
    hs, c_s, n_sm, m_s = _mlstm_sample_call(
        z, kt, gch, grh, m_tok[:, :, None], m_tok[:, None, :], state_mlstm_C[0],
        state_mlstm_n[0].reshape(dec_batch, M_HEADS * M_DK), n_p, dec_seq)

    x1 = _mix_call(hp, hs, z, yb, xp, xs, p_a[0].astype(BF16), p_b[0].astype(BF16), w_o[0].astype(BF16), 256)
    y_p, y_s = _ffn_call(x1, g_ffn[0][None, :], w_up[0].astype(BF16), w_down[0].astype(BF16),
                         g_final[None, :], n_p, 512, 1024)

    last_p = (jnp.arange(batch) + 1) * (seq // SUBLANES) - 1
    cv_p = jnp.stack([cu6[last_p], cu7[last_p]], axis=1)
    first_s = n_p // SUBLANES
    cv_s = jnp.stack([cu6[first_s:], cu7[first_s:]], axis=1)
    m_s = m_s[:, 0, dec_seq - 1::dec_seq].T
    return (y_p.reshape(batch, seq, D_MODEL), y_s.reshape(dec_batch, dec_seq, D_MODEL),
            ca_p[None, :, :, :, 0:M_DV], ca_p[None, :, :, :, M_DV], m_p[None, :, :, 0], cv_p[None],
            c_s[None], n_sm.reshape(dec_batch, M_HEADS, M_DK)[None], m_s[None], cv_s[None])
```

```python
import functools

import jax
import jax.numpy as jnp
from jax import lax
from jax.experimental import pallas as pl
from jax.experimental.pallas import tpu as pltpu

F32 = jnp.float32
BF16 = jnp.bfloat16

D_MODEL = 2048
N_META = 16
CHUNK = 128
M_HEADS = 4
M_DK = 256
M_DV = 256
M_WIDTH = M_HEADS * M_DV
C_WIDTH = 1024
CONV_W = 3
D_FF = 4 * D_MODEL
EPS = 1e-6
N_GATES = 2 * M_HEADS
OFF_Q = 0
OFF_K = OFF_Q + M_HEADS * M_DK
OFF_V = OFF_K + M_HEADS * M_DK
OFF_GATES = OFF_V + M_WIDTH
OFF_O = OFF_GATES + N_GATES
OFF_U = OFF_O + M_WIDTH
OFF_C = OFF_U + C_WIDTH
OFF_B = OFF_C + C_WIDTH
OFF_GA = OFF_B + C_WIDTH
OFF_GB = OFF_GA + D_MODEL

LANES = 128
SUBLANES = 8
VMEM_LIMIT_BYTES = 60000 * 1024

TN = 1024
Z_Q, Z_V, Z_GA, Z_GB, Z_SO = 0, 1, 2, 4, 6
Z_COLS = 7 * TN
CQ = 256
AUG = M_DV + LANES
SEQ_PER_STEP = CHUNK // SUBLANES
G_IG, G_B, G_R = 0, 1, 2

NEG = -1e30
NT_DIMS = (((1,), (1,)), ((), ()))


def _params(semantics):
    return pltpu.CompilerParams(dimension_semantics=semantics, vmem_limit_bytes=VMEM_LIMIT_BYTES)


def _rmsnorm(x, g):
    y = x * lax.rsqrt(jnp.mean(x * x, axis=-1, keepdims=True) + EPS)
    return y * g


def _log_sigmoid(x):
    return jnp.minimum(x, 0.0) - jnp.log1p(jnp.exp(-jnp.abs(x)))


def _dot(a, b):
    return jnp.dot(a, b, preferred_element_type=F32)


def _dot_nt(a, b):
    return lax.dot_general(a, b, NT_DIMS, preferred_element_type=F32)


def _gate_prep(xn, wg_ref, bg_ref, blk, n_valid, gch_ref, grh_ref):
    tm = xn.shape[0]
    wg = wg_ref[...].astype(BF16)
    wg = jnp.concatenate([wg, jnp.zeros((LANES - N_GATES, wg.shape[1]), BF16)], axis=0)
    g = _dot_nt(wg, xn)[0:SUBLANES, :] + bg_ref[...]
    row = lax.broadcasted_iota(jnp.int32, (SUBLANES, tm), 0)
    lane = lax.broadcasted_iota(jnp.int32, (SUBLANES, tm), 1)
    a = jnp.where(row < M_HEADS, g, _log_sigmoid(g))
    if n_valid < tm:
        a = jnp.where(lane < n_valid, a, jnp.where(row < M_HEADS, NEG, 0.0))
    pos = lane & (blk - 1)

    def scan(x, op, fill, reverse=False):
        shift = 1
        while shift < blk:
            if reverse:
                x = op(x, jnp.where(pos < blk - shift, pltpu.roll(x, tm - shift, 1), fill))
            else:
                x = op(x, jnp.where(pos >= shift, pltpu.roll(x, shift, 1), fill))
            shift *= 2
        return x

    pre = scan(a, jnp.add, 0.0)
    suf = scan(a, jnp.add, 0.0, reverse=True) - a
    b_up = pltpu.roll(pre, M_HEADS, 0)
    m_in = b_up + scan(a - b_up, jnp.maximum, -3e38)
    for h in range(M_HEADS):
        grh_ref[h] = jnp.where(
            row == G_IG, pltpu.roll(a, (G_IG - h) % SUBLANES, 0),
            jnp.where(row == G_B, pltpu.roll(pre, (G_B - M_HEADS - h) % SUBLANES, 0),
                      jnp.where(row == G_R, pltpu.roll(suf, (G_R - M_HEADS - h) % SUBLANES, 0), 0.0)))
        for c in range(tm // LANES):
            cs = slice(c * LANES, (c + 1) * LANES)
            gch_ref[h, 0, cs, :] = jnp.broadcast_to(b_up[h:h + 1, cs], (LANES, LANES)).T
            gch_ref[h, 1, cs, :] = jnp.broadcast_to(m_in[h:h + 1, cs], (LANES, LANES)).T


def _rms_kernel(n_prompt_tiles, dec_seq, xp_ref, xs_ref, g_ref, wg_ref, bg_ref, o_ref, gch_ref, grh_ref):
    i = pl.program_id(0)

    def run(x_ref, blk):
        xn = _rmsnorm(x_ref[...], g_ref[...]).astype(BF16)
        o_ref[...] = xn
        _gate_prep(xn, wg_ref, bg_ref, blk, xn.shape[0], gch_ref, grh_ref)

    @pl.when(i < n_prompt_tiles)
    def _():
        run(xp_ref, CHUNK)

    @pl.when(i >= n_prompt_tiles)
    def _():
        run(xs_ref, dec_seq)


def _rms_call(xp, xs, g, wt, bg, dec_seq, tm):
    tp, ts = xp.shape[0], xs.shape[0]
    npt, nst = tp // tm, ts // tm
    t = tp + ts
    return pl.pallas_call(
        functools.partial(_rms_kernel, npt, dec_seq),
        grid=(npt + nst,),
        in_specs=[
            pl.BlockSpec((tm, D_MODEL), lambda i: (jnp.minimum(i, npt - 1), 0)),
            pl.BlockSpec((tm, D_MODEL), lambda i: (jnp.maximum(i - npt, 0), 0)),
            pl.BlockSpec((1, D_MODEL), lambda i: (0, 0)),
            pl.BlockSpec((N_GATES, D_MODEL), lambda i: (OFF_GATES // N_GATES, 0)),
            pl.BlockSpec((N_GATES, 1), lambda i: (0, 0)),
        ],
        out_specs=[
            pl.BlockSpec((tm, D_MODEL), lambda i: (i, 0)),
            pl.BlockSpec((M_HEADS, 2, tm, LANES), lambda i: (0, 0, i, 0)),
            pl.BlockSpec((M_HEADS, SUBLANES, tm), lambda i: (0, 0, i)),
        ],
        out_shape=[
            jax.ShapeDtypeStruct((t, D_MODEL), BF16),
            jax.ShapeDtypeStruct((M_HEADS, 2, t, LANES), F32),
            jax.ShapeDtypeStruct((M_HEADS, SUBLANES, t), F32),
        ],
        compiler_params=_params(("arbitrary",)),
        name="rms",
    )(xp, xs, g, wt, bg)


def _meta_kernel(x_ref, g_ref, wt_ref, b_ref, bcol_ref, wg_ref, bg_ref,
                 z_ref, kt_ref, gch_ref, grh_ref, cu_ref, xn_ref, zu_ref):
    j = pl.program_id(0)

    @pl.when(j == 0)
    def _():
        xn_ref[...] = _rmsnorm(x_ref[...], g_ref[...]).astype(BF16)
        _gate_prep(xn_ref[...], wg_ref, bg_ref, CHUNK, N_META, gch_ref, grh_ref)

    w = wt_ref[...].astype(BF16)

    @pl.when(j == 1)
    def _():
        kt_ref[...] = ((_dot_nt(w, xn_ref[...]) + bcol_ref[...]) * (M_DK ** -0.5)).astype(BF16)

    @pl.when(j != 1)
    def _():
        z = _dot_nt(xn_ref[...], w) + b_ref[...]

        @pl.when(j <= 2)
        def _():
            z_ref[...] = z.astype(BF16)

        @pl.when(j == 3)
        def _():
            zu_ref[...] = z

        @pl.when(j == 4)
        def _():
            cu_ref[...] = z * zu_ref[...]


def _meta_call(x_meta, g, wt, b1, b1col, bg):
    rows = x_meta.shape[0]
    w_off = lambda j: jnp.where(j < 3, j * TN, OFF_U + (j - 3) * TN)
    b_blk = lambda j: j + (j >= 3).astype(jnp.int32)
    return pl.pallas_call(
        _meta_kernel,
        grid=(5,),
        in_specs=[
            pl.BlockSpec((rows, D_MODEL), lambda j: (0, 0)),
            pl.BlockSpec((1, D_MODEL), lambda j: (0, 0)),
            pl.BlockSpec((pl.Element(TN), pl.Element(D_MODEL)),
                         lambda j: (pl.multiple_of(w_off(j), SUBLANES), 0)),
            pl.BlockSpec((1, TN), lambda j: (0, b_blk(j))),
            pl.BlockSpec((TN, 1), lambda j: (b_blk(j), 0)),
            pl.BlockSpec((N_GATES, D_MODEL), lambda j: (OFF_GATES // N_GATES, 0)),
            pl.BlockSpec((N_GATES, 1), lambda j: (0, 0)),
        ],
        out_specs=[
            pl.BlockSpec((rows, TN), lambda j: (0, (j >= 2).astype(jnp.int32))),
            pl.BlockSpec((TN, rows), lambda j: (0, 0)),
            pl.BlockSpec((M_HEADS, 2, rows, LANES), lambda j: (0, 0, 0, 0)),
            pl.BlockSpec((M_HEADS, SUBLANES, rows), lambda j: (0, 0, 0)),
            pl.BlockSpec((rows, C_WIDTH), lambda j: (0, 0)),
        ],
        out_shape=[
            jax.ShapeDtypeStruct((rows, 2 * TN), BF16),
            jax.ShapeDtypeStruct((TN, rows), BF16),
            jax.ShapeDtypeStruct((M_HEADS, 2, rows, LANES), F32),
            jax.ShapeDtypeStruct((M_HEADS, SUBLANES, rows), F32),
            jax.ShapeDtypeStruct((rows, C_WIDTH), F32),
        ],
        scratch_shapes=[pltpu.VMEM((rows, D_MODEL), BF16), pltpu.VMEM((rows, C_WIDTH), F32)],
        compiler_params=_params(("arbitrary",)),
        name="meta_inproj",
    )(x_meta, g, wt, b1, b1col, wt, bg)


S_Q, S_K, S_V, S_O = 0, 1, 2, 3


def _cast_blocks(pairs):
    for src_ref, dst_ref in pairs:
        dst_ref[...] = src_ref[...].astype(BF16)


def _cast_specs(w, n_blocks, step):
    rows, cols = w.shape
    blk = lambda *g: (jnp.minimum(step(*g), n_blocks - 1), 0)
    spec = pl.BlockSpec((rows // n_blocks, cols), blk)
    return spec, spec, jax.ShapeDtypeStruct(w.shape, BF16)


def _inproj_kernel(xn_ref, wt_ref, b_ref, bcol_ref, wup_ref, wdn_ref, z_ref, kt_ref, wup16_ref, wdn16_ref,
                   w_ref):
    s = pl.program_id(0)
    casts = ((wup_ref, wup16_ref), (wdn_ref, wdn16_ref))

    @pl.when(pl.program_id(1) == 0)
    def _():
        w_ref[...] = wt_ref[...].astype(BF16)

    def z():
        return _dot_nt(xn_ref[...], w_ref[...]) + b_ref[...]

    @pl.when((s == S_Q) | (s == S_V))
    def _():
        _cast_blocks(casts)
        z_ref[...] = z().astype(BF16)

    @pl.when(s == S_K)
    def _():
        _cast_blocks(casts)
        kt_ref[...] = ((_dot_nt(w_ref[...], xn_ref[...]) + bcol_ref[...]) * (M_DK ** -0.5)).astype(BF16)

    @pl.when(s >= S_O)
    def _():
        _cast_blocks(casts)
        z_ref[...] = jax.nn.sigmoid(z()).astype(BF16)


def _inproj_call(xn, wt, b1, b1col, w_up, w_down, tm):
    t = xn.shape[0]
    n_m = t // tm
    n_cast = 64
    assert n_cast <= 8 * n_m
    up_in, up_out, up_shape = _cast_specs(w_up, n_cast, lambda s, m: s * n_m + m)
    dn_in, dn_out, dn_shape = _cast_specs(w_down, n_cast, lambda s, m: s * n_m + m)
    w_off = lambda s: jnp.where(s < S_O, s * TN, jnp.where(s == S_O, OFF_O, OFF_GA + (s - 4) * TN))
    b_blk = lambda s: jnp.where(s <= S_O, s, s + 3)
    z_col = lambda s: jnp.where(s <= S_K, Z_Q, jnp.where(s == S_V, Z_V, jnp.where(s == S_O, Z_SO, s - 2)))
    z_row = lambda s, m: jnp.where(s == S_K, n_m - 1, m)
    kt_blk = lambda s, m: jnp.where(s == S_K, m, jnp.where(s < S_K, 0, n_m - 1))
    return pl.pallas_call(
        _inproj_kernel,
        grid=(8, n_m),
        in_specs=[
            pl.BlockSpec((tm, D_MODEL), lambda s, m: (m, 0)),
            pl.BlockSpec((pl.Element(TN), pl.Element(D_MODEL)),
                         lambda s, m: (pl.multiple_of(w_off(s), SUBLANES), 0)),
            pl.BlockSpec((1, TN), lambda s, m: (0, b_blk(s))),
            pl.BlockSpec((TN, 1), lambda s, m: (b_blk(s), 0)),
            up_in, dn_in,
        ],
        out_specs=[
            pl.BlockSpec((tm, TN), lambda s, m: (z_row(s, m), z_col(s))),
            pl.BlockSpec((TN, tm), lambda s, m: (0, kt_blk(s, m))),
            up_out, dn_out,
        ],
        out_shape=[
            jax.ShapeDtypeStruct((t, Z_COLS), BF16),
            jax.ShapeDtypeStruct((TN, t), BF16),
            up_shape, dn_shape,
        ],
        scratch_shapes=[pltpu.VMEM((TN, D_MODEL), BF16)],
        compiler_params=_params(("arbitrary", "arbitrary")),
        name="inproj",
    )(xn, wt, b1, b1col, w_up, w_down)


def _conv_kernel(n_prompt_tiles, tiles_per_seq, xn_ref, wu_ref, wc_ref, wb_ref, bu_ref, bc_ref, bb_ref,
                 wconv_ref, mtail_ref, s0_ref, s1_ref, pa_ref, pb_ref, wo_ref,
                 yb_ref, cu6_ref, cu7_ref, pa16_ref, pb16_ref, wo16_ref,
                 w3_ref, cu_ref, h1_ref, h2_ref, carry_ref):
    m = pl.program_id(1)
    tm, cq = yb_ref.shape
    n_seq = tm // SUBLANES
    slabs = [(k, slice(k * LANES, (k + 1) * LANES)) for k in range(cq // LANES)]
    seq_row = lambda r: pl.ds(r, n_seq, stride=SUBLANES)
    is_prompt = m < n_prompt_tiles

    @pl.when(m == 0)
    def _():
        w3_ref[0] = wu_ref[...].astype(BF16)
        w3_ref[1] = wc_ref[...].astype(BF16)
        w3_ref[2] = wb_ref[...].astype(BF16)
        h1_ref[...] = jnp.zeros_like(h1_ref)
        h2_ref[...] = jnp.zeros_like(h2_ref)

    @pl.when(is_prompt)
    def _():
        first = (m % tiles_per_seq) == 0
        p6 = jnp.where(first, mtail_ref[6:7, :], carry_ref[6:7, :])
        p7 = jnp.where(first, mtail_ref[7:8, :], carry_ref[7:8, :])
        for k, ks in slabs:
            h2_ref[k, 0:1, :] = p6[:, ks]
            h2_ref[k, 1:2, :] = p7[:, ks]
            h1_ref[k, 0:1, :] = p7[:, ks]

    @pl.when(jnp.logical_not(is_prompt))
    def _():
        for k, ks in slabs:
            h2_ref[k, seq_row(0), :] = s0_ref[:, ks]
            h2_ref[k, seq_row(1), :] = s1_ref[:, ks]
            h1_ref[k, seq_row(0), :] = s1_ref[:, ks]

    _cast_blocks(((pa_ref, pa16_ref), (pb_ref, pb16_ref), (wo_ref, wo16_ref)))
    xn = xn_ref[...]
    zu = _dot_nt(xn, w3_ref[0]) + bu_ref[...]
    zc = _dot_nt(xn, w3_ref[1]) + bc_ref[...]
    zb = _dot_nt(xn, w3_ref[2]) + bb_ref[...]
    cu = zc * zu
    pos = lax.broadcasted_iota(jnp.int32, (tm, cq), 0) & jnp.where(is_prompt, tm - 1, SUBLANES - 1)
    h1 = jnp.concatenate([h1_ref[k] for k, _ in slabs], axis=1)
    h2 = jnp.concatenate([h2_ref[k] for k, _ in slabs], axis=1)
    x1 = jnp.where(pos >= 1, pltpu.roll(cu, 1, 0), h1)
    x2 = jnp.where(pos >= 2, pltpu.roll(cu, 2, 0), h2)
    w0, w1, w2 = wconv_ref[0:1, :], wconv_ref[1:2, :], wconv_ref[2:3, :]
    yb_ref[...] = (zb * ((w0 * x2 + w1 * x1) + w2 * cu)).astype(BF16)

    carry_ref[...] = cu[tm - SUBLANES:tm, :]
    for k, ks in slabs:
        cu_ref[k] = cu[:, ks]
        cu6_ref[:, ks] = cu_ref[k, seq_row(6), :]
        cu7_ref[:, ks] = cu_ref[k, seq_row(7), :]


def _conv_call(xn, wt, b1, wconv, mtail, s0, s1, p_a, p_b, w_o, n_prompt_tokens, seq_len, tm):
    t = xn.shape[0]
    assert tm & (tm - 1) == 0 and seq_len % tm == 0
    npt = n_prompt_tokens // tm
    n_m = t // tm
    n_seq = tm // SUBLANES
    n_slab = CQ // LANES
    n_cast = 32
    assert n_cast <= (C_WIDTH // CQ) * n_m
    casts = [_cast_specs(w, n_cast, lambda c, m: c * n_m + m) for w in (p_a, p_b, w_o)]
    w_spec = lambda off: pl.BlockSpec((pl.Element(CQ), pl.Element(D_MODEL)),
                                      lambda c, m: (pl.multiple_of(off + c * CQ, SUBLANES), 0))
    b_spec = lambda off: pl.BlockSpec((1, CQ), lambda c, m: (0, (off - N_GATES) // CQ + c))
    s_spec = pl.BlockSpec((n_seq, CQ), lambda c, m: (jnp.maximum(m - npt, 0), c))
    return pl.pallas_call(
        functools.partial(_conv_kernel, npt, seq_len // tm),
        grid=(C_WIDTH // CQ, t // tm),
        in_specs=[
            pl.BlockSpec((tm, D_MODEL), lambda c, m: (m, 0)),
            w_spec(OFF_U), w_spec(OFF_C), w_spec(OFF_B),
            b_spec(OFF_U), b_spec(OFF_C), b_spec(OFF_B),
            pl.BlockSpec((CONV_W, CQ), lambda c, m: (0, c)),
            pl.BlockSpec((SUBLANES, CQ), lambda c, m: (1, c)),
            s_spec, s_spec,
        ] + [cs[0] for cs in casts],
        out_specs=[
            pl.BlockSpec((tm, CQ), lambda c, m: (m, c)),
            pl.BlockSpec((n_seq, CQ), lambda c, m: (m, c)),
            pl.BlockSpec((n_seq, CQ), lambda c, m: (m, c)),
        ] + [cs[1] for cs in casts],
        out_shape=[
            jax.ShapeDtypeStruct((t, C_WIDTH), BF16),
            jax.ShapeDtypeStruct((t // SUBLANES, C_WIDTH), F32),
            jax.ShapeDtypeStruct((t // SUBLANES, C_WIDTH), F32),
        ] + [cs[2] for cs in casts],
        scratch_shapes=[
            pltpu.VMEM((3, CQ, D_MODEL), BF16),
            pltpu.VMEM((n_slab, tm, LANES), F32),
            pltpu.VMEM((n_slab, tm, LANES), F32),
            pltpu.VMEM((n_slab, tm, LANES), F32),
            pltpu.VMEM((SUBLANES, CQ), F32),
        ],
        compiler_params=_params(("arbitrary", "arbitrary")),
        name="inproj_conv",
    )(xn, wt, wt, wt, b1, b1, b1, wconv, mtail, s0, s1, p_a, p_b, w_o)


def _prompt_head(q, kt, v, gr, gc, ca, m_st):
    l = q.shape[0]
    assert l == LANES
    ig_row, b_row, r_row = gr[G_IG:G_IG + 1, :], gr[G_B:G_B + 1, :], gr[G_R:G_R + 1, :]
    b_col = gc[0]
    v_aug = jnp.concatenate([v, jnp.ones((l, LANES), BF16)], axis=1)
    t_idx = lax.broadcasted_iota(jnp.int32, (l, l), 0)
    s_idx = lax.broadcasted_iota(jnp.int32, (l, l), 1)
    logw = jnp.where(s_idx <= t_idx, b_col - b_row + ig_row, NEG)
    inter = b_col + m_st
    m_t = jnp.maximum(inter, gc[1])
    w_inter = jnp.exp(inter - m_t)
    s = _dot(q, kt) * jnp.exp(logw - m_t)
    nd = (jnp.concatenate([w_inter] * (AUG // LANES), axis=1) * _dot(q, ca.astype(BF16))
          + _dot(s.astype(BF16), v_aug))
    rcp = 1.0 / jnp.maximum(jnp.abs(nd[:, M_DV:AUG]), jnp.exp(-m_t))
    h = nd[:, 0:M_DV] * jnp.concatenate([rcp] * (M_DV // LANES), axis=1)
    m_new = m_t[l - 1:l, 0:1]
    decay = jnp.exp(b_col[l - 1:l, 0:1] + m_st - m_new)
    kwt = (kt.astype(F32) * jnp.exp(r_row + ig_row - m_new)).astype(BF16)
    return h, decay * ca + _dot(kwt, v_aug), m_new


def _mlstm_prompt_kernel(batch, qm_ref, ktm_ref, vm_ref, gcm_ref, grm_ref, *refs):
    ins, (h_ref, ca_ref, m_ref) = refs[:5 * batch], refs[5 * batch:]
    heads = [(hd, slice(hd * M_DK, (hd + 1) * M_DK)) for hd in range(M_HEADS)]

    @pl.when(pl.program_id(0) == 0)
    def _():
        for hd, sl in heads:
            _, ca_new, m_new = _prompt_head(qm_ref[:, sl], ktm_ref[sl, :], vm_ref[:, sl], grm_ref[hd],
                                            gcm_ref[hd], jnp.zeros((M_DK, AUG), F32), jnp.zeros((1, 1), F32))
            for b in range(batch):
                ca_ref[b, hd] = ca_new
                m_ref[b, hd:hd + 1, :] = jnp.broadcast_to(m_new, (1, LANES))

    for b in range(batch):
        q_ref, kt_ref, v_ref, gc_ref, gr_ref = ins[5 * b:5 * b + 5]
        for hd, sl in heads:
            h, ca_new, m_new = _prompt_head(q_ref[:, sl], kt_ref[sl, :], v_ref[:, sl], gr_ref[hd], gc_ref[hd],
                                            ca_ref[b, hd], m_ref[b, hd:hd + 1, 0:1])
            ca_ref[b, hd] = ca_new
            m_ref[b, hd:hd + 1, :] = jnp.broadcast_to(m_new, (1, LANES))
            h_ref[b, :, sl] = h.astype(h_ref.dtype)


def _mlstm_prompt_call(zm, ktm, gcm, grm, z, kt, gch, grh, batch, seq):
    nc = seq // CHUNK
    per_prompt_specs, per_prompt_args = [], []
    for b in range(batch):
        row = functools.partial(lambda b, c: b * nc + c, b)
        per_prompt_specs += [
            pl.BlockSpec((CHUNK, M_WIDTH), lambda c, row=row: (row(c), Z_Q)),
            pl.BlockSpec((M_WIDTH, CHUNK), lambda c, row=row: (0, row(c))),
            pl.BlockSpec((CHUNK, M_WIDTH), lambda c, row=row: (row(c), Z_V)),
            pl.BlockSpec((M_HEADS, 2, CHUNK, LANES), lambda c, row=row: (0, 0, row(c), 0)),
            pl.BlockSpec((M_HEADS, SUBLANES, CHUNK), lambda c, row=row: (0, 0, row(c))),
        ]
        per_prompt_args += [z, kt, z, gch, grh]
    return pl.pallas_call(
        functools.partial(_mlstm_prompt_kernel, batch),
        grid=(nc,),
        in_specs=[
            pl.BlockSpec((CHUNK, M_WIDTH), lambda c: (0, 0)),
            pl.BlockSpec((M_WIDTH, CHUNK), lambda c: (0, 0)),
            pl.BlockSpec((CHUNK, M_WIDTH), lambda c: (0, 1)),
            pl.BlockSpec((M_HEADS, 2, CHUNK, LANES), lambda c: (0, 0, 0, 0)),
            pl.BlockSpec((M_HEADS, SUBLANES, CHUNK), lambda c: (0, 0, 0)),
        ] + per_prompt_specs,
        out_specs=[
            pl.BlockSpec((batch, CHUNK, M_WIDTH), lambda c: (0, c, 0)),
            pl.BlockSpec((batch, M_HEADS, M_DK, AUG), lambda c: (0, 0, 0, 0)),
            pl.BlockSpec((batch, M_HEADS, LANES), lambda c: (0, 0, 0)),
        ],
        out_shape=[
            jax.ShapeDtypeStruct((batch, seq, M_WIDTH), BF16),
            jax.ShapeDtypeStruct((batch, M_HEADS, M_DK, AUG), F32),
            jax.ShapeDtypeStruct((batch, M_HEADS, LANES), F32),
        ],
        compiler_params=_params(("arbitrary",)),
        name="mlstm_prompt",
    )(zm, ktm, zm, gcm, grm, *per_prompt_args)


def _group_max(x, size):
    n = x.shape[-1]
    lane = lax.broadcasted_iota(jnp.int32, x.shape, x.ndim - 1)
    k = 1
    while k < size:
        partner = jnp.where((lane & k) == 0, pltpu.roll(x, n - k, x.ndim - 1), pltpu.roll(x, k, x.ndim - 1))
        x = jnp.maximum(x, partner)
        k *= 2
    return x


def _mlstm_sample_kernel(dec_seq, q_ref, kt_ref, v_ref, gc_ref, gr_ref, mcol_ref, mrow_ref, cin_ref, nin_ref,
                         h_ref, c_ref, n_ref, m_ref):
    l = q_ref.shape[0]
    n_seq = l // dec_seq
    shift = dec_seq.bit_length() - 1
    q, kt, v = q_ref[...], kt_ref[...], v_ref[...]
    assert l == LANES
    gr, gc = gr_ref[0], gc_ref[0]
    ig_row, b_row, r_row = gr[G_IG:G_IG + 1, :], gr[G_B:G_B + 1, :], gr[G_R:G_R + 1, :]
    b_col = gc[0]
    m_col, m_row = mcol_ref[0], mrow_ref[0]

    t_idx = lax.broadcasted_iota(jnp.int32, (l, l), 0)
    s_idx = lax.broadcasted_iota(jnp.int32, (l, l), 1)
    t_seq = t_idx >> shift
    logw = jnp.where((t_seq == (s_idx >> shift)) & (s_idx <= t_idx), b_col - b_row + ig_row, NEG)
    inter = b_col + m_col
    m_t = jnp.maximum(inter, gc[1])
    w_inter = jnp.exp(inter - m_t)
    s = _dot(q, kt) * jnp.exp(logw - m_t)

    e_row = r_row + ig_row
    b_last = b_row + r_row
    m_new = jnp.maximum(b_last + m_row, _group_max(e_row, dec_seq))
    decay = jnp.exp(b_last + m_row - m_new)
    kwt = (kt.astype(F32) * jnp.exp(e_row - m_new)).astype(BF16)

    seq_lane = s_idx == t_seq
    v_aug = jnp.concatenate([v, jnp.where(seq_lane, 1.0, 0.0).astype(BF16)], axis=1)
    pa = _dot(s.astype(BF16), v_aug)

    n_rows = jnp.concatenate([nin_ref[...], jnp.zeros((LANES - n_seq, M_DK), F32)], axis=0)
    n_t = jnp.concatenate([n_rows[:, 0:LANES].T, n_rows[:, LANES:2 * LANES].T], axis=0)
    lane_k = lax.broadcasted_iota(jnp.int32, (M_DK, LANES), 1)
    qf = q.astype(F32)
    qca_rows = []
    n_t_new = jnp.zeros((M_DK, LANES), F32)
    for i in range(n_seq):
        ca = jnp.concatenate([cin_ref[i, 0], jnp.where(lane_k == i, n_t, 0.0)], axis=1)
        rows = slice(i * dec_seq, (i + 1) * dec_seq)
        qca_rows.append(_dot(qf[rows, :].astype(BF16), ca.astype(BF16)))
        kw_i = jnp.where((lane_k >> shift) == i, kwt, jnp.zeros_like(kwt))
        ca_new = decay[:, i * dec_seq:i * dec_seq + 1] * ca + _dot(kw_i, v_aug)
        c_ref[i, 0] = ca_new[:, 0:M_DV]
        n_t_new = n_t_new + ca_new[:, M_DV:AUG]
    nd = jnp.concatenate([w_inter] * (AUG // LANES), axis=1) * jnp.concatenate(qca_rows, axis=0) + pa
    den = jnp.sum(jnp.where(seq_lane, nd[:, M_DV:AUG], 0.0), axis=-1, keepdims=True)
    rcp = 1.0 / jnp.maximum(jnp.abs(den), jnp.exp(-m_t[:, 0:1]))
    h_ref[...] = (nd[:, 0:M_DV] * rcp).astype(h_ref.dtype)
    n_ref[...] = jnp.concatenate([n_t_new[0:LANES, :].T, n_t_new[LANES:2 * LANES, :].T], axis=1)[0:n_seq, :]
    m_ref[0] = jnp.broadcast_to(m_new, (SUBLANES, l))


def _mlstm_sample_call(z, kt, gch, grh, m_col, m_row, c_in, n_in, row0, dec_seq):
    nb = c_in.shape[0]
    l = SEQ_PER_STEP * dec_seq
    blk0 = row0 // l
    return pl.pallas_call(
        functools.partial(_mlstm_sample_kernel, dec_seq),
        grid=(nb // SEQ_PER_STEP, M_HEADS),
        in_specs=[
            pl.BlockSpec((l, M_DK), lambda i, h: (blk0 + i, Z_Q * M_HEADS + h)),
            pl.BlockSpec((M_DK, l), lambda i, h: (h, blk0 + i)),
            pl.BlockSpec((l, M_DV), lambda i, h: (blk0 + i, Z_V * M_HEADS + h)),
            pl.BlockSpec((1, 2, l, LANES), lambda i, h: (h, 0, blk0 + i, 0)),
            pl.BlockSpec((1, SUBLANES, l), lambda i, h: (h, 0, blk0 + i)),
            pl.BlockSpec((1, l, LANES), lambda i, h: (h, i, 0)),
            pl.BlockSpec((1, 1, l), lambda i, h: (h, 0, i)),
            pl.BlockSpec((SEQ_PER_STEP, 1, M_DK, M_DV), lambda i, h: (i, h, 0, 0)),
            pl.BlockSpec((SEQ_PER_STEP, M_DK), lambda i, h: (i, h)),
        ],
        out_specs=[
            pl.BlockSpec((l, M_DV), lambda i, h: (i, h)),
            pl.BlockSpec((SEQ_PER_STEP, 1, M_DK, M_DV), lambda i, h: (i, h, 0, 0)),
            pl.BlockSpec((SEQ_PER_STEP, M_DK), lambda i, h: (i, h)),
            pl.BlockSpec((1, SUBLANES, l), lambda i, h: (h, 0, i)),
        ],
        out_shape=[
            jax.ShapeDtypeStruct((nb * dec_seq, M_WIDTH), BF16),
            jax.ShapeDtypeStruct(c_in.shape, F32),
            jax.ShapeDtypeStruct(n_in.shape, F32),
            jax.ShapeDtypeStruct((M_HEADS, SUBLANES, nb * dec_seq), F32),
        ],
        compiler_params=_params(("arbitrary", "arbitrary")),
        name="mlstm_sample",
    )(z, kt, z, gch, grh, m_col, m_row, c_in, n_in)


def _mix_kernel(n_prompt_tiles, hp_ref, hs_ref, so_ref, yb_ref, ga_ref, gb_ref, xp_ref, xs_ref,
                pa_ref, pb_ref, wo_ref, o_ref):
    i = pl.program_id(0)

    def body(h_ref, x_ref):
        h_a = (so_ref[...].astype(F32) * h_ref[...].astype(F32)).astype(BF16)
        a = _dot(h_a, pa_ref[...])
        b = _dot(yb_ref[...], pb_ref[...])
        merged = ga_ref[...].astype(F32) * a + gb_ref[...].astype(F32) * b
        o_ref[...] = x_ref[...] + _dot(merged.astype(BF16), wo_ref[...])

    @pl.when(i < n_prompt_tiles)
    def _():
        body(hp_ref, xp_ref)

    @pl.when(i >= n_prompt_tiles)
    def _():
        body(hs_ref, xs_ref)


def _mix_call(hp, hs, z, yb, xp, xs, pa, pb, wo, tm):
    tp, ts = xp.shape[0], xs.shape[0]
    npt, nst = tp // tm, ts // tm
    pi = lambda i: jnp.minimum(i, npt - 1)
    si = lambda i: jnp.maximum(i - npt, 0)
    const = lambda i: (0, 0)
    return pl.pallas_call(
        functools.partial(_mix_kernel, npt),
        grid=(npt + nst,),
        in_specs=[
            pl.BlockSpec((tm, M_WIDTH), lambda i: (pi(i), 0)),
            pl.BlockSpec((tm, M_WIDTH), lambda i: (si(i), 0)),
            pl.BlockSpec((tm, M_WIDTH), lambda i: (i, Z_SO)),
            pl.BlockSpec((tm, C_WIDTH), lambda i: (i, 0)),
            pl.BlockSpec((tm, D_MODEL), lambda i: (i, Z_GA // 2)),
            pl.BlockSpec((tm, D_MODEL), lambda i: (i, Z_GB // 2)),
            pl.BlockSpec((tm, D_MODEL), lambda i: (pi(i), 0)),
            pl.BlockSpec((tm, D_MODEL), lambda i: (si(i), 0)),
            pl.BlockSpec((M_WIDTH, D_MODEL), const, pipeline_mode=pl.Buffered(1)),
            pl.BlockSpec((C_WIDTH, D_MODEL), const, pipeline_mode=pl.Buffered(1)),
            pl.BlockSpec((D_MODEL, D_MODEL), const, pipeline_mode=pl.Buffered(1)),
        ],
        out_specs=pl.BlockSpec((tm, D_MODEL), lambda i: (i, 0)),
        out_shape=jax.ShapeDtypeStruct((tp + ts, D_MODEL), F32),
        compiler_params=_params(("arbitrary",)),
        name="mix",
    )(hp, hs, z, yb, z, z, xp, xs, pa, pb, wo)


def _ffn_kernel(n_prompt_tiles, x1_ref, gffn_ref, wup_ref, wdn_ref, gfin_ref, yp_ref, ys_ref,
                xn_ref, acc_ref):
    i = pl.program_id(0)
    f = pl.program_id(1)

    @pl.when(f == 0)
    def _():
        x1 = x1_ref[...]
        xn_ref[...] = _rmsnorm(x1, gffn_ref[...]).astype(BF16)
        acc_ref[...] = x1

    u = jnp.maximum(_dot(xn_ref[...], wup_ref[...]), 0.0)
    acc_ref[...] += _dot((u * u).astype(BF16), wdn_ref[...])

    @pl.when(f == pl.num_programs(1) - 1)
    def _():
        y = _rmsnorm(acc_ref[...], gfin_ref[...])

        @pl.when(i < n_prompt_tiles)
        def _():
            yp_ref[...] = y

        @pl.when(i >= n_prompt_tiles)
        def _():
            ys_ref[...] = y


def _ffn_call(x1, gffn, wup, wdn, gfin, n_prompt_tokens, tm, tf):
    t = x1.shape[0]
    npt = n_prompt_tokens // tm
    n_tiles = t // tm
    return pl.pallas_call(
        functools.partial(_ffn_kernel, npt),
        grid=(n_tiles, D_FF // tf),
        in_specs=[
            pl.BlockSpec((tm, D_MODEL), lambda i, f: (i, 0)),
            pl.BlockSpec((1, D_MODEL), lambda i, f: (0, 0)),
            pl.BlockSpec((D_MODEL, tf), lambda i, f: (0, f)),
            pl.BlockSpec((tf, D_MODEL), lambda i, f: (f, 0)),
            pl.BlockSpec((1, D_MODEL), lambda i, f: (0, 0)),
        ],
        out_specs=[
            pl.BlockSpec((tm, D_MODEL), lambda i, f: (jnp.minimum(i, npt - 1), 0)),
            pl.BlockSpec((tm, D_MODEL), lambda i, f: (jnp.maximum(i - npt, 0), 0)),
        ],
        out_shape=[
            jax.ShapeDtypeStruct((n_prompt_tokens, D_MODEL), F32),
            jax.ShapeDtypeStruct((t - n_prompt_tokens, D_MODEL), F32),
        ],
        scratch_shapes=[pltpu.VMEM((tm, D_MODEL), BF16), pltpu.VMEM((tm, D_MODEL), F32)],
        compiler_params=_params(("arbitrary", "arbitrary")),
        name="ffn",
    )(x1, gffn, wup, wdn, gfin)


def kernel(x_prompt, x_sample, state_mlstm_C, state_mlstm_n, state_mlstm_m, state_conv, meta_tokens,
           g_mix, w_in, b_in, w_conv, p_a, p_b, w_o, g_ffn, w_up, w_down, g_final):
    assert w_in.shape[0] == 1, "single-layer trunk"
    batch, seq, _ = x_prompt.shape
    dec_batch, dec_seq, _ = x_sample.shape
    assert dec_seq == SUBLANES and seq % 1024 == 0 and dec_batch % SEQ_PER_STEP == 0
    n_p, n_s = batch * seq, dec_batch * dec_seq

    wt = w_in[0].T
    bias = b_in[0]
    b1 = jnp.concatenate([bias[:OFF_GATES], bias[OFF_O:]])
    bg = bias[OFF_GATES:OFF_O][:, None]
    gmix = g_mix[0][None, :]

    xp = x_prompt.reshape(n_p, D_MODEL)
    xs = x_sample.reshape(n_s, D_MODEL)
    x_meta = jnp.pad(meta_tokens.astype(F32), ((0, CHUNK - N_META), (0, 0)))

    xn, gch, grh = _rms_call(xp, xs, gmix, wt, bg, dec_seq, 512)
    zm, ktm, gcm, grm, cu_m = _meta_call(x_meta, gmix, wt, b1[None, :], b1[:, None], bg)
    z, kt, w_up16, w_down16 = _inproj_call(xn, wt, b1[None, :], b1[:, None], w_up[0], w_down[0], 1024)
    yb, cu6, cu7, p_a16, p_b16, w_o16 = _conv_call(
        xn, wt, b1[None, :], w_conv[0], cu_m, state_conv[0, :, 0, :], state_conv[0, :, 1, :],
        p_a[0], p_b[0], w_o[0], n_p, seq, 1024)

    hp, ca_p, m_p = _mlstm_prompt_call(zm, ktm, gcm, grm, z, kt, gch, grh, batch, seq)
    m_tok = jnp.repeat(state_mlstm_m[0], dec_seq, axis=0).T
    hs, c_s, n_sm, m_s = _mlstm_sample_call(
        z, kt, gch, grh, jnp.broadcast_to(m_tok[:, :, None], m_tok.shape + (LANES,)), m_tok[:, None, :],
        state_mlstm_C[0],
        state_mlstm_n[0].reshape(dec_batch, M_HEADS * M_DK), n_p, dec_seq)

    x1 = _mix_call(hp.reshape(n_p, M_WIDTH), hs, z, yb, xp, xs, p_a16, p_b16, w_o16, 256)
    y_p, y_s = _ffn_call(x1, g_ffn[0][None, :], w_up16, w_down16, g_final[None, :], n_p, 512, 1024)

    last_p = (jnp.arange(batch) + 1) * (seq // SUBLANES) - 1
    cv_p = jnp.stack([cu6[last_p], cu7[last_p]], axis=1)
    first_s = n_p // SUBLANES
    cv_s = jnp.stack([cu6[first_s:], cu7[first_s:]], axis=1)
    m_s = m_s[:, 0, dec_seq - 1::dec_seq].T
    return (y_p.reshape(batch, seq, D_MODEL), y_s.reshape(dec_batch, dec_seq, D_MODEL),
            ca_p[None, :, :, :, 0:M_DV], ca_p[None, :, :, :, M_DV], m_p[None, :, :, 0], cv_p[None],
            c_s[None], n_sm.reshape(dec_batch, M_HEADS, M_DK)[None], m_s[None], cv_s[None])
```

```python
import functools

import jax
import jax.numpy as jnp
from jax import lax
from jax.experimental import pallas as pl
from jax.experimental.pallas import tpu as pltpu

F32 = jnp.float32
BF16 = jnp.bfloat16

D_MODEL = 2048
N_META = 16
CHUNK = 128
M_HEADS = 4
M_DK = 256
M_DV = 256
M_WIDTH = M_HEADS * M_DV
C_WIDTH = 1024
CONV_W = 3
D_FF = 4 * D_MODEL
EPS = 1e-6
N_GATES = 2 * M_HEADS
OFF_Q = 0
OFF_K = OFF_Q + M_HEADS * M_DK
OFF_V = OFF_K + M_HEADS * M_DK
OFF_GATES = OFF_V + M_WIDTH
OFF_O = OFF_GATES + N_GATES
OFF_U = OFF_O + M_WIDTH
OFF_C = OFF_U + C_WIDTH
OFF_B = OFF_C + C_WIDTH
OFF_GA = OFF_B + C_WIDTH
OFF_GB = OFF_GA + D_MODEL

LANES = 128
SUBLANES = 8
VMEM_LIMIT_BYTES = 60000 * 1024

TN = 1024
Z_Q, Z_V, Z_GA, Z_GB, Z_SO = 0, 1, 2, 4, 6
Z_COLS = 7 * TN
CQ = 256
AUG = M_DV + LANES
SEQ_PER_STEP = CHUNK // SUBLANES
G_IG, G_B, G_R = 0, 1, 2

NEG = -1e30
NT_DIMS = (((1,), (1,)), ((), ()))


def _params(semantics):
    return pltpu.CompilerParams(dimension_semantics=semantics, vmem_limit_bytes=VMEM_LIMIT_BYTES)


def _rmsnorm(x, g):
    y = x * lax.rsqrt(jnp.mean(x * x, axis=-1, keepdims=True) + EPS)
    return y * g


def _log_sigmoid(x):
    return jnp.minimum(x, 0.0) - jnp.log1p(jnp.exp(-jnp.abs(x)))


def _dot(a, b):
    return jnp.dot(a, b, preferred_element_type=F32)


def _dot_nt(a, b):
    return lax.dot_general(a, b, NT_DIMS, preferred_element_type=F32)


def _gate_prep(xn, wg_ref, bg_ref, blk, n_valid, gch_ref, grh_ref):
    tm = xn.shape[0]
    wg = wg_ref[...].astype(BF16)
    wg = jnp.concatenate([wg, jnp.zeros((LANES - N_GATES, wg.shape[1]), BF16)], axis=0)
    g = _dot_nt(wg, xn)[0:SUBLANES, :] + bg_ref[...]
    row = lax.broadcasted_iota(jnp.int32, (SUBLANES, tm), 0)
    lane = lax.broadcasted_iota(jnp.int32, (SUBLANES, tm), 1)
    a = jnp.where(row < M_HEADS, g, _log_sigmoid(g))
    if n_valid < tm:
        a = jnp.where(lane < n_valid, a, jnp.where(row < M_HEADS, NEG, 0.0))
    pos = lane & (blk - 1)

    def scan(x, op, fill, reverse=False):
        shift = 1
        while shift < blk:
            if reverse:
                x = op(x, jnp.where(pos < blk - shift, pltpu.roll(x, tm - shift, 1), fill))
            else:
                x = op(x, jnp.where(pos >= shift, pltpu.roll(x, shift, 1), fill))
            shift *= 2
        return x

    pre = scan(a, jnp.add, 0.0)
    suf = scan(a, jnp.add, 0.0, reverse=True) - a
    b_up = pltpu.roll(pre, M_HEADS, 0)
    m_in = b_up + scan(a - b_up, jnp.maximum, -3e38)
    for h in range(M_HEADS):
        grh_ref[h] = jnp.where(
            row == G_IG, pltpu.roll(a, (G_IG - h) % SUBLANES, 0),
            jnp.where(row == G_B, pltpu.roll(pre, (G_B - M_HEADS - h) % SUBLANES, 0),
                      jnp.where(row == G_R, pltpu.roll(suf, (G_R - M_HEADS - h) % SUBLANES, 0), 0.0)))
        for c in range(tm // LANES):
            cs = slice(c * LANES, (c + 1) * LANES)
            gch_ref[h, 0, cs, :] = jnp.broadcast_to(b_up[h:h + 1, cs], (LANES, LANES)).T
            gch_ref[h, 1, cs, :] = jnp.broadcast_to(m_in[h:h + 1, cs], (LANES, LANES)).T


def _rms_kernel(n_prompt_tiles, dec_seq, xp_ref, xs_ref, xm_ref, g_ref, wg_ref, bg_ref,
                o_ref, gch_ref, grh_ref, om_ref, gchm_ref, grhm_ref):
    i = pl.program_id(0)

    def run(x_ref, blk):
        xn = _rmsnorm(x_ref[...], g_ref[...]).astype(BF16)
        o_ref[...] = xn
        _gate_prep(xn, wg_ref, bg_ref, blk, xn.shape[0], gch_ref, grh_ref)

    @pl.when(i == 0)
    def _():
        n_meta = xm_ref.shape[0]
        xm = jnp.concatenate([xm_ref[...], jnp.zeros((CHUNK - n_meta, D_MODEL), F32)], axis=0)
        xn = _rmsnorm(xm, g_ref[...]).astype(BF16)
        om_ref[...] = xn
        _gate_prep(xn, wg_ref, bg_ref, CHUNK, n_meta, gchm_ref, grhm_ref)

    @pl.when(i < n_prompt_tiles)
    def _():
        run(xp_ref, CHUNK)

    @pl.when(i >= n_prompt_tiles)
    def _():
        run(xs_ref, dec_seq)


def _rms_call(xp, xs, x_meta, g, wt, bg, dec_seq, tm):
    tp, ts = xp.shape[0], xs.shape[0]
    npt, nst = tp // tm, ts // tm
    t = tp + ts
    return pl.pallas_call(
        functools.partial(_rms_kernel, npt, dec_seq),
        grid=(npt + nst,),
        in_specs=[
            pl.BlockSpec((tm, D_MODEL), lambda i: (jnp.minimum(i, npt - 1), 0)),
            pl.BlockSpec((tm, D_MODEL), lambda i: (jnp.maximum(i - npt, 0), 0)),
            pl.BlockSpec(x_meta.shape, lambda i: (0, 0)),
            pl.BlockSpec((1, D_MODEL), lambda i: (0, 0)),
            pl.BlockSpec((N_GATES, D_MODEL), lambda i: (OFF_GATES // N_GATES, 0)),
            pl.BlockSpec((N_GATES, 1), lambda i: (0, 0)),
        ],
        out_specs=[
            pl.BlockSpec((tm, D_MODEL), lambda i: (i, 0)),
            pl.BlockSpec((M_HEADS, 2, tm, LANES), lambda i: (0, 0, i, 0)),
            pl.BlockSpec((M_HEADS, SUBLANES, tm), lambda i: (0, 0, i)),
            pl.BlockSpec((CHUNK, D_MODEL), lambda i: (0, 0)),
            pl.BlockSpec((M_HEADS, 2, CHUNK, LANES), lambda i: (0, 0, 0, 0)),
            pl.BlockSpec((M_HEADS, SUBLANES, CHUNK), lambda i: (0, 0, 0)),
        ],
        out_shape=[
            jax.ShapeDtypeStruct((t, D_MODEL), BF16),
            jax.ShapeDtypeStruct((M_HEADS, 2, t, LANES), F32),
            jax.ShapeDtypeStruct((M_HEADS, SUBLANES, t), F32),
            jax.ShapeDtypeStruct((CHUNK, D_MODEL), BF16),
            jax.ShapeDtypeStruct((M_HEADS, 2, CHUNK, LANES), F32),
            jax.ShapeDtypeStruct((M_HEADS, SUBLANES, CHUNK), F32),
        ],
        compiler_params=_params(("arbitrary",)),
        name="rms",
    )(xp, xs, x_meta, g, wt, bg)


S_Q, S_K, S_V, S_O = 0, 1, 2, 3


def _cast_blocks(pairs):
    for src_ref, dst_ref in pairs:
        dst_ref[...] = src_ref[...].astype(BF16)


def _cast_specs(w, n_blocks, step):
    rows, cols = w.shape
    blk = lambda *g: (jnp.minimum(step(*g), n_blocks - 1), 0)
    spec = pl.BlockSpec((rows // n_blocks, cols), blk)
    return spec, spec, jax.ShapeDtypeStruct(w.shape, BF16)


def _inproj_kernel(xn_ref, xnm_ref, wt_ref, b_ref, bcol_ref, wup_ref, wdn_ref,
                   z_ref, kt_ref, zm_ref, ktm_ref, wup16_ref, wdn16_ref, w_ref):
    s = pl.program_id(0)
    casts = ((wup_ref, wup16_ref), (wdn_ref, wdn16_ref))
    k_scale = M_DK ** -0.5

    @pl.when(pl.program_id(1) == 0)
    def _():
        w_ref[...] = wt_ref[...].astype(BF16)

        @pl.when((s == S_Q) | (s == S_V))
        def _():
            zm_ref[...] = (_dot_nt(xnm_ref[...], w_ref[...]) + b_ref[...]).astype(BF16)

        @pl.when(s == S_K)
        def _():
            ktm_ref[...] = ((_dot_nt(w_ref[...], xnm_ref[...]) + bcol_ref[...]) * k_scale).astype(BF16)

    def z():
        return _dot_nt(xn_ref[...], w_ref[...]) + b_ref[...]

    @pl.when((s == S_Q) | (s == S_V))
    def _():
        _cast_blocks(casts)
        z_ref[...] = z().astype(BF16)

    @pl.when(s == S_K)
    def _():
        _cast_blocks(casts)
        kt_ref[...] = ((_dot_nt(w_ref[...], xn_ref[...]) + bcol_ref[...]) * k_scale).astype(BF16)

    @pl.when(s >= S_O)
    def _():
        _cast_blocks(casts)
        z_ref[...] = jax.nn.sigmoid(z()).astype(BF16)


def _inproj_call(xn, xnm, wt, b1, b1col, w_up, w_down, tm):
    t = xn.shape[0]
    assert t % tm == 0
    n_m = t // tm
    rows_m = xnm.shape[0]
    n_cast = 64
    assert n_cast <= 8 * n_m
    up_in, up_out, up_shape = _cast_specs(w_up, n_cast, lambda s, m: s * n_m + m)
    dn_in, dn_out, dn_shape = _cast_specs(w_down, n_cast, lambda s, m: s * n_m + m)
    w_off = lambda s: jnp.where(s < S_O, s * TN, jnp.where(s == S_O, OFF_O, OFF_GA + (s - 4) * TN))
    b_blk = lambda s: jnp.where(s <= S_O, s, s + 3)
    z_col = lambda s: jnp.where(s <= S_K, Z_Q, jnp.where(s == S_V, Z_V, jnp.where(s == S_O, Z_SO, s - 2)))
    z_row = lambda s, m: jnp.where(s == S_K, n_m - 1, m)
    kt_blk = lambda s, m: jnp.where(s == S_K, m, jnp.where(s < S_K, 0, n_m - 1))
    return pl.pallas_call(
        _inproj_kernel,
        grid=(8, n_m),
        in_specs=[
            pl.BlockSpec((tm, D_MODEL), lambda s, m: (m, 0)),
            pl.BlockSpec((rows_m, D_MODEL), lambda s, m: (0, 0)),
            pl.BlockSpec((pl.Element(TN), pl.Element(D_MODEL)),
                         lambda s, m: (pl.multiple_of(w_off(s), SUBLANES), 0)),
            pl.BlockSpec((1, TN), lambda s, m: (0, b_blk(s))),
            pl.BlockSpec((TN, 1), lambda s, m: (b_blk(s), 0)),
            up_in, dn_in,
        ],
        out_specs=[
            pl.BlockSpec((tm, TN), lambda s, m: (z_row(s, m), z_col(s))),
            pl.BlockSpec((TN, tm), lambda s, m: (0, kt_blk(s, m))),
            pl.BlockSpec((rows_m, TN), lambda s, m: (0, (s >= S_V).astype(jnp.int32))),
            pl.BlockSpec((TN, rows_m), lambda s, m: (0, 0)),
            up_out, dn_out,
        ],
        out_shape=[
            jax.ShapeDtypeStruct((t, Z_COLS), BF16),
            jax.ShapeDtypeStruct((TN, t), BF16),
            jax.ShapeDtypeStruct((rows_m, 2 * TN), BF16),
            jax.ShapeDtypeStruct((TN, rows_m), BF16),
            up_shape, dn_shape,
        ],
        scratch_shapes=[pltpu.VMEM((TN, D_MODEL), BF16)],
        compiler_params=_params(("arbitrary", "arbitrary")),
        name="inproj",
    )(xn, xnm, wt, b1, b1col, w_up, w_down)


def _conv_kernel(n_prompt_tiles, tiles_per_seq, n_meta, xn_ref, xnm_ref, wu_ref, wc_ref, wb_ref,
                 bu_ref, bc_ref, bb_ref, wconv_ref, s0_ref, s1_ref, pa_ref, pb_ref, wo_ref,
                 yb_ref, cu6_ref, cu7_ref, pa16_ref, pb16_ref, wo16_ref,
                 w3_ref, cu_ref, h1_ref, h2_ref, carry_ref, mtail_ref):
    m = pl.program_id(1)
    tm, cq = yb_ref.shape
    n_seq = tm // SUBLANES
    slabs = [(k, slice(k * LANES, (k + 1) * LANES)) for k in range(cq // LANES)]
    seq_row = lambda r: pl.ds(r, n_seq, stride=SUBLANES)
    is_prompt = m < n_prompt_tiles

    @pl.when(m == 0)
    def _():
        w3_ref[0] = wu_ref[...].astype(BF16)
        w3_ref[1] = wc_ref[...].astype(BF16)
        w3_ref[2] = wb_ref[...].astype(BF16)
        h1_ref[...] = jnp.zeros_like(h1_ref)
        h2_ref[...] = jnp.zeros_like(h2_ref)
        xnm = xnm_ref[...]
        cu_m = (_dot_nt(xnm, w3_ref[1]) + bc_ref[...]) * (_dot_nt(xnm, w3_ref[0]) + bu_ref[...])
        mtail_ref[...] = cu_m[n_meta - SUBLANES:n_meta, :]

    @pl.when(is_prompt)
    def _():
        first = (m % tiles_per_seq) == 0
        p6 = jnp.where(first, mtail_ref[6:7, :], carry_ref[6:7, :])
        p7 = jnp.where(first, mtail_ref[7:8, :], carry_ref[7:8, :])
        for k, ks in slabs:
            h2_ref[k, 0:1, :] = p6[:, ks]
            h2_ref[k, 1:2, :] = p7[:, ks]
            h1_ref[k, 0:1, :] = p7[:, ks]

    @pl.when(jnp.logical_not(is_prompt))
    def _():
        for k, ks in slabs:
            h2_ref[k, seq_row(0), :] = s0_ref[:, ks]
            h2_ref[k, seq_row(1), :] = s1_ref[:, ks]
            h1_ref[k, seq_row(0), :] = s1_ref[:, ks]

    _cast_blocks(((pa_ref, pa16_ref), (pb_ref, pb16_ref), (wo_ref, wo16_ref)))
    xn = xn_ref[...]
    zu = _dot_nt(xn, w3_ref[0]) + bu_ref[...]
    zc = _dot_nt(xn, w3_ref[1]) + bc_ref[...]
    zb = _dot_nt(xn, w3_ref[2]) + bb_ref[...]
    cu = zc * zu
    pos = lax.broadcasted_iota(jnp.int32, (tm, cq), 0) & jnp.where(is_prompt, tm - 1, SUBLANES - 1)
    h1 = jnp.concatenate([h1_ref[k] for k, _ in slabs], axis=1)
    h2 = jnp.concatenate([h2_ref[k] for k, _ in slabs], axis=1)
    x1 = jnp.where(pos >= 1, pltpu.roll(cu, 1, 0), h1)
    x2 = jnp.where(pos >= 2, pltpu.roll(cu, 2, 0), h2)
    w0, w1, w2 = wconv_ref[0:1, :], wconv_ref[1:2, :], wconv_ref[2:3, :]
    yb_ref[...] = (zb * ((w0 * x2 + w1 * x1) + w2 * cu)).astype(BF16)

    carry_ref[...] = cu[tm - SUBLANES:tm, :]
    for k, ks in slabs:
        cu_ref[k] = cu[:, ks]
        cu6_ref[:, ks] = cu_ref[k, seq_row(6), :]
        cu7_ref[:, ks] = cu_ref[k, seq_row(7), :]


def _conv_call(xn, xnm, n_meta, wt, b1, wconv, sconv, p_a, p_b, w_o, n_prompt_tokens, seq_len, tm):
    t = xn.shape[0]
    assert tm & (tm - 1) == 0 and seq_len % tm == 0 and n_meta >= SUBLANES
    npt = n_prompt_tokens // tm
    n_m = t // tm
    n_seq = tm // SUBLANES
    n_slab = CQ // LANES
    n_cast = 32
    assert n_cast <= (C_WIDTH // CQ) * n_m
    casts = [_cast_specs(w, n_cast, lambda c, m: c * n_m + m) for w in (p_a, p_b, w_o)]
    w_spec = lambda off: pl.BlockSpec((pl.Element(CQ), pl.Element(D_MODEL)),
                                      lambda c, m: (pl.multiple_of(off + c * CQ, SUBLANES), 0))
    b_spec = lambda off: pl.BlockSpec((1, CQ), lambda c, m: (0, (off - N_GATES) // CQ + c))
    s_spec = lambda tok: pl.BlockSpec((n_seq, CQ),
                                      lambda c, m: (jnp.maximum(m - npt, 0), tok * (C_WIDTH // CQ) + c))
    return pl.pallas_call(
        functools.partial(_conv_kernel, npt, seq_len // tm, n_meta),
        grid=(C_WIDTH // CQ, t // tm),
        in_specs=[
            pl.BlockSpec((tm, D_MODEL), lambda c, m: (m, 0)),
            pl.BlockSpec(xnm.shape, lambda c, m: (0, 0)),
            w_spec(OFF_U), w_spec(OFF_C), w_spec(OFF_B),
            b_spec(OFF_U), b_spec(OFF_C), b_spec(OFF_B),
            pl.BlockSpec((CONV_W, CQ), lambda c, m: (0, c)),
            s_spec(0), s_spec(1),
        ] + [cs[0] for cs in casts],
        out_specs=[
            pl.BlockSpec((tm, CQ), lambda c, m: (m, c)),
            pl.BlockSpec((n_seq, CQ), lambda c, m: (m, c)),
            pl.BlockSpec((n_seq, CQ), lambda c, m: (m, c)),
        ] + [cs[1] for cs in casts],
        out_shape=[
            jax.ShapeDtypeStruct((t, C_WIDTH), BF16),
            jax.ShapeDtypeStruct((t // SUBLANES, C_WIDTH), F32),
            jax.ShapeDtypeStruct((t // SUBLANES, C_WIDTH), F32),
        ] + [cs[2] for cs in casts],
        scratch_shapes=[
            pltpu.VMEM((3, CQ, D_MODEL), BF16),
            pltpu.VMEM((n_slab, tm, LANES), F32),
            pltpu.VMEM((n_slab, tm, LANES), F32),
            pltpu.VMEM((n_slab, tm, LANES), F32),
            pltpu.VMEM((SUBLANES, CQ), F32),
            pltpu.VMEM((SUBLANES, CQ), F32),
        ],
        compiler_params=_params(("arbitrary", "arbitrary")),
        name="inproj_conv",
    )(xn, xnm, wt, wt, wt, b1, b1, b1, wconv, sconv, sconv, p_a, p_b, w_o)


def _prompt_head(q, kt, v, gr, gc, ca, m_st):
    l = q.shape[0]
    assert l == LANES
    ig_row, b_row, r_row = gr[G_IG:G_IG + 1, :], gr[G_B:G_B + 1, :], gr[G_R:G_R + 1, :]
    b_col = gc[0]
    v_aug = jnp.concatenate([v, jnp.ones((l, LANES), BF16)], axis=1)
    t_idx = lax.broadcasted_iota(jnp.int32, (l, l), 0)
    s_idx = lax.broadcasted_iota(jnp.int32, (l, l), 1)
    logw = jnp.where(s_idx <= t_idx, b_col - b_row + ig_row, NEG)
    inter = b_col + m_st
    m_t = jnp.maximum(inter, gc[1])
    w_inter = jnp.exp(inter - m_t)
    s = _dot(q, kt) * jnp.exp(logw - m_t)
    nd = (jnp.concatenate([w_inter] * (AUG // LANES), axis=1) * _dot(q, ca.astype(BF16))
          + _dot(s.astype(BF16), v_aug))
    rcp = 1.0 / jnp.maximum(jnp.abs(nd[:, M_DV:AUG]), jnp.exp(-m_t))
    h = nd[:, 0:M_DV] * jnp.concatenate([rcp] * (M_DV // LANES), axis=1)
    m_new = m_t[l - 1:l, 0:1]
    decay = jnp.exp(b_col[l - 1:l, 0:1] + m_st - m_new)
    kwt = (kt.astype(F32) * jnp.exp(r_row + ig_row - m_new)).astype(BF16)
    return h, decay * ca + _dot(kwt, v_aug), m_new


def _mlstm_prompt_kernel(batch, qm_ref, ktm_ref, vm_ref, gcm_ref, grm_ref, *refs):
    ins, (h_ref, c_ref, n_ref, m_ref, ca_ref) = refs[:5 * batch], refs[5 * batch:]
    heads = [(hd, slice(hd * M_DK, (hd + 1) * M_DK)) for hd in range(M_HEADS)]

    @pl.when(pl.program_id(0) == 0)
    def _():
        for hd, sl in heads:
            _, ca_new, m_new = _prompt_head(qm_ref[:, sl], ktm_ref[sl, :], vm_ref[:, sl], grm_ref[hd],
                                            gcm_ref[hd], jnp.zeros((M_DK, AUG), F32), jnp.zeros((1, 1), F32))
            for b in range(batch):
                ca_ref[b, hd] = ca_new
                m_ref[b, hd:hd + 1, :] = jnp.broadcast_to(m_new, (1, LANES))

    for b in range(batch):
        q_ref, kt_ref, v_ref, gc_ref, gr_ref = ins[5 * b:5 * b + 5]
        for hd, sl in heads:
            h, ca_new, m_new = _prompt_head(q_ref[:, sl], kt_ref[sl, :], v_ref[:, sl], gr_ref[hd], gc_ref[hd],
                                            ca_ref[b, hd], m_ref[b, hd:hd + 1, 0:1])
            ca_ref[b, hd] = ca_new
            m_ref[b, hd:hd + 1, :] = jnp.broadcast_to(m_new, (1, LANES))
            h_ref[b, :, sl] = h.astype(h_ref.dtype)

    @pl.when(pl.program_id(0) == pl.num_programs(0) - 1)
    def _():
        for b in range(batch):
            for hd, _ in heads:
                ca = ca_ref[b, hd]
                c_ref[b, hd] = ca[:, 0:M_DV]
                n_t = ca[:, M_DV:AUG]
                n_ref[b, hd:hd + 1, :] = jnp.concatenate(
                    [n_t[k * LANES:(k + 1) * LANES, :].T[0:1, :] for k in range(M_DK // LANES)], axis=1)


def _mlstm_prompt_call(zm, ktm, gcm, grm, z, kt, gch, grh, batch, seq):
    nc = seq // CHUNK
    per_prompt_specs, per_prompt_args = [], []
    for b in range(batch):
        row = functools.partial(lambda b, c: b * nc + c, b)
        per_prompt_specs += [
            pl.BlockSpec((CHUNK, M_WIDTH), lambda c, row=row: (row(c), Z_Q)),
            pl.BlockSpec((M_WIDTH, CHUNK), lambda c, row=row: (0, row(c))),
            pl.BlockSpec((CHUNK, M_WIDTH), lambda c, row=row: (row(c), Z_V)),
            pl.BlockSpec((M_HEADS, 2, CHUNK, LANES), lambda c, row=row: (0, 0, row(c), 0)),
            pl.BlockSpec((M_HEADS, SUBLANES, CHUNK), lambda c, row=row: (0, 0, row(c))),
        ]
        per_prompt_args += [z, kt, z, gch, grh]
    return pl.pallas_call(
        functools.partial(_mlstm_prompt_kernel, batch),
        grid=(nc,),
        in_specs=[
            pl.BlockSpec((CHUNK, M_WIDTH), lambda c: (0, 0)),
            pl.BlockSpec((M_WIDTH, CHUNK), lambda c: (0, 0)),
            pl.BlockSpec((CHUNK, M_WIDTH), lambda c: (0, 1)),
            pl.BlockSpec((M_HEADS, 2, CHUNK, LANES), lambda c: (0, 0, 0, 0)),
            pl.BlockSpec((M_HEADS, SUBLANES, CHUNK), lambda c: (0, 0, 0)),
        ] + per_prompt_specs,
        out_specs=[
            pl.BlockSpec((batch, CHUNK, M_WIDTH), lambda c: (0, c, 0)),
            pl.BlockSpec((batch, M_HEADS, M_DK, M_DV), lambda c: (0, 0, 0, 0)),
            pl.BlockSpec((batch, M_HEADS, M_DK), lambda c: (0, 0, 0)),
            pl.BlockSpec((batch, M_HEADS, LANES), lambda c: (0, 0, 0)),
        ],
        out_shape=[
            jax.ShapeDtypeStruct((batch, seq, M_WIDTH), BF16),
            jax.ShapeDtypeStruct((batch, M_HEADS, M_DK, M_DV), F32),
            jax.ShapeDtypeStruct((batch, M_HEADS, M_DK), F32),
            jax.ShapeDtypeStruct((batch, M_HEADS, LANES), F32),
        ],
        scratch_shapes=[pltpu.VMEM((batch, M_HEADS, M_DK, AUG), F32)],
        compiler_params=_params(("arbitrary",)),
        name="mlstm_prompt",
    )(zm, ktm, zm, gcm, grm, *per_prompt_args)


def _group_max(x, size):
    n = x.shape[-1]
    lane = lax.broadcasted_iota(jnp.int32, x.shape, x.ndim - 1)
    k = 1
    while k < size:
        partner = jnp.where((lane & k) == 0, pltpu.roll(x, n - k, x.ndim - 1), pltpu.roll(x, k, x.ndim - 1))
        x = jnp.maximum(x, partner)
        k *= 2
    return x


def _mlstm_sample_kernel(dec_seq, q_ref, kt_ref, v_ref, gc_ref, gr_ref, mcol_ref, mrow_ref, cin_ref, nin_ref,
                         h_ref, c_ref, n_ref, m_ref):
    l = q_ref.shape[0]
    n_seq = l // dec_seq
    shift = dec_seq.bit_length() - 1
    q, kt, v = q_ref[...], kt_ref[...], v_ref[...]
    assert l == LANES
    gr, gc = gr_ref[0], gc_ref[0]
    ig_row, b_row, r_row = gr[G_IG:G_IG + 1, :], gr[G_B:G_B + 1, :], gr[G_R:G_R + 1, :]
    b_col = gc[0]
    m_col, m_row = mcol_ref[0], mrow_ref[0]

    t_idx = lax.broadcasted_iota(jnp.int32, (l, l), 0)
    s_idx = lax.broadcasted_iota(jnp.int32, (l, l), 1)
    t_seq = t_idx >> shift
    logw = jnp.where((t_seq == (s_idx >> shift)) & (s_idx <= t_idx), b_col - b_row + ig_row, NEG)
    inter = b_col + m_col
    m_t = jnp.maximum(inter, gc[1])
    w_inter = jnp.exp(inter - m_t)
    s = _dot(q, kt) * jnp.exp(logw - m_t)

    e_row = r_row + ig_row
    b_last = b_row + r_row
    m_new = jnp.maximum(b_last + m_row, _group_max(e_row, dec_seq))
    decay = jnp.exp(b_last + m_row - m_new)
    kwt = (kt.astype(F32) * jnp.exp(e_row - m_new)).astype(BF16)

    seq_lane = s_idx == t_seq
    v_aug = jnp.concatenate([v, jnp.where(seq_lane, 1.0, 0.0).astype(BF16)], axis=1)
    pa = _dot(s.astype(BF16), v_aug)

    n_rows = jnp.concatenate([nin_ref[...], jnp.zeros((LANES - n_seq, M_DK), F32)], axis=0)
    n_t = jnp.concatenate([n_rows[:, 0:LANES].T, n_rows[:, LANES:2 * LANES].T], axis=0)
    lane_k = lax.broadcasted_iota(jnp.int32, (M_DK, LANES), 1)
    qf = q.astype(F32)
    qca_rows = []
    n_t_new = jnp.zeros((M_DK, LANES), F32)
    for i in range(n_seq):
        ca = jnp.concatenate([cin_ref[i, 0], jnp.where(lane_k == i, n_t, 0.0)], axis=1)
        rows = slice(i * dec_seq, (i + 1) * dec_seq)
        qca_rows.append(_dot(qf[rows, :].astype(BF16), ca.astype(BF16)))
        kw_i = jnp.where((lane_k >> shift) == i, kwt, jnp.zeros_like(kwt))
        ca_new = decay[:, i * dec_seq:i * dec_seq + 1] * ca + _dot(kw_i, v_aug)
        c_ref[i, 0] = ca_new[:, 0:M_DV]
        n_t_new = n_t_new + ca_new[:, M_DV:AUG]
    nd = jnp.concatenate([w_inter] * (AUG // LANES), axis=1) * jnp.concatenate(qca_rows, axis=0) + pa
    den = jnp.sum(jnp.where(seq_lane, nd[:, M_DV:AUG], 0.0), axis=-1, keepdims=True)
    rcp = 1.0 / jnp.maximum(jnp.abs(den), jnp.exp(-m_t[:, 0:1]))
    h_ref[...] = (nd[:, 0:M_DV] * rcp).astype(h_ref.dtype)
    n_ref[...] = jnp.concatenate([n_t_new[0:LANES, :].T, n_t_new[LANES:2 * LANES, :].T], axis=1)[0:n_seq, :]
    m_ref[0] = jnp.broadcast_to(m_new, (SUBLANES, l))


def _mlstm_sample_call(z, kt, gch, grh, m_col, m_row, c_in, n_in, row0, dec_seq):
    nb = c_in.shape[0]
    l = SEQ_PER_STEP * dec_seq
    blk0 = row0 // l
    return pl.pallas_call(
        functools.partial(_mlstm_sample_kernel, dec_seq),
        grid=(nb // SEQ_PER_STEP, M_HEADS),
        in_specs=[
            pl.BlockSpec((l, M_DK), lambda i, h: (blk0 + i, Z_Q * M_HEADS + h)),
            pl.BlockSpec((M_DK, l), lambda i, h: (h, blk0 + i)),
            pl.BlockSpec((l, M_DV), lambda i, h: (blk0 + i, Z_V * M_HEADS + h)),
            pl.BlockSpec((1, 2, l, LANES), lambda i, h: (h, 0, blk0 + i, 0)),
            pl.BlockSpec((1, SUBLANES, l), lambda i, h: (h, 0, blk0 + i)),
            pl.BlockSpec((1, l, LANES), lambda i, h: (h, i, 0)),
            pl.BlockSpec((1, 1, l), lambda i, h: (h, 0, i)),
            pl.BlockSpec((SEQ_PER_STEP, 1, M_DK, M_DV), lambda i, h: (i, h, 0, 0)),
            pl.BlockSpec((SEQ_PER_STEP, M_DK), lambda i, h: (i, h)),
        ],
        out_specs=[
            pl.BlockSpec((l, M_DV), lambda i, h: (i, h)),
            pl.BlockSpec((SEQ_PER_STEP, 1, M_DK, M_DV), lambda i, h: (i, h, 0, 0)),
            pl.BlockSpec((SEQ_PER_STEP, M_DK), lambda i, h: (i, h)),
            pl.BlockSpec((1, SUBLANES, l), lambda i, h: (h, 0, i)),
        ],
        out_shape=[
            jax.ShapeDtypeStruct((nb * dec_seq, M_WIDTH), BF16),
            jax.ShapeDtypeStruct(c_in.shape, F32),
            jax.ShapeDtypeStruct(n_in.shape, F32),
            jax.ShapeDtypeStruct((M_HEADS, SUBLANES, nb * dec_seq), F32),
        ],
        compiler_params=_params(("arbitrary", "arbitrary")),
        name="mlstm_sample",
    )(z, kt, z, gch, grh, m_col, m_row, c_in, n_in)


def _mix_kernel(n_prompt_tiles, hp_ref, hs_ref, so_ref, yb_ref, ga_ref, gb_ref, xp_ref, xs_ref,
                pa_ref, pb_ref, wo_ref, gffn_ref, o_ref, on_ref):
    i = pl.program_id(0)

    def body(h_ref, x_ref):
        h_a = (so_ref[...].astype(F32) * h_ref[...].astype(F32)).astype(BF16)
        a = _dot(h_a, pa_ref[...])
        b = _dot(yb_ref[...], pb_ref[...])
        merged = ga_ref[...].astype(F32) * a + gb_ref[...].astype(F32) * b
        x1 = x_ref[...] + _dot(merged.astype(BF16), wo_ref[...])
        o_ref[...] = x1
        on_ref[...] = _rmsnorm(x1, gffn_ref[...]).astype(BF16)

    @pl.when(i < n_prompt_tiles)
    def _():
        body(hp_ref, xp_ref)

    @pl.when(i >= n_prompt_tiles)
    def _():
        body(hs_ref, xs_ref)


def _mix_call(hp, hs, z, yb, xp, xs, pa, pb, wo, gffn, tm):
    tp, ts = xp.shape[0], xs.shape[0]
    npt, nst = tp // tm, ts // tm
    pi = lambda i: jnp.minimum(i, npt - 1)
    si = lambda i: jnp.maximum(i - npt, 0)
    const = lambda i: (0, 0)
    return pl.pallas_call(
        functools.partial(_mix_kernel, npt),
        grid=(npt + nst,),
        in_specs=[
            pl.BlockSpec((tm, M_WIDTH), lambda i: (pi(i), 0)),
            pl.BlockSpec((tm, M_WIDTH), lambda i: (si(i), 0)),
            pl.BlockSpec((tm, M_WIDTH), lambda i: (i, Z_SO)),
            pl.BlockSpec((tm, C_WIDTH), lambda i: (i, 0)),
            pl.BlockSpec((tm, D_MODEL), lambda i: (i, Z_GA // 2)),
            pl.BlockSpec((tm, D_MODEL), lambda i: (i, Z_GB // 2)),
            pl.BlockSpec((tm, D_MODEL), lambda i: (pi(i), 0)),
            pl.BlockSpec((tm, D_MODEL), lambda i: (si(i), 0)),
            pl.BlockSpec((M_WIDTH, D_MODEL), const, pipeline_mode=pl.Buffered(1)),
            pl.BlockSpec((C_WIDTH, D_MODEL), const, pipeline_mode=pl.Buffered(1)),
            pl.BlockSpec((D_MODEL, D_MODEL), const, pipeline_mode=pl.Buffered(1)),
            pl.BlockSpec((1, D_MODEL), const),
        ],
        out_specs=[pl.BlockSpec((tm, D_MODEL), lambda i: (i, 0)), pl.BlockSpec((tm, D_MODEL), lambda i: (i, 0))],
        out_shape=[jax.ShapeDtypeStruct((tp + ts, D_MODEL), F32), jax.ShapeDtypeStruct((tp + ts, D_MODEL), BF16)],
        compiler_params=_params(("arbitrary",)),
        name="mix",
    )(hp, hs, z, yb, z, z, xp, xs, pa, pb, wo, gffn)


def _ffn_kernel(n_prompt_tiles, x1_ref, xn_ref, wup_ref, wdn_ref, gfin_ref, yp_ref, ys_ref, acc_ref):
    i = pl.program_id(0)
    f = pl.program_id(1)

    def mlp_part():
        u = jnp.maximum(_dot(xn_ref[...], wup_ref[...]), 0.0)
        return _dot((u * u).astype(BF16), wdn_ref[...])

    @pl.when(f == 0)
    def _():
        acc_ref[...] = x1_ref[...] + mlp_part()

    @pl.when(f > 0)
    def _():
        acc_ref[...] += mlp_part()

    @pl.when(f == pl.num_programs(1) - 1)
    def _():
        y = _rmsnorm(acc_ref[...], gfin_ref[...])

        @pl.when(i < n_prompt_tiles)
        def _():
            yp_ref[...] = y

        @pl.when(i >= n_prompt_tiles)
        def _():
            ys_ref[...] = y


def _ffn_call(x1, xn, wup, wdn, gfin, n_prompt_tokens, tm, tf):
    t = x1.shape[0]
    npt = n_prompt_tokens // tm
    n_tiles = t // tm
    return pl.pallas_call(
        functools.partial(_ffn_kernel, npt),
        grid=(n_tiles, D_FF // tf),
        in_specs=[
            pl.BlockSpec((tm, D_MODEL), lambda i, f: (i, 0)),
            pl.BlockSpec((tm, D_MODEL), lambda i, f: (i, 0)),
            pl.BlockSpec((D_MODEL, tf), lambda i, f: (0, f)),
            pl.BlockSpec((tf, D_MODEL), lambda i, f: (f, 0)),
            pl.BlockSpec((1, D_MODEL), lambda i, f: (0, 0)),
        ],
        out_specs=[
            pl.BlockSpec((tm, D_MODEL), lambda i, f: (jnp.minimum(i, npt - 1), 0)),
            pl.BlockSpec((tm, D_MODEL), lambda i, f: (jnp.maximum(i - npt, 0), 0)),
        ],
        out_shape=[
            jax.ShapeDtypeStruct((n_prompt_tokens, D_MODEL), F32),
            jax.ShapeDtypeStruct((t - n_prompt_tokens, D_MODEL), F32),
        ],
        scratch_shapes=[pltpu.VMEM((tm, D_MODEL), F32)],
        compiler_params=_params(("arbitrary", "arbitrary")),
        name="ffn",
    )(x1, xn, wup, wdn, gfin)


def kernel(x_prompt, x_sample, state_mlstm_C, state_mlstm_n, state_mlstm_m, state_conv, meta_tokens,
           g_mix, w_in, b_in, w_conv, p_a, p_b, w_o, g_ffn, w_up, w_down, g_final):
    assert w_in.shape[0] == 1, "single-layer trunk"
    batch, seq, _ = x_prompt.shape
    dec_batch, dec_seq, _ = x_sample.shape
    assert dec_seq == SUBLANES and seq % 1024 == 0 and dec_batch % SEQ_PER_STEP == 0
    n_p, n_s = batch * seq, dec_batch * dec_seq

    wt = w_in[0].T
    bias = b_in[0]
    b1 = jnp.concatenate([bias[:OFF_GATES], bias[OFF_O:]])
    bg = bias[OFF_GATES:OFF_O][:, None]
    gmix = g_mix[0][None, :]

    xp = x_prompt.reshape(n_p, D_MODEL)
    xs = x_sample.reshape(n_s, D_MODEL)
    n_meta = meta_tokens.shape[0]

    xn, gch, grh, xnm, gcm, grm = _rms_call(xp, xs, meta_tokens.astype(F32), gmix, wt, bg, dec_seq, 512)
    z, kt, zm, ktm, w_up16, w_down16 = _inproj_call(xn, xnm, wt, b1[None, :], b1[:, None],
                                                    w_up[0], w_down[0], 1024)
    yb, cu6, cu7, p_a16, p_b16, w_o16 = _conv_call(
        xn, xnm, n_meta, wt, b1[None, :], w_conv[0], state_conv[0].reshape(dec_batch, (CONV_W - 1) * C_WIDTH),
        p_a[0], p_b[0], w_o[0], n_p, seq, 1024)

    hp, c_p, n_pr, m_p = _mlstm_prompt_call(zm, ktm, gcm, grm, z, kt, gch, grh, batch, seq)
    m_tok = jnp.repeat(state_mlstm_m[0], dec_seq, axis=0).T
    hs, c_s, n_sm, m_s = _mlstm_sample_call(
        z, kt, gch, grh, jnp.broadcast_to(m_tok[:, :, None], m_tok.shape + (LANES,)), m_tok[:, None, :],
        state_mlstm_C[0],
        state_mlstm_n[0].reshape(dec_batch, M_HEADS * M_DK), n_p, dec_seq)

    x1, xn2 = _mix_call(hp.reshape(n_p, M_WIDTH), hs, z, yb, xp, xs, p_a16, p_b16, w_o16, g_ffn[0][None, :], 256)
    y_p, y_s = _ffn_call(x1, xn2, w_up16, w_down16, g_final[None, :], n_p, 512, 1024)

    per_seq = seq // SUBLANES
    cv_p = jnp.stack([cu6[per_seq - 1:batch * per_seq:per_seq], cu7[per_seq - 1:batch * per_seq:per_seq]], axis=1)
    first_s = n_p // SUBLANES
    cv_s = jnp.stack([cu6[first_s:], cu7[first_s:]], axis=1)
    m_s = m_s[:, 0, dec_seq - 1::dec_seq].T
    return (y_p.reshape(batch, seq, D_MODEL), y_s.reshape(dec_batch, dec_seq, D_MODEL),
            c_p[None], n_pr[None], m_p[None, :, :, 0], cv_p[None],
            c_s[None], n_sm.reshape(dec_batch, M_HEADS, M_DK)[None], m_s[None], cv_s[None])
```

```python
import functools

import jax
import jax.numpy as jnp
from jax import lax
from jax.experimental import pallas as pl
from jax.experimental.pallas import tpu as pltpu

F32 = jnp.float32
BF16 = jnp.bfloat16

D_MODEL = 2048
N_META = 16
CHUNK = 128
M_HEADS = 4
M_DK = 256
M_DV = 256
M_WIDTH = M_HEADS * M_DV
C_WIDTH = 1024
CONV_W = 3
D_FF = 4 * D_MODEL
EPS = 1e-6
N_GATES = 2 * M_HEADS
OFF_Q = 0
OFF_K = OFF_Q + M_HEADS * M_DK
OFF_V = OFF_K + M_HEADS * M_DK
OFF_GATES = OFF_V + M_WIDTH
OFF_O = OFF_GATES + N_GATES
OFF_U = OFF_O + M_WIDTH
OFF_C = OFF_U + C_WIDTH
OFF_B = OFF_C + C_WIDTH
OFF_GA = OFF_B + C_WIDTH
OFF_GB = OFF_GA + D_MODEL

LANES = 128
SUBLANES = 8
VMEM_LIMIT_BYTES = 60000 * 1024

TN = 1024
Z_Q, Z_V, Z_GA, Z_GB, Z_SO = 0, 1, 2, 4, 6
Z_COLS = 7 * TN
CQ = 256
AUG = M_DV + LANES
SEQ_PER_STEP = CHUNK // SUBLANES
G_IG, G_B, G_R = 0, 1, 2

NEG = -1e30
NT_DIMS = (((1,), (1,)), ((), ()))


def _params(semantics):
    return pltpu.CompilerParams(dimension_semantics=semantics, vmem_limit_bytes=VMEM_LIMIT_BYTES)


def _rmsnorm(x, g):
    y = x * lax.rsqrt(jnp.mean(x * x, axis=-1, keepdims=True) + EPS)
    return y * g


def _log_sigmoid(x):
    return jnp.minimum(x, 0.0) - jnp.log1p(jnp.exp(-jnp.abs(x)))


def _dot(a, b):
    return jnp.dot(a, b, preferred_element_type=F32)


def _dot_nt(a, b):
    return lax.dot_general(a, b, NT_DIMS, preferred_element_type=F32)


def _gate_prep(xn, wg_ref, bg_ref, blk, n_valid, gch_ref, grh_ref):
    tm = xn.shape[0]
    wg = wg_ref[...].astype(BF16)
    wg = jnp.concatenate([wg, jnp.zeros((LANES - N_GATES, wg.shape[1]), BF16)], axis=0)
    g = _dot_nt(wg, xn)[0:SUBLANES, :] + bg_ref[...]
    row = lax.broadcasted_iota(jnp.int32, (SUBLANES, tm), 0)
    lane = lax.broadcasted_iota(jnp.int32, (SUBLANES, tm), 1)
    a = jnp.where(row < M_HEADS, g, _log_sigmoid(g))
    if n_valid < tm:
        a = jnp.where(lane < n_valid, a, jnp.where(row < M_HEADS, NEG, 0.0))
    pos = lane & (blk - 1)
    n_steps_blk = blk if isinstance(blk, int) else LANES

    def scan(x, op, fill, reverse=False):
        shift = 1
        while shift < n_steps_blk:
            if reverse:
                x = op(x, jnp.where(pos < blk - shift, pltpu.roll(x, tm - shift, 1), fill))
            else:
                x = op(x, jnp.where(pos >= shift, pltpu.roll(x, shift, 1), fill))
            shift *= 2
        return x

    pre = scan(a, jnp.add, 0.0)
    suf = scan(a, jnp.add, 0.0, reverse=True) - a
    b_up = pltpu.roll(pre, M_HEADS, 0)
    m_in = b_up + scan(a - b_up, jnp.maximum, -3e38)
    for h in range(M_HEADS):
        grh_ref[h] = jnp.where(
            row == G_IG, pltpu.roll(a, (G_IG - h) % SUBLANES, 0),
            jnp.where(row == G_B, pltpu.roll(pre, (G_B - M_HEADS - h) % SUBLANES, 0),
                      jnp.where(row == G_R, pltpu.roll(suf, (G_R - M_HEADS - h) % SUBLANES, 0), 0.0)))
        for c in range(tm // LANES):
            cs = slice(c * LANES, (c + 1) * LANES)
            gch_ref[h, 0, cs, :] = jnp.broadcast_to(b_up[h:h + 1, cs], (LANES, LANES)).T
            gch_ref[h, 1, cs, :] = jnp.broadcast_to(m_in[h:h + 1, cs], (LANES, LANES)).T


S_Q, S_K, S_V, S_O = 0, 1, 2, 3


def _cast_blocks(pairs):
    for src_ref, dst_ref in pairs:
        dst_ref[...] = src_ref[...].astype(BF16)


def _cast_specs(w, n_blocks, step):
    rows, cols = w.shape
    blk = lambda *g: (jnp.minimum(step(*g), n_blocks - 1), 0)
    spec = pl.BlockSpec((rows // n_blocks, cols), blk)
    return spec, spec, jax.ShapeDtypeStruct(w.shape, BF16)


def _inproj_kernel(xn_ref, xnm_ref, wt_ref, b_ref, bcol_ref, wup_ref, wdn_ref,
                   z_ref, kt_ref, zm_ref, ktm_ref, wup16_ref, wdn16_ref, w_ref):
    s = pl.program_id(0)
    casts = ((wup_ref, wup16_ref), (wdn_ref, wdn16_ref))
    k_scale = M_DK ** -0.5

    @pl.when(pl.program_id(1) == 0)
    def _():
        w_ref[...] = wt_ref[...].astype(BF16)

        @pl.when((s == S_Q) | (s == S_V))
        def _():
            zm_ref[...] = (_dot_nt(xnm_ref[...], w_ref[...]) + b_ref[...]).astype(BF16)

        @pl.when(s == S_K)
        def _():
            ktm_ref[...] = ((_dot_nt(w_ref[...], xnm_ref[...]) + bcol_ref[...]) * k_scale).astype(BF16)

    def z():
        return _dot_nt(xn_ref[...], w_ref[...]) + b_ref[...]

    @pl.when((s == S_Q) | (s == S_V))
    def _():
        _cast_blocks(casts)
        z_ref[...] = z().astype(BF16)

    @pl.when(s == S_K)
    def _():
        _cast_blocks(casts)
        kt_ref[...] = ((_dot_nt(w_ref[...], xn_ref[...]) + bcol_ref[...]) * k_scale).astype(BF16)

    @pl.when(s >= S_O)
    def _():
        _cast_blocks(casts)
        z_ref[...] = jax.nn.sigmoid(z()).astype(BF16)


def _inproj_call(xn, xnm, wt, b1, b1col, w_up, w_down, tm):
    t = xn.shape[0]
    assert t % tm == 0
    n_m = t // tm
    rows_m = xnm.shape[0]
    n_cast = 64
    assert n_cast <= 8 * n_m
    up_in, up_out, up_shape = _cast_specs(w_up, n_cast, lambda s, m: s * n_m + m)
    dn_in, dn_out, dn_shape = _cast_specs(w_down, n_cast, lambda s, m: s * n_m + m)
    w_off = lambda s: jnp.where(s < S_O, s * TN, jnp.where(s == S_O, OFF_O, OFF_GA + (s - 4) * TN))
    b_blk = lambda s: jnp.where(s <= S_O, s, s + 3)
    z_col = lambda s: jnp.where(s <= S_K, Z_Q, jnp.where(s == S_V, Z_V, jnp.where(s == S_O, Z_SO, s - 2)))
    z_row = lambda s, m: jnp.where(s == S_K, n_m - 1, m)
    kt_blk = lambda s, m: jnp.where(s == S_K, m, jnp.where(s < S_K, 0, n_m - 1))
    return pl.pallas_call(
        _inproj_kernel,
        grid=(8, n_m),
        in_specs=[
            pl.BlockSpec((tm, D_MODEL), lambda s, m: (m, 0)),
            pl.BlockSpec((rows_m, D_MODEL), lambda s, m: (0, 0)),
            pl.BlockSpec((pl.Element(TN), pl.Element(D_MODEL)),
                         lambda s, m: (pl.multiple_of(w_off(s), SUBLANES), 0)),
            pl.BlockSpec((1, TN), lambda s, m: (0, b_blk(s))),
            pl.BlockSpec((TN, 1), lambda s, m: (b_blk(s), 0)),
            up_in, dn_in,
        ],
        out_specs=[
            pl.BlockSpec((tm, TN), lambda s, m: (z_row(s, m), z_col(s))),
            pl.BlockSpec((TN, tm), lambda s, m: (0, kt_blk(s, m))),
            pl.BlockSpec((rows_m, TN), lambda s, m: (0, (s >= S_V).astype(jnp.int32))),
            pl.BlockSpec((TN, rows_m), lambda s, m: (0, 0)),
            up_out, dn_out,
        ],
        out_shape=[
            jax.ShapeDtypeStruct((t, Z_COLS), BF16),
            jax.ShapeDtypeStruct((TN, t), BF16),
            jax.ShapeDtypeStruct((rows_m, 2 * TN), BF16),
            jax.ShapeDtypeStruct((TN, rows_m), BF16),
            up_shape, dn_shape,
        ],
        scratch_shapes=[pltpu.VMEM((TN, D_MODEL), BF16)],
        compiler_params=_params(("arbitrary", "arbitrary")),
        name="inproj",
    )(xn, xnm, wt, b1, b1col, w_up, w_down)


def _conv_prologue(m, is_prompt, tiles_per_seq, n_meta, xnm_ref, w_refs, b_refs, s0_ref, s1_ref, scratch,
                   make_xnm=None):
    wu_ref, wc_ref, wb_ref = w_refs
    bu_ref, bc_ref, _ = b_refs
    w3_ref, _, h1_ref, h2_ref, carry_ref, mtail_ref = scratch
    n_slab, tm, _ = h1_ref.shape
    n_seq = tm // SUBLANES
    slabs = [(k, slice(k * LANES, (k + 1) * LANES)) for k in range(n_slab)]
    seq_row = lambda r: pl.ds(r, n_seq, stride=SUBLANES)

    @pl.when(m == 0)
    def _():
        if make_xnm is not None:
            make_xnm()
        w3_ref[0] = wu_ref[...].astype(BF16)
        w3_ref[1] = wc_ref[...].astype(BF16)
        w3_ref[2] = wb_ref[...].astype(BF16)
        h1_ref[...] = jnp.zeros_like(h1_ref)
        h2_ref[...] = jnp.zeros_like(h2_ref)
        xnm = xnm_ref[...]
        cu_m = (_dot_nt(xnm, w3_ref[1]) + bc_ref[...]) * (_dot_nt(xnm, w3_ref[0]) + bu_ref[...])
        mtail_ref[...] = cu_m[n_meta - SUBLANES:n_meta, :]

    @pl.when(is_prompt)
    def _():
        first = (m % tiles_per_seq) == 0
        p6 = jnp.where(first, mtail_ref[6:7, :], carry_ref[6:7, :])
        p7 = jnp.where(first, mtail_ref[7:8, :], carry_ref[7:8, :])
        for k, ks in slabs:
            h2_ref[k, 0:1, :] = p6[:, ks]
            h2_ref[k, 1:2, :] = p7[:, ks]
            h1_ref[k, 0:1, :] = p7[:, ks]

    @pl.when(jnp.logical_not(is_prompt))
    def _():
        for k, ks in slabs:
            h2_ref[k, seq_row(0), :] = s0_ref[:, ks]
            h2_ref[k, seq_row(1), :] = s1_ref[:, ks]
            h1_ref[k, seq_row(0), :] = s1_ref[:, ks]


def _conv_main(xn, is_prompt, b_refs, wconv_ref, yb_ref, cu6_ref, cu7_ref, scratch):
    bu_ref, bc_ref, bb_ref = b_refs
    w3_ref, cu_ref, h1_ref, h2_ref, carry_ref, _ = scratch
    tm, cq = yb_ref.shape
    n_seq = tm // SUBLANES
    slabs = [(k, slice(k * LANES, (k + 1) * LANES)) for k in range(cq // LANES)]
    seq_row = lambda r: pl.ds(r, n_seq, stride=SUBLANES)
    zu = _dot_nt(xn, w3_ref[0]) + bu_ref[...]
    zc = _dot_nt(xn, w3_ref[1]) + bc_ref[...]
    zb = _dot_nt(xn, w3_ref[2]) + bb_ref[...]
    cu = zc * zu
    pos = lax.broadcasted_iota(jnp.int32, (tm, cq), 0) & jnp.where(is_prompt, tm - 1, SUBLANES - 1)
    h1 = jnp.concatenate([h1_ref[k] for k, _ in slabs], axis=1)
    h2 = jnp.concatenate([h2_ref[k] for k, _ in slabs], axis=1)
    x1 = jnp.where(pos >= 1, pltpu.roll(cu, 1, 0), h1)
    x2 = jnp.where(pos >= 2, pltpu.roll(cu, 2, 0), h2)
    w0, w1, w2 = wconv_ref[0:1, :], wconv_ref[1:2, :], wconv_ref[2:3, :]
    yb_ref[...] = (zb * ((w0 * x2 + w1 * x1) + w2 * cu)).astype(BF16)

    carry_ref[...] = cu[tm - SUBLANES:tm, :]
    for k, ks in slabs:
        cu_ref[k] = cu[:, ks]
        cu6_ref[:, ks] = cu_ref[k, seq_row(6), :]
        cu7_ref[:, ks] = cu_ref[k, seq_row(7), :]


def _conv_kernel(n_prompt_tiles, tiles_per_seq, n_meta, xn_ref, xnm_ref, wu_ref, wc_ref, wb_ref,
                 bu_ref, bc_ref, bb_ref, wconv_ref, s0_ref, s1_ref, pa_ref, pb_ref, wo_ref,
                 yb_ref, cu6_ref, cu7_ref, pa16_ref, pb16_ref, wo16_ref, *scratch):
    m = pl.program_id(1)
    is_prompt = m < n_prompt_tiles
    b_refs = (bu_ref, bc_ref, bb_ref)
    _conv_prologue(m, is_prompt, tiles_per_seq, n_meta, xnm_ref, (wu_ref, wc_ref, wb_ref), b_refs,
                   s0_ref, s1_ref, scratch)
    _cast_blocks(((pa_ref, pa16_ref), (pb_ref, pb16_ref), (wo_ref, wo16_ref)))
    _conv_main(xn_ref[...], is_prompt, b_refs, wconv_ref, yb_ref, cu6_ref, cu7_ref, scratch)


def _rms_conv_kernel(n_prompt_tiles, tiles_per_seq, dec_seq, xp_ref, xs_ref, xm_ref, g_ref, wg_ref, bg_ref,
                     wu_ref, wc_ref, wb_ref, bu_ref, bc_ref, bb_ref, wconv_ref, s0_ref, s1_ref,
                     xn_ref, gch_ref, grh_ref, xnm_ref, gchm_ref, grhm_ref, yb_ref, cu6_ref, cu7_ref, *scratch):
    m = pl.program_id(0)
    is_prompt = m < n_prompt_tiles
    n_meta = xm_ref.shape[0]
    b_refs = (bu_ref, bc_ref, bb_ref)

    def make_xnm():
        xm = jnp.concatenate([xm_ref[...], jnp.zeros((CHUNK - n_meta, D_MODEL), F32)], axis=0)
        xnm = _rmsnorm(xm, g_ref[...]).astype(BF16)
        xnm_ref[...] = xnm
        _gate_prep(xnm, wg_ref, bg_ref, CHUNK, n_meta, gchm_ref, grhm_ref)

    _conv_prologue(m, is_prompt, tiles_per_seq, n_meta, xnm_ref, (wu_ref, wc_ref, wb_ref), b_refs,
                   s0_ref, s1_ref, scratch, make_xnm)
    xn = _rmsnorm(jnp.where(is_prompt, xp_ref[...], xs_ref[...]), g_ref[...]).astype(BF16)
    xn_ref[...] = xn
    _gate_prep(xn, wg_ref, bg_ref, jnp.where(is_prompt, CHUNK, dec_seq), xn.shape[0], gch_ref, grh_ref)
    _conv_main(xn, is_prompt, b_refs, wconv_ref, yb_ref, cu6_ref, cu7_ref, scratch)


def _conv_specs(c0, npt, tm, cm):
    n_seq = tm // SUBLANES
    n_slab = CQ // LANES
    chan = lambda *g: c0 + cm(*g)[0]
    tile = lambda *g: cm(*g)[1]
    w_spec = lambda off: pl.BlockSpec((pl.Element(CQ), pl.Element(D_MODEL)),
                                      lambda *g: (pl.multiple_of(off + chan(*g) * CQ, SUBLANES), 0))
    b_spec = lambda off: pl.BlockSpec((1, CQ), lambda *g: (0, (off - N_GATES) // CQ + chan(*g)))
    s_spec = lambda tok: pl.BlockSpec(
        (n_seq, CQ), lambda *g: (jnp.maximum(tile(*g) - npt, 0), tok * (C_WIDTH // CQ) + chan(*g)))
    in_specs = [w_spec(OFF_U), w_spec(OFF_C), w_spec(OFF_B), b_spec(OFF_U), b_spec(OFF_C), b_spec(OFF_B),
                pl.BlockSpec((CONV_W, CQ), lambda *g: (0, chan(*g))), s_spec(0), s_spec(1)]
    out_specs = [pl.BlockSpec((tm, CQ), lambda *g: (tile(*g), cm(*g)[0])),
                 pl.BlockSpec((n_seq, CQ), lambda *g: (tile(*g), cm(*g)[0])),
                 pl.BlockSpec((n_seq, CQ), lambda *g: (tile(*g), cm(*g)[0]))]
    scratch = [pltpu.VMEM((3, CQ, D_MODEL), BF16),
               pltpu.VMEM((n_slab, tm, LANES), F32),
               pltpu.VMEM((n_slab, tm, LANES), F32),
               pltpu.VMEM((n_slab, tm, LANES), F32),
               pltpu.VMEM((SUBLANES, CQ), F32),
               pltpu.VMEM((SUBLANES, CQ), F32)]
    return in_specs, out_specs, scratch


def _conv_out_shapes(t, n_blocks):
    return [jax.ShapeDtypeStruct((t, n_blocks * CQ), BF16),
            jax.ShapeDtypeStruct((t // SUBLANES, n_blocks * CQ), F32),
            jax.ShapeDtypeStruct((t // SUBLANES, n_blocks * CQ), F32)]


def _rms_conv_call(xp, xs, x_meta, g, wt, bg, b1, wconv, sconv, dec_seq, seq_len, tm):
    tp, ts = xp.shape[0], xs.shape[0]
    t = tp + ts
    assert tm & (tm - 1) == 0 and seq_len % tm == 0 and x_meta.shape[0] >= SUBLANES
    npt, nst = tp // tm, ts // tm
    conv_in, conv_out, scratch = _conv_specs(0, npt, tm, lambda m: (0, m))
    return pl.pallas_call(
        functools.partial(_rms_conv_kernel, npt, seq_len // tm, dec_seq),
        grid=(npt + nst,),
        in_specs=[
            pl.BlockSpec((tm, D_MODEL), lambda m: (jnp.minimum(m, npt - 1), 0)),
            pl.BlockSpec((tm, D_MODEL), lambda m: (jnp.maximum(m - npt, 0), 0)),
            pl.BlockSpec(x_meta.shape, lambda m: (0, 0)),
            pl.BlockSpec((1, D_MODEL), lambda m: (0, 0)),
            pl.BlockSpec((N_GATES, D_MODEL), lambda m: (OFF_GATES // N_GATES, 0)),
            pl.BlockSpec((N_GATES, 1), lambda m: (0, 0)),
        ] + conv_in,
        out_specs=[
            pl.BlockSpec((tm, D_MODEL), lambda m: (m, 0)),
            pl.BlockSpec((M_HEADS, 2, tm, LANES), lambda m: (0, 0, m, 0)),
            pl.BlockSpec((M_HEADS, SUBLANES, tm), lambda m: (0, 0, m)),
            pl.BlockSpec((CHUNK, D_MODEL), lambda m: (0, 0)),
            pl.BlockSpec((M_HEADS, 2, CHUNK, LANES), lambda m: (0, 0, 0, 0)),
            pl.BlockSpec((M_HEADS, SUBLANES, CHUNK), lambda m: (0, 0, 0)),
        ] + conv_out,
        out_shape=[
            jax.ShapeDtypeStruct((t, D_MODEL), BF16),
            jax.ShapeDtypeStruct((M_HEADS, 2, t, LANES), F32),
            jax.ShapeDtypeStruct((M_HEADS, SUBLANES, t), F32),
            jax.ShapeDtypeStruct((CHUNK, D_MODEL), BF16),
            jax.ShapeDtypeStruct((M_HEADS, 2, CHUNK, LANES), F32),
            jax.ShapeDtypeStruct((M_HEADS, SUBLANES, CHUNK), F32),
        ] + _conv_out_shapes(t, 1),
        scratch_shapes=scratch,
        compiler_params=_params(("arbitrary",)),
        name="rms_conv",
    )(xp, xs, x_meta, g, wt, bg, wt, wt, wt, b1, b1, b1, wconv, sconv, sconv)


def _conv_call(xn, xnm, n_meta, wt, b1, wconv, sconv, p_a, p_b, w_o, n_prompt_tokens, seq_len, tm):
    t = xn.shape[0]
    assert tm & (tm - 1) == 0 and seq_len % tm == 0 and n_meta >= SUBLANES
    npt = n_prompt_tokens // tm
    n_m = t // tm
    n_blocks = C_WIDTH // CQ - 1
    n_cast = 16
    assert n_cast <= n_blocks * n_m
    casts = [_cast_specs(w, n_cast, lambda c, m: c * n_m + m) for w in (p_a, p_b, w_o)]
    conv_in, conv_out, scratch = _conv_specs(1, npt, tm, lambda c, m: (c, m))
    return pl.pallas_call(
        functools.partial(_conv_kernel, npt, seq_len // tm, n_meta),
        grid=(n_blocks, n_m),
        in_specs=[
            pl.BlockSpec((tm, D_MODEL), lambda c, m: (m, 0)),
            pl.BlockSpec(xnm.shape, lambda c, m: (0, 0)),
        ] + conv_in + [cs[0] for cs in casts],
        out_specs=conv_out + [cs[1] for cs in casts],
        out_shape=_conv_out_shapes(t, n_blocks) + [cs[2] for cs in casts],
        scratch_shapes=scratch,
        compiler_params=_params(("arbitrary", "arbitrary")),
        name="inproj_conv",
    )(xn, xnm, wt, wt, wt, b1, b1, b1, wconv, sconv, sconv, p_a, p_b, w_o)


def _prompt_head(q, kt, v, gr, gc, ca, m_st):
    l = q.shape[0]
    assert l == LANES
    ig_row, b_row, r_row = gr[G_IG:G_IG + 1, :], gr[G_B:G_B + 1, :], gr[G_R:G_R + 1, :]
    b_col = gc[0]
    v_aug = jnp.concatenate([v, jnp.ones((l, LANES), BF16)], axis=1)
    t_idx = lax.broadcasted_iota(jnp.int32, (l, l), 0)
    s_idx = lax.broadcasted_iota(jnp.int32, (l, l), 1)
    logw = jnp.where(s_idx <= t_idx, b_col - b_row + ig_row, NEG)
    inter = b_col + m_st
    m_t = jnp.maximum(inter, gc[1])
    w_inter = jnp.exp(inter - m_t)
    s = _dot(q, kt) * jnp.exp(logw - m_t)
    nd = (jnp.concatenate([w_inter] * (AUG // LANES), axis=1) * _dot(q, ca.astype(BF16))
          + _dot(s.astype(BF16), v_aug))
    rcp = 1.0 / jnp.maximum(jnp.abs(nd[:, M_DV:AUG]), jnp.exp(-m_t))
    h = nd[:, 0:M_DV] * jnp.concatenate([rcp] * (M_DV // LANES), axis=1)
    m_new = m_t[l - 1:l, 0:1]
    decay = jnp.exp(b_col[l - 1:l, 0:1] + m_st - m_new)
    kwt = (kt.astype(F32) * jnp.exp(r_row + ig_row - m_new)).astype(BF16)
    return h, decay * ca + _dot(kwt, v_aug), m_new


def _mlstm_prompt_kernel(batch, qm_ref, ktm_ref, vm_ref, gcm_ref, grm_ref, *refs):
    ins, (h_ref, c_ref, n_ref, m_ref, ca_ref) = refs[:5 * batch], refs[5 * batch:]
    heads = [(hd, slice(hd * M_DK, (hd + 1) * M_DK)) for hd in range(M_HEADS)]

    @pl.when(pl.program_id(0) == 0)
    def _():
        for hd, sl in heads:
            _, ca_new, m_new = _prompt_head(qm_ref[:, sl], ktm_ref[sl, :], vm_ref[:, sl], grm_ref[hd],
                                            gcm_ref[hd], jnp.zeros((M_DK, AUG), F32), jnp.zeros((1, 1), F32))
            for b in range(batch):
                ca_ref[b, hd] = ca_new
                m_ref[b, hd:hd + 1, :] = jnp.broadcast_to(m_new, (1, LANES))

    for b in range(batch):
        q_ref, kt_ref, v_ref, gc_ref, gr_ref = ins[5 * b:5 * b + 5]
        for hd, sl in heads:
            h, ca_new, m_new = _prompt_head(q_ref[:, sl], kt_ref[sl, :], v_ref[:, sl], gr_ref[hd], gc_ref[hd],
                                            ca_ref[b, hd], m_ref[b, hd:hd + 1, 0:1])
            ca_ref[b, hd] = ca_new
            m_ref[b, hd:hd + 1, :] = jnp.broadcast_to(m_new, (1, LANES))
            h_ref[b, :, sl] = h.astype(h_ref.dtype)

    @pl.when(pl.program_id(0) == pl.num_programs(0) - 1)
    def _():
        for b in range(batch):
            for hd, _ in heads:
                ca = ca_ref[b, hd]
                c_ref[b, hd] = ca[:, 0:M_DV]
                n_t = ca[:, M_DV:AUG]
                n_ref[b, hd:hd + 1, :] = jnp.concatenate(
                    [n_t[k * LANES:(k + 1) * LANES, :].T[0:1, :] for k in range(M_DK // LANES)], axis=1)


def _mlstm_prompt_call(zm, ktm, gcm, grm, z, kt, gch, grh, batch, seq):
    nc = seq // CHUNK
    per_prompt_specs, per_prompt_args = [], []
    for b in range(batch):
        row = functools.partial(lambda b, c: b * nc + c, b)
        per_prompt_specs += [
            pl.BlockSpec((CHUNK, M_WIDTH), lambda c, row=row: (row(c), Z_Q)),
            pl.BlockSpec((M_WIDTH, CHUNK), lambda c, row=row: (0, row(c))),
            pl.BlockSpec((CHUNK, M_WIDTH), lambda c, row=row: (row(c), Z_V)),
            pl.BlockSpec((M_HEADS, 2, CHUNK, LANES), lambda c, row=row: (0, 0, row(c), 0)),
            pl.BlockSpec((M_HEADS, SUBLANES, CHUNK), lambda c, row=row: (0, 0, row(c))),
        ]
        per_prompt_args += [z, kt, z, gch, grh]
    return pl.pallas_call(
        functools.partial(_mlstm_prompt_kernel, batch),
        grid=(nc,),
        in_specs=[
            pl.BlockSpec((CHUNK, M_WIDTH), lambda c: (0, 0)),
            pl.BlockSpec((M_WIDTH, CHUNK), lambda c: (0, 0)),
            pl.BlockSpec((CHUNK, M_WIDTH), lambda c: (0, 1)),
            pl.BlockSpec((M_HEADS, 2, CHUNK, LANES), lambda c: (0, 0, 0, 0)),
            pl.BlockSpec((M_HEADS, SUBLANES, CHUNK), lambda c: (0, 0, 0)),
        ] + per_prompt_specs,
        out_specs=[
            pl.BlockSpec((batch, CHUNK, M_WIDTH), lambda c: (0, c, 0)),
            pl.BlockSpec((batch, M_HEADS, M_DK, M_DV), lambda c: (0, 0, 0, 0)),
            pl.BlockSpec((batch, M_HEADS, M_DK), lambda c: (0, 0, 0)),
            pl.BlockSpec((batch, M_HEADS, LANES), lambda c: (0, 0, 0)),
        ],
        out_shape=[
            jax.ShapeDtypeStruct((batch, seq, M_WIDTH), BF16),
            jax.ShapeDtypeStruct((batch, M_HEADS, M_DK, M_DV), F32),
            jax.ShapeDtypeStruct((batch, M_HEADS, M_DK), F32),
            jax.ShapeDtypeStruct((batch, M_HEADS, LANES), F32),
        ],
        scratch_shapes=[pltpu.VMEM((batch, M_HEADS, M_DK, AUG), F32)],
        compiler_params=_params(("arbitrary",)),
        name="mlstm_prompt",
    )(zm, ktm, zm, gcm, grm, *per_prompt_args)


def _group_max(x, size):
    n = x.shape[-1]
    lane = lax.broadcasted_iota(jnp.int32, x.shape, x.ndim - 1)
    k = 1
    while k < size:
        partner = jnp.where((lane & k) == 0, pltpu.roll(x, n - k, x.ndim - 1), pltpu.roll(x, k, x.ndim - 1))
        x = jnp.maximum(x, partner)
        k *= 2
    return x


def _mlstm_sample_kernel(dec_seq, q_ref, kt_ref, v_ref, gc_ref, gr_ref, mcol_ref, mrow_ref, cin_ref, nin_ref,
                         h_ref, c_ref, n_ref, m_ref):
    l = q_ref.shape[0]
    n_seq = l // dec_seq
    shift = dec_seq.bit_length() - 1
    q, kt, v = q_ref[...], kt_ref[...], v_ref[...]
    assert l == LANES
    gr, gc = gr_ref[0], gc_ref[0]
    ig_row, b_row, r_row = gr[G_IG:G_IG + 1, :], gr[G_B:G_B + 1, :], gr[G_R:G_R + 1, :]
    b_col = gc[0]
    m_col, m_row = mcol_ref[0], mrow_ref[0]

    t_idx = lax.broadcasted_iota(jnp.int32, (l, l), 0)
    s_idx = lax.broadcasted_iota(jnp.int32, (l, l), 1)
    t_seq = t_idx >> shift
    logw = jnp.where((t_seq == (s_idx >> shift)) & (s_idx <= t_idx), b_col - b_row + ig_row, NEG)
    inter = b_col + m_col
    m_t = jnp.maximum(inter, gc[1])
    w_inter = jnp.exp(inter - m_t)
    s = _dot(q, kt) * jnp.exp(logw - m_t)

    e_row = r_row + ig_row
    b_last = b_row + r_row
    m_new = jnp.maximum(b_last + m_row, _group_max(e_row, dec_seq))
    decay = jnp.exp(b_last + m_row - m_new)
    kwt = (kt.astype(F32) * jnp.exp(e_row - m_new)).astype(BF16)

    seq_lane = s_idx == t_seq
    v_aug = jnp.concatenate([v, jnp.where(seq_lane, 1.0, 0.0).astype(BF16)], axis=1)
    pa = _dot(s.astype(BF16), v_aug)

    n_rows = jnp.concatenate([nin_ref[...], jnp.zeros((LANES - n_seq, M_DK), F32)], axis=0)
    n_t = jnp.concatenate([n_rows[:, 0:LANES].T, n_rows[:, LANES:2 * LANES].T], axis=0)
    lane_k = lax.broadcasted_iota(jnp.int32, (M_DK, LANES), 1)
    qf = q.astype(F32)
    qca_rows = []
    n_t_new = jnp.zeros((M_DK, LANES), F32)
    for i in range(n_seq):
        ca = jnp.concatenate([cin_ref[i, 0], jnp.where(lane_k == i, n_t, 0.0)], axis=1)
        rows = slice(i * dec_seq, (i + 1) * dec_seq)
        qca_rows.append(_dot(qf[rows, :].astype(BF16), ca.astype(BF16)))
        kw_i = jnp.where((lane_k >> shift) == i, kwt, jnp.zeros_like(kwt))
        ca_new = decay[:, i * dec_seq:i * dec_seq + 1] * ca + _dot(kw_i, v_aug)
        c_ref[i, 0] = ca_new[:, 0:M_DV]
        n_t_new = n_t_new + ca_new[:, M_DV:AUG]
    nd = jnp.concatenate([w_inter] * (AUG // LANES), axis=1) * jnp.concatenate(qca_rows, axis=0) + pa
    den = jnp.sum(jnp.where(seq_lane, nd[:, M_DV:AUG], 0.0), axis=-1, keepdims=True)
    rcp = 1.0 / jnp.maximum(jnp.abs(den), jnp.exp(-m_t[:, 0:1]))
    h_ref[...] = (nd[:, 0:M_DV] * rcp).astype(h_ref.dtype)
    n_ref[...] = jnp.concatenate([n_t_new[0:LANES, :].T, n_t_new[LANES:2 * LANES, :].T], axis=1)[0:n_seq, :]
    m_ref[0] = jnp.broadcast_to(m_new, (SUBLANES, l))


def _mlstm_sample_call(z, kt, gch, grh, m_col, m_row, c_in, n_in, row0, dec_seq):
    nb = c_in.shape[0]
    l = SEQ_PER_STEP * dec_seq
    blk0 = row0 // l
    return pl.pallas_call(
        functools.partial(_mlstm_sample_kernel, dec_seq),
        grid=(nb // SEQ_PER_STEP, M_HEADS),
        in_specs=[
            pl.BlockSpec((l, M_DK), lambda i, h: (blk0 + i, Z_Q * M_HEADS + h)),
            pl.BlockSpec((M_DK, l), lambda i, h: (h, blk0 + i)),
            pl.BlockSpec((l, M_DV), lambda i, h: (blk0 + i, Z_V * M_HEADS + h)),
            pl.BlockSpec((1, 2, l, LANES), lambda i, h: (h, 0, blk0 + i, 0)),
            pl.BlockSpec((1, SUBLANES, l), lambda i, h: (h, 0, blk0 + i)),
            pl.BlockSpec((1, l, LANES), lambda i, h: (h, i, 0)),
            pl.BlockSpec((1, 1, l), lambda i, h: (h, 0, i)),
            pl.BlockSpec((SEQ_PER_STEP, 1, M_DK, M_DV), lambda i, h: (i, h, 0, 0)),
            pl.BlockSpec((SEQ_PER_STEP, M_DK), lambda i, h: (i, h)),
        ],
        out_specs=[
            pl.BlockSpec((l, M_DV), lambda i, h: (i, h)),
            pl.BlockSpec((SEQ_PER_STEP, 1, M_DK, M_DV), lambda i, h: (i, h, 0, 0)),
            pl.BlockSpec((SEQ_PER_STEP, M_DK), lambda i, h: (i, h)),
            pl.BlockSpec((1, SUBLANES, l), lambda i, h: (h, 0, i)),
        ],
        out_shape=[
            jax.ShapeDtypeStruct((nb * dec_seq, M_WIDTH), BF16),
            jax.ShapeDtypeStruct(c_in.shape, F32),
            jax.ShapeDtypeStruct(n_in.shape, F32),
            jax.ShapeDtypeStruct((M_HEADS, SUBLANES, nb * dec_seq), F32),
        ],
        compiler_params=_params(("arbitrary", "arbitrary")),
        name="mlstm_sample",
    )(z, kt, z, gch, grh, m_col, m_row, c_in, n_in)


def _mix_kernel(n_prompt_tiles, hp_ref, hs_ref, so_ref, yb0_ref, yb1_ref, ga_ref, gb_ref, xp_ref, xs_ref,
                pa_ref, pb_ref, wo_ref, gffn_ref, o_ref, on_ref):
    i = pl.program_id(0)

    def body(h_ref, x_ref):
        h_a = (so_ref[...].astype(F32) * h_ref[...].astype(F32)).astype(BF16)
        a = _dot(h_a, pa_ref[...])
        b = _dot(jnp.concatenate([yb0_ref[...], yb1_ref[...]], axis=1), pb_ref[...])
        merged = ga_ref[...].astype(F32) * a + gb_ref[...].astype(F32) * b
        x1 = x_ref[...] + _dot(merged.astype(BF16), wo_ref[...])
        o_ref[...] = x1
        on_ref[...] = _rmsnorm(x1, gffn_ref[...]).astype(BF16)

    @pl.when(i < n_prompt_tiles)
    def _():
        body(hp_ref, xp_ref)

    @pl.when(i >= n_prompt_tiles)
    def _():
        body(hs_ref, xs_ref)


def _mix_call(hp, hs, z, yb0, yb1, xp, xs, pa, pb, wo, gffn, tm):
    tp, ts = xp.shape[0], xs.shape[0]
    npt, nst = tp // tm, ts // tm
    pi = lambda i: jnp.minimum(i, npt - 1)
    si = lambda i: jnp.maximum(i - npt, 0)
    const = lambda i: (0, 0)
    return pl.pallas_call(
        functools.partial(_mix_kernel, npt),
        grid=(npt + nst,),
        in_specs=[
            pl.BlockSpec((tm, M_WIDTH), lambda i: (pi(i), 0)),
            pl.BlockSpec((tm, M_WIDTH), lambda i: (si(i), 0)),
            pl.BlockSpec((tm, M_WIDTH), lambda i: (i, Z_SO)),
            pl.BlockSpec((tm, yb0.shape[1]), lambda i: (i, 0)),
            pl.BlockSpec((tm, yb1.shape[1]), lambda i: (i, 0)),
            pl.BlockSpec((tm, D_MODEL), lambda i: (i, Z_GA // 2)),
            pl.BlockSpec((tm, D_MODEL), lambda i: (i, Z_GB // 2)),
            pl.BlockSpec((tm, D_MODEL), lambda i: (pi(i), 0)),
            pl.BlockSpec((tm, D_MODEL), lambda i: (si(i), 0)),
            pl.BlockSpec((M_WIDTH, D_MODEL), const, pipeline_mode=pl.Buffered(1)),
            pl.BlockSpec((C_WIDTH, D_MODEL), const, pipeline_mode=pl.Buffered(1)),
            pl.BlockSpec((D_MODEL, D_MODEL), const, pipeline_mode=pl.Buffered(1)),
            pl.BlockSpec((1, D_MODEL), const),
        ],
        out_specs=[pl.BlockSpec((tm, D_MODEL), lambda i: (i, 0)), pl.BlockSpec((tm, D_MODEL), lambda i: (i, 0))],
        out_shape=[jax.ShapeDtypeStruct((tp + ts, D_MODEL), F32), jax.ShapeDtypeStruct((tp + ts, D_MODEL), BF16)],
        compiler_params=_params(("arbitrary",)),
        name="mix",
    )(hp, hs, z, yb0, yb1, z, z, xp, xs, pa, pb, wo, gffn)


def _ffn_kernel(n_prompt_tiles, x1_ref, xn_ref, wup_ref, wdn_ref, gfin_ref, yp_ref, ys_ref, acc_ref):
    i = pl.program_id(0)
    f = pl.program_id(1)

    def mlp_part():
        u = jnp.maximum(_dot(xn_ref[...], wup_ref[...]), 0.0)
        return _dot((u * u).astype(BF16), wdn_ref[...])

    @pl.when(f == 0)
    def _():
        acc_ref[...] = x1_ref[...] + mlp_part()

    @pl.when(f > 0)
    def _():
        acc_ref[...] += mlp_part()

    @pl.when(f == pl.num_programs(1) - 1)
    def _():
        y = _rmsnorm(acc_ref[...], gfin_ref[...])

        @pl.when(i < n_prompt_tiles)
        def _():
            yp_ref[...] = y

        @pl.when(i >= n_prompt_tiles)
        def _():
            ys_ref[...] = y


def _ffn_call(x1, xn, wup, wdn, gfin, n_prompt_tokens, tm, tf):
    t = x1.shape[0]
    npt = n_prompt_tokens // tm
    n_tiles = t // tm
    return pl.pallas_call(
        functools.partial(_ffn_kernel, npt),
        grid=(n_tiles, D_FF // tf),
        in_specs=[
            pl.BlockSpec((tm, D_MODEL), lambda i, f: (i, 0)),
            pl.BlockSpec((tm, D_MODEL), lambda i, f: (i, 0)),
            pl.BlockSpec((D_MODEL, tf), lambda i, f: (0, f)),
            pl.BlockSpec((tf, D_MODEL), lambda i, f: (f, 0)),
            pl.BlockSpec((1, D_MODEL), lambda i, f: (0, 0)),
        ],
        out_specs=[
            pl.BlockSpec((tm, D_MODEL), lambda i, f: (jnp.minimum(i, npt - 1), 0)),
            pl.BlockSpec((tm, D_MODEL), lambda i, f: (jnp.maximum(i - npt, 0), 0)),
        ],
        out_shape=[
            jax.ShapeDtypeStruct((n_prompt_tokens, D_MODEL), F32),
            jax.ShapeDtypeStruct((t - n_prompt_tokens, D_MODEL), F32),
        ],
        scratch_shapes=[pltpu.VMEM((tm, D_MODEL), F32)],
        compiler_params=_params(("arbitrary", "arbitrary")),
        name="ffn",
    )(x1, xn, wup, wdn, gfin)


def kernel(x_prompt, x_sample, state_mlstm_C, state_mlstm_n, state_mlstm_m, state_conv, meta_tokens,
           g_mix, w_in, b_in, w_conv, p_a, p_b, w_o, g_ffn, w_up, w_down, g_final):
    assert w_in.shape[0] == 1, "single-layer trunk"
    batch, seq, _ = x_prompt.shape
    dec_batch, dec_seq, _ = x_sample.shape
    assert dec_seq == SUBLANES and seq % 1024 == 0 and dec_batch % SEQ_PER_STEP == 0
    n_p, n_s = batch * seq, dec_batch * dec_seq

    wt = w_in[0].T
    bias = b_in[0]
    b1 = jnp.concatenate([bias[:OFF_GATES], bias[OFF_O:]])
    bg = bias[OFF_GATES:OFF_O][:, None]
    gmix = g_mix[0][None, :]

    xp = x_prompt.reshape(n_p, D_MODEL)
    xs = x_sample.reshape(n_s, D_MODEL)
    n_meta = meta_tokens.shape[0]

    sconv = state_conv[0].reshape(dec_batch, (CONV_W - 1) * C_WIDTH)
    xn, gch, grh, xnm, gcm, grm, yb0, cu6a, cu7a = _rms_conv_call(
        xp, xs, meta_tokens.astype(F32), gmix, wt, bg, b1[None, :], w_conv[0], sconv, dec_seq, seq, 512)
    z, kt, zm, ktm, w_up16, w_down16 = _inproj_call(xn, xnm, wt, b1[None, :], b1[:, None],
                                                    w_up[0], w_down[0], 1024)
    yb1, cu6b, cu7b, p_a16, p_b16, w_o16 = _conv_call(
        xn, xnm, n_meta, wt, b1[None, :], w_conv[0], sconv, p_a[0], p_b[0], w_o[0], n_p, seq, 1024)

    hp, c_p, n_pr, m_p = _mlstm_prompt_call(zm, ktm, gcm, grm, z, kt, gch, grh, batch, seq)
    m_tok = jnp.repeat(state_mlstm_m[0], dec_seq, axis=0).T
    hs, c_s, n_sm, m_s = _mlstm_sample_call(
        z, kt, gch, grh, jnp.broadcast_to(m_tok[:, :, None], m_tok.shape + (LANES,)), m_tok[:, None, :],
        state_mlstm_C[0],
        state_mlstm_n[0].reshape(dec_batch, M_HEADS * M_DK), n_p, dec_seq)

    x1, xn2 = _mix_call(hp.reshape(n_p, M_WIDTH), hs, z, yb0, yb1, xp, xs, p_a16, p_b16, w_o16,
                        g_ffn[0][None, :], 256)
    y_p, y_s = _ffn_call(x1, xn2, w_up16, w_down16, g_final[None, :], n_p, 512, 1024)

    per_seq = seq // SUBLANES
    first_s = n_p // SUBLANES
    pick = lambda rows: jnp.stack([jnp.concatenate([cu6a[rows], cu6b[rows]], axis=1),
                                   jnp.concatenate([cu7a[rows], cu7b[rows]], axis=1)], axis=1)
    cv_p = pick(slice(per_seq - 1, batch * per_seq, per_seq))
    cv_s = pick(slice(first_s, None))
    m_s = m_s[:, 0, dec_seq - 1::dec_seq].T
    return (y_p.reshape(batch, seq, D_MODEL), y_s.reshape(dec_batch, dec_seq, D_MODEL),
            c_p[None], n_pr[None], m_p[None, :, :, 0], cv_p[None],
            c_s[None], n_sm.reshape(dec_batch, M_HEADS, M_DK)[None], m_s[None], cv_s[None])
```

```python
import functools

import jax
import jax.numpy as jnp
from jax import lax
from jax.experimental import pallas as pl
from jax.experimental.pallas import tpu as pltpu

F32 = jnp.float32
BF16 = jnp.bfloat16

D_MODEL = 2048
N_META = 16
CHUNK = 128
M_HEADS = 4
M_DK = 256
M_DV = 256
M_WIDTH = M_HEADS * M_DV
C_WIDTH = 1024
CONV_W = 3
D_FF = 4 * D_MODEL
EPS = 1e-6
N_GATES = 2 * M_HEADS
OFF_Q = 0
OFF_K = OFF_Q + M_HEADS * M_DK
OFF_V = OFF_K + M_HEADS * M_DK
OFF_GATES = OFF_V + M_WIDTH
OFF_O = OFF_GATES + N_GATES
OFF_U = OFF_O + M_WIDTH
OFF_C = OFF_U + C_WIDTH
OFF_B = OFF_C + C_WIDTH
OFF_GA = OFF_B + C_WIDTH
OFF_GB = OFF_GA + D_MODEL

LANES = 128
SUBLANES = 8
VMEM_LIMIT_BYTES = 60000 * 1024

TN = 1024
Z_Q, Z_V = 0, 1
Z_GA, Z_GB, Z_SO = 0, 2, 4
ZG_COLS = 5 * TN
CQ = 256
AUG = M_DV + LANES
SEQ_PER_STEP = CHUNK // SUBLANES
G_IG, G_B, G_R = 0, 1, 2

NEG = -1e30
NT_DIMS = (((1,), (1,)), ((), ()))


def _params(semantics):
    return pltpu.CompilerParams(dimension_semantics=semantics, vmem_limit_bytes=VMEM_LIMIT_BYTES)


def _rmsnorm(x, g):
    y = x * lax.rsqrt(jnp.mean(x * x, axis=-1, keepdims=True) + EPS)
    return y * g


def _log_sigmoid(x):
    return jnp.minimum(x, 0.0) - jnp.log1p(jnp.exp(-jnp.abs(x)))


def _dot(a, b):
    return jnp.dot(a, b, preferred_element_type=F32)


def _dot_nt(a, b):
    return lax.dot_general(a, b, NT_DIMS, preferred_element_type=F32)


def _gate_prep(xn, wg_ref, bg_ref, blk, n_valid, gch_ref, grh_ref):
    tm = xn.shape[0]
    wg = wg_ref[...].astype(BF16)
    wg = jnp.concatenate([wg, jnp.zeros((LANES - N_GATES, wg.shape[1]), BF16)], axis=0)
    g = _dot_nt(wg, xn)[0:SUBLANES, :] + bg_ref[...]
    row = lax.broadcasted_iota(jnp.int32, (SUBLANES, tm), 0)
    lane = lax.broadcasted_iota(jnp.int32, (SUBLANES, tm), 1)
    a = jnp.where(row < M_HEADS, g, _log_sigmoid(g))
    if n_valid < tm:
        a = jnp.where(lane < n_valid, a, jnp.where(row < M_HEADS, NEG, 0.0))
    pos = lane & (blk - 1)
    n_steps_blk = blk if isinstance(blk, int) else LANES

    def scan(x, op, fill, reverse=False):
        shift = 1
        while shift < n_steps_blk:
            if reverse:
                x = op(x, jnp.where(pos < blk - shift, pltpu.roll(x, tm - shift, 1), fill))
            else:
                x = op(x, jnp.where(pos >= shift, pltpu.roll(x, shift, 1), fill))
            shift *= 2
        return x

    pre = scan(a, jnp.add, 0.0)
    suf = scan(a, jnp.add, 0.0, reverse=True) - a
    b_up = pltpu.roll(pre, M_HEADS, 0)
    m_in = b_up + scan(a - b_up, jnp.maximum, -3e38)
    for h in range(M_HEADS):
        grh_ref[h] = jnp.where(
            row == G_IG, pltpu.roll(a, (G_IG - h) % SUBLANES, 0),
            jnp.where(row == G_B, pltpu.roll(pre, (G_B - M_HEADS - h) % SUBLANES, 0),
                      jnp.where(row == G_R, pltpu.roll(suf, (G_R - M_HEADS - h) % SUBLANES, 0), 0.0)))
        for c in range(tm // LANES):
            cs = slice(c * LANES, (c + 1) * LANES)
            gch_ref[h, 0, cs, :] = jnp.broadcast_to(b_up[h:h + 1, cs], (LANES, LANES)).T
            gch_ref[h, 1, cs, :] = jnp.broadcast_to(m_in[h:h + 1, cs], (LANES, LANES)).T


S_Q, S_K, S_V = 0, 1, 2


def _cast_blocks(pairs):
    for src_ref, dst_ref in pairs:
        dst_ref[...] = src_ref[...].astype(BF16)


def _cast_specs(w, n_blocks, step):
    rows, cols = w.shape
    blk = lambda *g: (jnp.minimum(step(*g), n_blocks - 1), 0)
    spec = pl.BlockSpec((rows // n_blocks, cols), blk)
    return spec, spec, jax.ShapeDtypeStruct(w.shape, BF16)


def _qkv_kernel(xn_ref, xnm_ref, wt_ref, b_ref, bcol_ref, wdn_ref, z_ref, kt_ref, zm_ref, ktm_ref, wdn16_ref,
                w_ref):
    s = pl.program_id(0)
    casts = ((wdn_ref, wdn16_ref),)
    k_scale = M_DK ** -0.5

    @pl.when(pl.program_id(1) == 0)
    def _():
        w_ref[...] = wt_ref[...].astype(BF16)

        @pl.when(s != S_K)
        def _():
            zm_ref[...] = (_dot_nt(xnm_ref[...], w_ref[...]) + b_ref[...]).astype(BF16)

        @pl.when(s == S_K)
        def _():
            ktm_ref[...] = ((_dot_nt(w_ref[...], xnm_ref[...]) + bcol_ref[...]) * k_scale).astype(BF16)

    @pl.when(s != S_K)
    def _():
        _cast_blocks(casts)
        z_ref[...] = (_dot_nt(xn_ref[...], w_ref[...]) + b_ref[...]).astype(BF16)

    @pl.when(s == S_K)
    def _():
        _cast_blocks(casts)
        kt_ref[...] = ((_dot_nt(w_ref[...], xn_ref[...]) + bcol_ref[...]) * k_scale).astype(BF16)


def _qkv_call(xn, xnm, wt, b1, b1col, w_down, tm):
    t = xn.shape[0]
    assert t % tm == 0
    n_m = t // tm
    rows_m = xnm.shape[0]
    n_cast = 16
    assert n_cast <= 3 * n_m
    dn_in, dn_out, dn_shape = _cast_specs(w_down, n_cast, lambda s, m: s * n_m + m)
    z_col = lambda s: jnp.where(s == S_V, Z_V, Z_Q)
    z_row = lambda s, m: jnp.where(s == S_K, n_m - 1, m)
    kt_blk = lambda s, m: jnp.where(s == S_K, m, jnp.where(s < S_K, 0, n_m - 1))
    return pl.pallas_call(
        _qkv_kernel,
        grid=(3, n_m),
        in_specs=[
            pl.BlockSpec((tm, D_MODEL), lambda s, m: (m, 0)),
            pl.BlockSpec((rows_m, D_MODEL), lambda s, m: (0, 0)),
            pl.BlockSpec((pl.Element(TN), pl.Element(D_MODEL)),
                         lambda s, m: (pl.multiple_of(s * TN, SUBLANES), 0)),
            pl.BlockSpec((1, TN), lambda s, m: (0, s)),
            pl.BlockSpec((TN, 1), lambda s, m: (s, 0)),
            dn_in,
        ],
        out_specs=[
            pl.BlockSpec((tm, TN), lambda s, m: (z_row(s, m), z_col(s))),
            pl.BlockSpec((TN, tm), lambda s, m: (0, kt_blk(s, m))),
            pl.BlockSpec((rows_m, TN), lambda s, m: (0, z_col(s))),
            pl.BlockSpec((TN, rows_m), lambda s, m: (0, 0)),
            dn_out,
        ],
        out_shape=[
            jax.ShapeDtypeStruct((t, 2 * TN), BF16),
            jax.ShapeDtypeStruct((TN, t), BF16),
            jax.ShapeDtypeStruct((rows_m, 2 * TN), BF16),
            jax.ShapeDtypeStruct((TN, rows_m), BF16),
            dn_shape,
        ],
        scratch_shapes=[pltpu.VMEM((TN, D_MODEL), BF16)],
        compiler_params=_params(("arbitrary", "arbitrary")),
        name="inproj_qkv",
    )(xn, xnm, wt, b1, b1col, w_down)


def _conv_prologue(m, is_prompt, tiles_per_seq, n_meta, xnm_ref, w_refs, b_refs, s0_ref, s1_ref, scratch,
                   make_xnm=None):
    wu_ref, wc_ref, wb_ref = w_refs
    bu_ref, bc_ref, _ = b_refs
    w3_ref, _, h1_ref, h2_ref, carry_ref, mtail_ref = scratch
    n_slab, tm, _ = h1_ref.shape
    n_seq = tm // SUBLANES
    slabs = [(k, slice(k * LANES, (k + 1) * LANES)) for k in range(n_slab)]
    seq_row = lambda r: pl.ds(r, n_seq, stride=SUBLANES)

    @pl.when(m == 0)
    def _():
        if make_xnm is not None:
            make_xnm()
        w3_ref[0] = wu_ref[...].astype(BF16)
        w3_ref[1] = wc_ref[...].astype(BF16)
        w3_ref[2] = wb_ref[...].astype(BF16)
        h1_ref[...] = jnp.zeros_like(h1_ref)
        h2_ref[...] = jnp.zeros_like(h2_ref)
        xnm = xnm_ref[...]
        cu_m = (_dot_nt(xnm, w3_ref[1]) + bc_ref[...]) * (_dot_nt(xnm, w3_ref[0]) + bu_ref[...])
        mtail_ref[...] = cu_m[n_meta - SUBLANES:n_meta, :]

    @pl.when(is_prompt)
    def _():
        first = (m % tiles_per_seq) == 0
        p6 = jnp.where(first, mtail_ref[6:7, :], carry_ref[6:7, :])
        p7 = jnp.where(first, mtail_ref[7:8, :], carry_ref[7:8, :])
        for k, ks in slabs:
            h2_ref[k, 0:1, :] = p6[:, ks]
            h2_ref[k, 1:2, :] = p7[:, ks]
            h1_ref[k, 0:1, :] = p7[:, ks]

    @pl.when(jnp.logical_not(is_prompt))
    def _():
        for k, ks in slabs:
            h2_ref[k, seq_row(0), :] = s0_ref[:, ks]
            h2_ref[k, seq_row(1), :] = s1_ref[:, ks]
            h1_ref[k, seq_row(0), :] = s1_ref[:, ks]


def _conv_main(xn, is_prompt, b_refs, wconv_ref, yb_ref, cu6_ref, cu7_ref, scratch):
    bu_ref, bc_ref, bb_ref = b_refs
    w3_ref, cu_ref, h1_ref, h2_ref, carry_ref, _ = scratch
    tm, cq = yb_ref.shape
    n_seq = tm // SUBLANES
    slabs = [(k, slice(k * LANES, (k + 1) * LANES)) for k in range(cq // LANES)]
    seq_row = lambda r: pl.ds(r, n_seq, stride=SUBLANES)
    zu = _dot_nt(xn, w3_ref[0]) + bu_ref[...]
    zc = _dot_nt(xn, w3_ref[1]) + bc_ref[...]
    zb = _dot_nt(xn, w3_ref[2]) + bb_ref[...]
    cu = zc * zu
    pos = lax.broadcasted_iota(jnp.int32, (tm, cq), 0) & jnp.where(is_prompt, tm - 1, SUBLANES - 1)
    h1 = jnp.concatenate([h1_ref[k] for k, _ in slabs], axis=1)
    h2 = jnp.concatenate([h2_ref[k] for k, _ in slabs], axis=1)
    x1 = jnp.where(pos >= 1, pltpu.roll(cu, 1, 0), h1)
    x2 = jnp.where(pos >= 2, pltpu.roll(cu, 2, 0), h2)
    w0, w1, w2 = wconv_ref[0:1, :], wconv_ref[1:2, :], wconv_ref[2:3, :]
    yb_ref[...] = (zb * ((w0 * x2 + w1 * x1) + w2 * cu)).astype(BF16)

    carry_ref[...] = cu[tm - SUBLANES:tm, :]
    for k, ks in slabs:
        cu_ref[k] = cu[:, ks]
        cu6_ref[:, ks] = cu_ref[k, seq_row(6), :]
        cu7_ref[:, ks] = cu_ref[k, seq_row(7), :]


def _conv_kernel(n_prompt_tiles, tiles_per_seq, n_meta, n_cast, xn_ref, xnm_ref, wu_ref, wc_ref, wb_ref,
                 bu_ref, bc_ref, bb_ref, wconv_ref, s0_ref, s1_ref, *refs):
    cast_in, (yb_ref, cu6_ref, cu7_ref) = refs[:n_cast], refs[n_cast:n_cast + 3]
    cast_out, scratch = refs[n_cast + 3:2 * n_cast + 3], refs[2 * n_cast + 3:]
    m = pl.program_id(1)
    is_prompt = m < n_prompt_tiles
    b_refs = (bu_ref, bc_ref, bb_ref)
    _conv_prologue(m, is_prompt, tiles_per_seq, n_meta, xnm_ref, (wu_ref, wc_ref, wb_ref), b_refs,
                   s0_ref, s1_ref, scratch)
    _cast_blocks(tuple(zip(cast_in, cast_out)))
    _conv_main(xn_ref[...], is_prompt, b_refs, wconv_ref, yb_ref, cu6_ref, cu7_ref, scratch)


def _rms_conv_kernel(n_prompt_tiles, tiles_per_seq, dec_seq, xp_ref, xs_ref, xm_ref, g_ref, wg_ref, bg_ref,
                     wu_ref, wc_ref, wb_ref, bu_ref, bc_ref, bb_ref, wconv_ref, s0_ref, s1_ref,
                     xn_ref, gch_ref, grh_ref, xnm_ref, gchm_ref, grhm_ref, yb_ref, cu6_ref, cu7_ref, *scratch):
    m = pl.program_id(0)
    is_prompt = m < n_prompt_tiles
    n_meta = xm_ref.shape[0]
    b_refs = (bu_ref, bc_ref, bb_ref)

    def make_xnm():
        xm = jnp.concatenate([xm_ref[...], jnp.zeros((CHUNK - n_meta, D_MODEL), F32)], axis=0)
        xnm = _rmsnorm(xm, g_ref[...]).astype(BF16)
        xnm_ref[...] = xnm
        _gate_prep(xnm, wg_ref, bg_ref, CHUNK, n_meta, gchm_ref, grhm_ref)

    _conv_prologue(m, is_prompt, tiles_per_seq, n_meta, xnm_ref, (wu_ref, wc_ref, wb_ref), b_refs,
                   s0_ref, s1_ref, scratch, make_xnm)
    xn = _rmsnorm(jnp.where(is_prompt, xp_ref[...], xs_ref[...]), g_ref[...]).astype(BF16)
    xn_ref[...] = xn
    _gate_prep(xn, wg_ref, bg_ref, jnp.where(is_prompt, CHUNK, dec_seq), xn.shape[0], gch_ref, grh_ref)
    _conv_main(xn, is_prompt, b_refs, wconv_ref, yb_ref, cu6_ref, cu7_ref, scratch)


def _conv_specs(c0, npt, tm, cm):
    n_seq = tm // SUBLANES
    n_slab = CQ // LANES
    chan = lambda *g: c0 + cm(*g)[0]
    tile = lambda *g: cm(*g)[1]
    w_spec = lambda off: pl.BlockSpec((pl.Element(CQ), pl.Element(D_MODEL)),
                                      lambda *g: (pl.multiple_of(off + chan(*g) * CQ, SUBLANES), 0))
    b_spec = lambda off: pl.BlockSpec((1, CQ), lambda *g: (0, (off - N_GATES) // CQ + chan(*g)))
    s_spec = lambda tok: pl.BlockSpec(
        (n_seq, CQ), lambda *g: (jnp.maximum(tile(*g) - npt, 0), tok * (C_WIDTH // CQ) + chan(*g)))
    in_specs = [w_spec(OFF_U), w_spec(OFF_C), w_spec(OFF_B), b_spec(OFF_U), b_spec(OFF_C), b_spec(OFF_B),
                pl.BlockSpec((CONV_W, CQ), lambda *g: (0, chan(*g))), s_spec(0), s_spec(1)]
    out_specs = [pl.BlockSpec((tm, CQ), lambda *g: (tile(*g), cm(*g)[0])),
                 pl.BlockSpec((n_seq, CQ), lambda *g: (tile(*g), cm(*g)[0])),
                 pl.BlockSpec((n_seq, CQ), lambda *g: (tile(*g), cm(*g)[0]))]
    scratch = [pltpu.VMEM((3, CQ, D_MODEL), BF16),
               pltpu.VMEM((n_slab, tm, LANES), F32),
               pltpu.VMEM((n_slab, tm, LANES), F32),
               pltpu.VMEM((n_slab, tm, LANES), F32),
               pltpu.VMEM((SUBLANES, CQ), F32),
               pltpu.VMEM((SUBLANES, CQ), F32)]
    return in_specs, out_specs, scratch


def _conv_out_shapes(t, n_blocks):
    return [jax.ShapeDtypeStruct((t, n_blocks * CQ), BF16),
            jax.ShapeDtypeStruct((t // SUBLANES, n_blocks * CQ), F32),
            jax.ShapeDtypeStruct((t // SUBLANES, n_blocks * CQ), F32)]


def _rms_conv_call(xp, xs, x_meta, g, wt, bg, b1, wconv, sconv, dec_seq, seq_len, tm):
    tp, ts = xp.shape[0], xs.shape[0]
    t = tp + ts
    assert tm & (tm - 1) == 0 and seq_len % tm == 0 and x_meta.shape[0] >= SUBLANES
    npt, nst = tp // tm, ts // tm
    conv_in, conv_out, scratch = _conv_specs(0, npt, tm, lambda m: (0, m))
    return pl.pallas_call(
        functools.partial(_rms_conv_kernel, npt, seq_len // tm, dec_seq),
        grid=(npt + nst,),
        in_specs=[
            pl.BlockSpec((tm, D_MODEL), lambda m: (jnp.minimum(m, npt - 1), 0)),
            pl.BlockSpec((tm, D_MODEL), lambda m: (jnp.maximum(m - npt, 0), 0)),
            pl.BlockSpec(x_meta.shape, lambda m: (0, 0)),
            pl.BlockSpec((1, D_MODEL), lambda m: (0, 0)),
            pl.BlockSpec((N_GATES, D_MODEL), lambda m: (OFF_GATES // N_GATES, 0)),
            pl.BlockSpec((N_GATES, 1), lambda m: (0, 0)),
        ] + conv_in,
        out_specs=[
            pl.BlockSpec((tm, D_MODEL), lambda m: (m, 0)),
            pl.BlockSpec((M_HEADS, 2, tm, LANES), lambda m: (0, 0, m, 0)),
            pl.BlockSpec((M_HEADS, SUBLANES, tm), lambda m: (0, 0, m)),
            pl.BlockSpec((CHUNK, D_MODEL), lambda m: (0, 0)),
            pl.BlockSpec((M_HEADS, 2, CHUNK, LANES), lambda m: (0, 0, 0, 0)),
            pl.BlockSpec((M_HEADS, SUBLANES, CHUNK), lambda m: (0, 0, 0)),
        ] + conv_out,
        out_shape=[
            jax.ShapeDtypeStruct((t, D_MODEL), BF16),
            jax.ShapeDtypeStruct((M_HEADS, 2, t, LANES), F32),
            jax.ShapeDtypeStruct((M_HEADS, SUBLANES, t), F32),
            jax.ShapeDtypeStruct((CHUNK, D_MODEL), BF16),
            jax.ShapeDtypeStruct((M_HEADS, 2, CHUNK, LANES), F32),
            jax.ShapeDtypeStruct((M_HEADS, SUBLANES, CHUNK), F32),
        ] + _conv_out_shapes(t, 1),
        scratch_shapes=scratch,
        compiler_params=_params(("arbitrary",)),
        name="rms_conv",
    )(xp, xs, x_meta, g, wt, bg, wt, wt, wt, b1, b1, b1, wconv, sconv, sconv)


def _conv_call(xn, xnm, n_meta, wt, b1, wconv, sconv, cast_ws, n_prompt_tokens, seq_len, tm):
    t = xn.shape[0]
    assert tm & (tm - 1) == 0 and seq_len % tm == 0 and n_meta >= SUBLANES
    npt = n_prompt_tokens // tm
    n_m = t // tm
    n_blocks = C_WIDTH // CQ - 1
    n_cast = 16
    assert n_cast <= n_blocks * n_m
    casts = [_cast_specs(w, n_cast, lambda c, m: c * n_m + m) for w in cast_ws]
    conv_in, conv_out, scratch = _conv_specs(1, npt, tm, lambda c, m: (c, m))
    return pl.pallas_call(
        functools.partial(_conv_kernel, npt, seq_len // tm, n_meta, len(cast_ws)),
        grid=(n_blocks, n_m),
        in_specs=[
            pl.BlockSpec((tm, D_MODEL), lambda c, m: (m, 0)),
            pl.BlockSpec(xnm.shape, lambda c, m: (0, 0)),
        ] + conv_in + [cs[0] for cs in casts],
        out_specs=conv_out + [cs[1] for cs in casts],
        out_shape=_conv_out_shapes(t, n_blocks) + [cs[2] for cs in casts],
        scratch_shapes=scratch,
        compiler_params=_params(("arbitrary", "arbitrary")),
        name="inproj_conv",
    )(xn, xnm, wt, wt, wt, b1, b1, b1, wconv, sconv, sconv, *cast_ws)


def _prompt_head(q, kt, v, gr, gc, ca, m_st):
    l = q.shape[0]
    assert l == LANES
    ig_row, b_row, r_row = gr[G_IG:G_IG + 1, :], gr[G_B:G_B + 1, :], gr[G_R:G_R + 1, :]
    b_col = gc[0]
    v_aug = jnp.concatenate([v, jnp.ones((l, LANES), BF16)], axis=1)
    t_idx = lax.broadcasted_iota(jnp.int32, (l, l), 0)
    s_idx = lax.broadcasted_iota(jnp.int32, (l, l), 1)
    logw = jnp.where(s_idx <= t_idx, b_col - b_row + ig_row, NEG)
    inter = b_col + m_st
    m_t = jnp.maximum(inter, gc[1])
    w_inter = jnp.exp(inter - m_t)
    s = _dot(q, kt) * jnp.exp(logw - m_t)
    nd = (jnp.concatenate([w_inter] * (AUG // LANES), axis=1) * _dot(q, ca.astype(BF16))
          + _dot(s.astype(BF16), v_aug))
    rcp = 1.0 / jnp.maximum(jnp.abs(nd[:, M_DV:AUG]), jnp.exp(-m_t))
    h = nd[:, 0:M_DV] * jnp.concatenate([rcp] * (M_DV // LANES), axis=1)
    m_new = m_t[l - 1:l, 0:1]
    decay = jnp.exp(b_col[l - 1:l, 0:1] + m_st - m_new)
    kwt = (kt.astype(F32) * jnp.exp(r_row + ig_row - m_new)).astype(BF16)
    return h, decay * ca + _dot(kwt, v_aug), m_new


def _mlstm_prompt_kernel(batch, qm_ref, ktm_ref, vm_ref, gcm_ref, grm_ref, *refs):
    ins, (h_ref, c_ref, n_ref, m_ref, ca_ref) = refs[:5 * batch], refs[5 * batch:]
    heads = [(hd, slice(hd * M_DK, (hd + 1) * M_DK)) for hd in range(M_HEADS)]

    @pl.when(pl.program_id(0) == 0)
    def _():
        for hd, sl in heads:
            _, ca_new, m_new = _prompt_head(qm_ref[:, sl], ktm_ref[sl, :], vm_ref[:, sl], grm_ref[hd],
                                            gcm_ref[hd], jnp.zeros((M_DK, AUG), F32), jnp.zeros((1, 1), F32))
            for b in range(batch):
                ca_ref[b, hd] = ca_new
                m_ref[b, hd:hd + 1, :] = jnp.broadcast_to(m_new, (1, LANES))

    for b in range(batch):
        q_ref, kt_ref, v_ref, gc_ref, gr_ref = ins[5 * b:5 * b + 5]
        for hd, sl in heads:
            h, ca_new, m_new = _prompt_head(q_ref[:, sl], kt_ref[sl, :], v_ref[:, sl], gr_ref[hd], gc_ref[hd],
                                            ca_ref[b, hd], m_ref[b, hd:hd + 1, 0:1])
            ca_ref[b, hd] = ca_new
            m_ref[b, hd:hd + 1, :] = jnp.broadcast_to(m_new, (1, LANES))
            h_ref[b, :, sl] = h.astype(h_ref.dtype)

    @pl.when(pl.program_id(0) == pl.num_programs(0) - 1)
    def _():
        for b in range(batch):
            for hd, _ in heads:
                ca = ca_ref[b, hd]
                c_ref[b, hd] = ca[:, 0:M_DV]
                n_t = ca[:, M_DV:AUG]
                n_ref[b, hd:hd + 1, :] = jnp.concatenate(
                    [n_t[k * LANES:(k + 1) * LANES, :].T[0:1, :] for k in range(M_DK // LANES)], axis=1)


def _mlstm_prompt_call(zm, ktm, gcm, grm, z, kt, gch, grh, batch, seq):
    nc = seq // CHUNK
    per_prompt_specs, per_prompt_args = [], []
    for b in range(batch):
        row = functools.partial(lambda b, c: b * nc + c, b)
        per_prompt_specs += [
            pl.BlockSpec((CHUNK, M_WIDTH), lambda c, row=row: (row(c), Z_Q)),
            pl.BlockSpec((M_WIDTH, CHUNK), lambda c, row=row: (0, row(c))),
            pl.BlockSpec((CHUNK, M_WIDTH), lambda c, row=row: (row(c), Z_V)),
            pl.BlockSpec((M_HEADS, 2, CHUNK, LANES), lambda c, row=row: (0, 0, row(c), 0)),
            pl.BlockSpec((M_HEADS, SUBLANES, CHUNK), lambda c, row=row: (0, 0, row(c))),
        ]
        per_prompt_args += [z, kt, z, gch, grh]
    return pl.pallas_call(
        functools.partial(_mlstm_prompt_kernel, batch),
        grid=(nc,),
        in_specs=[
            pl.BlockSpec((CHUNK, M_WIDTH), lambda c: (0, 0)),
            pl.BlockSpec((M_WIDTH, CHUNK), lambda c: (0, 0)),
            pl.BlockSpec((CHUNK, M_WIDTH), lambda c: (0, 1)),
            pl.BlockSpec((M_HEADS, 2, CHUNK, LANES), lambda c: (0, 0, 0, 0)),
            pl.BlockSpec((M_HEADS, SUBLANES, CHUNK), lambda c: (0, 0, 0)),
        ] + per_prompt_specs,
        out_specs=[
            pl.BlockSpec((batch, CHUNK, M_WIDTH), lambda c: (0, c, 0)),
            pl.BlockSpec((batch, M_HEADS, M_DK, M_DV), lambda c: (0, 0, 0, 0)),
            pl.BlockSpec((batch, M_HEADS, M_DK), lambda c: (0, 0, 0)),
            pl.BlockSpec((batch, M_HEADS, LANES), lambda c: (0, 0, 0)),
        ],
        out_shape=[
            jax.ShapeDtypeStruct((batch, seq, M_WIDTH), BF16),
            jax.ShapeDtypeStruct((batch, M_HEADS, M_DK, M_DV), F32),
            jax.ShapeDtypeStruct((batch, M_HEADS, M_DK), F32),
            jax.ShapeDtypeStruct((batch, M_HEADS, LANES), F32),
        ],
        scratch_shapes=[pltpu.VMEM((batch, M_HEADS, M_DK, AUG), F32)],
        compiler_params=_params(("arbitrary",)),
        name="mlstm_prompt",
    )(zm, ktm, zm, gcm, grm, *per_prompt_args)


def _group_max(x, size):
    n = x.shape[-1]
    lane = lax.broadcasted_iota(jnp.int32, x.shape, x.ndim - 1)
    k = 1
    while k < size:
        partner = jnp.where((lane & k) == 0, pltpu.roll(x, n - k, x.ndim - 1), pltpu.roll(x, k, x.ndim - 1))
        x = jnp.maximum(x, partner)
        k *= 2
    return x


def _mlstm_sample_kernel(dec_seq, q_ref, kt_ref, v_ref, gc_ref, gr_ref, mcol_ref, mrow_ref, cin_ref, nin_ref,
                         h_ref, c_ref, n_ref, m_ref):
    l = q_ref.shape[0]
    n_seq = l // dec_seq
    shift = dec_seq.bit_length() - 1
    q, kt, v = q_ref[...], kt_ref[...], v_ref[...]
    assert l == LANES
    gr, gc = gr_ref[0], gc_ref[0]
    ig_row, b_row, r_row = gr[G_IG:G_IG + 1, :], gr[G_B:G_B + 1, :], gr[G_R:G_R + 1, :]
    b_col = gc[0]
    m_col, m_row = mcol_ref[0], mrow_ref[0]

    t_idx = lax.broadcasted_iota(jnp.int32, (l, l), 0)
    s_idx = lax.broadcasted_iota(jnp.int32, (l, l), 1)
    t_seq = t_idx >> shift
    logw = jnp.where((t_seq == (s_idx >> shift)) & (s_idx <= t_idx), b_col - b_row + ig_row, NEG)
    inter = b_col + m_col
    m_t = jnp.maximum(inter, gc[1])
    w_inter = jnp.exp(inter - m_t)
    s = _dot(q, kt) * jnp.exp(logw - m_t)

    e_row = r_row + ig_row
    b_last = b_row + r_row
    m_new = jnp.maximum(b_last + m_row, _group_max(e_row, dec_seq))
    decay = jnp.exp(b_last + m_row - m_new)
    kwt = (kt.astype(F32) * jnp.exp(e_row - m_new)).astype(BF16)

    seq_lane = s_idx == t_seq
    v_aug = jnp.concatenate([v, jnp.where(seq_lane, 1.0, 0.0).astype(BF16)], axis=1)
    pa = _dot(s.astype(BF16), v_aug)

    n_rows = jnp.concatenate([nin_ref[...], jnp.zeros((LANES - n_seq, M_DK), F32)], axis=0)
    n_t = jnp.concatenate([n_rows[:, 0:LANES].T, n_rows[:, LANES:2 * LANES].T], axis=0)
    lane_k = lax.broadcasted_iota(jnp.int32, (M_DK, LANES), 1)
    qf = q.astype(F32)
    qca_rows = []
    n_t_new = jnp.zeros((M_DK, LANES), F32)
    for i in range(n_seq):
        ca = jnp.concatenate([cin_ref[i, 0], jnp.where(lane_k == i, n_t, 0.0)], axis=1)
        rows = slice(i * dec_seq, (i + 1) * dec_seq)
        qca_rows.append(_dot(qf[rows, :].astype(BF16), ca.astype(BF16)))
        kw_i = jnp.where((lane_k >> shift) == i, kwt, jnp.zeros_like(kwt))
        ca_new = decay[:, i * dec_seq:i * dec_seq + 1] * ca + _dot(kw_i, v_aug)
        c_ref[i, 0] = ca_new[:, 0:M_DV]
        n_t_new = n_t_new + ca_new[:, M_DV:AUG]
    nd = jnp.concatenate([w_inter] * (AUG // LANES), axis=1) * jnp.concatenate(qca_rows, axis=0) + pa
    den = jnp.sum(jnp.where(seq_lane, nd[:, M_DV:AUG], 0.0), axis=-1, keepdims=True)
    rcp = 1.0 / jnp.maximum(jnp.abs(den), jnp.exp(-m_t[:, 0:1]))
    h_ref[...] = (nd[:, 0:M_DV] * rcp).astype(h_ref.dtype)
    n_ref[...] = jnp.concatenate([n_t_new[0:LANES, :].T, n_t_new[LANES:2 * LANES, :].T], axis=1)[0:n_seq, :]
    m_ref[0] = jnp.broadcast_to(m_new, (SUBLANES, l))


def _gates_sample_kernel(dec_seq, xn_ref, wt_ref, b_ref, *refs):
    sample_in, z_ref, sample_out, w_ref = refs[:9], refs[9], refs[10:14], refs[14]

    @pl.when(pl.program_id(1) == 0)
    def _():
        w_ref[...] = wt_ref[...].astype(BF16)

    z = _dot_nt(xn_ref[...], w_ref[...]) + b_ref[...]
    z_ref[...] = (0.5 * jnp.tanh(0.5 * z) + 0.5).astype(BF16)
    _mlstm_sample_kernel(dec_seq, *sample_in, *sample_out)


def _gates_sample_call(xn, wt, b1, zqv, kt, gch, grh, m_col, m_row, c_in, n_in, row0, dec_seq, tm):
    t = xn.shape[0]
    n_m = t // tm
    nb = c_in.shape[0]
    l = SEQ_PER_STEP * dec_seq
    blk0 = row0 // l
    n_guest = (nb // SEQ_PER_STEP) * M_HEADS
    assert n_guest <= 5 * n_m
    w_off = lambda s: jnp.where(s == 0, OFF_O, OFF_GA + (s - 1) * TN)
    b_blk = lambda s: jnp.where(s == 0, OFF_O - N_GATES, OFF_GA - N_GATES + (s - 1) * TN) // TN
    z_col = lambda s: jnp.where(s == 0, Z_SO, s - 1)
    guest = lambda s, m: jnp.minimum(s * n_m + m, n_guest - 1)
    gi = lambda s, m: guest(s, m) // M_HEADS
    gh = lambda s, m: guest(s, m) % M_HEADS
    return pl.pallas_call(
        functools.partial(_gates_sample_kernel, dec_seq),
        grid=(5, n_m),
        in_specs=[
            pl.BlockSpec((tm, D_MODEL), lambda s, m: (m, 0)),
            pl.BlockSpec((pl.Element(TN), pl.Element(D_MODEL)),
                         lambda s, m: (pl.multiple_of(w_off(s), SUBLANES), 0)),
            pl.BlockSpec((1, TN), lambda s, m: (0, b_blk(s))),
            pl.BlockSpec((l, M_DK), lambda s, m: (blk0 + gi(s, m), Z_Q * M_HEADS + gh(s, m))),
            pl.BlockSpec((M_DK, l), lambda s, m: (gh(s, m), blk0 + gi(s, m))),
            pl.BlockSpec((l, M_DV), lambda s, m: (blk0 + gi(s, m), Z_V * M_HEADS + gh(s, m))),
            pl.BlockSpec((1, 2, l, LANES), lambda s, m: (gh(s, m), 0, blk0 + gi(s, m), 0)),
            pl.BlockSpec((1, SUBLANES, l), lambda s, m: (gh(s, m), 0, blk0 + gi(s, m))),
            pl.BlockSpec((1, l, LANES), lambda s, m: (gh(s, m), gi(s, m), 0)),
            pl.BlockSpec((1, 1, l), lambda s, m: (gh(s, m), 0, gi(s, m))),
            pl.BlockSpec((SEQ_PER_STEP, 1, M_DK, M_DV), lambda s, m: (gi(s, m), gh(s, m), 0, 0)),
            pl.BlockSpec((SEQ_PER_STEP, M_DK), lambda s, m: (gi(s, m), gh(s, m))),
        ],
        out_specs=[
            pl.BlockSpec((tm, TN), lambda s, m: (m, z_col(s))),
            pl.BlockSpec((l, M_DV), lambda s, m: (gi(s, m), gh(s, m))),
            pl.BlockSpec((SEQ_PER_STEP, 1, M_DK, M_DV), lambda s, m: (gi(s, m), gh(s, m), 0, 0)),
            pl.BlockSpec((SEQ_PER_STEP, M_DK), lambda s, m: (gi(s, m), gh(s, m))),
            pl.BlockSpec((1, SUBLANES, l), lambda s, m: (gh(s, m), 0, gi(s, m))),
        ],
        out_shape=[
            jax.ShapeDtypeStruct((t, ZG_COLS), BF16),
            jax.ShapeDtypeStruct((nb * dec_seq, M_WIDTH), BF16),
            jax.ShapeDtypeStruct(c_in.shape, F32),
            jax.ShapeDtypeStruct(n_in.shape, F32),
            jax.ShapeDtypeStruct((M_HEADS, SUBLANES, nb * dec_seq), F32),
        ],
        scratch_shapes=[pltpu.VMEM((TN, D_MODEL), BF16)],
        compiler_params=_params(("arbitrary", "arbitrary")),
        name="inproj_gates_mlstm_sample",
    )(xn, wt, b1, zqv, kt, zqv, gch, grh, m_col, m_row, c_in, n_in)


def _mix_kernel(n_prompt_tiles, hp_ref, hs_ref, so_ref, yb0_ref, yb1_ref, ga_ref, gb_ref, xp_ref, xs_ref,
                pa_ref, pb_ref, wo_ref, gffn_ref, o_ref, on_ref):
    i = pl.program_id(0)

    def body(h_ref, x_ref):
        h_a = (so_ref[...].astype(F32) * h_ref[...].astype(F32)).astype(BF16)
        a = _dot(h_a, pa_ref[...])
        b = _dot(jnp.concatenate([yb0_ref[...], yb1_ref[...]], axis=1), pb_ref[...])
        merged = ga_ref[...].astype(F32) * a + gb_ref[...].astype(F32) * b
        x1 = x_ref[...] + _dot(merged.astype(BF16), wo_ref[...])
        o_ref[...] = x1
        on_ref[...] = _rmsnorm(x1, gffn_ref[...]).astype(BF16)

    @pl.when(i < n_prompt_tiles)
    def _():
        body(hp_ref, xp_ref)

    @pl.when(i >= n_prompt_tiles)
    def _():
        body(hs_ref, xs_ref)


def _mix_call(hp, hs, z, yb0, yb1, xp, xs, pa, pb, wo, gffn, tm):
    tp, ts = xp.shape[0], xs.shape[0]
    npt, nst = tp // tm, ts // tm
    pi = lambda i: jnp.minimum(i, npt - 1)
    si = lambda i: jnp.maximum(i - npt, 0)
    const = lambda i: (0, 0)
    return pl.pallas_call(
        functools.partial(_mix_kernel, npt),
        grid=(npt + nst,),
        in_specs=[
            pl.BlockSpec((tm, M_WIDTH), lambda i: (pi(i), 0)),
            pl.BlockSpec((tm, M_WIDTH), lambda i: (si(i), 0)),
            pl.BlockSpec((tm, M_WIDTH), lambda i: (i, Z_SO)),
            pl.BlockSpec((tm, yb0.shape[1]), lambda i: (i, 0)),
            pl.BlockSpec((tm, yb1.shape[1]), lambda i: (i, 0)),
            pl.BlockSpec((tm, D_MODEL), lambda i: (i, Z_GA // 2)),
            pl.BlockSpec((tm, D_MODEL), lambda i: (i, Z_GB // 2)),
            pl.BlockSpec((tm, D_MODEL), lambda i: (pi(i), 0)),
            pl.BlockSpec((tm, D_MODEL), lambda i: (si(i), 0)),
            pl.BlockSpec((M_WIDTH, D_MODEL), const, pipeline_mode=pl.Buffered(1)),
            pl.BlockSpec((C_WIDTH, D_MODEL), const, pipeline_mode=pl.Buffered(1)),
            pl.BlockSpec((D_MODEL, D_MODEL), const, pipeline_mode=pl.Buffered(1)),
            pl.BlockSpec((1, D_MODEL), const),
        ],
        out_specs=[pl.BlockSpec((tm, D_MODEL), lambda i: (i, 0)), pl.BlockSpec((tm, D_MODEL), lambda i: (i, 0))],
        out_shape=[jax.ShapeDtypeStruct((tp + ts, D_MODEL), F32), jax.ShapeDtypeStruct((tp + ts, D_MODEL), BF16)],
        compiler_params=_params(("arbitrary",)),
        name="mix",
    )(hp, hs, z, yb0, yb1, z, z, xp, xs, pa, pb, wo, gffn)


def _ffn_kernel(n_prompt_tiles, x1_ref, xn_ref, wup_ref, wdn_ref, gfin_ref, yp_ref, ys_ref, acc_ref):
    i = pl.program_id(0)
    f = pl.program_id(1)

    def mlp_part():
        u = jnp.maximum(_dot(xn_ref[...], wup_ref[...]), 0.0)
        return _dot((u * u).astype(BF16), wdn_ref[...])

    @pl.when(f == 0)
    def _():
        acc_ref[...] = x1_ref[...] + mlp_part()

    @pl.when(f > 0)
    def _():
        acc_ref[...] += mlp_part()

    @pl.when(f == pl.num_programs(1) - 1)
    def _():
        y = _rmsnorm(acc_ref[...], gfin_ref[...])

        @pl.when(i < n_prompt_tiles)
        def _():
            yp_ref[...] = y

        @pl.when(i >= n_prompt_tiles)
        def _():
            ys_ref[...] = y


def _ffn_call(x1, xn, wup, wdn, gfin, n_prompt_tokens, tm, tf):
    t = x1.shape[0]
    npt = n_prompt_tokens // tm
    n_tiles = t // tm
    return pl.pallas_call(
        functools.partial(_ffn_kernel, npt),
        grid=(n_tiles, D_FF // tf),
        in_specs=[
            pl.BlockSpec((tm, D_MODEL), lambda i, f: (i, 0)),
            pl.BlockSpec((tm, D_MODEL), lambda i, f: (i, 0)),
            pl.BlockSpec((D_MODEL, tf), lambda i, f: (0, f)),
            pl.BlockSpec((tf, D_MODEL), lambda i, f: (f, 0)),
            pl.BlockSpec((1, D_MODEL), lambda i, f: (0, 0)),
        ],
        out_specs=[
            pl.BlockSpec((tm, D_MODEL), lambda i, f: (jnp.minimum(i, npt - 1), 0)),
            pl.BlockSpec((tm, D_MODEL), lambda i, f: (jnp.maximum(i - npt, 0), 0)),
        ],
        out_shape=[
            jax.ShapeDtypeStruct((n_prompt_tokens, D_MODEL), F32),
            jax.ShapeDtypeStruct((t - n_prompt_tokens, D_MODEL), F32),
        ],
        scratch_shapes=[pltpu.VMEM((tm, D_MODEL), F32)],
        compiler_params=_params(("arbitrary", "arbitrary")),
        name="ffn",
    )(x1, xn, wup, wdn, gfin)


def kernel(x_prompt, x_sample, state_mlstm_C, state_mlstm_n, state_mlstm_m, state_conv, meta_tokens,
           g_mix, w_in, b_in, w_conv, p_a, p_b, w_o, g_ffn, w_up, w_down, g_final):
    assert w_in.shape[0] == 1, "single-layer trunk"
    batch, seq, _ = x_prompt.shape
    dec_batch, dec_seq, _ = x_sample.shape
    assert dec_seq == SUBLANES and seq % 1024 == 0 and dec_batch % SEQ_PER_STEP == 0
    n_p, n_s = batch * seq, dec_batch * dec_seq

    wt = w_in[0].T
    bias = b_in[0]
    b1 = jnp.concatenate([bias[:OFF_GATES], bias[OFF_O:]])
    bg = bias[OFF_GATES:OFF_O][:, None]
    gmix = g_mix[0][None, :]

    xp = x_prompt.reshape(n_p, D_MODEL)
    xs = x_sample.reshape(n_s, D_MODEL)
    n_meta = meta_tokens.shape[0]

    sconv = state_conv[0].reshape(dec_batch, (CONV_W - 1) * C_WIDTH)
    xn, gch, grh, xnm, gcm, grm, yb0, cu6a, cu7a = _rms_conv_call(
        xp, xs, meta_tokens.astype(F32), gmix, wt, bg, b1[None, :], w_conv[0], sconv, dec_seq, seq, 512)
    zqv, kt, zm, ktm, w_down16 = _qkv_call(xn, xnm, wt, b1[None, :], b1[:, None], w_down[0], 1024)
    m_tok = jnp.repeat(state_mlstm_m[0], dec_seq, axis=0).T
    zg, hs, c_s, n_sm, m_s = _gates_sample_call(
        xn, wt, b1[None, :], zqv, kt, gch, grh,
        jnp.broadcast_to(m_tok[:, :, None], m_tok.shape + (LANES,)), m_tok[:, None, :],
        state_mlstm_C[0], state_mlstm_n[0].reshape(dec_batch, M_HEADS * M_DK), n_p, dec_seq, 1024)
    yb1, cu6b, cu7b, p_a16, p_b16, w_o16, w_up16 = _conv_call(
        xn, xnm, n_meta, wt, b1[None, :], w_conv[0], sconv, (p_a[0], p_b[0], w_o[0], w_up[0]), n_p, seq, 1024)

    hp, c_p, n_pr, m_p = _mlstm_prompt_call(zm, ktm, gcm, grm, zqv, kt, gch, grh, batch, seq)

    x1, xn2 = _mix_call(hp.reshape(n_p, M_WIDTH), hs, zg, yb0, yb1, xp, xs, p_a16, p_b16, w_o16,
                        g_ffn[0][None, :], 256)
    y_p, y_s = _ffn_call(x1, xn2, w_up16, w_down16, g_final[None, :], n_p, 512, 1024)

    per_seq = seq // SUBLANES
    first_s = n_p // SUBLANES
    pick = lambda rows: jnp.stack([jnp.concatenate([cu6a[rows], cu6b[rows]], axis=1),
                                   jnp.concatenate([cu7a[rows], cu7b[rows]], axis=1)], axis=1)
    cv_p = pick(slice(per_seq - 1, batch * per_seq, per_seq))
    cv_s = pick(slice(first_s, None))
    m_s = m_s[:, 0, dec_seq - 1::dec_seq].T
    return (y_p.reshape(batch, seq, D_MODEL), y_s.reshape(dec_batch, dec_seq, D_MODEL),
            c_p[None], n_pr[None], m_p[None, :, :, 0], cv_p[None],
            c_s[None], n_sm.reshape(dec_batch, M_HEADS, M_DK)[None], m_s[None], cv_s[None])
```

```python
import functools

import jax
import jax.numpy as jnp
from jax import lax
from jax.experimental import pallas as pl
from jax.experimental.pallas import tpu as pltpu

F32 = jnp.float32
BF16 = jnp.bfloat16

D_MODEL = 2048
N_META = 16
CHUNK = 128
M_HEADS = 4
M_DK = 256
M_DV = 256
M_WIDTH = M_HEADS * M_DV
C_WIDTH = 1024
CONV_W = 3
D_FF = 4 * D_MODEL
EPS = 1e-6
N_GATES = 2 * M_HEADS
OFF_Q = 0
OFF_K = OFF_Q + M_HEADS * M_DK
OFF_V = OFF_K + M_HEADS * M_DK
OFF_GATES = OFF_V + M_WIDTH
OFF_O = OFF_GATES + N_GATES
OFF_U = OFF_O + M_WIDTH
OFF_C = OFF_U + C_WIDTH
OFF_B = OFF_C + C_WIDTH
OFF_GA = OFF_B + C_WIDTH
OFF_GB = OFF_GA + D_MODEL

LANES = 128
SUBLANES = 8
VMEM_LIMIT_BYTES = 60000 * 1024

TN = 1024
Z_Q, Z_V, Z_SO = 0, 1, 2
Z_GA, Z_GB = 0, 2
ZG_COLS = 4 * TN
CQ = 256
AUG = M_DV + LANES
SEQ_PER_STEP = CHUNK // SUBLANES
G_IG, G_B, G_R = 0, 1, 2

NEG = -1e30
NT_DIMS = (((1,), (1,)), ((), ()))


def _params(semantics):
    return pltpu.CompilerParams(dimension_semantics=semantics, vmem_limit_bytes=VMEM_LIMIT_BYTES)


def _rmsnorm(x, g):
    y = x * lax.rsqrt(jnp.mean(x * x, axis=-1, keepdims=True) + EPS)
    return y * g


def _log_sigmoid(x):
    return jnp.minimum(x, 0.0) - jnp.log1p(jnp.exp(-jnp.abs(x)))


def _dot(a, b):
    return jnp.dot(a, b, preferred_element_type=F32)


def _dot_nt(a, b):
    return lax.dot_general(a, b, NT_DIMS, preferred_element_type=F32)


def _gate_prep(xn, wg_ref, bg_ref, blk, n_valid, gch_ref, grh_ref):
    tm = xn.shape[0]
    wg = wg_ref[...].astype(BF16)
    wg = jnp.concatenate([wg, jnp.zeros((LANES - N_GATES, wg.shape[1]), BF16)], axis=0)
    g = _dot_nt(wg, xn)[0:SUBLANES, :] + bg_ref[...]
    row = lax.broadcasted_iota(jnp.int32, (SUBLANES, tm), 0)
    lane = lax.broadcasted_iota(jnp.int32, (SUBLANES, tm), 1)
    a = jnp.where(row < M_HEADS, g, _log_sigmoid(g))
    if n_valid < tm:
        a = jnp.where(lane < n_valid, a, jnp.where(row < M_HEADS, NEG, 0.0))
    pos = lane & (blk - 1)
    n_steps_blk = blk if isinstance(blk, int) else LANES

    def scan(x, op, fill, reverse=False):
        shift = 1
        while shift < n_steps_blk:
            if reverse:
                x = op(x, jnp.where(pos < blk - shift, pltpu.roll(x, tm - shift, 1), fill))
            else:
                x = op(x, jnp.where(pos >= shift, pltpu.roll(x, shift, 1), fill))
            shift *= 2
        return x

    pre = scan(a, jnp.add, 0.0)
    suf = scan(a, jnp.add, 0.0, reverse=True) - a
    b_up = pltpu.roll(pre, M_HEADS, 0)
    m_in = b_up + scan(a - b_up, jnp.maximum, -3e38)
    for h in range(M_HEADS):
        grh_ref[h] = jnp.where(
            row == G_IG, pltpu.roll(a, (G_IG - h) % SUBLANES, 0),
            jnp.where(row == G_B, pltpu.roll(pre, (G_B - M_HEADS - h) % SUBLANES, 0),
                      jnp.where(row == G_R, pltpu.roll(suf, (G_R - M_HEADS - h) % SUBLANES, 0), 0.0)))
        for c in range(tm // LANES):
            cs = slice(c * LANES, (c + 1) * LANES)
            gch_ref[h, 0, cs, :] = jnp.broadcast_to(b_up[h:h + 1, cs], (LANES, LANES)).T
            gch_ref[h, 1, cs, :] = jnp.broadcast_to(m_in[h:h + 1, cs], (LANES, LANES)).T


S_Q, S_K, S_V, S_O = 0, 1, 2, 3


def _cast_blocks(pairs):
    for src_ref, dst_ref in pairs:
        dst_ref[...] = src_ref[...].astype(BF16)


def _cast_specs(w, n_blocks, step):
    rows, cols = w.shape
    blk = lambda *g: (jnp.minimum(step(*g), n_blocks - 1), 0)
    spec = pl.BlockSpec((rows // n_blocks, cols), blk)
    return spec, spec, jax.ShapeDtypeStruct(w.shape, BF16)


def _sigmoid(z):
    return 0.5 * jnp.tanh(0.5 * z) + 0.5


def _qkvo_kernel(xn_ref, xnm_ref, wt_ref, b_ref, bcol_ref, wup_ref, wdn_ref,
                 z_ref, kt_ref, zm_ref, ktm_ref, wup16_ref, wdn16_ref, w_ref):
    s = pl.program_id(0)
    casts = ((wup_ref, wup16_ref), (wdn_ref, wdn16_ref))
    k_scale = M_DK ** -0.5

    @pl.when(pl.program_id(1) == 0)
    def _():
        w_ref[...] = wt_ref[...].astype(BF16)

        @pl.when((s == S_Q) | (s == S_V))
        def _():
            zm_ref[...] = (_dot_nt(xnm_ref[...], w_ref[...]) + b_ref[...]).astype(BF16)

        @pl.when(s == S_K)
        def _():
            ktm_ref[...] = ((_dot_nt(w_ref[...], xnm_ref[...]) + bcol_ref[...]) * k_scale).astype(BF16)

    def z():
        return _dot_nt(xn_ref[...], w_ref[...]) + b_ref[...]

    @pl.when((s == S_Q) | (s == S_V))
    def _():
        _cast_blocks(casts)
        z_ref[...] = z().astype(BF16)

    @pl.when(s == S_K)
    def _():
        _cast_blocks(casts)
        kt_ref[...] = ((_dot_nt(w_ref[...], xn_ref[...]) + bcol_ref[...]) * k_scale).astype(BF16)

    @pl.when(s == S_O)
    def _():
        _cast_blocks(casts)
        z_ref[...] = _sigmoid(z()).astype(BF16)


def _qkvo_call(xn, xnm, wt, b1, b1col, w_up, w_down, tm):
    t = xn.shape[0]
    assert t % tm == 0
    n_m = t // tm
    rows_m = xnm.shape[0]
    n_cast = 32
    assert n_cast <= 4 * n_m
    up_in, up_out, up_shape = _cast_specs(w_up, n_cast, lambda s, m: s * n_m + m)
    dn_in, dn_out, dn_shape = _cast_specs(w_down, n_cast, lambda s, m: s * n_m + m)
    w_off = lambda s: jnp.where(s == S_O, OFF_O, s * TN)
    z_col = lambda s: jnp.where(s <= S_K, Z_Q, s - 1)
    z_row = lambda s, m: jnp.where(s == S_K, n_m - 1, m)
    kt_blk = lambda s, m: jnp.where(s == S_K, m, jnp.where(s < S_K, 0, n_m - 1))
    return pl.pallas_call(
        _qkvo_kernel,
        grid=(4, n_m),
        in_specs=[
            pl.BlockSpec((tm, D_MODEL), lambda s, m: (m, 0)),
            pl.BlockSpec((rows_m, D_MODEL), lambda s, m: (0, 0)),
            pl.BlockSpec((pl.Element(TN), pl.Element(D_MODEL)),
                         lambda s, m: (pl.multiple_of(w_off(s), SUBLANES), 0)),
            pl.BlockSpec((1, TN), lambda s, m: (0, s)),
            pl.BlockSpec((TN, 1), lambda s, m: (s, 0)),
            up_in, dn_in,
        ],
        out_specs=[
            pl.BlockSpec((tm, TN), lambda s, m: (z_row(s, m), z_col(s))),
            pl.BlockSpec((TN, tm), lambda s, m: (0, kt_blk(s, m))),
            pl.BlockSpec((rows_m, TN), lambda s, m: (0, (s >= S_V).astype(jnp.int32))),
            pl.BlockSpec((TN, rows_m), lambda s, m: (0, 0)),
            up_out, dn_out,
        ],
        out_shape=[
            jax.ShapeDtypeStruct((t, 3 * TN), BF16),
            jax.ShapeDtypeStruct((TN, t), BF16),
            jax.ShapeDtypeStruct((rows_m, 2 * TN), BF16),
            jax.ShapeDtypeStruct((TN, rows_m), BF16),
            up_shape, dn_shape,
        ],
        scratch_shapes=[pltpu.VMEM((TN, D_MODEL), BF16)],
        compiler_params=_params(("arbitrary", "arbitrary")),
        name="inproj_qkvo",
    )(xn, xnm, wt, b1, b1col, w_up, w_down)


def _conv_prologue(m, is_prompt, tiles_per_seq, n_meta, xnm_ref, w_refs, b_refs, s0_ref, s1_ref, scratch,
                   make_xnm=None):
    wu_ref, wc_ref, wb_ref = w_refs
    bu_ref, bc_ref, _ = b_refs
    w3_ref, _, h1_ref, h2_ref, carry_ref, mtail_ref = scratch
    n_slab, tm, _ = h1_ref.shape
    n_seq = tm // SUBLANES
    slabs = [(k, slice(k * LANES, (k + 1) * LANES)) for k in range(n_slab)]
    seq_row = lambda r: pl.ds(r, n_seq, stride=SUBLANES)

    @pl.when(m == 0)
    def _():
        if make_xnm is not None:
            make_xnm()
        w3_ref[0] = wu_ref[...].astype(BF16)
        w3_ref[1] = wc_ref[...].astype(BF16)
        w3_ref[2] = wb_ref[...].astype(BF16)
        h1_ref[...] = jnp.zeros_like(h1_ref)
        h2_ref[...] = jnp.zeros_like(h2_ref)
        xnm = xnm_ref[...]
        cu_m = (_dot_nt(xnm, w3_ref[1]) + bc_ref[...]) * (_dot_nt(xnm, w3_ref[0]) + bu_ref[...])
        mtail_ref[...] = cu_m[n_meta - SUBLANES:n_meta, :]

    @pl.when(is_prompt)
    def _():
        first = (m % tiles_per_seq) == 0
        p6 = jnp.where(first, mtail_ref[6:7, :], carry_ref[6:7, :])
        p7 = jnp.where(first, mtail_ref[7:8, :], carry_ref[7:8, :])
        for k, ks in slabs:
            h2_ref[k, 0:1, :] = p6[:, ks]
            h2_ref[k, 1:2, :] = p7[:, ks]
            h1_ref[k, 0:1, :] = p7[:, ks]

    @pl.when(jnp.logical_not(is_prompt))
    def _():
        for k, ks in slabs:
            h2_ref[k, seq_row(0), :] = s0_ref[:, ks]
            h2_ref[k, seq_row(1), :] = s1_ref[:, ks]
            h1_ref[k, seq_row(0), :] = s1_ref[:, ks]


def _conv_main(xn, is_prompt, b_refs, wconv_ref, yb_ref, cu6_ref, cu7_ref, scratch):
    bu_ref, bc_ref, bb_ref = b_refs
    w3_ref, cu_ref, h1_ref, h2_ref, carry_ref, _ = scratch
    tm, cq = yb_ref.shape
    n_seq = tm // SUBLANES
    slabs = [(k, slice(k * LANES, (k + 1) * LANES)) for k in range(cq // LANES)]
    seq_row = lambda r: pl.ds(r, n_seq, stride=SUBLANES)
    zu = _dot_nt(xn, w3_ref[0]) + bu_ref[...]
    zc = _dot_nt(xn, w3_ref[1]) + bc_ref[...]
    zb = _dot_nt(xn, w3_ref[2]) + bb_ref[...]
    cu = zc * zu
    pos = lax.broadcasted_iota(jnp.int32, (tm, cq), 0) & jnp.where(is_prompt, tm - 1, SUBLANES - 1)
    h1 = jnp.concatenate([h1_ref[k] for k, _ in slabs], axis=1)
    h2 = jnp.concatenate([h2_ref[k] for k, _ in slabs], axis=1)
    x1 = jnp.where(pos >= 1, pltpu.roll(cu, 1, 0), h1)
    x2 = jnp.where(pos >= 2, pltpu.roll(cu, 2, 0), h2)
    w0, w1, w2 = wconv_ref[0:1, :], wconv_ref[1:2, :], wconv_ref[2:3, :]
    yb_ref[...] = (zb * ((w0 * x2 + w1 * x1) + w2 * cu)).astype(BF16)

    carry_ref[...] = cu[tm - SUBLANES:tm, :]
    for k, ks in slabs:
        cu_ref[k] = cu[:, ks]
        cu6_ref[:, ks] = cu_ref[k, seq_row(6), :]
        cu7_ref[:, ks] = cu_ref[k, seq_row(7), :]


def _conv_kernel(n_prompt_tiles, tiles_per_seq, n_meta, n_cast, xn_ref, xnm_ref, wu_ref, wc_ref, wb_ref,
                 bu_ref, bc_ref, bb_ref, wconv_ref, s0_ref, s1_ref, *refs):
    cast_in, (yb_ref, cu6_ref, cu7_ref) = refs[:n_cast], refs[n_cast:n_cast + 3]
    cast_out, scratch = refs[n_cast + 3:2 * n_cast + 3], refs[2 * n_cast + 3:]
    m = pl.program_id(1)
    is_prompt = m < n_prompt_tiles
    b_refs = (bu_ref, bc_ref, bb_ref)
    _conv_prologue(m, is_prompt, tiles_per_seq, n_meta, xnm_ref, (wu_ref, wc_ref, wb_ref), b_refs,
                   s0_ref, s1_ref, scratch)
    _cast_blocks(tuple(zip(cast_in, cast_out)))
    _conv_main(xn_ref[...], is_prompt, b_refs, wconv_ref, yb_ref, cu6_ref, cu7_ref, scratch)


def _rms_conv_kernel(n_prompt_tiles, tiles_per_seq, dec_seq, xp_ref, xs_ref, xm_ref, g_ref, wg_ref, bg_ref,
                     wu_ref, wc_ref, wb_ref, bu_ref, bc_ref, bb_ref, wconv_ref, s0_ref, s1_ref,
                     xn_ref, gch_ref, grh_ref, xnm_ref, gchm_ref, grhm_ref, yb_ref, cu6_ref, cu7_ref, *scratch):
    m = pl.program_id(0)
    is_prompt = m < n_prompt_tiles
    n_meta = xm_ref.shape[0]
    b_refs = (bu_ref, bc_ref, bb_ref)

    def make_xnm():
        xm = jnp.concatenate([xm_ref[...], jnp.zeros((CHUNK - n_meta, D_MODEL), F32)], axis=0)
        xnm = _rmsnorm(xm, g_ref[...]).astype(BF16)
        xnm_ref[...] = xnm
        _gate_prep(xnm, wg_ref, bg_ref, CHUNK, n_meta, gchm_ref, grhm_ref)

    _conv_prologue(m, is_prompt, tiles_per_seq, n_meta, xnm_ref, (wu_ref, wc_ref, wb_ref), b_refs,
                   s0_ref, s1_ref, scratch, make_xnm)
    xn = _rmsnorm(jnp.where(is_prompt, xp_ref[...], xs_ref[...]), g_ref[...]).astype(BF16)
    xn_ref[...] = xn
    _gate_prep(xn, wg_ref, bg_ref, jnp.where(is_prompt, CHUNK, dec_seq), xn.shape[0], gch_ref, grh_ref)
    _conv_main(xn, is_prompt, b_refs, wconv_ref, yb_ref, cu6_ref, cu7_ref, scratch)


def _conv_specs(c0, npt, tm, cm):
    n_seq = tm // SUBLANES
    n_slab = CQ // LANES
    chan = lambda *g: c0 + cm(*g)[0]
    tile = lambda *g: cm(*g)[1]
    w_spec = lambda off: pl.BlockSpec((pl.Element(CQ), pl.Element(D_MODEL)),
                                      lambda *g: (pl.multiple_of(off + chan(*g) * CQ, SUBLANES), 0))
    b_spec = lambda off: pl.BlockSpec((1, CQ), lambda *g: (0, (off - N_GATES) // CQ + chan(*g)))
    s_spec = lambda tok: pl.BlockSpec(
        (n_seq, CQ), lambda *g: (jnp.maximum(tile(*g) - npt, 0), tok * (C_WIDTH // CQ) + chan(*g)))
    in_specs = [w_spec(OFF_U), w_spec(OFF_C), w_spec(OFF_B), b_spec(OFF_U), b_spec(OFF_C), b_spec(OFF_B),
                pl.BlockSpec((CONV_W, CQ), lambda *g: (0, chan(*g))), s_spec(0), s_spec(1)]
    out_specs = [pl.BlockSpec((tm, CQ), lambda *g: (tile(*g), cm(*g)[0])),
                 pl.BlockSpec((n_seq, CQ), lambda *g: (tile(*g), cm(*g)[0])),
                 pl.BlockSpec((n_seq, CQ), lambda *g: (tile(*g), cm(*g)[0]))]
    scratch = [pltpu.VMEM((3, CQ, D_MODEL), BF16),
               pltpu.VMEM((n_slab, tm, LANES), F32),
               pltpu.VMEM((n_slab, tm, LANES), F32),
               pltpu.VMEM((n_slab, tm, LANES), F32),
               pltpu.VMEM((SUBLANES, CQ), F32),
               pltpu.VMEM((SUBLANES, CQ), F32)]
    return in_specs, out_specs, scratch


def _conv_out_shapes(t, n_blocks):
    return [jax.ShapeDtypeStruct((t, n_blocks * CQ), BF16),
            jax.ShapeDtypeStruct((t // SUBLANES, n_blocks * CQ), F32),
            jax.ShapeDtypeStruct((t // SUBLANES, n_blocks * CQ), F32)]


def _rms_conv_call(xp, xs, x_meta, g, wt, bg, b1, wconv, sconv, dec_seq, seq_len, tm):
    tp, ts = xp.shape[0], xs.shape[0]
    t = tp + ts
    assert tm & (tm - 1) == 0 and seq_len % tm == 0 and x_meta.shape[0] >= SUBLANES
    npt, nst = tp // tm, ts // tm
    conv_in, conv_out, scratch = _conv_specs(0, npt, tm, lambda m: (0, m))
    return pl.pallas_call(
        functools.partial(_rms_conv_kernel, npt, seq_len // tm, dec_seq),
        grid=(npt + nst,),
        in_specs=[
            pl.BlockSpec((tm, D_MODEL), lambda m: (jnp.minimum(m, npt - 1), 0)),
            pl.BlockSpec((tm, D_MODEL), lambda m: (jnp.maximum(m - npt, 0), 0)),
            pl.BlockSpec(x_meta.shape, lambda m: (0, 0)),
            pl.BlockSpec((1, D_MODEL), lambda m: (0, 0)),
            pl.BlockSpec((N_GATES, D_MODEL), lambda m: (OFF_GATES // N_GATES, 0)),
            pl.BlockSpec((N_GATES, 1), lambda m: (0, 0)),
        ] + conv_in,
        out_specs=[
            pl.BlockSpec((tm, D_MODEL), lambda m: (m, 0)),
            pl.BlockSpec((M_HEADS, 2, tm, LANES), lambda m: (0, 0, m, 0)),
            pl.BlockSpec((M_HEADS, SUBLANES, tm), lambda m: (0, 0, m)),
            pl.BlockSpec((CHUNK, D_MODEL), lambda m: (0, 0)),
            pl.BlockSpec((M_HEADS, 2, CHUNK, LANES), lambda m: (0, 0, 0, 0)),
            pl.BlockSpec((M_HEADS, SUBLANES, CHUNK), lambda m: (0, 0, 0)),
        ] + conv_out,
        out_shape=[
            jax.ShapeDtypeStruct((t, D_MODEL), BF16),
            jax.ShapeDtypeStruct((M_HEADS, 2, t, LANES), F32),
            jax.ShapeDtypeStruct((M_HEADS, SUBLANES, t), F32),
            jax.ShapeDtypeStruct((CHUNK, D_MODEL), BF16),
            jax.ShapeDtypeStruct((M_HEADS, 2, CHUNK, LANES), F32),
            jax.ShapeDtypeStruct((M_HEADS, SUBLANES, CHUNK), F32),
        ] + _conv_out_shapes(t, 1),
        scratch_shapes=scratch,
        compiler_params=_params(("arbitrary",)),
        name="rms_conv",
    )(xp, xs, x_meta, g, wt, bg, wt, wt, wt, b1, b1, b1, wconv, sconv, sconv)


def _conv_call(xn, xnm, n_meta, wt, b1, wconv, sconv, cast_ws, n_prompt_tokens, seq_len, tm):
    t = xn.shape[0]
    assert tm & (tm - 1) == 0 and seq_len % tm == 0 and n_meta >= SUBLANES
    npt = n_prompt_tokens // tm
    n_m = t // tm
    n_blocks = C_WIDTH // CQ - 1
    n_cast = 16
    assert n_cast <= n_blocks * n_m
    casts = [_cast_specs(w, n_cast, lambda c, m: c * n_m + m) for w in cast_ws]
    conv_in, conv_out, scratch = _conv_specs(1, npt, tm, lambda c, m: (c, m))
    return pl.pallas_call(
        functools.partial(_conv_kernel, npt, seq_len // tm, n_meta, len(cast_ws)),
        grid=(n_blocks, n_m),
        in_specs=[
            pl.BlockSpec((tm, D_MODEL), lambda c, m: (m, 0)),
            pl.BlockSpec(xnm.shape, lambda c, m: (0, 0)),
        ] + conv_in + [cs[0] for cs in casts],
        out_specs=conv_out + [cs[1] for cs in casts],
        out_shape=_conv_out_shapes(t, n_blocks) + [cs[2] for cs in casts],
        scratch_shapes=scratch,
        compiler_params=_params(("arbitrary", "arbitrary")),
        name="inproj_conv",
    )(xn, xnm, wt, wt, wt, b1, b1, b1, wconv, sconv, sconv, *cast_ws)


def _prompt_head(q, kt, v, gr, gc, ca, m_st):
    l = q.shape[0]
    assert l == LANES
    ig_row, b_row, r_row = gr[G_IG:G_IG + 1, :], gr[G_B:G_B + 1, :], gr[G_R:G_R + 1, :]
    b_col = gc[0]
    v_aug = jnp.concatenate([v, jnp.ones((l, LANES), BF16)], axis=1)
    t_idx = lax.broadcasted_iota(jnp.int32, (l, l), 0)
    s_idx = lax.broadcasted_iota(jnp.int32, (l, l), 1)
    logw = jnp.where(s_idx <= t_idx, b_col - b_row + ig_row, NEG)
    inter = b_col + m_st
    m_t = jnp.maximum(inter, gc[1])
    w_inter = jnp.exp(inter - m_t)
    s = _dot(q, kt) * jnp.exp(logw - m_t)
    nd = (jnp.concatenate([w_inter] * (AUG // LANES), axis=1) * _dot(q, ca.astype(BF16))
          + _dot(s.astype(BF16), v_aug))
    rcp = 1.0 / jnp.maximum(jnp.abs(nd[:, M_DV:AUG]), jnp.exp(-m_t))
    h = nd[:, 0:M_DV] * jnp.concatenate([rcp] * (M_DV // LANES), axis=1)
    m_new = m_t[l - 1:l, 0:1]
    decay = jnp.exp(b_col[l - 1:l, 0:1] + m_st - m_new)
    kwt = (kt.astype(F32) * jnp.exp(r_row + ig_row - m_new)).astype(BF16)
    return h, decay * ca + _dot(kwt, v_aug), m_new


def _mlstm_prompt_kernel(batch, qm_ref, ktm_ref, vm_ref, gcm_ref, grm_ref, *refs):
    ins, (h_ref, c_ref, n_ref, m_ref, ca_ref) = refs[:5 * batch], refs[5 * batch:]
    heads = [(hd, slice(hd * M_DK, (hd + 1) * M_DK)) for hd in range(M_HEADS)]

    @pl.when(pl.program_id(0) == 0)
    def _():
        for hd, sl in heads:
            _, ca_new, m_new = _prompt_head(qm_ref[:, sl], ktm_ref[sl, :], vm_ref[:, sl], grm_ref[hd],
                                            gcm_ref[hd], jnp.zeros((M_DK, AUG), F32), jnp.zeros((1, 1), F32))
            for b in range(batch):
                ca_ref[b, hd] = ca_new
                m_ref[b, hd:hd + 1, :] = jnp.broadcast_to(m_new, (1, LANES))

    for b in range(batch):
        q_ref, kt_ref, v_ref, gc_ref, gr_ref = ins[5 * b:5 * b + 5]
        for hd, sl in heads:
            h, ca_new, m_new = _prompt_head(q_ref[:, sl], kt_ref[sl, :], v_ref[:, sl], gr_ref[hd], gc_ref[hd],
                                            ca_ref[b, hd], m_ref[b, hd:hd + 1, 0:1])
            ca_ref[b, hd] = ca_new
            m_ref[b, hd:hd + 1, :] = jnp.broadcast_to(m_new, (1, LANES))
            h_ref[b, :, sl] = h.astype(h_ref.dtype)

    @pl.when(pl.program_id(0) == pl.num_programs(0) - 1)
    def _():
        for b in range(batch):
            for hd, _ in heads:
                ca = ca_ref[b, hd]
                c_ref[b, hd] = ca[:, 0:M_DV]
                n_t = ca[:, M_DV:AUG]
                n_ref[b, hd:hd + 1, :] = jnp.concatenate(
                    [n_t[k * LANES:(k + 1) * LANES, :].T[0:1, :] for k in range(M_DK // LANES)], axis=1)


def _mlstm_prompt_call(zm, ktm, gcm, grm, z, kt, gch, grh, batch, seq):
    nc = seq // CHUNK
    per_prompt_specs, per_prompt_args = [], []
    for b in range(batch):
        row = functools.partial(lambda b, c: b * nc + c, b)
        per_prompt_specs += [
            pl.BlockSpec((CHUNK, M_WIDTH), lambda c, row=row: (row(c), Z_Q)),
            pl.BlockSpec((M_WIDTH, CHUNK), lambda c, row=row: (0, row(c))),
            pl.BlockSpec((CHUNK, M_WIDTH), lambda c, row=row: (row(c), Z_V)),
            pl.BlockSpec((M_HEADS, 2, CHUNK, LANES), lambda c, row=row: (0, 0, row(c), 0)),
            pl.BlockSpec((M_HEADS, SUBLANES, CHUNK), lambda c, row=row: (0, 0, row(c))),
        ]
        per_prompt_args += [z, kt, z, gch, grh]
    return pl.pallas_call(
        functools.partial(_mlstm_prompt_kernel, batch),
        grid=(nc,),
        in_specs=[
            pl.BlockSpec((CHUNK, M_WIDTH), lambda c: (0, 0)),
            pl.BlockSpec((M_WIDTH, CHUNK), lambda c: (0, 0)),
            pl.BlockSpec((CHUNK, M_WIDTH), lambda c: (0, 1)),
            pl.BlockSpec((M_HEADS, 2, CHUNK, LANES), lambda c: (0, 0, 0, 0)),
            pl.BlockSpec((M_HEADS, SUBLANES, CHUNK), lambda c: (0, 0, 0)),
        ] + per_prompt_specs,
        out_specs=[
            pl.BlockSpec((batch, CHUNK, M_WIDTH), lambda c: (0, c, 0)),
            pl.BlockSpec((batch, M_HEADS, M_DK, M_DV), lambda c: (0, 0, 0, 0)),
            pl.BlockSpec((batch, M_HEADS, M_DK), lambda c: (0, 0, 0)),
            pl.BlockSpec((batch, M_HEADS, LANES), lambda c: (0, 0, 0)),
        ],
        out_shape=[
            jax.ShapeDtypeStruct((batch, seq, M_WIDTH), BF16),
            jax.ShapeDtypeStruct((batch, M_HEADS, M_DK, M_DV), F32),
            jax.ShapeDtypeStruct((batch, M_HEADS, M_DK), F32),
            jax.ShapeDtypeStruct((batch, M_HEADS, LANES), F32),
        ],
        scratch_shapes=[pltpu.VMEM((batch, M_HEADS, M_DK, AUG), F32)],
        compiler_params=_params(("arbitrary",)),
        name="mlstm_prompt",
    )(zm, ktm, zm, gcm, grm, *per_prompt_args)


def _group_max(x, size):
    n = x.shape[-1]
    lane = lax.broadcasted_iota(jnp.int32, x.shape, x.ndim - 1)
    k = 1
    while k < size:
        partner = jnp.where((lane & k) == 0, pltpu.roll(x, n - k, x.ndim - 1), pltpu.roll(x, k, x.ndim - 1))
        x = jnp.maximum(x, partner)
        k *= 2
    return x


def _mlstm_sample_kernel(dec_seq, q_ref, kt_ref, v_ref, gc_ref, gr_ref, mcol_ref, mrow_ref, cin_ref, nin_ref,
                         h_ref, c_ref, n_ref, m_ref):
    l = q_ref.shape[0]
    n_seq = l // dec_seq
    shift = dec_seq.bit_length() - 1
    q, kt, v = q_ref[...], kt_ref[...], v_ref[...]
    assert l == LANES
    gr, gc = gr_ref[0], gc_ref[0]
    ig_row, b_row, r_row = gr[G_IG:G_IG + 1, :], gr[G_B:G_B + 1, :], gr[G_R:G_R + 1, :]
    b_col = gc[0]
    m_col, m_row = mcol_ref[0], mrow_ref[0]

    t_idx = lax.broadcasted_iota(jnp.int32, (l, l), 0)
    s_idx = lax.broadcasted_iota(jnp.int32, (l, l), 1)
    t_seq = t_idx >> shift
    logw = jnp.where((t_seq == (s_idx >> shift)) & (s_idx <= t_idx), b_col - b_row + ig_row, NEG)
    inter = b_col + m_col
    m_t = jnp.maximum(inter, gc[1])
    w_inter = jnp.exp(inter - m_t)
    s = _dot(q, kt) * jnp.exp(logw - m_t)

    e_row = r_row + ig_row
    b_last = b_row + r_row
    m_new = jnp.maximum(b_last + m_row, _group_max(e_row, dec_seq))
    decay = jnp.exp(b_last + m_row - m_new)
    kwt = (kt.astype(F32) * jnp.exp(e_row - m_new)).astype(BF16)

    seq_lane = s_idx == t_seq
    v_aug = jnp.concatenate([v, jnp.where(seq_lane, 1.0, 0.0).astype(BF16)], axis=1)
    pa = _dot(s.astype(BF16), v_aug)

    n_rows = jnp.concatenate([nin_ref[...], jnp.zeros((LANES - n_seq, M_DK), F32)], axis=0)
    n_t = jnp.concatenate([n_rows[:, 0:LANES].T, n_rows[:, LANES:2 * LANES].T], axis=0)
    lane_k = lax.broadcasted_iota(jnp.int32, (M_DK, LANES), 1)
    qf = q.astype(F32)
    qca_rows = []
    n_t_new = jnp.zeros((M_DK, LANES), F32)
    for i in range(n_seq):
        ca = jnp.concatenate([cin_ref[i, 0], jnp.where(lane_k == i, n_t, 0.0)], axis=1)
        rows = slice(i * dec_seq, (i + 1) * dec_seq)
        qca_rows.append(_dot(qf[rows, :].astype(BF16), ca.astype(BF16)))
        kw_i = jnp.where((lane_k >> shift) == i, kwt, jnp.zeros_like(kwt))
        ca_new = decay[:, i * dec_seq:i * dec_seq + 1] * ca + _dot(kw_i, v_aug)
        c_ref[i, 0] = ca_new[:, 0:M_DV]
        n_t_new = n_t_new + ca_new[:, M_DV:AUG]
    nd = jnp.concatenate([w_inter] * (AUG // LANES), axis=1) * jnp.concatenate(qca_rows, axis=0) + pa
    den = jnp.sum(jnp.where(seq_lane, nd[:, M_DV:AUG], 0.0), axis=-1, keepdims=True)
    rcp = 1.0 / jnp.maximum(jnp.abs(den), jnp.exp(-m_t[:, 0:1]))
    h_ref[...] = (nd[:, 0:M_DV] * rcp).astype(h_ref.dtype)
    n_ref[...] = jnp.concatenate([n_t_new[0:LANES, :].T, n_t_new[LANES:2 * LANES, :].T], axis=1)[0:n_seq, :]
    m_ref[0] = jnp.broadcast_to(m_new, (SUBLANES, l))


def _gates_sample_kernel(dec_seq, xn_ref, wt_ref, b_ref, *refs):
    sample_in, z_ref, sample_out, w_ref = refs[:9], refs[9], refs[10:14], refs[14]

    @pl.when(pl.program_id(1) == 0)
    def _():
        w_ref[...] = wt_ref[...].astype(BF16)

    z_ref[...] = _sigmoid(_dot_nt(xn_ref[...], w_ref[...]) + b_ref[...]).astype(BF16)
    _mlstm_sample_kernel(dec_seq, *sample_in, *sample_out)


def _gates_sample_call(xn, wt, b1, zqv, kt, gch, grh, m_col, m_row, c_in, n_in, row0, dec_seq, tm):
    t = xn.shape[0]
    n_m = t // tm
    nb = c_in.shape[0]
    l = SEQ_PER_STEP * dec_seq
    blk0 = row0 // l
    n_steps = 2 * D_MODEL // TN
    n_guest = (nb // SEQ_PER_STEP) * M_HEADS
    assert n_guest <= n_steps * n_m
    w_off = lambda s: OFF_GA + s * TN
    b_blk = lambda s: (OFF_GA - N_GATES) // TN + s
    z_col = lambda s: s
    guest = lambda s, m: jnp.minimum(s * n_m + m, n_guest - 1)
    gi = lambda s, m: guest(s, m) // M_HEADS
    gh = lambda s, m: guest(s, m) % M_HEADS
    return pl.pallas_call(
        functools.partial(_gates_sample_kernel, dec_seq),
        grid=(n_steps, n_m),
        in_specs=[
            pl.BlockSpec((tm, D_MODEL), lambda s, m: (m, 0)),
            pl.BlockSpec((pl.Element(TN), pl.Element(D_MODEL)),
                         lambda s, m: (pl.multiple_of(w_off(s), SUBLANES), 0)),
            pl.BlockSpec((1, TN), lambda s, m: (0, b_blk(s))),
            pl.BlockSpec((l, M_DK), lambda s, m: (blk0 + gi(s, m), Z_Q * M_HEADS + gh(s, m))),
            pl.BlockSpec((M_DK, l), lambda s, m: (gh(s, m), blk0 + gi(s, m))),
            pl.BlockSpec((l, M_DV), lambda s, m: (blk0 + gi(s, m), Z_V * M_HEADS + gh(s, m))),
            pl.BlockSpec((1, 2, l, LANES), lambda s, m: (gh(s, m), 0, blk0 + gi(s, m), 0)),
            pl.BlockSpec((1, SUBLANES, l), lambda s, m: (gh(s, m), 0, blk0 + gi(s, m))),
            pl.BlockSpec((1, l, LANES), lambda s, m: (gh(s, m), gi(s, m), 0)),
            pl.BlockSpec((1, 1, l), lambda s, m: (gh(s, m), 0, gi(s, m))),
            pl.BlockSpec((SEQ_PER_STEP, 1, M_DK, M_DV), lambda s, m: (gi(s, m), gh(s, m), 0, 0)),
            pl.BlockSpec((SEQ_PER_STEP, M_DK), lambda s, m: (gi(s, m), gh(s, m))),
        ],
        out_specs=[
            pl.BlockSpec((tm, TN), lambda s, m: (m, z_col(s))),
            pl.BlockSpec((l, M_DV), lambda s, m: (gi(s, m), gh(s, m))),
            pl.BlockSpec((SEQ_PER_STEP, 1, M_DK, M_DV), lambda s, m: (gi(s, m), gh(s, m), 0, 0)),
            pl.BlockSpec((SEQ_PER_STEP, M_DK), lambda s, m: (gi(s, m), gh(s, m))),
            pl.BlockSpec((1, SUBLANES, l), lambda s, m: (gh(s, m), 0, gi(s, m))),
        ],
        out_shape=[
            jax.ShapeDtypeStruct((t, ZG_COLS), BF16),
            jax.ShapeDtypeStruct((nb * dec_seq, M_WIDTH), BF16),
            jax.ShapeDtypeStruct(c_in.shape, F32),
            jax.ShapeDtypeStruct(n_in.shape, F32),
            jax.ShapeDtypeStruct((M_HEADS, SUBLANES, nb * dec_seq), F32),
        ],
        scratch_shapes=[pltpu.VMEM((TN, D_MODEL), BF16)],
        compiler_params=_params(("arbitrary", "arbitrary")),
        name="inproj_gates_mlstm_sample",
    )(xn, wt, b1, zqv, kt, zqv, gch, grh, m_col, m_row, c_in, n_in)


def _mix_kernel(n_prompt_tiles, hp_ref, hs_ref, so_ref, yb0_ref, yb1_ref, ga_ref, gb_ref, xp_ref, xs_ref,
                pa_ref, pb_ref, wo_ref, gffn_ref, o_ref, on_ref):
    i = pl.program_id(0)

    def body(h_ref, x_ref):
        h_a = (so_ref[...].astype(F32) * h_ref[...].astype(F32)).astype(BF16)
        a = _dot(h_a, pa_ref[...])
        b = _dot(jnp.concatenate([yb0_ref[...], yb1_ref[...]], axis=1), pb_ref[...])
        merged = ga_ref[...].astype(F32) * a + gb_ref[...].astype(F32) * b
        x1 = x_ref[...] + _dot(merged.astype(BF16), wo_ref[...])
        o_ref[...] = x1
        on_ref[...] = _rmsnorm(x1, gffn_ref[...]).astype(BF16)

    @pl.when(i < n_prompt_tiles)
    def _():
        body(hp_ref, xp_ref)

    @pl.when(i >= n_prompt_tiles)
    def _():
        body(hs_ref, xs_ref)


def _mix_call(hp, hs, zqvo, zg, yb0, yb1, xp, xs, pa, pb, wo, gffn, tm):
    tp, ts = xp.shape[0], xs.shape[0]
    npt, nst = tp // tm, ts // tm
    pi = lambda i: jnp.minimum(i, npt - 1)
    si = lambda i: jnp.maximum(i - npt, 0)
    const = lambda i: (0, 0)
    return pl.pallas_call(
        functools.partial(_mix_kernel, npt),
        grid=(npt + nst,),
        in_specs=[
            pl.BlockSpec((tm, M_WIDTH), lambda i: (pi(i), 0)),
            pl.BlockSpec((tm, M_WIDTH), lambda i: (si(i), 0)),
            pl.BlockSpec((tm, M_WIDTH), lambda i: (i, Z_SO)),
            pl.BlockSpec((tm, yb0.shape[1]), lambda i: (i, 0)),
            pl.BlockSpec((tm, yb1.shape[1]), lambda i: (i, 0)),
            pl.BlockSpec((tm, D_MODEL), lambda i: (i, Z_GA // 2)),
            pl.BlockSpec((tm, D_MODEL), lambda i: (i, Z_GB // 2)),
            pl.BlockSpec((tm, D_MODEL), lambda i: (pi(i), 0)),
            pl.BlockSpec((tm, D_MODEL), lambda i: (si(i), 0)),
            pl.BlockSpec((M_WIDTH, D_MODEL), const, pipeline_mode=pl.Buffered(1)),
            pl.BlockSpec((C_WIDTH, D_MODEL), const, pipeline_mode=pl.Buffered(1)),
            pl.BlockSpec((D_MODEL, D_MODEL), const, pipeline_mode=pl.Buffered(1)),
            pl.BlockSpec((1, D_MODEL), const),
        ],
        out_specs=[pl.BlockSpec((tm, D_MODEL), lambda i: (i, 0)), pl.BlockSpec((tm, D_MODEL), lambda i: (i, 0))],
        out_shape=[jax.ShapeDtypeStruct((tp + ts, D_MODEL), F32), jax.ShapeDtypeStruct((tp + ts, D_MODEL), BF16)],
        compiler_params=_params(("arbitrary",)),
        name="mix",
    )(hp, hs, zqvo, yb0, yb1, zg, zg, xp, xs, pa, pb, wo, gffn)


def _ffn_kernel(n_prompt_tiles, x1_ref, xn_ref, wup_ref, wdn_ref, gfin_ref, yp_ref, ys_ref, acc_ref):
    i = pl.program_id(0)
    f = pl.program_id(1)

    def mlp_part():
        u = jnp.maximum(_dot(xn_ref[...], wup_ref[...]), 0.0)
        return _dot((u * u).astype(BF16), wdn_ref[...])

    @pl.when(f == 0)
    def _():
        acc_ref[...] = x1_ref[...] + mlp_part()

    @pl.when(f > 0)
    def _():
        acc_ref[...] += mlp_part()

    @pl.when(f == pl.num_programs(1) - 1)
    def _():
        y = _rmsnorm(acc_ref[...], gfin_ref[...])

        @pl.when(i < n_prompt_tiles)
        def _():
            yp_ref[...] = y

        @pl.when(i >= n_prompt_tiles)
        def _():
            ys_ref[...] = y


def _ffn_call(x1, xn, wup, wdn, gfin, n_prompt_tokens, tm, tf):
    t = x1.shape[0]
    npt = n_prompt_tokens // tm
    n_tiles = t // tm
    return pl.pallas_call(
        functools.partial(_ffn_kernel, npt),
        grid=(n_tiles, D_FF // tf),
        in_specs=[
            pl.BlockSpec((tm, D_MODEL), lambda i, f: (i, 0)),
            pl.BlockSpec((tm, D_MODEL), lambda i, f: (i, 0)),
            pl.BlockSpec((D_MODEL, tf), lambda i, f: (0, f)),
            pl.BlockSpec((tf, D_MODEL), lambda i, f: (f, 0)),
            pl.BlockSpec((1, D_MODEL), lambda i, f: (0, 0)),
        ],
        out_specs=[
            pl.BlockSpec((tm, D_MODEL), lambda i, f: (jnp.minimum(i, npt - 1), 0)),
            pl.BlockSpec((tm, D_MODEL), lambda i, f: (jnp.maximum(i - npt, 0), 0)),
        ],
        out_shape=[
            jax.ShapeDtypeStruct((n_prompt_tokens, D_MODEL), F32),
            jax.ShapeDtypeStruct((t - n_prompt_tokens, D_MODEL), F32),
        ],
        scratch_shapes=[pltpu.VMEM((tm, D_MODEL), F32)],
        compiler_params=_params(("arbitrary", "arbitrary")),
        name="ffn",
    )(x1, xn, wup, wdn, gfin)


def kernel(x_prompt, x_sample, state_mlstm_C, state_mlstm_n, state_mlstm_m, state_conv, meta_tokens,
           g_mix, w_in, b_in, w_conv, p_a, p_b, w_o, g_ffn, w_up, w_down, g_final):
    assert w_in.shape[0] == 1, "single-layer trunk"
    batch, seq, _ = x_prompt.shape
    dec_batch, dec_seq, _ = x_sample.shape
    assert dec_seq == SUBLANES and seq % 1024 == 0 and dec_batch % SEQ_PER_STEP == 0
    n_p, n_s = batch * seq, dec_batch * dec_seq

    wt = w_in[0].T
    bias = b_in[0]
    b1 = jnp.concatenate([bias[:OFF_GATES], bias[OFF_O:]])
    bg = bias[OFF_GATES:OFF_O][:, None]
    gmix = g_mix[0][None, :]

    xp = x_prompt.reshape(n_p, D_MODEL)
    xs = x_sample.reshape(n_s, D_MODEL)
    n_meta = meta_tokens.shape[0]

    sconv = state_conv[0].reshape(dec_batch, (CONV_W - 1) * C_WIDTH)
    xn, gch, grh, xnm, gcm, grm, yb0, cu6a, cu7a = _rms_conv_call(
        xp, xs, meta_tokens.astype(F32), gmix, wt, bg, b1[None, :], w_conv[0], sconv, dec_seq, seq, 512)
    zqv, kt, zm, ktm, w_up16, w_down16 = _qkvo_call(xn, xnm, wt, b1[None, :], b1[:, None],
                                                    w_up[0], w_down[0], 1024)
    m_tok = jnp.repeat(state_mlstm_m[0], dec_seq, axis=0).T
    zg, hs, c_s, n_sm, m_s = _gates_sample_call(
        xn, wt, b1[None, :], zqv, kt, gch, grh,
        jnp.broadcast_to(m_tok[:, :, None], m_tok.shape + (LANES,)), m_tok[:, None, :],
        state_mlstm_C[0], state_mlstm_n[0].reshape(dec_batch, M_HEADS * M_DK), n_p, dec_seq, 1024)
    yb1, cu6b, cu7b, p_a16, p_b16, w_o16 = _conv_call(
        xn, xnm, n_meta, wt, b1[None, :], w_conv[0], sconv, (p_a[0], p_b[0], w_o[0]), n_p, seq, 1024)

    hp, c_p, n_pr, m_p = _mlstm_prompt_call(zm, ktm, gcm, grm, zqv, kt, gch, grh, batch, seq)

    x1, xn2 = _mix_call(hp.reshape(n_p, M_WIDTH), hs, zqv, zg, yb0, yb1, xp, xs, p_a16, p_b16, w_o16,
                        g_ffn[0][None, :], 256)
    y_p, y_s = _ffn_call(x1, xn2, w_up16, w_down16, g_final[None, :], n_p, 512, 1024)

    per_seq = seq // SUBLANES
    first_s = n_p // SUBLANES
    pick = lambda rows: jnp.stack([jnp.concatenate([cu6a[rows], cu6b[rows]], axis=1),
                                   jnp.concatenate([cu7a[rows], cu7b[rows]], axis=1)], axis=1)
    cv_p = pick(slice(per_seq - 1, batch * per_seq, per_seq))
    cv_s = pick(slice(first_s, None))
    m_s = m_s[:, 0, dec_seq - 1::dec_seq].T
    return (y_p.reshape(batch, seq, D_MODEL), y_s.reshape(dec_batch, dec_seq, D_MODEL),
            c_p[None], n_pr[None], m_p[None, :, :, 0], cv_p[None],
            c_s[None], n_sm.reshape(dec_batch, M_HEADS, M_DK)[None], m_s[None], cv_s[None])
```

```python
import functools

import jax
import jax.numpy as jnp
from jax import lax
from jax.experimental import pallas as pl
from jax.experimental.pallas import tpu as pltpu

F32 = jnp.float32
BF16 = jnp.bfloat16

D_MODEL = 2048
N_META = 16
CHUNK = 128
M_HEADS = 4
M_DK = 256
M_DV = 256
M_WIDTH = M_HEADS * M_DV
C_WIDTH = 1024
CONV_W = 3
D_FF = 4 * D_MODEL
EPS = 1e-6
N_GATES = 2 * M_HEADS
OFF_Q = 0
OFF_K = OFF_Q + M_HEADS * M_DK
OFF_V = OFF_K + M_HEADS * M_DK
OFF_GATES = OFF_V + M_WIDTH
OFF_O = OFF_GATES + N_GATES
OFF_U = OFF_O + M_WIDTH
OFF_C = OFF_U + C_WIDTH
OFF_B = OFF_C + C_WIDTH
OFF_GA = OFF_B + C_WIDTH
OFF_GB = OFF_GA + D_MODEL

LANES = 128
SUBLANES = 8
VMEM_LIMIT_BYTES = 60000 * 1024

TN = 1024
Z_Q, Z_V, Z_SO = 0, 1, 2
Z_GA, Z_GB = 0, 2
ZG_COLS = 4 * TN
CQ = 256
AUG = M_DV + LANES
SEQ_PER_STEP = CHUNK // SUBLANES
G_IG, G_B, G_R = 0, 1, 2

NEG = -1e30
NT_DIMS = (((1,), (1,)), ((), ()))


def _params(semantics):
    return pltpu.CompilerParams(dimension_semantics=semantics, vmem_limit_bytes=VMEM_LIMIT_BYTES)


def _rmsnorm(x, g):
    y = x * lax.rsqrt(jnp.mean(x * x, axis=-1, keepdims=True) + EPS)
    return y * g


def _log_sigmoid(x):
    return jnp.minimum(x, 0.0) - jnp.log1p(jnp.exp(-jnp.abs(x)))


def _dot(a, b):
    return jnp.dot(a, b, preferred_element_type=F32)


def _dot_nt(a, b):
    return lax.dot_general(a, b, NT_DIMS, preferred_element_type=F32)


def _gate_prep(xn, wg_ref, bg_ref, blk, n_valid, gch_ref, grh_ref):
    tm = xn.shape[0]
    wg = wg_ref[...].astype(BF16)
    wg = jnp.concatenate([wg, jnp.zeros((LANES - N_GATES, wg.shape[1]), BF16)], axis=0)
    g = _dot_nt(wg, xn)[0:SUBLANES, :] + bg_ref[...]
    row = lax.broadcasted_iota(jnp.int32, (SUBLANES, tm), 0)
    lane = lax.broadcasted_iota(jnp.int32, (SUBLANES, tm), 1)
    a = jnp.where(row < M_HEADS, g, _log_sigmoid(g))
    if n_valid < tm:
        a = jnp.where(lane < n_valid, a, jnp.where(row < M_HEADS, NEG, 0.0))
    pos = lane & (blk - 1)
    n_steps_blk = blk if isinstance(blk, int) else LANES

    def scan(x, op, fill, reverse=False):
        shift = 1
        while shift < n_steps_blk:
            if reverse:
                x = op(x, jnp.where(pos < blk - shift, pltpu.roll(x, tm - shift, 1), fill))
            else:
                x = op(x, jnp.where(pos >= shift, pltpu.roll(x, shift, 1), fill))
            shift *= 2
        return x

    pre = scan(a, jnp.add, 0.0)
    suf = scan(a, jnp.add, 0.0, reverse=True) - a
    b_up = pltpu.roll(pre, M_HEADS, 0)
    m_in = b_up + scan(a - b_up, jnp.maximum, -3e38)
    for h in range(M_HEADS):
        grh_ref[h] = jnp.where(
            row == G_IG, pltpu.roll(a, (G_IG - h) % SUBLANES, 0),
            jnp.where(row == G_B, pltpu.roll(pre, (G_B - M_HEADS - h) % SUBLANES, 0),
                      jnp.where(row == G_R, pltpu.roll(suf, (G_R - M_HEADS - h) % SUBLANES, 0), 0.0)))
        for c in range(tm // LANES):
            cs = slice(c * LANES, (c + 1) * LANES)
            gch_ref[h, 0, cs, :] = jnp.broadcast_to(b_up[h:h + 1, cs], (LANES, LANES)).T
            gch_ref[h, 1, cs, :] = jnp.broadcast_to(m_in[h:h + 1, cs], (LANES, LANES)).T


S_Q, S_K, S_V, S_O = 0, 1, 2, 3


def _cast_blocks(pairs):
    for src_ref, dst_ref in pairs:
        dst_ref[...] = src_ref[...].astype(BF16)


def _cast_specs(w, n_blocks, step):
    rows, cols = w.shape
    blk = lambda *g: (jnp.minimum(step(*g), n_blocks - 1), 0)
    spec = pl.BlockSpec((rows // n_blocks, cols), blk)
    return spec, spec, jax.ShapeDtypeStruct(w.shape, BF16)


def _sigmoid(z):
    return 0.5 * jnp.tanh(0.5 * z) + 0.5


def _qkvo_kernel(xn_ref, xnm_ref, wt_ref, b_ref, wup_ref, wdn_ref,
                 z_ref, kt_ref, zm_ref, ktm_ref, wup16_ref, wdn16_ref, w_ref, bcol_ref):
    s = pl.program_id(0)
    casts = ((wup_ref, wup16_ref), (wdn_ref, wdn16_ref))
    k_scale = M_DK ** -0.5
    lane_tiles = lambda x, n: jnp.concatenate([x] * (n // LANES), axis=1)

    @pl.when(pl.program_id(1) == 0)
    def _():
        w_ref[...] = wt_ref[...].astype(BF16)

        @pl.when((s == S_Q) | (s == S_V))
        def _():
            zm_ref[...] = (_dot_nt(xnm_ref[...], w_ref[...]) + b_ref[...]).astype(BF16)

        @pl.when(s == S_K)
        def _():
            for c in range(TN // LANES):
                cs = slice(c * LANES, (c + 1) * LANES)
                bcol_ref[cs, :] = jnp.broadcast_to(b_ref[:, cs], (LANES, LANES)).T
            ktm = _dot_nt(w_ref[...], xnm_ref[...]) + lane_tiles(bcol_ref[...], xnm_ref.shape[0])
            ktm_ref[...] = (ktm * k_scale).astype(BF16)

    def z():
        return _dot_nt(xn_ref[...], w_ref[...]) + b_ref[...]

    @pl.when((s == S_Q) | (s == S_V))
    def _():
        _cast_blocks(casts)
        z_ref[...] = z().astype(BF16)

    @pl.when(s == S_K)
    def _():
        _cast_blocks(casts)
        kt = _dot_nt(w_ref[...], xn_ref[...]) + lane_tiles(bcol_ref[...], xn_ref.shape[0])
        kt_ref[...] = (kt * k_scale).astype(BF16)

    @pl.when(s == S_O)
    def _():
        _cast_blocks(casts)
        z_ref[...] = _sigmoid(z()).astype(BF16)


def _qkvo_call(xn, xnm, wt, b1, w_up, w_down, tm):
    t = xn.shape[0]
    assert t % tm == 0
    n_m = t // tm
    rows_m = xnm.shape[0]
    n_cast = 32
    assert n_cast <= 4 * n_m
    up_in, up_out, up_shape = _cast_specs(w_up, n_cast, lambda s, m: s * n_m + m)
    dn_in, dn_out, dn_shape = _cast_specs(w_down, n_cast, lambda s, m: s * n_m + m)
    w_off = lambda s: jnp.where(s == S_O, OFF_O, s * TN)
    z_col = lambda s: jnp.where(s <= S_K, Z_Q, s - 1)
    z_row = lambda s, m: jnp.where(s == S_K, n_m - 1, m)
    kt_blk = lambda s, m: jnp.where(s == S_K, m, jnp.where(s < S_K, 0, n_m - 1))
    return pl.pallas_call(
        _qkvo_kernel,
        grid=(4, n_m),
        in_specs=[
            pl.BlockSpec((tm, D_MODEL), lambda s, m: (m, 0)),
            pl.BlockSpec((rows_m, D_MODEL), lambda s, m: (0, 0)),
            pl.BlockSpec((pl.Element(TN), pl.Element(D_MODEL)),
                         lambda s, m: (pl.multiple_of(w_off(s), SUBLANES), 0)),
            pl.BlockSpec((1, TN), lambda s, m: (0, s)),
            up_in, dn_in,
        ],
        out_specs=[
            pl.BlockSpec((tm, TN), lambda s, m: (z_row(s, m), z_col(s))),
            pl.BlockSpec((TN, tm), lambda s, m: (0, kt_blk(s, m))),
            pl.BlockSpec((rows_m, TN), lambda s, m: (0, (s >= S_V).astype(jnp.int32))),
            pl.BlockSpec((TN, rows_m), lambda s, m: (0, 0)),
            up_out, dn_out,
        ],
        out_shape=[
            jax.ShapeDtypeStruct((t, 3 * TN), BF16),
            jax.ShapeDtypeStruct((TN, t), BF16),
            jax.ShapeDtypeStruct((rows_m, 2 * TN), BF16),
            jax.ShapeDtypeStruct((TN, rows_m), BF16),
            up_shape, dn_shape,
        ],
        scratch_shapes=[pltpu.VMEM((TN, D_MODEL), BF16), pltpu.VMEM((TN, LANES), F32)],
        compiler_params=_params(("arbitrary", "arbitrary")),
        name="inproj_qkvo",
    )(xn, xnm, wt, b1, w_up, w_down)


def _conv_prologue(m, is_prompt, tiles_per_seq, n_meta, xnm_ref, w_refs, b_refs, s0_ref, s1_ref, scratch,
                   make_xnm=None):
    wu_ref, wc_ref, wb_ref = w_refs
    bu_ref, bc_ref, _ = b_refs
    w3_ref, _, h1_ref, h2_ref, carry_ref, mtail_ref = scratch
    n_slab, tm, _ = h1_ref.shape
    n_seq = tm // SUBLANES
    slabs = [(k, slice(k * LANES, (k + 1) * LANES)) for k in range(n_slab)]
    seq_row = lambda r: pl.ds(r, n_seq, stride=SUBLANES)

    @pl.when(m == 0)
    def _():
        if make_xnm is not None:
            make_xnm()
        w3_ref[0] = wu_ref[...].astype(BF16)
        w3_ref[1] = wc_ref[...].astype(BF16)
        w3_ref[2] = wb_ref[...].astype(BF16)
        h1_ref[...] = jnp.zeros_like(h1_ref)
        h2_ref[...] = jnp.zeros_like(h2_ref)
        xnm = xnm_ref[...]
        cu_m = (_dot_nt(xnm, w3_ref[1]) + bc_ref[...]) * (_dot_nt(xnm, w3_ref[0]) + bu_ref[...])
        mtail_ref[...] = cu_m[n_meta - SUBLANES:n_meta, :]

    @pl.when(is_prompt)
    def _():
        first = (m % tiles_per_seq) == 0
        p6 = jnp.where(first, mtail_ref[6:7, :], carry_ref[6:7, :])
        p7 = jnp.where(first, mtail_ref[7:8, :], carry_ref[7:8, :])
        for k, ks in slabs:
            h2_ref[k, 0:1, :] = p6[:, ks]
            h2_ref[k, 1:2, :] = p7[:, ks]
            h1_ref[k, 0:1, :] = p7[:, ks]

    @pl.when(jnp.logical_not(is_prompt))
    def _():
        for k, ks in slabs:
            h2_ref[k, seq_row(0), :] = s0_ref[:, ks]
            h2_ref[k, seq_row(1), :] = s1_ref[:, ks]
            h1_ref[k, seq_row(0), :] = s1_ref[:, ks]


def _conv_main(xn, is_prompt, b_refs, wconv_ref, yb_ref, cu6_ref, cu7_ref, scratch):
    bu_ref, bc_ref, bb_ref = b_refs
    w3_ref, cu_ref, h1_ref, h2_ref, carry_ref, _ = scratch
    tm, cq = yb_ref.shape
    n_seq = tm // SUBLANES
    slabs = [(k, slice(k * LANES, (k + 1) * LANES)) for k in range(cq // LANES)]
    seq_row = lambda r: pl.ds(r, n_seq, stride=SUBLANES)
    proj = lambda k, b_ref: jnp.concatenate([_dot_nt(x, w3_ref[k]) for x in xn], axis=0) + b_ref[...]
    zu, zc, zb = proj(0, bu_ref), proj(1, bc_ref), proj(2, bb_ref)
    cu = zc * zu
    pos = lax.broadcasted_iota(jnp.int32, (tm, cq), 0) & jnp.where(is_prompt, tm - 1, SUBLANES - 1)
    h1 = jnp.concatenate([h1_ref[k] for k, _ in slabs], axis=1)
    h2 = jnp.concatenate([h2_ref[k] for k, _ in slabs], axis=1)
    x1 = jnp.where(pos >= 1, pltpu.roll(cu, 1, 0), h1)
    x2 = jnp.where(pos >= 2, pltpu.roll(cu, 2, 0), h2)
    w0, w1, w2 = wconv_ref[0:1, :], wconv_ref[1:2, :], wconv_ref[2:3, :]
    yb_ref[...] = (zb * ((w0 * x2 + w1 * x1) + w2 * cu)).astype(BF16)

    carry_ref[...] = cu[tm - SUBLANES:tm, :]
    for k, ks in slabs:
        cu_ref[k] = cu[:, ks]
        cu6_ref[:, ks] = cu_ref[k, seq_row(6), :]
        cu7_ref[:, ks] = cu_ref[k, seq_row(7), :]


def _conv_kernel(n_prompt_tiles, tiles_per_seq, n_meta, n_cast, xn_ref, xnm_ref, wu_ref, wc_ref, wb_ref,
                 bu_ref, bc_ref, bb_ref, wconv_ref, s0_ref, s1_ref, *refs):
    cast_in, (yb_ref, cu6_ref, cu7_ref) = refs[:n_cast], refs[n_cast:n_cast + 3]
    cast_out, scratch = refs[n_cast + 3:2 * n_cast + 3], refs[2 * n_cast + 3:]
    m = pl.program_id(1)
    is_prompt = m < n_prompt_tiles
    b_refs = (bu_ref, bc_ref, bb_ref)
    _conv_prologue(m, is_prompt, tiles_per_seq, n_meta, xnm_ref, (wu_ref, wc_ref, wb_ref), b_refs,
                   s0_ref, s1_ref, scratch)
    _cast_blocks(tuple(zip(cast_in, cast_out)))
    _conv_main([xn_ref[...]], is_prompt, b_refs, wconv_ref, yb_ref, cu6_ref, cu7_ref, scratch)


def _rms_conv_kernel(n_prompt_tiles, tiles_per_seq, dec_seq, xp_ref, xs_ref, xm_ref, g_ref, wg_ref, bg_ref,
                     wu_ref, wc_ref, wb_ref, bu_ref, bc_ref, bb_ref, wconv_ref, s0_ref, s1_ref,
                     xn_ref, gch_ref, grh_ref, xnm_ref, gchm_ref, grhm_ref, yb_ref, cu6_ref, cu7_ref, *scratch):
    m = pl.program_id(0)
    is_prompt = m < n_prompt_tiles
    n_meta = xm_ref.shape[0]
    b_refs = (bu_ref, bc_ref, bb_ref)

    def make_xnm():
        xm = jnp.concatenate([xm_ref[...], jnp.zeros((CHUNK - n_meta, D_MODEL), F32)], axis=0)
        xnm = _rmsnorm(xm, g_ref[...]).astype(BF16)
        xnm_ref[...] = xnm
        _gate_prep(xnm, wg_ref, bg_ref, CHUNK, n_meta, gchm_ref, grhm_ref)

    _conv_prologue(m, is_prompt, tiles_per_seq, n_meta, xnm_ref, (wu_ref, wc_ref, wb_ref), b_refs,
                   s0_ref, s1_ref, scratch, make_xnm)
    tm = xn_ref.shape[0]
    half = tm // 2
    xn_blocks = []
    for r in (slice(0, half), slice(half, tm)):
        x = jnp.where(is_prompt, xp_ref[r, :], xs_ref[r, :])
        xn_blocks.append(_rmsnorm(x, g_ref[...]).astype(BF16))
        xn_ref[r, :] = xn_blocks[-1]
    _gate_prep(jnp.concatenate(xn_blocks, axis=0), wg_ref, bg_ref, jnp.where(is_prompt, CHUNK, dec_seq), tm,
               gch_ref, grh_ref)
    _conv_main(xn_blocks, is_prompt, b_refs, wconv_ref, yb_ref, cu6_ref, cu7_ref, scratch)


def _conv_specs(c0, npt, tm, cm):
    n_seq = tm // SUBLANES
    n_slab = CQ // LANES
    chan = lambda *g: c0 + cm(*g)[0]
    tile = lambda *g: cm(*g)[1]
    w_spec = lambda off: pl.BlockSpec((pl.Element(CQ), pl.Element(D_MODEL)),
                                      lambda *g: (pl.multiple_of(off + chan(*g) * CQ, SUBLANES), 0))
    b_spec = lambda off: pl.BlockSpec((1, CQ), lambda *g: (0, (off - N_GATES) // CQ + chan(*g)))
    s_spec = lambda tok: pl.BlockSpec(
        (n_seq, CQ), lambda *g: (jnp.maximum(tile(*g) - npt, 0), tok * (C_WIDTH // CQ) + chan(*g)))
    in_specs = [w_spec(OFF_U), w_spec(OFF_C), w_spec(OFF_B), b_spec(OFF_U), b_spec(OFF_C), b_spec(OFF_B),
                pl.BlockSpec((CONV_W, CQ), lambda *g: (0, chan(*g))), s_spec(0), s_spec(1)]
    out_specs = [pl.BlockSpec((tm, CQ), lambda *g: (tile(*g), cm(*g)[0])),
                 pl.BlockSpec((n_seq, CQ), lambda *g: (tile(*g), cm(*g)[0])),
                 pl.BlockSpec((n_seq, CQ), lambda *g: (tile(*g), cm(*g)[0]))]
    scratch = [pltpu.VMEM((3, CQ, D_MODEL), BF16),
               pltpu.VMEM((n_slab, tm, LANES), F32),
               pltpu.VMEM((n_slab, tm, LANES), F32),
               pltpu.VMEM((n_slab, tm, LANES), F32),
               pltpu.VMEM((SUBLANES, CQ), F32),
               pltpu.VMEM((SUBLANES, CQ), F32)]
    return in_specs, out_specs, scratch


def _conv_out_shapes(t, n_blocks):
    return [jax.ShapeDtypeStruct((t, n_blocks * CQ), BF16),
            jax.ShapeDtypeStruct((t // SUBLANES, n_blocks * CQ), F32),
            jax.ShapeDtypeStruct((t // SUBLANES, n_blocks * CQ), F32)]


def _rms_conv_call(xp, xs, x_meta, g, wt, bg, b1, wconv, sconv, dec_seq, seq_len, tm):
    tp, ts = xp.shape[0], xs.shape[0]
    t = tp + ts
    assert tm & (tm - 1) == 0 and seq_len % tm == 0 and x_meta.shape[0] >= SUBLANES
    npt, nst = tp // tm, ts // tm
    conv_in, conv_out, scratch = _conv_specs(0, npt, tm, lambda m: (0, m))
    return pl.pallas_call(
        functools.partial(_rms_conv_kernel, npt, seq_len // tm, dec_seq),
        grid=(npt + nst,),
        in_specs=[
            pl.BlockSpec((tm, D_MODEL), lambda m: (jnp.minimum(m, npt - 1), 0)),
            pl.BlockSpec((tm, D_MODEL), lambda m: (jnp.maximum(m - npt, 0), 0)),
            pl.BlockSpec(x_meta.shape, lambda m: (0, 0)),
            pl.BlockSpec((1, D_MODEL), lambda m: (0, 0)),
            pl.BlockSpec((N_GATES, D_MODEL), lambda m: (OFF_GATES // N_GATES, 0)),
            pl.BlockSpec((N_GATES, 1), lambda m: (0, 0)),
        ] + conv_in,
        out_specs=[
            pl.BlockSpec((tm, D_MODEL), lambda m: (m, 0)),
            pl.BlockSpec((M_HEADS, 2, tm, LANES), lambda m: (0, 0, m, 0)),
            pl.BlockSpec((M_HEADS, SUBLANES, tm), lambda m: (0, 0, m)),
            pl.BlockSpec((CHUNK, D_MODEL), lambda m: (0, 0)),
            pl.BlockSpec((M_HEADS, 2, CHUNK, LANES), lambda m: (0, 0, 0, 0)),
            pl.BlockSpec((M_HEADS, SUBLANES, CHUNK), lambda m: (0, 0, 0)),
        ] + conv_out,
        out_shape=[
            jax.ShapeDtypeStruct((t, D_MODEL), BF16),
            jax.ShapeDtypeStruct((M_HEADS, 2, t, LANES), F32),
            jax.ShapeDtypeStruct((M_HEADS, SUBLANES, t), F32),
            jax.ShapeDtypeStruct((CHUNK, D_MODEL), BF16),
            jax.ShapeDtypeStruct((M_HEADS, 2, CHUNK, LANES), F32),
            jax.ShapeDtypeStruct((M_HEADS, SUBLANES, CHUNK), F32),
        ] + _conv_out_shapes(t, 1),
        scratch_shapes=scratch,
        compiler_params=_params(("arbitrary",)),
        name="rms_conv",
    )(xp, xs, x_meta, g, wt, bg, wt, wt, wt, b1, b1, b1, wconv, sconv, sconv)


def _conv_call(xn, xnm, n_meta, wt, b1, wconv, sconv, cast_ws, n_prompt_tokens, seq_len, tm):
    t = xn.shape[0]
    assert tm & (tm - 1) == 0 and seq_len % tm == 0 and n_meta >= SUBLANES
    npt = n_prompt_tokens // tm
    n_m = t // tm
    n_blocks = C_WIDTH // CQ - 1
    n_cast = 16
    assert n_cast <= n_blocks * n_m
    casts = [_cast_specs(w, n_cast, lambda c, m: c * n_m + m) for w in cast_ws]
    conv_in, conv_out, scratch = _conv_specs(1, npt, tm, lambda c, m: (c, m))
    return pl.pallas_call(
        functools.partial(_conv_kernel, npt, seq_len // tm, n_meta, len(cast_ws)),
        grid=(n_blocks, n_m),
        in_specs=[
            pl.BlockSpec((tm, D_MODEL), lambda c, m: (m, 0)),
            pl.BlockSpec(xnm.shape, lambda c, m: (0, 0)),
        ] + conv_in + [cs[0] for cs in casts],
        out_specs=conv_out + [cs[1] for cs in casts],
        out_shape=_conv_out_shapes(t, n_blocks) + [cs[2] for cs in casts],
        scratch_shapes=scratch,
        compiler_params=_params(("arbitrary", "arbitrary")),
        name="inproj_conv",
    )(xn, xnm, wt, wt, wt, b1, b1, b1, wconv, sconv, sconv, *cast_ws)


def _prompt_head(q, kt, v, gr, gc, ca, m_st):
    l = q.shape[0]
    assert l == LANES
    ig_row, b_row, r_row = gr[G_IG:G_IG + 1, :], gr[G_B:G_B + 1, :], gr[G_R:G_R + 1, :]
    b_col = gc[0]
    v_aug = jnp.concatenate([v, jnp.ones((l, LANES), BF16)], axis=1)
    t_idx = lax.broadcasted_iota(jnp.int32, (l, l), 0)
    s_idx = lax.broadcasted_iota(jnp.int32, (l, l), 1)
    logw = jnp.where(s_idx <= t_idx, b_col - b_row + ig_row, NEG)
    inter = b_col + m_st
    m_t = jnp.maximum(inter, gc[1])
    w_inter = jnp.exp(inter - m_t)
    s = _dot(q, kt) * jnp.exp(logw - m_t)
    nd = (jnp.concatenate([w_inter] * (AUG // LANES), axis=1) * _dot(q, ca.astype(BF16))
          + _dot(s.astype(BF16), v_aug))
    rcp = 1.0 / jnp.maximum(jnp.abs(nd[:, M_DV:AUG]), jnp.exp(-m_t))
    h = nd[:, 0:M_DV] * jnp.concatenate([rcp] * (M_DV // LANES), axis=1)
    m_new = m_t[l - 1:l, 0:1]
    decay = jnp.exp(b_col[l - 1:l, 0:1] + m_st - m_new)
    kwt = (kt.astype(F32) * jnp.exp(r_row + ig_row - m_new)).astype(BF16)
    return h, decay * ca + _dot(kwt, v_aug), m_new


def _mlstm_prompt_kernel(batch, qm_ref, ktm_ref, vm_ref, gcm_ref, grm_ref, *refs):
    ins, (h_ref, c_ref, n_ref, m_ref, ca_ref) = refs[:5 * batch], refs[5 * batch:]
    heads = [(hd, slice(hd * M_DK, (hd + 1) * M_DK)) for hd in range(M_HEADS)]

    @pl.when(pl.program_id(0) == 0)
    def _():
        for hd, sl in heads:
            _, ca_new, m_new = _prompt_head(qm_ref[:, sl], ktm_ref[sl, :], vm_ref[:, sl], grm_ref[hd],
                                            gcm_ref[hd], jnp.zeros((M_DK, AUG), F32), jnp.zeros((1, 1), F32))
            for b in range(batch):
                ca_ref[b, hd] = ca_new
                m_ref[b, hd:hd + 1, :] = jnp.broadcast_to(m_new, (1, LANES))

    for b in range(batch):
        q_ref, kt_ref, v_ref, gc_ref, gr_ref = ins[5 * b:5 * b + 5]
        for hd, sl in heads:
            h, ca_new, m_new = _prompt_head(q_ref[:, sl], kt_ref[sl, :], v_ref[:, sl], gr_ref[hd], gc_ref[hd],
                                            ca_ref[b, hd], m_ref[b, hd:hd + 1, 0:1])
            ca_ref[b, hd] = ca_new
            m_ref[b, hd:hd + 1, :] = jnp.broadcast_to(m_new, (1, LANES))
            h_ref[b, :, sl] = h.astype(h_ref.dtype)

    @pl.when(pl.program_id(0) == pl.num_programs(0) - 1)
    def _():
        for b in range(batch):
            for hd, _ in heads:
                ca = ca_ref[b, hd]
                c_ref[b, hd] = ca[:, 0:M_DV]
                n_t = ca[:, M_DV:AUG]
                n_ref[b, hd:hd + 1, :] = jnp.concatenate(
                    [n_t[k * LANES:(k + 1) * LANES, :].T[0:1, :] for k in range(M_DK // LANES)], axis=1)


def _mlstm_prompt_call(zm, ktm, gcm, grm, z, kt, gch, grh, batch, seq):
    nc = seq // CHUNK
    per_prompt_specs, per_prompt_args = [], []
    for b in range(batch):
        row = functools.partial(lambda b, c: b * nc + c, b)
        per_prompt_specs += [
            pl.BlockSpec((CHUNK, M_WIDTH), lambda c, row=row: (row(c), Z_Q)),
            pl.BlockSpec((M_WIDTH, CHUNK), lambda c, row=row: (0, row(c))),
            pl.BlockSpec((CHUNK, M_WIDTH), lambda c, row=row: (row(c), Z_V)),
            pl.BlockSpec((M_HEADS, 2, CHUNK, LANES), lambda c, row=row: (0, 0, row(c), 0)),
            pl.BlockSpec((M_HEADS, SUBLANES, CHUNK), lambda c, row=row: (0, 0, row(c))),
        ]
        per_prompt_args += [z, kt, z, gch, grh]
    return pl.pallas_call(
        functools.partial(_mlstm_prompt_kernel, batch),
        grid=(nc,),
        in_specs=[
            pl.BlockSpec((CHUNK, M_WIDTH), lambda c: (0, 0)),
            pl.BlockSpec((M_WIDTH, CHUNK), lambda c: (0, 0)),
            pl.BlockSpec((CHUNK, M_WIDTH), lambda c: (0, 1)),
            pl.BlockSpec((M_HEADS, 2, CHUNK, LANES), lambda c: (0, 0, 0, 0)),
            pl.BlockSpec((M_HEADS, SUBLANES, CHUNK), lambda c: (0, 0, 0)),
        ] + per_prompt_specs,
        out_specs=[
            pl.BlockSpec((batch, CHUNK, M_WIDTH), lambda c: (0, c, 0)),
            pl.BlockSpec((batch, M_HEADS, M_DK, M_DV), lambda c: (0, 0, 0, 0)),
            pl.BlockSpec((batch, M_HEADS, M_DK), lambda c: (0, 0, 0)),
            pl.BlockSpec((batch, M_HEADS, LANES), lambda c: (0, 0, 0)),
        ],
        out_shape=[
            jax.ShapeDtypeStruct((batch, seq, M_WIDTH), BF16),
            jax.ShapeDtypeStruct((batch, M_HEADS, M_DK, M_DV), F32),
            jax.ShapeDtypeStruct((batch, M_HEADS, M_DK), F32),
            jax.ShapeDtypeStruct((batch, M_HEADS, LANES), F32),
        ],
        scratch_shapes=[pltpu.VMEM((batch, M_HEADS, M_DK, AUG), F32)],
        compiler_params=_params(("arbitrary",)),
        name="mlstm_prompt",
    )(zm, ktm, zm, gcm, grm, *per_prompt_args)


def _group_max(x, size):
    n = x.shape[-1]
    lane = lax.broadcasted_iota(jnp.int32, x.shape, x.ndim - 1)
    k = 1
    while k < size:
        partner = jnp.where((lane & k) == 0, pltpu.roll(x, n - k, x.ndim - 1), pltpu.roll(x, k, x.ndim - 1))
        x = jnp.maximum(x, partner)
        k *= 2
    return x


def _mlstm_sample_kernel(dec_seq, q_ref, kt_ref, v_ref, gc_ref, gr_ref, mcol_ref, mrow_ref, cin_ref, nin_ref,
                         h_ref, c_ref, n_ref, m_ref):
    l = q_ref.shape[0]
    n_seq = l // dec_seq
    shift = dec_seq.bit_length() - 1
    q, kt, v = q_ref[...], kt_ref[...], v_ref[...]
    assert l == LANES
    gr, gc = gr_ref[0], gc_ref[0]
    ig_row, b_row, r_row = gr[G_IG:G_IG + 1, :], gr[G_B:G_B + 1, :], gr[G_R:G_R + 1, :]
    b_col = gc[0]
    m_col, m_row = mcol_ref[0], mrow_ref[0]

    t_idx = lax.broadcasted_iota(jnp.int32, (l, l), 0)
    s_idx = lax.broadcasted_iota(jnp.int32, (l, l), 1)
    t_seq = t_idx >> shift
    logw = jnp.where((t_seq == (s_idx >> shift)) & (s_idx <= t_idx), b_col - b_row + ig_row, NEG)
    inter = b_col + m_col
    m_t = jnp.maximum(inter, gc[1])
    w_inter = jnp.exp(inter - m_t)
    s = _dot(q, kt) * jnp.exp(logw - m_t)

    e_row = r_row + ig_row
    b_last = b_row + r_row
    m_new = jnp.maximum(b_last + m_row, _group_max(e_row, dec_seq))
    decay = jnp.exp(b_last + m_row - m_new)
    kwt = (kt.astype(F32) * jnp.exp(e_row - m_new)).astype(BF16)

    seq_lane = s_idx == t_seq
    v_aug = jnp.concatenate([v, jnp.where(seq_lane, 1.0, 0.0).astype(BF16)], axis=1)
    pa = _dot(s.astype(BF16), v_aug)

    n_rows = jnp.concatenate([nin_ref[...], jnp.zeros((LANES - n_seq, M_DK), F32)], axis=0)
    n_t = jnp.concatenate([n_rows[:, 0:LANES].T, n_rows[:, LANES:2 * LANES].T], axis=0)
    lane_k = lax.broadcasted_iota(jnp.int32, (M_DK, LANES), 1)
    qf = q.astype(F32)
    qca_rows = []
    n_t_new = jnp.zeros((M_DK, LANES), F32)
    for i in range(n_seq):
        ca = jnp.concatenate([cin_ref[i, 0], jnp.where(lane_k == i, n_t, 0.0)], axis=1)
        rows = slice(i * dec_seq, (i + 1) * dec_seq)
        qca_rows.append(_dot(qf[rows, :].astype(BF16), ca.astype(BF16)))
        kw_i = jnp.where((lane_k >> shift) == i, kwt, jnp.zeros_like(kwt))
        ca_new = decay[:, i * dec_seq:i * dec_seq + 1] * ca + _dot(kw_i, v_aug)
        c_ref[i, 0] = ca_new[:, 0:M_DV]
        n_t_new = n_t_new + ca_new[:, M_DV:AUG]
    nd = jnp.concatenate([w_inter] * (AUG // LANES), axis=1) * jnp.concatenate(qca_rows, axis=0) + pa
    den = jnp.sum(jnp.where(seq_lane, nd[:, M_DV:AUG], 0.0), axis=-1, keepdims=True)
    rcp = 1.0 / jnp.maximum(jnp.abs(den), jnp.exp(-m_t[:, 0:1]))
    h_ref[...] = (nd[:, 0:M_DV] * rcp).astype(h_ref.dtype)
    n_ref[...] = jnp.concatenate([n_t_new[0:LANES, :].T, n_t_new[LANES:2 * LANES, :].T], axis=1)[0:n_seq, :]
    m_ref[0] = jnp.broadcast_to(m_new, (SUBLANES, l))


def _gates_sample_kernel(dec_seq, xn_ref, wt_ref, b_ref, *refs):
    sample_in, z_ref, sample_out, w_ref = refs[:9], refs[9], refs[10:14], refs[14]

    @pl.when(pl.program_id(1) == 0)
    def _():
        w_ref[...] = wt_ref[...].astype(BF16)

    z_ref[...] = _sigmoid(_dot_nt(xn_ref[...], w_ref[...]) + b_ref[...]).astype(BF16)
    _mlstm_sample_kernel(dec_seq, *sample_in, *sample_out)


def _gates_sample_call(xn, wt, b1, zqv, kt, gch, grh, m_col, m_row, c_in, n_in, row0, dec_seq, tm):
    t = xn.shape[0]
    n_m = t // tm
    nb = c_in.shape[0]
    l = SEQ_PER_STEP * dec_seq
    blk0 = row0 // l
    n_steps = 2 * D_MODEL // TN
    n_guest = (nb // SEQ_PER_STEP) * M_HEADS
    assert n_guest <= n_steps * n_m
    w_off = lambda s: OFF_GA + s * TN
    b_blk = lambda s: (OFF_GA - N_GATES) // TN + s
    z_col = lambda s: s
    guest = lambda s, m: jnp.minimum(s * n_m + m, n_guest - 1)
    gi = lambda s, m: guest(s, m) // M_HEADS
    gh = lambda s, m: guest(s, m) % M_HEADS
    return pl.pallas_call(
        functools.partial(_gates_sample_kernel, dec_seq),
        grid=(n_steps, n_m),
        in_specs=[
            pl.BlockSpec((tm, D_MODEL), lambda s, m: (m, 0)),
            pl.BlockSpec((pl.Element(TN), pl.Element(D_MODEL)),
                         lambda s, m: (pl.multiple_of(w_off(s), SUBLANES), 0)),
            pl.BlockSpec((1, TN), lambda s, m: (0, b_blk(s))),
            pl.BlockSpec((l, M_DK), lambda s, m: (blk0 + gi(s, m), Z_Q * M_HEADS + gh(s, m))),
            pl.BlockSpec((M_DK, l), lambda s, m: (gh(s, m), blk0 + gi(s, m))),
            pl.BlockSpec((l, M_DV), lambda s, m: (blk0 + gi(s, m), Z_V * M_HEADS + gh(s, m))),
            pl.BlockSpec((1, 2, l, LANES), lambda s, m: (gh(s, m), 0, blk0 + gi(s, m), 0)),
            pl.BlockSpec((1, SUBLANES, l), lambda s, m: (gh(s, m), 0, blk0 + gi(s, m))),
            pl.BlockSpec((1, l, LANES), lambda s, m: (gh(s, m), gi(s, m), 0)),
            pl.BlockSpec((1, 1, l), lambda s, m: (gh(s, m), 0, gi(s, m))),
            pl.BlockSpec((SEQ_PER_STEP, 1, M_DK, M_DV), lambda s, m: (gi(s, m), gh(s, m), 0, 0)),
            pl.BlockSpec((SEQ_PER_STEP, M_DK), lambda s, m: (gi(s, m), gh(s, m))),
        ],
        out_specs=[
            pl.BlockSpec((tm, TN), lambda s, m: (m, z_col(s))),
            pl.BlockSpec((l, M_DV), lambda s, m: (gi(s, m), gh(s, m))),
            pl.BlockSpec((SEQ_PER_STEP, 1, M_DK, M_DV), lambda s, m: (gi(s, m), gh(s, m), 0, 0)),
            pl.BlockSpec((SEQ_PER_STEP, M_DK), lambda s, m: (gi(s, m), gh(s, m))),
            pl.BlockSpec((1, SUBLANES, l), lambda s, m: (gh(s, m), 0, gi(s, m))),
        ],
        out_shape=[
            jax.ShapeDtypeStruct((t, ZG_COLS), BF16),
            jax.ShapeDtypeStruct((nb * dec_seq, M_WIDTH), BF16),
            jax.ShapeDtypeStruct(c_in.shape, F32),
            jax.ShapeDtypeStruct(n_in.shape, F32),
            jax.ShapeDtypeStruct((M_HEADS, SUBLANES, nb * dec_seq), F32),
        ],
        scratch_shapes=[pltpu.VMEM((TN, D_MODEL), BF16)],
        compiler_params=_params(("arbitrary", "arbitrary")),
        name="inproj_gates_mlstm_sample",
    )(xn, wt, b1, zqv, kt, zqv, gch, grh, m_col, m_row, c_in, n_in)


def _mix_kernel(n_prompt_tiles, hp_ref, hs_ref, so_ref, yb0_ref, yb1_ref, ga_ref, gb_ref, xp_ref, xs_ref,
                pa_ref, pb_ref, wo_ref, gffn_ref, o_ref, on_ref):
    i = pl.program_id(0)

    def body(h_ref, x_ref):
        h_a = (so_ref[...].astype(F32) * h_ref[...].astype(F32)).astype(BF16)
        a = _dot(h_a, pa_ref[...])
        b = _dot(jnp.concatenate([yb0_ref[...], yb1_ref[...]], axis=1), pb_ref[...])
        merged = ga_ref[...].astype(F32) * a + gb_ref[...].astype(F32) * b
        x1 = x_ref[...] + _dot(merged.astype(BF16), wo_ref[...])
        o_ref[...] = x1
        on_ref[...] = _rmsnorm(x1, gffn_ref[...]).astype(BF16)

    @pl.when(i < n_prompt_tiles)
    def _():
        body(hp_ref, xp_ref)

    @pl.when(i >= n_prompt_tiles)
    def _():
        body(hs_ref, xs_ref)


def _mix_call(hp, hs, zqvo, zg, yb0, yb1, xp, xs, pa, pb, wo, gffn, tm):
    tp, ts = xp.shape[0], xs.shape[0]
    npt, nst = tp // tm, ts // tm
    pi = lambda i: jnp.minimum(i, npt - 1)
    si = lambda i: jnp.maximum(i - npt, 0)
    const = lambda i: (0, 0)
    return pl.pallas_call(
        functools.partial(_mix_kernel, npt),
        grid=(npt + nst,),
        in_specs=[
            pl.BlockSpec((tm, M_WIDTH), lambda i: (pi(i), 0)),
            pl.BlockSpec((tm, M_WIDTH), lambda i: (si(i), 0)),
            pl.BlockSpec((tm, M_WIDTH), lambda i: (i, Z_SO)),
            pl.BlockSpec((tm, yb0.shape[1]), lambda i: (i, 0)),
            pl.BlockSpec((tm, yb1.shape[1]), lambda i: (i, 0)),
            pl.BlockSpec((tm, D_MODEL), lambda i: (i, Z_GA // 2)),
            pl.BlockSpec((tm, D_MODEL), lambda i: (i, Z_GB // 2)),
            pl.BlockSpec((tm, D_MODEL), lambda i: (pi(i), 0)),
            pl.BlockSpec((tm, D_MODEL), lambda i: (si(i), 0)),
            pl.BlockSpec((M_WIDTH, D_MODEL), const, pipeline_mode=pl.Buffered(1)),
            pl.BlockSpec((C_WIDTH, D_MODEL), const, pipeline_mode=pl.Buffered(1)),
            pl.BlockSpec((D_MODEL, D_MODEL), const, pipeline_mode=pl.Buffered(1)),
            pl.BlockSpec((1, D_MODEL), const),
        ],
        out_specs=[pl.BlockSpec((tm, D_MODEL), lambda i: (i, 0)), pl.BlockSpec((tm, D_MODEL), lambda i: (i, 0))],
        out_shape=[jax.ShapeDtypeStruct((tp + ts, D_MODEL), F32), jax.ShapeDtypeStruct((tp + ts, D_MODEL), BF16)],
        compiler_params=_params(("arbitrary",)),
        name="mix",
    )(hp, hs, zqvo, yb0, yb1, zg, zg, xp, xs, pa, pb, wo, gffn)


def _ffn_kernel(n_prompt_tiles, x1_ref, xn_ref, wup_ref, wdn_ref, gfin_ref, yp_ref, ys_ref, acc_ref):
    i = pl.program_id(0)
    f = pl.program_id(1)

    def mlp_part():
        u = jnp.maximum(_dot(xn_ref[...], wup_ref[...]), 0.0)
        return _dot((u * u).astype(BF16), wdn_ref[...])

    @pl.when(f == 0)
    def _():
        acc_ref[...] = x1_ref[...] + mlp_part()

    last = pl.num_programs(1) - 1

    @pl.when((f > 0) & (f < last))
    def _():
        acc_ref[...] += mlp_part()

    @pl.when(f == last)
    def _():
        y = _rmsnorm(acc_ref[...] + mlp_part(), gfin_ref[...])
        ys_ref[...] = y
        yp_ref[...] = jnp.where(i < n_prompt_tiles, y, yp_ref[...])


def _ffn_call(x1, xn, wup, wdn, gfin, n_prompt_tokens, tm, tf):
    t = x1.shape[0]
    npt = n_prompt_tokens // tm
    n_tiles = t // tm
    return pl.pallas_call(
        functools.partial(_ffn_kernel, npt),
        grid=(n_tiles, D_FF // tf),
        in_specs=[
            pl.BlockSpec((tm, D_MODEL), lambda i, f: (i, 0)),
            pl.BlockSpec((tm, D_MODEL), lambda i, f: (i, 0)),
            pl.BlockSpec((D_MODEL, tf), lambda i, f: (0, f)),
            pl.BlockSpec((tf, D_MODEL), lambda i, f: (f, 0)),
            pl.BlockSpec((1, D_MODEL), lambda i, f: (0, 0)),
        ],
        out_specs=[
            pl.BlockSpec((tm, D_MODEL), lambda i, f: (jnp.minimum(i, npt - 1), 0)),
            pl.BlockSpec((tm, D_MODEL), lambda i, f: (jnp.maximum(i - npt, 0), 0)),
        ],
        out_shape=[
            jax.ShapeDtypeStruct((n_prompt_tokens, D_MODEL), F32),
            jax.ShapeDtypeStruct((t - n_prompt_tokens, D_MODEL), F32),
        ],
        scratch_shapes=[pltpu.VMEM((tm, D_MODEL), F32)],
        compiler_params=_params(("arbitrary", "arbitrary")),
        name="ffn",
    )(x1, xn, wup, wdn, gfin)


def kernel(x_prompt, x_sample, state_mlstm_C, state_mlstm_n, state_mlstm_m, state_conv, meta_tokens,
           g_mix, w_in, b_in, w_conv, p_a, p_b, w_o, g_ffn, w_up, w_down, g_final):
    assert w_in.shape[0] == 1, "single-layer trunk"
    batch, seq, _ = x_prompt.shape
    dec_batch, dec_seq, _ = x_sample.shape
    assert dec_seq == SUBLANES and seq % 1024 == 0 and dec_batch % SEQ_PER_STEP == 0
    n_p, n_s = batch * seq, dec_batch * dec_seq

    wt = w_in[0].T
    bias = b_in[0]
    b1 = jnp.concatenate([bias[:OFF_GATES], bias[OFF_O:]])
    bg = bias[OFF_GATES:OFF_O][:, None]
    gmix = g_mix[0][None, :]

    xp = x_prompt.reshape(n_p, D_MODEL)
    xs = x_sample.reshape(n_s, D_MODEL)
    n_meta = meta_tokens.shape[0]

    sconv = state_conv[0].reshape(dec_batch, (CONV_W - 1) * C_WIDTH)
    xn, gch, grh, xnm, gcm, grm, yb0, cu6a, cu7a = _rms_conv_call(
        xp, xs, meta_tokens.astype(F32), gmix, wt, bg, b1[None, :], w_conv[0], sconv, dec_seq, seq, 512)
    zqv, kt, zm, ktm, w_up16, w_down16 = _qkvo_call(xn, xnm, wt, b1[None, :], w_up[0], w_down[0], 1024)
    m_tok = jnp.repeat(state_mlstm_m[0], dec_seq, axis=0).T
    zg, hs, c_s, n_sm, m_s = _gates_sample_call(
        xn, wt, b1[None, :], zqv, kt, gch, grh,
        jnp.broadcast_to(m_tok[:, :, None], m_tok.shape + (LANES,)), m_tok[:, None, :],
        state_mlstm_C[0], state_mlstm_n[0].reshape(dec_batch, M_HEADS * M_DK), n_p, dec_seq, 1024)
    yb1, cu6b, cu7b, p_a16, p_b16, w_o16 = _conv_call(
        xn, xnm, n_meta, wt, b1[None, :], w_conv[0], sconv, (p_a[0], p_b[0], w_o[0]), n_p, seq, 1024)

    hp, c_p, n_pr, m_p = _mlstm_prompt_call(zm, ktm, gcm, grm, zqv, kt, gch, grh, batch, seq)

    x1, xn2 = _mix_call(hp.reshape(n_p, M_WIDTH), hs, zqv, zg, yb0, yb1, xp, xs, p_a16, p_b16, w_o16,
                        g_ffn[0][None, :], 256)
    y_p, y_s = _ffn_call(x1, xn2, w_up16, w_down16, g_final[None, :], n_p, 512, 1024)

    per_seq = seq // SUBLANES
    first_s = n_p // SUBLANES
    pick = lambda rows: jnp.stack([jnp.concatenate([cu6a[rows], cu6b[rows]], axis=1),
                                   jnp.concatenate([cu7a[rows], cu7b[rows]], axis=1)], axis=1)
    cv_p = pick(slice(per_seq - 1, batch * per_seq, per_seq))
    cv_s = pick(slice(first_s, None))
    m_s = m_s[:, 0, dec_seq - 1::dec_seq].T
    return (y_p.reshape(batch, seq, D_MODEL), y_s.reshape(dec_batch, dec_seq, D_MODEL),
            c_p[None], n_pr[None], m_p[None, :, :, 0], cv_p[None],
            c_s[None], n_sm.reshape(dec_batch, M_HEADS, M_DK)[None], m_s[None], cv_s[None])
```

```python
import functools

import jax
import jax.numpy as jnp
from jax import lax
from jax.experimental import pallas as pl
from jax.experimental.pallas import tpu as pltpu

F32 = jnp.float32
BF16 = jnp.bfloat16

D_MODEL = 2048
N_META = 16
CHUNK = 128
M_HEADS = 4
M_DK = 256
M_DV = 256
M_WIDTH = M_HEADS * M_DV
C_WIDTH = 1024
CONV_W = 3
D_FF = 4 * D_MODEL
EPS = 1e-6
N_GATES = 2 * M_HEADS
OFF_Q = 0
OFF_K = OFF_Q + M_HEADS * M_DK
OFF_V = OFF_K + M_HEADS * M_DK
OFF_GATES = OFF_V + M_WIDTH
OFF_O = OFF_GATES + N_GATES
OFF_U = OFF_O + M_WIDTH
OFF_C = OFF_U + C_WIDTH
OFF_B = OFF_C + C_WIDTH
OFF_GA = OFF_B + C_WIDTH
OFF_GB = OFF_GA + D_MODEL

LANES = 128
SUBLANES = 8
VMEM_LIMIT_BYTES = 60000 * 1024

TN = 1024
Z_Q, Z_V, Z_SO = 0, 1, 2
Z_GA, Z_GB = 0, 2
ZG_COLS = 4 * TN
CQ = 256
AUG = M_DV + LANES
SEQ_PER_STEP = CHUNK // SUBLANES
G_IG, G_B, G_R = 0, 1, 2

NEG = -1e30
NT_DIMS = (((1,), (1,)), ((), ()))


def _params(semantics):
    return pltpu.CompilerParams(dimension_semantics=semantics, vmem_limit_bytes=VMEM_LIMIT_BYTES)


def _rmsnorm(x, g):
    y = x * lax.rsqrt(jnp.mean(x * x, axis=-1, keepdims=True) + EPS)
    return y * g


def _log_sigmoid(x):
    return jnp.minimum(x, 0.0) - jnp.log1p(jnp.exp(-jnp.abs(x)))


def _dot(a, b):
    return jnp.dot(a, b, preferred_element_type=F32)


def _dot_nt(a, b):
    return lax.dot_general(a, b, NT_DIMS, preferred_element_type=F32)


def _gate_prep(xn, wg_ref, bg_ref, blk, n_valid, gch_ref, grh_ref):
    tm = xn.shape[0]
    wg = wg_ref[...].astype(BF16)
    wg = jnp.concatenate([wg, jnp.zeros((LANES - N_GATES, wg.shape[1]), BF16)], axis=0)
    g = _dot_nt(wg, xn)[0:SUBLANES, :] + bg_ref[...]
    row = lax.broadcasted_iota(jnp.int32, (SUBLANES, tm), 0)
    lane = lax.broadcasted_iota(jnp.int32, (SUBLANES, tm), 1)
    a = jnp.where(row < M_HEADS, g, _log_sigmoid(g))
    if n_valid < tm:
        a = jnp.where(lane < n_valid, a, jnp.where(row < M_HEADS, NEG, 0.0))
    pos = lane & (blk - 1)
    n_steps_blk = blk if isinstance(blk, int) else LANES

    def scan(x, op, fill, reverse=False):
        shift = 1
        while shift < n_steps_blk:
            if reverse:
                x = op(x, jnp.where(pos < blk - shift, pltpu.roll(x, tm - shift, 1), fill))
            else:
                x = op(x, jnp.where(pos >= shift, pltpu.roll(x, shift, 1), fill))
            shift *= 2
        return x

    pre = scan(a, jnp.add, 0.0)
    suf = scan(a, jnp.add, 0.0, reverse=True) - a
    b_up = pltpu.roll(pre, M_HEADS, 0)
    m_in = b_up + scan(a - b_up, jnp.maximum, -3e38)
    for h in range(M_HEADS):
        grh_ref[h] = jnp.where(
            row == G_IG, pltpu.roll(a, (G_IG - h) % SUBLANES, 0),
            jnp.where(row == G_B, pltpu.roll(pre, (G_B - M_HEADS - h) % SUBLANES, 0),
                      jnp.where(row == G_R, pltpu.roll(suf, (G_R - M_HEADS - h) % SUBLANES, 0), 0.0)))
        for c in range(tm // LANES):
            cs = slice(c * LANES, (c + 1) * LANES)
            gch_ref[h, 0, cs, :] = jnp.broadcast_to(b_up[h:h + 1, cs], (LANES, LANES)).T
            gch_ref[h, 1, cs, :] = jnp.broadcast_to(m_in[h:h + 1, cs], (LANES, LANES)).T


S_Q, S_K, S_V, S_O = 0, 1, 2, 3


def _cast_blocks(pairs):
    for src_ref, dst_ref in pairs:
        dst_ref[...] = src_ref[...].astype(BF16)


def _cast_specs(w, n_blocks, step):
    rows, cols = w.shape
    blk = lambda *g: (jnp.minimum(step(*g), n_blocks - 1), 0)
    spec = pl.BlockSpec((rows // n_blocks, cols), blk)
    return spec, spec, jax.ShapeDtypeStruct(w.shape, BF16)


def _sigmoid(z):
    return 0.5 * jnp.tanh(0.5 * z) + 0.5


def _qkvo_kernel(xn_ref, xnm_ref, wt_ref, b_ref, wup_ref, wdn_ref,
                 z_ref, kt_ref, zm_ref, ktm_ref, wup16_ref, wdn16_ref, w_ref, bcol_ref):
    s = pl.program_id(0)
    casts = ((wup_ref, wup16_ref), (wdn_ref, wdn16_ref))
    k_scale = M_DK ** -0.5
    lane_tiles = lambda x, n: jnp.concatenate([x] * (n // LANES), axis=1)

    @pl.when(pl.program_id(1) == 0)
    def _():
        w_ref[...] = wt_ref[...].astype(BF16)

        @pl.when((s == S_Q) | (s == S_V))
        def _():
            zm_ref[...] = (_dot_nt(xnm_ref[...], w_ref[...]) + b_ref[...]).astype(BF16)

        @pl.when(s == S_K)
        def _():
            for c in range(TN // LANES):
                cs = slice(c * LANES, (c + 1) * LANES)
                bcol_ref[cs, :] = jnp.broadcast_to(b_ref[:, cs], (LANES, LANES)).T
            ktm = _dot_nt(w_ref[...], xnm_ref[...]) + lane_tiles(bcol_ref[...], xnm_ref.shape[0])
            ktm_ref[...] = (ktm * k_scale).astype(BF16)

    def z():
        return _dot_nt(xn_ref[...], w_ref[...]) + b_ref[...]

    @pl.when((s == S_Q) | (s == S_V))
    def _():
        _cast_blocks(casts)
        z_ref[...] = z().astype(BF16)

    @pl.when(s == S_K)
    def _():
        _cast_blocks(casts)
        kt = _dot_nt(w_ref[...], xn_ref[...]) + lane_tiles(bcol_ref[...], xn_ref.shape[0])
        kt_ref[...] = (kt * k_scale).astype(BF16)

    @pl.when(s == S_O)
    def _():
        _cast_blocks(casts)
        z_ref[...] = _sigmoid(z()).astype(BF16)


def _qkvo_call(xn, xnm, wt, b1, w_up, w_down, tm):
    t = xn.shape[0]
    assert t % tm == 0
    n_m = t // tm
    rows_m = xnm.shape[0]
    n_cast = 32
    assert n_cast <= 4 * n_m
    up_in, up_out, up_shape = _cast_specs(w_up, n_cast, lambda s, m: s * n_m + m)
    dn_in, dn_out, dn_shape = _cast_specs(w_down, n_cast, lambda s, m: s * n_m + m)
    w_off = lambda s: jnp.where(s == S_O, OFF_O, s * TN)
    z_col = lambda s: jnp.where(s <= S_K, Z_Q, s - 1)
    z_row = lambda s, m: jnp.where(s == S_K, n_m - 1, m)
    kt_blk = lambda s, m: jnp.where(s == S_K, m, jnp.where(s < S_K, 0, n_m - 1))
    return pl.pallas_call(
        _qkvo_kernel,
        grid=(4, n_m),
        in_specs=[
            pl.BlockSpec((tm, D_MODEL), lambda s, m: (m, 0)),
            pl.BlockSpec((rows_m, D_MODEL), lambda s, m: (0, 0)),
            pl.BlockSpec((pl.Element(TN), pl.Element(D_MODEL)),
                         lambda s, m: (pl.multiple_of(w_off(s), SUBLANES), 0)),
            pl.BlockSpec((1, TN), lambda s, m: (0, s)),
            up_in, dn_in,
        ],
        out_specs=[
            pl.BlockSpec((tm, TN), lambda s, m: (z_row(s, m), z_col(s))),
            pl.BlockSpec((TN, tm), lambda s, m: (0, kt_blk(s, m))),
            pl.BlockSpec((rows_m, TN), lambda s, m: (0, (s >= S_V).astype(jnp.int32))),
            pl.BlockSpec((TN, rows_m), lambda s, m: (0, 0)),
            up_out, dn_out,
        ],
        out_shape=[
            jax.ShapeDtypeStruct((t, 3 * TN), BF16),
            jax.ShapeDtypeStruct((TN, t), BF16),
            jax.ShapeDtypeStruct((rows_m, 2 * TN), BF16),
            jax.ShapeDtypeStruct((TN, rows_m), BF16),
            up_shape, dn_shape,
        ],
        scratch_shapes=[pltpu.VMEM((TN, D_MODEL), BF16), pltpu.VMEM((TN, LANES), F32)],
        compiler_params=_params(("arbitrary", "arbitrary")),
        name="inproj_qkvo",
    )(xn, xnm, wt, b1, w_up, w_down)


def _conv_prologue(m, is_prompt, tiles_per_seq, n_meta, xnm_ref, w_refs, b_refs, s0_ref, s1_ref, scratch,
                   make_xnm=None):
    wu_ref, wc_ref, wb_ref = w_refs
    bu_ref, bc_ref, _ = b_refs
    w3_ref, _, h1_ref, h2_ref, carry_ref, mtail_ref = scratch
    n_slab, tm, _ = h1_ref.shape
    n_seq = tm // SUBLANES
    slabs = [(k, slice(k * LANES, (k + 1) * LANES)) for k in range(n_slab)]
    seq_row = lambda r: pl.ds(r, n_seq, stride=SUBLANES)

    @pl.when(m == 0)
    def _():
        if make_xnm is not None:
            make_xnm()
        w3_ref[0] = wu_ref[...].astype(BF16)
        w3_ref[1] = wc_ref[...].astype(BF16)
        w3_ref[2] = wb_ref[...].astype(BF16)
        h1_ref[...] = jnp.zeros_like(h1_ref)
        h2_ref[...] = jnp.zeros_like(h2_ref)
        xnm = xnm_ref[...]
        cu_m = (_dot_nt(xnm, w3_ref[1]) + bc_ref[...]) * (_dot_nt(xnm, w3_ref[0]) + bu_ref[...])
        mtail_ref[...] = cu_m[n_meta - SUBLANES:n_meta, :]

    @pl.when(is_prompt)
    def _():
        first = (m % tiles_per_seq) == 0
        p6 = jnp.where(first, mtail_ref[6:7, :], carry_ref[6:7, :])
        p7 = jnp.where(first, mtail_ref[7:8, :], carry_ref[7:8, :])
        for k, ks in slabs:
            h2_ref[k, 0:1, :] = p6[:, ks]
            h2_ref[k, 1:2, :] = p7[:, ks]
            h1_ref[k, 0:1, :] = p7[:, ks]

    @pl.when(jnp.logical_not(is_prompt))
    def _():
        for k, ks in slabs:
            h2_ref[k, seq_row(0), :] = s0_ref[:, ks]
            h2_ref[k, seq_row(1), :] = s1_ref[:, ks]
            h1_ref[k, seq_row(0), :] = s1_ref[:, ks]


def _conv_main(xn, is_prompt, b_refs, wconv_ref, yb_ref, cu6_ref, cu7_ref, scratch):
    bu_ref, bc_ref, bb_ref = b_refs
    w3_ref, cu_ref, h1_ref, h2_ref, carry_ref, _ = scratch
    tm, cq = yb_ref.shape
    n_seq = tm // SUBLANES
    slabs = [(k, slice(k * LANES, (k + 1) * LANES)) for k in range(cq // LANES)]
    seq_row = lambda r: pl.ds(r, n_seq, stride=SUBLANES)
    proj = lambda k, b_ref: jnp.concatenate([_dot_nt(x, w3_ref[k]) for x in xn], axis=0) + b_ref[...]
    zu, zc, zb = proj(0, bu_ref), proj(1, bc_ref), proj(2, bb_ref)
    cu = zc * zu
    pos = lax.broadcasted_iota(jnp.int32, (tm, cq), 0) & jnp.where(is_prompt, tm - 1, SUBLANES - 1)
    h1 = jnp.concatenate([h1_ref[k] for k, _ in slabs], axis=1)
    h2 = jnp.concatenate([h2_ref[k] for k, _ in slabs], axis=1)
    x1 = jnp.where(pos >= 1, pltpu.roll(cu, 1, 0), h1)
    x2 = jnp.where(pos >= 2, pltpu.roll(cu, 2, 0), h2)
    w0, w1, w2 = wconv_ref[0:1, :], wconv_ref[1:2, :], wconv_ref[2:3, :]
    yb_ref[...] = (zb * ((w0 * x2 + w1 * x1) + w2 * cu)).astype(BF16)

    carry_ref[...] = cu[tm - SUBLANES:tm, :]
    for k, ks in slabs:
        cu_ref[k] = cu[:, ks]
        cu6_ref[:, ks] = cu_ref[k, seq_row(6), :]
        cu7_ref[:, ks] = cu_ref[k, seq_row(7), :]


def _conv_kernel(n_prompt_tiles, tiles_per_seq, n_meta, n_cast, xn_ref, xnm_ref, wu_ref, wc_ref, wb_ref,
                 bu_ref, bc_ref, bb_ref, wconv_ref, s0_ref, s1_ref, *refs):
    cast_in, (yb_ref, cu6_ref, cu7_ref) = refs[:n_cast], refs[n_cast:n_cast + 3]
    cast_out, scratch = refs[n_cast + 3:2 * n_cast + 3], refs[2 * n_cast + 3:]
    m = pl.program_id(1)
    is_prompt = m < n_prompt_tiles
    b_refs = (bu_ref, bc_ref, bb_ref)
    _conv_prologue(m, is_prompt, tiles_per_seq, n_meta, xnm_ref, (wu_ref, wc_ref, wb_ref), b_refs,
                   s0_ref, s1_ref, scratch)
    _cast_blocks(tuple(zip(cast_in, cast_out)))
    _conv_main([xn_ref[...]], is_prompt, b_refs, wconv_ref, yb_ref, cu6_ref, cu7_ref, scratch)


def _rms_conv_kernel(n_prompt_tiles, tiles_per_seq, dec_seq, xp_ref, xs_ref, xm_ref, g_ref, wg_ref, bg_ref,
                     wu_ref, wc_ref, wb_ref, bu_ref, bc_ref, bb_ref, wconv_ref, s0_ref, s1_ref,
                     xn_ref, gch_ref, grh_ref, xnm_ref, gchm_ref, grhm_ref, yb_ref, cu6_ref, cu7_ref, *scratch):
    m = pl.program_id(0)
    is_prompt = m < n_prompt_tiles
    n_meta = xm_ref.shape[0]
    b_refs = (bu_ref, bc_ref, bb_ref)

    def make_xnm():
        xm = jnp.concatenate([xm_ref[...], jnp.zeros((CHUNK - n_meta, D_MODEL), F32)], axis=0)
        xnm = _rmsnorm(xm, g_ref[...]).astype(BF16)
        xnm_ref[...] = xnm
        _gate_prep(xnm, wg_ref, bg_ref, CHUNK, n_meta, gchm_ref, grhm_ref)

    _conv_prologue(m, is_prompt, tiles_per_seq, n_meta, xnm_ref, (wu_ref, wc_ref, wb_ref), b_refs,
                   s0_ref, s1_ref, scratch, make_xnm)
    tm = xn_ref.shape[0]
    half = tm // 2
    xn_blocks = []
    for r in (slice(0, half), slice(half, tm)):
        x = jnp.where(is_prompt, xp_ref[r, :], xs_ref[r, :])
        xn_blocks.append(_rmsnorm(x, g_ref[...]).astype(BF16))
        xn_ref[r, :] = xn_blocks[-1]
    _gate_prep(jnp.concatenate(xn_blocks, axis=0), wg_ref, bg_ref, jnp.where(is_prompt, CHUNK, dec_seq), tm,
               gch_ref, grh_ref)
    _conv_main(xn_blocks, is_prompt, b_refs, wconv_ref, yb_ref, cu6_ref, cu7_ref, scratch)


def _conv_specs(c0, npt, tm, cm):
    n_seq = tm // SUBLANES
    n_slab = CQ // LANES
    chan = lambda *g: c0 + cm(*g)[0]
    tile = lambda *g: cm(*g)[1]
    w_spec = lambda off: pl.BlockSpec((pl.Element(CQ), pl.Element(D_MODEL)),
                                      lambda *g: (pl.multiple_of(off + chan(*g) * CQ, SUBLANES), 0))
    b_spec = lambda off: pl.BlockSpec((1, CQ), lambda *g: (0, (off - N_GATES) // CQ + chan(*g)))
    s_spec = lambda tok: pl.BlockSpec(
        (n_seq, CQ), lambda *g: (jnp.maximum(tile(*g) - npt, 0), tok * (C_WIDTH // CQ) + chan(*g)))
    in_specs = [w_spec(OFF_U), w_spec(OFF_C), w_spec(OFF_B), b_spec(OFF_U), b_spec(OFF_C), b_spec(OFF_B),
                pl.BlockSpec((CONV_W, CQ), lambda *g: (0, chan(*g))), s_spec(0), s_spec(1)]
    out_specs = [pl.BlockSpec((tm, CQ), lambda *g: (tile(*g), cm(*g)[0])),
                 pl.BlockSpec((n_seq, CQ), lambda *g: (tile(*g), cm(*g)[0])),
                 pl.BlockSpec((n_seq, CQ), lambda *g: (tile(*g), cm(*g)[0]))]
    scratch = [pltpu.VMEM((3, CQ, D_MODEL), BF16),
               pltpu.VMEM((n_slab, tm, LANES), F32),
               pltpu.VMEM((n_slab, tm, LANES), F32),
               pltpu.VMEM((n_slab, tm, LANES), F32),
               pltpu.VMEM((SUBLANES, CQ), F32),
               pltpu.VMEM((SUBLANES, CQ), F32)]
    return in_specs, out_specs, scratch


def _conv_out_shapes(t, n_blocks):
    return [jax.ShapeDtypeStruct((t, n_blocks * CQ), BF16),
            jax.ShapeDtypeStruct((t // SUBLANES, n_blocks * CQ), F32),
            jax.ShapeDtypeStruct((t // SUBLANES, n_blocks * CQ), F32)]


def _rms_conv_call(xp, xs, x_meta, g, wt, bg, b1, wconv, sconv, dec_seq, seq_len, tm):
    tp, ts = xp.shape[0], xs.shape[0]
    t = tp + ts
    assert tm & (tm - 1) == 0 and seq_len % tm == 0 and x_meta.shape[0] >= SUBLANES
    npt, nst = tp // tm, ts // tm
    conv_in, conv_out, scratch = _conv_specs(0, npt, tm, lambda m: (0, m))
    return pl.pallas_call(
        functools.partial(_rms_conv_kernel, npt, seq_len // tm, dec_seq),
        grid=(npt + nst,),
        in_specs=[
            pl.BlockSpec((tm, D_MODEL), lambda m: (jnp.minimum(m, npt - 1), 0)),
            pl.BlockSpec((tm, D_MODEL), lambda m: (jnp.maximum(m - npt, 0), 0)),
            pl.BlockSpec(x_meta.shape, lambda m: (0, 0)),
            pl.BlockSpec((1, D_MODEL), lambda m: (0, 0)),
            pl.BlockSpec((N_GATES, D_MODEL), lambda m: (OFF_GATES // N_GATES, 0)),
            pl.BlockSpec((N_GATES, 1), lambda m: (0, 0)),
        ] + conv_in,
        out_specs=[
            pl.BlockSpec((tm, D_MODEL), lambda m: (m, 0)),
            pl.BlockSpec((M_HEADS, 2, tm, LANES), lambda m: (0, 0, m, 0)),
            pl.BlockSpec((M_HEADS, SUBLANES, tm), lambda m: (0, 0, m)),
            pl.BlockSpec((CHUNK, D_MODEL), lambda m: (0, 0)),
            pl.BlockSpec((M_HEADS, 2, CHUNK, LANES), lambda m: (0, 0, 0, 0)),
            pl.BlockSpec((M_HEADS, SUBLANES, CHUNK), lambda m: (0, 0, 0)),
        ] + conv_out,
        out_shape=[
            jax.ShapeDtypeStruct((t, D_MODEL), BF16),
            jax.ShapeDtypeStruct((M_HEADS, 2, t, LANES), F32),
            jax.ShapeDtypeStruct((M_HEADS, SUBLANES, t), F32),
            jax.ShapeDtypeStruct((CHUNK, D_MODEL), BF16),
            jax.ShapeDtypeStruct((M_HEADS, 2, CHUNK, LANES), F32),
            jax.ShapeDtypeStruct((M_HEADS, SUBLANES, CHUNK), F32),
        ] + _conv_out_shapes(t, 1),
        scratch_shapes=scratch,
        compiler_params=_params(("arbitrary",)),
        name="rms_conv",
    )(xp, xs, x_meta, g, wt, bg, wt, wt, wt, b1, b1, b1, wconv, sconv, sconv)


def _conv_call(xn, xnm, n_meta, wt, b1, wconv, sconv, cast_ws, n_prompt_tokens, seq_len, tm):
    t = xn.shape[0]
    assert tm & (tm - 1) == 0 and seq_len % tm == 0 and n_meta >= SUBLANES
    npt = n_prompt_tokens // tm
    n_m = t // tm
    n_blocks = C_WIDTH // CQ - 1
    n_cast = 16
    assert n_cast <= n_blocks * n_m
    casts = [_cast_specs(w, n_cast, lambda c, m: c * n_m + m) for w in cast_ws]
    conv_in, conv_out, scratch = _conv_specs(1, npt, tm, lambda c, m: (c, m))
    return pl.pallas_call(
        functools.partial(_conv_kernel, npt, seq_len // tm, n_meta, len(cast_ws)),
        grid=(n_blocks, n_m),
        in_specs=[
            pl.BlockSpec((tm, D_MODEL), lambda c, m: (m, 0)),
            pl.BlockSpec(xnm.shape, lambda c, m: (0, 0)),
        ] + conv_in + [cs[0] for cs in casts],
        out_specs=conv_out + [cs[1] for cs in casts],
        out_shape=_conv_out_shapes(t, n_blocks) + [cs[2] for cs in casts],
        scratch_shapes=scratch,
        compiler_params=_params(("arbitrary", "arbitrary")),
        name="inproj_conv",
    )(xn, xnm, wt, wt, wt, b1, b1, b1, wconv, sconv, sconv, *cast_ws)


def _prompt_head(q, kt, v, gr, gc, ca, m_st):
    l = q.shape[0]
    assert l == LANES
    ig_row, b_row, r_row = gr[G_IG:G_IG + 1, :], gr[G_B:G_B + 1, :], gr[G_R:G_R + 1, :]
    b_col = gc[0]
    v_aug = jnp.concatenate([v, jnp.ones((l, LANES), BF16)], axis=1)
    t_idx = lax.broadcasted_iota(jnp.int32, (l, l), 0)
    s_idx = lax.broadcasted_iota(jnp.int32, (l, l), 1)
    logw = jnp.where(s_idx <= t_idx, b_col - b_row + ig_row, NEG)
    inter = b_col + m_st
    m_t = jnp.maximum(inter, gc[1])
    w_inter = jnp.exp(inter - m_t)
    s = _dot(q, kt) * jnp.exp(logw - m_t)
    nd = (jnp.concatenate([w_inter] * (AUG // LANES), axis=1) * _dot(q, ca.astype(BF16))
          + _dot(s.astype(BF16), v_aug))
    rcp = 1.0 / jnp.maximum(jnp.abs(nd[:, M_DV:AUG]), jnp.exp(-m_t))
    h = nd[:, 0:M_DV] * jnp.concatenate([rcp] * (M_DV // LANES), axis=1)
    m_new = m_t[l - 1:l, 0:1]
    decay = jnp.exp(b_col[l - 1:l, 0:1] + m_st - m_new)
    kwt = (kt.astype(F32) * jnp.exp(r_row + ig_row - m_new)).astype(BF16)
    return h, decay * ca + _dot(kwt, v_aug), m_new


def _mlstm_prompt_kernel(batch, qm_ref, ktm_ref, vm_ref, gcm_ref, grm_ref, *refs):
    ins, (h_ref, c_ref, n_ref, m_ref, ca_ref) = refs[:5 * batch], refs[5 * batch:]
    heads = [(hd, slice(hd * M_DK, (hd + 1) * M_DK)) for hd in range(M_HEADS)]

    @pl.when(pl.program_id(0) == 0)
    def _():
        for hd, sl in heads:
            _, ca_new, m_new = _prompt_head(qm_ref[:, sl], ktm_ref[sl, :], vm_ref[:, sl], grm_ref[hd],
                                            gcm_ref[hd], jnp.zeros((M_DK, AUG), F32), jnp.zeros((1, 1), F32))
            for b in range(batch):
                ca_ref[b, hd] = ca_new
                m_ref[b, hd:hd + 1, :] = jnp.broadcast_to(m_new, (1, LANES))

    for b in range(batch):
        q_ref, kt_ref, v_ref, gc_ref, gr_ref = ins[5 * b:5 * b + 5]
        for hd, sl in heads:
            h, ca_new, m_new = _prompt_head(q_ref[:, sl], kt_ref[sl, :], v_ref[:, sl], gr_ref[hd], gc_ref[hd],
                                            ca_ref[b, hd], m_ref[b, hd:hd + 1, 0:1])
            ca_ref[b, hd] = ca_new
            m_ref[b, hd:hd + 1, :] = jnp.broadcast_to(m_new, (1, LANES))
            h_ref[b, :, sl] = h.astype(h_ref.dtype)

    @pl.when(pl.program_id(0) == pl.num_programs(0) - 1)
    def _():
        for b in range(batch):
            for hd, _ in heads:
                ca = ca_ref[b, hd]
                c_ref[b, hd] = ca[:, 0:M_DV]
                n_t = ca[:, M_DV:AUG]
                n_ref[b, hd:hd + 1, :] = jnp.concatenate(
                    [n_t[k * LANES:(k + 1) * LANES, :].T[0:1, :] for k in range(M_DK // LANES)], axis=1)


def _mlstm_prompt_call(zm, ktm, gcm, grm, z, kt, gch, grh, batch, seq):
    nc = seq // CHUNK
    per_prompt_specs, per_prompt_args = [], []
    for b in range(batch):
        row = functools.partial(lambda b, c: b * nc + c, b)
        per_prompt_specs += [
            pl.BlockSpec((CHUNK, M_WIDTH), lambda c, row=row: (row(c), Z_Q)),
            pl.BlockSpec((M_WIDTH, CHUNK), lambda c, row=row: (0, row(c))),
            pl.BlockSpec((CHUNK, M_WIDTH), lambda c, row=row: (row(c), Z_V)),
            pl.BlockSpec((M_HEADS, 2, CHUNK, LANES), lambda c, row=row: (0, 0, row(c), 0)),
            pl.BlockSpec((M_HEADS, SUBLANES, CHUNK), lambda c, row=row: (0, 0, row(c))),
        ]
        per_prompt_args += [z, kt, z, gch, grh]
    return pl.pallas_call(
        functools.partial(_mlstm_prompt_kernel, batch),
        grid=(nc,),
        in_specs=[
            pl.BlockSpec((CHUNK, M_WIDTH), lambda c: (0, 0)),
            pl.BlockSpec((M_WIDTH, CHUNK), lambda c: (0, 0)),
            pl.BlockSpec((CHUNK, M_WIDTH), lambda c: (0, 1)),
            pl.BlockSpec((M_HEADS, 2, CHUNK, LANES), lambda c: (0, 0, 0, 0)),
            pl.BlockSpec((M_HEADS, SUBLANES, CHUNK), lambda c: (0, 0, 0)),
        ] + per_prompt_specs,
        out_specs=[
            pl.BlockSpec((batch, CHUNK, M_WIDTH), lambda c: (0, c, 0)),
            pl.BlockSpec((batch, M_HEADS, M_DK, M_DV), lambda c: (0, 0, 0, 0)),
            pl.BlockSpec((batch, M_HEADS, M_DK), lambda c: (0, 0, 0)),
            pl.BlockSpec((batch, M_HEADS, LANES), lambda c: (0, 0, 0)),
        ],
        out_shape=[
            jax.ShapeDtypeStruct((batch, seq, M_WIDTH), BF16),
            jax.ShapeDtypeStruct((batch, M_HEADS, M_DK, M_DV), F32),
            jax.ShapeDtypeStruct((batch, M_HEADS, M_DK), F32),
            jax.ShapeDtypeStruct((batch, M_HEADS, LANES), F32),
        ],
        scratch_shapes=[pltpu.VMEM((batch, M_HEADS, M_DK, AUG), F32)],
        compiler_params=_params(("arbitrary",)),
        name="mlstm_prompt",
    )(zm, ktm, zm, gcm, grm, *per_prompt_args)


def _group_max(x, size):
    n = x.shape[-1]
    lane = lax.broadcasted_iota(jnp.int32, x.shape, x.ndim - 1)
    k = 1
    while k < size:
        partner = jnp.where((lane & k) == 0, pltpu.roll(x, n - k, x.ndim - 1), pltpu.roll(x, k, x.ndim - 1))
        x = jnp.maximum(x, partner)
        k *= 2
    return x


def _mlstm_sample_kernel(dec_seq, q_ref, kt_ref, v_ref, gc_ref, gr_ref, mcol_ref, mrow_ref, cin_ref, nin_ref,
                         h_ref, c_ref, n_ref, m_ref):
    l = q_ref.shape[0]
    n_seq = l // dec_seq
    shift = dec_seq.bit_length() - 1
    q, kt, v = q_ref[...], kt_ref[...], v_ref[...]
    assert l == LANES
    gr, gc = gr_ref[0], gc_ref[0]
    ig_row, b_row, r_row = gr[G_IG:G_IG + 1, :], gr[G_B:G_B + 1, :], gr[G_R:G_R + 1, :]
    b_col = gc[0]
    m_col, m_row = mcol_ref[0], mrow_ref[0]

    t_idx = lax.broadcasted_iota(jnp.int32, (l, l), 0)
    s_idx = lax.broadcasted_iota(jnp.int32, (l, l), 1)
    t_seq = t_idx >> shift
    logw = jnp.where((t_seq == (s_idx >> shift)) & (s_idx <= t_idx), b_col - b_row + ig_row, NEG)
    inter = b_col + m_col
    m_t = jnp.maximum(inter, gc[1])
    w_inter = jnp.exp(inter - m_t)
    s = _dot(q, kt) * jnp.exp(logw - m_t)

    e_row = r_row + ig_row
    b_last = b_row + r_row
    m_new = jnp.maximum(b_last + m_row, _group_max(e_row, dec_seq))
    decay = jnp.exp(b_last + m_row - m_new)
    kwt = (kt.astype(F32) * jnp.exp(e_row - m_new)).astype(BF16)

    seq_lane = s_idx == t_seq
    v_aug = jnp.concatenate([v, jnp.where(seq_lane, 1.0, 0.0).astype(BF16)], axis=1)
    pa = _dot(s.astype(BF16), v_aug)

    n_rows = jnp.concatenate([nin_ref[...], jnp.zeros((LANES - n_seq, M_DK), F32)], axis=0)
    n_t = jnp.concatenate([n_rows[:, 0:LANES].T, n_rows[:, LANES:2 * LANES].T], axis=0)
    lane_k = lax.broadcasted_iota(jnp.int32, (M_DK, LANES), 1)
    qf = q.astype(F32)
    qca_rows = []
    n_t_new = jnp.zeros((M_DK, LANES), F32)
    for i in range(n_seq):
        ca = jnp.concatenate([cin_ref[i, 0], jnp.where(lane_k == i, n_t, 0.0)], axis=1)
        rows = slice(i * dec_seq, (i + 1) * dec_seq)
        qca_rows.append(_dot(qf[rows, :].astype(BF16), ca.astype(BF16)))
        kw_i = jnp.where((lane_k >> shift) == i, kwt, jnp.zeros_like(kwt))
        ca_new = decay[:, i * dec_seq:i * dec_seq + 1] * ca + _dot(kw_i, v_aug)
        c_ref[i, 0] = ca_new[:, 0:M_DV]
        n_t_new = n_t_new + ca_new[:, M_DV:AUG]
    nd = jnp.concatenate([w_inter] * (AUG // LANES), axis=1) * jnp.concatenate(qca_rows, axis=0) + pa
    den = jnp.sum(jnp.where(seq_lane, nd[:, M_DV:AUG], 0.0), axis=-1, keepdims=True)
    rcp = 1.0 / jnp.maximum(jnp.abs(den), jnp.exp(-m_t[:, 0:1]))
    h_ref[...] = (nd[:, 0:M_DV] * rcp).astype(h_ref.dtype)
    n_ref[...] = jnp.concatenate([n_t_new[0:LANES, :].T, n_t_new[LANES:2 * LANES, :].T], axis=1)[0:n_seq, :]
    m_ref[0] = jnp.broadcast_to(m_new, (SUBLANES, l))


def _gates_sample_kernel(dec_seq, xn_ref, wt_ref, b_ref, *refs):
    sample_in, z_ref, sample_out, w_ref = refs[:9], refs[9], refs[10:14], refs[14]

    @pl.when(pl.program_id(1) == 0)
    def _():
        w_ref[...] = wt_ref[...].astype(BF16)

    z_ref[...] = _sigmoid(_dot_nt(xn_ref[...], w_ref[...]) + b_ref[...]).astype(BF16)
    _mlstm_sample_kernel(dec_seq, *sample_in, *sample_out)


def _gates_sample_call(xn, wt, b1, zqv, kt, gch, grh, m_col, m_row, c_in, n_in, row0, dec_seq, tm):
    t = xn.shape[0]
    n_m = t // tm
    nb = c_in.shape[0]
    l = SEQ_PER_STEP * dec_seq
    blk0 = row0 // l
    n_steps = 2 * D_MODEL // TN
    n_guest = (nb // SEQ_PER_STEP) * M_HEADS
    assert n_guest <= n_steps * n_m
    w_off = lambda s: OFF_GA + s * TN
    b_blk = lambda s: (OFF_GA - N_GATES) // TN + s
    z_col = lambda s: s
    guest = lambda s, m: jnp.minimum(s * n_m + m, n_guest - 1)
    gi = lambda s, m: guest(s, m) // M_HEADS
    gh = lambda s, m: guest(s, m) % M_HEADS
    return pl.pallas_call(
        functools.partial(_gates_sample_kernel, dec_seq),
        grid=(n_steps, n_m),
        in_specs=[
            pl.BlockSpec((tm, D_MODEL), lambda s, m: (m, 0)),
            pl.BlockSpec((pl.Element(TN), pl.Element(D_MODEL)),
                         lambda s, m: (pl.multiple_of(w_off(s), SUBLANES), 0)),
            pl.BlockSpec((1, TN), lambda s, m: (0, b_blk(s))),
            pl.BlockSpec((l, M_DK), lambda s, m: (blk0 + gi(s, m), Z_Q * M_HEADS + gh(s, m))),
            pl.BlockSpec((M_DK, l), lambda s, m: (gh(s, m), blk0 + gi(s, m))),
            pl.BlockSpec((l, M_DV), lambda s, m: (blk0 + gi(s, m), Z_V * M_HEADS + gh(s, m))),
            pl.BlockSpec((1, 2, l, LANES), lambda s, m: (gh(s, m), 0, blk0 + gi(s, m), 0)),
            pl.BlockSpec((1, SUBLANES, l), lambda s, m: (gh(s, m), 0, blk0 + gi(s, m))),
            pl.BlockSpec((1, l, LANES), lambda s, m: (gh(s, m), gi(s, m), 0)),
            pl.BlockSpec((1, 1, l), lambda s, m: (gh(s, m), 0, gi(s, m))),
            pl.BlockSpec((SEQ_PER_STEP, 1, M_DK, M_DV), lambda s, m: (gi(s, m), gh(s, m), 0, 0)),
            pl.BlockSpec((SEQ_PER_STEP, M_DK), lambda s, m: (gi(s, m), gh(s, m))),
        ],
        out_specs=[
            pl.BlockSpec((tm, TN), lambda s, m: (m, z_col(s))),
            pl.BlockSpec((l, M_DV), lambda s, m: (gi(s, m), gh(s, m))),
            pl.BlockSpec((SEQ_PER_STEP, 1, M_DK, M_DV), lambda s, m: (gi(s, m), gh(s, m), 0, 0)),
            pl.BlockSpec((SEQ_PER_STEP, M_DK), lambda s, m: (gi(s, m), gh(s, m))),
            pl.BlockSpec((1, SUBLANES, l), lambda s, m: (gh(s, m), 0, gi(s, m))),
        ],
        out_shape=[
            jax.ShapeDtypeStruct((t, ZG_COLS), BF16),
            jax.ShapeDtypeStruct((nb * dec_seq, M_WIDTH), BF16),
            jax.ShapeDtypeStruct(c_in.shape, F32),
            jax.ShapeDtypeStruct(n_in.shape, F32),
            jax.ShapeDtypeStruct((M_HEADS, SUBLANES, nb * dec_seq), F32),
        ],
        scratch_shapes=[pltpu.VMEM((TN, D_MODEL), BF16)],
        compiler_params=_params(("arbitrary", "arbitrary")),
        name="inproj_gates_mlstm_sample",
    )(xn, wt, b1, zqv, kt, zqv, gch, grh, m_col, m_row, c_in, n_in)


def _mix_kernel(n_prompt_tiles, hp_ref, hs_ref, so_ref, yb0_ref, yb1_ref, ga_ref, gb_ref, xp_ref, xs_ref,
                pa_ref, pb_ref, wo_ref, gffn_ref, o_ref, on_ref):
    i = pl.program_id(0)

    def body(h_ref, x_ref):
        h_a = (so_ref[...].astype(F32) * h_ref[...].astype(F32)).astype(BF16)
        a = _dot(h_a, pa_ref[...])
        b = _dot(jnp.concatenate([yb0_ref[...], yb1_ref[...]], axis=1), pb_ref[...])
        merged = ga_ref[...].astype(F32) * a + gb_ref[...].astype(F32) * b
        x1 = x_ref[...] + _dot(merged.astype(BF16), wo_ref[...])
        o_ref[...] = x1
        on_ref[...] = _rmsnorm(x1, gffn_ref[...]).astype(BF16)

    @pl.when(i < n_prompt_tiles)
    def _():
        body(hp_ref, xp_ref)

    @pl.when(i >= n_prompt_tiles)
    def _():
        body(hs_ref, xs_ref)


def _mix_call(hp, hs, zqvo, zg, yb0, yb1, xp, xs, pa, pb, wo, gffn, tm):
    tp, ts = xp.shape[0], xs.shape[0]
    npt, nst = tp // tm, ts // tm
    pi = lambda i: jnp.minimum(i, npt - 1)
    si = lambda i: jnp.maximum(i - npt, 0)
    const = lambda i: (0, 0)
    return pl.pallas_call(
        functools.partial(_mix_kernel, npt),
        grid=(npt + nst,),
        in_specs=[
            pl.BlockSpec((tm, M_WIDTH), lambda i: (pi(i), 0)),
            pl.BlockSpec((tm, M_WIDTH), lambda i: (si(i), 0)),
            pl.BlockSpec((tm, M_WIDTH), lambda i: (i, Z_SO)),
            pl.BlockSpec((tm, yb0.shape[1]), lambda i: (i, 0)),
            pl.BlockSpec((tm, yb1.shape[1]), lambda i: (i, 0)),
            pl.BlockSpec((tm, D_MODEL), lambda i: (i, Z_GA // 2)),
            pl.BlockSpec((tm, D_MODEL), lambda i: (i, Z_GB // 2)),
            pl.BlockSpec((tm, D_MODEL), lambda i: (pi(i), 0)),
            pl.BlockSpec((tm, D_MODEL), lambda i: (si(i), 0)),
            pl.BlockSpec((M_WIDTH, D_MODEL), const, pipeline_mode=pl.Buffered(1)),
            pl.BlockSpec((C_WIDTH, D_MODEL), const, pipeline_mode=pl.Buffered(1)),
            pl.BlockSpec((D_MODEL, D_MODEL), const, pipeline_mode=pl.Buffered(1)),
            pl.BlockSpec((1, D_MODEL), const),
        ],
        out_specs=[pl.BlockSpec((tm, D_MODEL), lambda i: (i, 0)), pl.BlockSpec((tm, D_MODEL), lambda i: (i, 0))],
        out_shape=[jax.ShapeDtypeStruct((tp + ts, D_MODEL), F32), jax.ShapeDtypeStruct((tp + ts, D_MODEL), BF16)],
        compiler_params=_params(("arbitrary",)),
        name="mix",
    )(hp, hs, zqvo, yb0, yb1, zg, zg, xp, xs, pa, pb, wo, gffn)


def _ffn_kernel(x1_ref, xn_ref, wup_ref, wdn_ref, gfin_ref, y_ref):
    f = pl.program_id(1)
    last = pl.num_programs(1) - 1

    def mlp_part():
        u = jnp.maximum(_dot(xn_ref[...], wup_ref[...]), 0.0)
        return _dot((u * u).astype(BF16), wdn_ref[...])

    @pl.when(f == 0)
    def _():
        y_ref[...] = x1_ref[...] + mlp_part()

    @pl.when((f > 0) & (f < last))
    def _():
        y_ref[...] += mlp_part()

    @pl.when(f == last)
    def _():
        y_ref[...] = _rmsnorm(y_ref[...] + mlp_part(), gfin_ref[...])


def _ffn_call(x1, xn, wup, wdn, gfin, row0, rows, tm, tf):
    assert D_FF // tf >= 2 and row0 % tm == 0 and rows % tm == 0
    blk0 = row0 // tm
    return pl.pallas_call(
        _ffn_kernel,
        grid=(rows // tm, D_FF // tf),
        in_specs=[
            pl.BlockSpec((tm, D_MODEL), lambda i, f: (blk0 + i, 0)),
            pl.BlockSpec((tm, D_MODEL), lambda i, f: (blk0 + i, 0)),
            pl.BlockSpec((D_MODEL, tf), lambda i, f: (0, f)),
            pl.BlockSpec((tf, D_MODEL), lambda i, f: (f, 0)),
            pl.BlockSpec((1, D_MODEL), lambda i, f: (0, 0)),
        ],
        out_specs=pl.BlockSpec((tm, D_MODEL), lambda i, f: (i, 0)),
        out_shape=jax.ShapeDtypeStruct((rows, D_MODEL), F32),
        compiler_params=_params(("arbitrary", "arbitrary")),
        name="ffn",
    )(x1, xn, wup, wdn, gfin)


def kernel(x_prompt, x_sample, state_mlstm_C, state_mlstm_n, state_mlstm_m, state_conv, meta_tokens,
           g_mix, w_in, b_in, w_conv, p_a, p_b, w_o, g_ffn, w_up, w_down, g_final):
    assert w_in.shape[0] == 1, "single-layer trunk"
    batch, seq, _ = x_prompt.shape
    dec_batch, dec_seq, _ = x_sample.shape
    assert dec_seq == SUBLANES and seq % 1024 == 0 and dec_batch % SEQ_PER_STEP == 0
    n_p, n_s = batch * seq, dec_batch * dec_seq

    wt = w_in[0].T
    bias = b_in[0]
    b1 = jnp.concatenate([bias[:OFF_GATES], bias[OFF_O:]])
    bg = bias[OFF_GATES:OFF_O][:, None]
    gmix = g_mix[0][None, :]

    xp = x_prompt.reshape(n_p, D_MODEL)
    xs = x_sample.reshape(n_s, D_MODEL)
    n_meta = meta_tokens.shape[0]

    sconv = state_conv[0].reshape(dec_batch, (CONV_W - 1) * C_WIDTH)
    xn, gch, grh, xnm, gcm, grm, yb0, cu6a, cu7a = _rms_conv_call(
        xp, xs, meta_tokens.astype(F32), gmix, wt, bg, b1[None, :], w_conv[0], sconv, dec_seq, seq, 512)
    zqv, kt, zm, ktm, w_up16, w_down16 = _qkvo_call(xn, xnm, wt, b1[None, :], w_up[0], w_down[0], 1024)
    m_tok = jnp.repeat(state_mlstm_m[0], dec_seq, axis=0).T
    zg, hs, c_s, n_sm, m_s = _gates_sample_call(
        xn, wt, b1[None, :], zqv, kt, gch, grh,
        jnp.broadcast_to(m_tok[:, :, None], m_tok.shape + (LANES,)), m_tok[:, None, :],
        state_mlstm_C[0], state_mlstm_n[0].reshape(dec_batch, M_HEADS * M_DK), n_p, dec_seq, 1024)
    yb1, cu6b, cu7b, p_a16, p_b16, w_o16 = _conv_call(
        xn, xnm, n_meta, wt, b1[None, :], w_conv[0], sconv, (p_a[0], p_b[0], w_o[0]), n_p, seq, 1024)

    hp, c_p, n_pr, m_p = _mlstm_prompt_call(zm, ktm, gcm, grm, zqv, kt, gch, grh, batch, seq)

    x1, xn2 = _mix_call(hp.reshape(n_p, M_WIDTH), hs, zqv, zg, yb0, yb1, xp, xs, p_a16, p_b16, w_o16,
                        g_ffn[0][None, :], 256)
    y_p = _ffn_call(x1, xn2, w_up16, w_down16, g_final[None, :], 0, n_p, 512, 2048)
    y_s = _ffn_call(x1, xn2, w_up16, w_down16, g_final[None, :], n_p, n_s, 512, 2048)

    per_seq = seq // SUBLANES
    first_s = n_p // SUBLANES
    pick = lambda rows: jnp.stack([jnp.concatenate([cu6a[rows], cu6b[rows]], axis=1),
                                   jnp.concatenate([cu7a[rows], cu7b[rows]], axis=1)], axis=1)
    cv_p = pick(slice(per_seq - 1, batch * per_seq, per_seq))
    cv_s = pick(slice(first_s, None))
    m_s = m_s[:, 0, dec_seq - 1::dec_seq].T
    return (y_p.reshape(batch, seq, D_MODEL), y_s.reshape(dec_batch, dec_seq, D_MODEL),
            c_p[None], n_pr[None], m_p[None, :, :, 0], cv_p[None],
            c_s[None], n_sm.reshape(dec_batch, M_HEADS, M_DK)[None], m_s[None], cv_s[None])
```

```python
import functools

import jax
import jax.numpy as jnp
from jax import lax
from jax.experimental import pallas as pl
from jax.experimental.pallas import tpu as pltpu

F32 = jnp.float32
BF16 = jnp.bfloat16

D_MODEL = 2048
N_META = 16
CHUNK = 128
M_HEADS = 4
M_DK = 256
M_DV = 256
M_WIDTH = M_HEADS * M_DV
C_WIDTH = 1024
CONV_W = 3
D_FF = 4 * D_MODEL
EPS = 1e-6
N_GATES = 2 * M_HEADS
OFF_Q = 0
OFF_K = OFF_Q + M_HEADS * M_DK
OFF_V = OFF_K + M_HEADS * M_DK
OFF_GATES = OFF_V + M_WIDTH
OFF_O = OFF_GATES + N_GATES
OFF_U = OFF_O + M_WIDTH
OFF_C = OFF_U + C_WIDTH
OFF_B = OFF_C + C_WIDTH
OFF_GA = OFF_B + C_WIDTH
OFF_GB = OFF_GA + D_MODEL

LANES = 128
SUBLANES = 8
VMEM_LIMIT_BYTES = 60000 * 1024

TN = 1024
Z_Q, Z_V, Z_SO = 0, 1, 2
Z_GA, Z_GB = 0, 2
ZG_COLS = 4 * TN
CQ = 256
AUG = M_DV + LANES
SEQ_PER_STEP = CHUNK // SUBLANES
G_IG, G_B, G_R = 0, 1, 2

NEG = -1e30
NT_DIMS = (((1,), (1,)), ((), ()))


def _params(semantics):
    return pltpu.CompilerParams(dimension_semantics=semantics, vmem_limit_bytes=VMEM_LIMIT_BYTES)


def _rmsnorm(x, g):
    y = x * lax.rsqrt(jnp.mean(x * x, axis=-1, keepdims=True) + EPS)
    return y * g


def _log_sigmoid(x):
    return jnp.minimum(x, 0.0) - jnp.log1p(jnp.exp(-jnp.abs(x)))


def _dot(a, b):
    return jnp.dot(a, b, preferred_element_type=F32)


def _dot_nt(a, b):
    return lax.dot_general(a, b, NT_DIMS, preferred_element_type=F32)


def _gate_prep(xn, wg_ref, bg_ref, blk, n_valid, gch_ref, grh_ref):
    tm = xn.shape[0]
    wg = wg_ref[...].astype(BF16)
    wg = jnp.concatenate([wg, jnp.zeros((LANES - N_GATES, wg.shape[1]), BF16)], axis=0)
    g = _dot_nt(wg, xn)[0:SUBLANES, :] + bg_ref[...]
    row = lax.broadcasted_iota(jnp.int32, (SUBLANES, tm), 0)
    lane = lax.broadcasted_iota(jnp.int32, (SUBLANES, tm), 1)
    a = jnp.where(row < M_HEADS, g, _log_sigmoid(g))
    if n_valid < tm:
        a = jnp.where(lane < n_valid, a, jnp.where(row < M_HEADS, NEG, 0.0))
    pos = lane & (blk - 1)
    n_steps_blk = blk if isinstance(blk, int) else LANES

    def scan(x, op, fill, reverse=False):
        shift = 1
        while shift < n_steps_blk:
            if reverse:
                x = op(x, jnp.where(pos < blk - shift, pltpu.roll(x, tm - shift, 1), fill))
            else:
                x = op(x, jnp.where(pos >= shift, pltpu.roll(x, shift, 1), fill))
            shift *= 2
        return x

    pre = scan(a, jnp.add, 0.0)
    suf = scan(a, jnp.add, 0.0, reverse=True) - a
    b_up = pltpu.roll(pre, M_HEADS, 0)
    m_in = b_up + scan(a - b_up, jnp.maximum, -3e38)
    for h in range(M_HEADS):
        grh_ref[h] = jnp.where(
            row == G_IG, pltpu.roll(a, (G_IG - h) % SUBLANES, 0),
            jnp.where(row == G_B, pltpu.roll(pre, (G_B - M_HEADS - h) % SUBLANES, 0),
                      jnp.where(row == G_R, pltpu.roll(suf, (G_R - M_HEADS - h) % SUBLANES, 0), 0.0)))
        for c in range(tm // LANES):
            cs = slice(c * LANES, (c + 1) * LANES)
            gch_ref[h, 0, cs, :] = jnp.broadcast_to(b_up[h:h + 1, cs], (LANES, LANES)).T
            gch_ref[h, 1, cs, :] = jnp.broadcast_to(m_in[h:h + 1, cs], (LANES, LANES)).T


S_Q, S_K, S_V, S_O = 0, 1, 2, 3


def _cast_blocks(pairs):
    for src_ref, dst_ref in pairs:
        dst_ref[...] = src_ref[...].astype(BF16)


def _cast_specs(w, n_blocks, step):
    rows, cols = w.shape
    blk = lambda *g: (jnp.minimum(step(*g), n_blocks - 1), 0)
    spec = pl.BlockSpec((rows // n_blocks, cols), blk)
    return spec, spec, jax.ShapeDtypeStruct(w.shape, BF16)


def _sigmoid(z):
    return 0.5 * jnp.tanh(0.5 * z) + 0.5


def _qkvo_kernel(n_cast, xn_ref, xnm_ref, wt_ref, b_ref, *refs):
    cast_in, (z_ref, kt_ref, zm_ref, ktm_ref) = refs[:n_cast], refs[n_cast:n_cast + 4]
    cast_out, (w_ref, bcol_ref) = refs[n_cast + 4:2 * n_cast + 4], refs[2 * n_cast + 4:]
    s = pl.program_id(0)
    casts = tuple(zip(cast_in, cast_out))
    k_scale = M_DK ** -0.5
    lane_tiles = lambda x, n: jnp.concatenate([x] * (n // LANES), axis=1)

    @pl.when(pl.program_id(1) == 0)
    def _():
        w_ref[...] = wt_ref[...].astype(BF16)

        @pl.when((s == S_Q) | (s == S_V))
        def _():
            zm_ref[...] = (_dot_nt(xnm_ref[...], w_ref[...]) + b_ref[...]).astype(BF16)

        @pl.when(s == S_K)
        def _():
            for c in range(TN // LANES):
                cs = slice(c * LANES, (c + 1) * LANES)
                bcol_ref[cs, :] = jnp.broadcast_to(b_ref[:, cs], (LANES, LANES)).T
            ktm = _dot_nt(w_ref[...], xnm_ref[...]) + lane_tiles(bcol_ref[...], xnm_ref.shape[0])
            ktm_ref[...] = (ktm * k_scale).astype(BF16)

    def z():
        return _dot_nt(xn_ref[...], w_ref[...]) + b_ref[...]

    @pl.when((s == S_Q) | (s == S_V))
    def _():
        _cast_blocks(casts)
        z_ref[...] = z().astype(BF16)

    @pl.when(s == S_K)
    def _():
        _cast_blocks(casts)
        kt = _dot_nt(w_ref[...], xn_ref[...]) + lane_tiles(bcol_ref[...], xn_ref.shape[0])
        kt_ref[...] = (kt * k_scale).astype(BF16)

    @pl.when(s == S_O)
    def _():
        _cast_blocks(casts)
        z_ref[...] = _sigmoid(z()).astype(BF16)


def _qkvo_call(xn, xnm, wt, b1, cast_ws, tm):
    t = xn.shape[0]
    assert t % tm == 0
    n_m = t // tm
    rows_m = xnm.shape[0]
    n_cast = 32
    assert n_cast <= 4 * n_m
    casts = [_cast_specs(w, n_cast, lambda s, m: s * n_m + m) for w in cast_ws]
    w_off = lambda s: jnp.where(s == S_O, OFF_O, s * TN)
    z_col = lambda s: jnp.where(s <= S_K, Z_Q, s - 1)
    z_row = lambda s, m: jnp.where(s == S_K, n_m - 1, m)
    kt_blk = lambda s, m: jnp.where(s == S_K, m, jnp.where(s < S_K, 0, n_m - 1))
    return pl.pallas_call(
        functools.partial(_qkvo_kernel, len(cast_ws)),
        grid=(4, n_m),
        in_specs=[
            pl.BlockSpec((tm, D_MODEL), lambda s, m: (m, 0)),
            pl.BlockSpec((rows_m, D_MODEL), lambda s, m: (0, 0)),
            pl.BlockSpec((pl.Element(TN), pl.Element(D_MODEL)),
                         lambda s, m: (pl.multiple_of(w_off(s), SUBLANES), 0)),
            pl.BlockSpec((1, TN), lambda s, m: (0, s)),
        ] + [cs[0] for cs in casts],
        out_specs=[
            pl.BlockSpec((tm, TN), lambda s, m: (z_row(s, m), z_col(s))),
            pl.BlockSpec((TN, tm), lambda s, m: (0, kt_blk(s, m))),
            pl.BlockSpec((rows_m, TN), lambda s, m: (0, (s >= S_V).astype(jnp.int32))),
            pl.BlockSpec((TN, rows_m), lambda s, m: (0, 0)),
        ] + [cs[1] for cs in casts],
        out_shape=[
            jax.ShapeDtypeStruct((t, 3 * TN), BF16),
            jax.ShapeDtypeStruct((TN, t), BF16),
            jax.ShapeDtypeStruct((rows_m, 2 * TN), BF16),
            jax.ShapeDtypeStruct((TN, rows_m), BF16),
        ] + [cs[2] for cs in casts],
        scratch_shapes=[pltpu.VMEM((TN, D_MODEL), BF16), pltpu.VMEM((TN, LANES), F32)],
        compiler_params=_params(("arbitrary", "arbitrary")),
        name="inproj_qkvo",
    )(xn, xnm, wt, b1, *cast_ws)


def _conv_prologue(m, is_prompt, tiles_per_seq, n_meta, xnm_ref, w_refs, b_refs, s0_ref, s1_ref, scratch,
                   make_xnm=None):
    wu_ref, wc_ref, wb_ref = w_refs
    bu_ref, bc_ref, _ = b_refs
    w3_ref, _, h1_ref, h2_ref, carry_ref, mtail_ref = scratch
    n_slab, tm, _ = h1_ref.shape
    n_seq = tm // SUBLANES
    slabs = [(k, slice(k * LANES, (k + 1) * LANES)) for k in range(n_slab)]
    seq_row = lambda r: pl.ds(r, n_seq, stride=SUBLANES)

    @pl.when(m == 0)
    def _():
        if make_xnm is not None:
            make_xnm()
        w3_ref[0] = wu_ref[...].astype(BF16)
        w3_ref[1] = wc_ref[...].astype(BF16)
        w3_ref[2] = wb_ref[...].astype(BF16)
        h1_ref[...] = jnp.zeros_like(h1_ref)
        h2_ref[...] = jnp.zeros_like(h2_ref)
        xnm = xnm_ref[...]
        cu_m = (_dot_nt(xnm, w3_ref[1]) + bc_ref[...]) * (_dot_nt(xnm, w3_ref[0]) + bu_ref[...])
        mtail_ref[...] = cu_m[n_meta - SUBLANES:n_meta, :]

    @pl.when(is_prompt)
    def _():
        first = (m % tiles_per_seq) == 0
        p6 = jnp.where(first, mtail_ref[6:7, :], carry_ref[6:7, :])
        p7 = jnp.where(first, mtail_ref[7:8, :], carry_ref[7:8, :])
        for k, ks in slabs:
            h2_ref[k, 0:1, :] = p6[:, ks]
            h2_ref[k, 1:2, :] = p7[:, ks]
            h1_ref[k, 0:1, :] = p7[:, ks]

    @pl.when(jnp.logical_not(is_prompt))
    def _():
        for k, ks in slabs:
            h2_ref[k, seq_row(0), :] = s0_ref[:, ks]
            h2_ref[k, seq_row(1), :] = s1_ref[:, ks]
            h1_ref[k, seq_row(0), :] = s1_ref[:, ks]


def _conv_main(xn, is_prompt, b_refs, wconv_ref, yb_ref, cu6_ref, cu7_ref, scratch):
    bu_ref, bc_ref, bb_ref = b_refs
    w3_ref, cu_ref, h1_ref, h2_ref, carry_ref, _ = scratch
    tm, cq = yb_ref.shape
    n_seq = tm // SUBLANES
    slabs = [(k, slice(k * LANES, (k + 1) * LANES)) for k in range(cq // LANES)]
    seq_row = lambda r: pl.ds(r, n_seq, stride=SUBLANES)
    proj = lambda k, b_ref: jnp.concatenate([_dot_nt(x, w3_ref[k]) for x in xn], axis=0) + b_ref[...]
    zu, zc, zb = proj(0, bu_ref), proj(1, bc_ref), proj(2, bb_ref)
    cu = zc * zu
    pos = lax.broadcasted_iota(jnp.int32, (tm, cq), 0) & jnp.where(is_prompt, tm - 1, SUBLANES - 1)
    h1 = jnp.concatenate([h1_ref[k] for k, _ in slabs], axis=1)
    h2 = jnp.concatenate([h2_ref[k] for k, _ in slabs], axis=1)
    x1 = jnp.where(pos >= 1, pltpu.roll(cu, 1, 0), h1)
    x2 = jnp.where(pos >= 2, pltpu.roll(cu, 2, 0), h2)
    w0, w1, w2 = wconv_ref[0:1, :], wconv_ref[1:2, :], wconv_ref[2:3, :]
    yb_ref[...] = (zb * ((w0 * x2 + w1 * x1) + w2 * cu)).astype(BF16)

    carry_ref[...] = cu[tm - SUBLANES:tm, :]
    for k, ks in slabs:
        cu_ref[k] = cu[:, ks]
        cu6_ref[:, ks] = cu_ref[k, seq_row(6), :]
        cu7_ref[:, ks] = cu_ref[k, seq_row(7), :]


def _conv_kernel(n_prompt_tiles, tiles_per_seq, n_meta, n_cast, xn_ref, xnm_ref, wu_ref, wc_ref, wb_ref,
                 bu_ref, bc_ref, bb_ref, wconv_ref, s0_ref, s1_ref, *refs):
    cast_in, (yb_ref, cu6_ref, cu7_ref) = refs[:n_cast], refs[n_cast:n_cast + 3]
    cast_out, scratch = refs[n_cast + 3:2 * n_cast + 3], refs[2 * n_cast + 3:]
    m = pl.program_id(1)
    is_prompt = m < n_prompt_tiles
    b_refs = (bu_ref, bc_ref, bb_ref)
    _conv_prologue(m, is_prompt, tiles_per_seq, n_meta, xnm_ref, (wu_ref, wc_ref, wb_ref), b_refs,
                   s0_ref, s1_ref, scratch)
    _cast_blocks(tuple(zip(cast_in, cast_out)))
    _conv_main([xn_ref[...]], is_prompt, b_refs, wconv_ref, yb_ref, cu6_ref, cu7_ref, scratch)


def _rms_conv_kernel(n_prompt_tiles, tiles_per_seq, dec_seq, xp_ref, xs_ref, xm_ref, g_ref, wg_ref, bg_ref,
                     wu_ref, wc_ref, wb_ref, bu_ref, bc_ref, bb_ref, wconv_ref, s0_ref, s1_ref,
                     xn_ref, gch_ref, grh_ref, xnm_ref, gchm_ref, grhm_ref, yb_ref, cu6_ref, cu7_ref, *scratch):
    m = pl.program_id(0)
    is_prompt = m < n_prompt_tiles
    n_meta = xm_ref.shape[0]
    b_refs = (bu_ref, bc_ref, bb_ref)

    def make_xnm():
        xm = jnp.concatenate([xm_ref[...], jnp.zeros((CHUNK - n_meta, D_MODEL), F32)], axis=0)
        xnm = _rmsnorm(xm, g_ref[...]).astype(BF16)
        xnm_ref[...] = xnm
        _gate_prep(xnm, wg_ref, bg_ref, CHUNK, n_meta, gchm_ref, grhm_ref)

    _conv_prologue(m, is_prompt, tiles_per_seq, n_meta, xnm_ref, (wu_ref, wc_ref, wb_ref), b_refs,
                   s0_ref, s1_ref, scratch, make_xnm)
    tm = xn_ref.shape[0]
    half = tm // 2
    xn_blocks = []
    for r in (slice(0, half), slice(half, tm)):
        x = jnp.where(is_prompt, xp_ref[r, :], xs_ref[r, :])
        xn_blocks.append(_rmsnorm(x, g_ref[...]).astype(BF16))
        xn_ref[r, :] = xn_blocks[-1]
    _gate_prep(jnp.concatenate(xn_blocks, axis=0), wg_ref, bg_ref, jnp.where(is_prompt, CHUNK, dec_seq), tm,
               gch_ref, grh_ref)
    _conv_main(xn_blocks, is_prompt, b_refs, wconv_ref, yb_ref, cu6_ref, cu7_ref, scratch)


def _conv_specs(c0, npt, tm, cm):
    n_seq = tm // SUBLANES
    n_slab = CQ // LANES
    chan = lambda *g: c0 + cm(*g)[0]
    tile = lambda *g: cm(*g)[1]
    w_spec = lambda off: pl.BlockSpec((pl.Element(CQ), pl.Element(D_MODEL)),
                                      lambda *g: (pl.multiple_of(off + chan(*g) * CQ, SUBLANES), 0))
    b_spec = lambda off: pl.BlockSpec((1, CQ), lambda *g: (0, (off - N_GATES) // CQ + chan(*g)))
    s_spec = lambda tok: pl.BlockSpec(
        (n_seq, CQ), lambda *g: (jnp.maximum(tile(*g) - npt, 0), tok * (C_WIDTH // CQ) + chan(*g)))
    in_specs = [w_spec(OFF_U), w_spec(OFF_C), w_spec(OFF_B), b_spec(OFF_U), b_spec(OFF_C), b_spec(OFF_B),
                pl.BlockSpec((CONV_W, CQ), lambda *g: (0, chan(*g))), s_spec(0), s_spec(1)]
    out_specs = [pl.BlockSpec((tm, CQ), lambda *g: (tile(*g), cm(*g)[0])),
                 pl.BlockSpec((n_seq, CQ), lambda *g: (tile(*g), cm(*g)[0])),
                 pl.BlockSpec((n_seq, CQ), lambda *g: (tile(*g), cm(*g)[0]))]
    scratch = [pltpu.VMEM((3, CQ, D_MODEL), BF16),
               pltpu.VMEM((n_slab, tm, LANES), F32),
               pltpu.VMEM((n_slab, tm, LANES), F32),
               pltpu.VMEM((n_slab, tm, LANES), F32),
               pltpu.VMEM((SUBLANES, CQ), F32),
               pltpu.VMEM((SUBLANES, CQ), F32)]
    return in_specs, out_specs, scratch


def _conv_out_shapes(t, n_blocks):
    return [jax.ShapeDtypeStruct((t, n_blocks * CQ), BF16),
            jax.ShapeDtypeStruct((t // SUBLANES, n_blocks * CQ), F32),
            jax.ShapeDtypeStruct((t // SUBLANES, n_blocks * CQ), F32)]


def _rms_conv_call(xp, xs, x_meta, g, wt, bg, b1, wconv, sconv, dec_seq, seq_len, tm):
    tp, ts = xp.shape[0], xs.shape[0]
    t = tp + ts
    assert tm & (tm - 1) == 0 and seq_len % tm == 0 and x_meta.shape[0] >= SUBLANES
    npt, nst = tp // tm, ts // tm
    conv_in, conv_out, scratch = _conv_specs(0, npt, tm, lambda m: (0, m))
    return pl.pallas_call(
        functools.partial(_rms_conv_kernel, npt, seq_len // tm, dec_seq),
        grid=(npt + nst,),
        in_specs=[
            pl.BlockSpec((tm, D_MODEL), lambda m: (jnp.minimum(m, npt - 1), 0)),
            pl.BlockSpec((tm, D_MODEL), lambda m: (jnp.maximum(m - npt, 0), 0)),
            pl.BlockSpec(x_meta.shape, lambda m: (0, 0)),
            pl.BlockSpec((1, D_MODEL), lambda m: (0, 0)),
            pl.BlockSpec((N_GATES, D_MODEL), lambda m: (OFF_GATES // N_GATES, 0)),
            pl.BlockSpec((N_GATES, 1), lambda m: (0, 0)),
        ] + conv_in,
        out_specs=[
            pl.BlockSpec((tm, D_MODEL), lambda m: (m, 0)),
            pl.BlockSpec((M_HEADS, 2, tm, LANES), lambda m: (0, 0, m, 0)),
            pl.BlockSpec((M_HEADS, SUBLANES, tm), lambda m: (0, 0, m)),
            pl.BlockSpec((CHUNK, D_MODEL), lambda m: (0, 0)),
            pl.BlockSpec((M_HEADS, 2, CHUNK, LANES), lambda m: (0, 0, 0, 0)),
            pl.BlockSpec((M_HEADS, SUBLANES, CHUNK), lambda m: (0, 0, 0)),
        ] + conv_out,
        out_shape=[
            jax.ShapeDtypeStruct((t, D_MODEL), BF16),
            jax.ShapeDtypeStruct((M_HEADS, 2, t, LANES), F32),
            jax.ShapeDtypeStruct((M_HEADS, SUBLANES, t), F32),
            jax.ShapeDtypeStruct((CHUNK, D_MODEL), BF16),
            jax.ShapeDtypeStruct((M_HEADS, 2, CHUNK, LANES), F32),
            jax.ShapeDtypeStruct((M_HEADS, SUBLANES, CHUNK), F32),
        ] + _conv_out_shapes(t, 1),
        scratch_shapes=scratch,
        compiler_params=_params(("arbitrary",)),
        name="rms_conv",
    )(xp, xs, x_meta, g, wt, bg, wt, wt, wt, b1, b1, b1, wconv, sconv, sconv)


def _conv_call(xn, xnm, n_meta, wt, b1, wconv, sconv, cast_ws, n_prompt_tokens, seq_len, tm):
    t = xn.shape[0]
    assert tm & (tm - 1) == 0 and seq_len % tm == 0 and n_meta >= SUBLANES
    npt = n_prompt_tokens // tm
    n_m = t // tm
    n_blocks = C_WIDTH // CQ - 1
    n_cast = 16
    assert n_cast <= n_blocks * n_m
    casts = [_cast_specs(w, n_cast, lambda c, m: c * n_m + m) for w in cast_ws]
    conv_in, conv_out, scratch = _conv_specs(1, npt, tm, lambda c, m: (c, m))
    return pl.pallas_call(
        functools.partial(_conv_kernel, npt, seq_len // tm, n_meta, len(cast_ws)),
        grid=(n_blocks, n_m),
        in_specs=[
            pl.BlockSpec((tm, D_MODEL), lambda c, m: (m, 0)),
            pl.BlockSpec(xnm.shape, lambda c, m: (0, 0)),
        ] + conv_in + [cs[0] for cs in casts],
        out_specs=conv_out + [cs[1] for cs in casts],
        out_shape=_conv_out_shapes(t, n_blocks) + [cs[2] for cs in casts],
        scratch_shapes=scratch,
        compiler_params=_params(("arbitrary", "arbitrary")),
        name="inproj_conv",
    )(xn, xnm, wt, wt, wt, b1, b1, b1, wconv, sconv, sconv, *cast_ws)


def _prompt_head(q, kt, v, gr, gc, ca, m_st):
    l = q.shape[0]
    assert l == LANES
    ig_row, b_row, r_row = gr[G_IG:G_IG + 1, :], gr[G_B:G_B + 1, :], gr[G_R:G_R + 1, :]
    b_col = gc[0]
    v_aug = jnp.concatenate([v, jnp.ones((l, LANES), BF16)], axis=1)
    t_idx = lax.broadcasted_iota(jnp.int32, (l, l), 0)
    s_idx = lax.broadcasted_iota(jnp.int32, (l, l), 1)
    logw = jnp.where(s_idx <= t_idx, b_col - b_row + ig_row, NEG)
    inter = b_col + m_st
    m_t = jnp.maximum(inter, gc[1])
    w_inter = jnp.exp(inter - m_t)
    s = _dot(q, kt) * jnp.exp(logw - m_t)
    nd = (jnp.concatenate([w_inter] * (AUG // LANES), axis=1) * _dot(q, ca.astype(BF16))
          + _dot(s.astype(BF16), v_aug))
    rcp = 1.0 / jnp.maximum(jnp.abs(nd[:, M_DV:AUG]), jnp.exp(-m_t))
    h = nd[:, 0:M_DV] * jnp.concatenate([rcp] * (M_DV // LANES), axis=1)
    m_new = m_t[l - 1:l, 0:1]
    decay = jnp.exp(b_col[l - 1:l, 0:1] + m_st - m_new)
    kwt = (kt.astype(F32) * jnp.exp(r_row + ig_row - m_new)).astype(BF16)
    return h, decay * ca + _dot(kwt, v_aug), m_new


def _mlstm_prompt_kernel(batch, n_cast, qm_ref, ktm_ref, vm_ref, gcm_ref, grm_ref, *refs):
    ins, refs = refs[:5 * batch], refs[5 * batch:]
    cast_in, (h_ref, c_ref, n_ref, m_ref) = refs[:n_cast], refs[n_cast:n_cast + 4]
    cast_out, (ca_ref,) = refs[n_cast + 4:2 * n_cast + 4], refs[2 * n_cast + 4:]
    heads = [(hd, slice(hd * M_DK, (hd + 1) * M_DK)) for hd in range(M_HEADS)]

    @pl.when(pl.program_id(0) == 0)
    def _():
        for hd, sl in heads:
            _, ca_new, m_new = _prompt_head(qm_ref[:, sl], ktm_ref[sl, :], vm_ref[:, sl], grm_ref[hd],
                                            gcm_ref[hd], jnp.zeros((M_DK, AUG), F32), jnp.zeros((1, 1), F32))
            for b in range(batch):
                ca_ref[b, hd] = ca_new
                m_ref[b, hd:hd + 1, :] = jnp.broadcast_to(m_new, (1, LANES))

    _cast_blocks(tuple(zip(cast_in, cast_out)))
    for b in range(batch):
        q_ref, kt_ref, v_ref, gc_ref, gr_ref = ins[5 * b:5 * b + 5]
        for hd, sl in heads:
            h, ca_new, m_new = _prompt_head(q_ref[:, sl], kt_ref[sl, :], v_ref[:, sl], gr_ref[hd], gc_ref[hd],
                                            ca_ref[b, hd], m_ref[b, hd:hd + 1, 0:1])
            ca_ref[b, hd] = ca_new
            m_ref[b, hd:hd + 1, :] = jnp.broadcast_to(m_new, (1, LANES))
            h_ref[b, :, sl] = h.astype(h_ref.dtype)

    @pl.when(pl.program_id(0) == pl.num_programs(0) - 1)
    def _():
        for b in range(batch):
            for hd, _ in heads:
                ca = ca_ref[b, hd]
                c_ref[b, hd] = ca[:, 0:M_DV]
                n_t = ca[:, M_DV:AUG]
                n_ref[b, hd:hd + 1, :] = jnp.concatenate(
                    [n_t[k * LANES:(k + 1) * LANES, :].T[0:1, :] for k in range(M_DK // LANES)], axis=1)


def _mlstm_prompt_call(zm, ktm, gcm, grm, z, kt, gch, grh, cast_ws, batch, seq):
    nc = seq // CHUNK
    casts = [_cast_specs(w, nc, lambda c: c) for w in cast_ws]
    per_prompt_specs, per_prompt_args = [], []
    for b in range(batch):
        row = functools.partial(lambda b, c: b * nc + c, b)
        per_prompt_specs += [
            pl.BlockSpec((CHUNK, M_WIDTH), lambda c, row=row: (row(c), Z_Q)),
            pl.BlockSpec((M_WIDTH, CHUNK), lambda c, row=row: (0, row(c))),
            pl.BlockSpec((CHUNK, M_WIDTH), lambda c, row=row: (row(c), Z_V)),
            pl.BlockSpec((M_HEADS, 2, CHUNK, LANES), lambda c, row=row: (0, 0, row(c), 0)),
            pl.BlockSpec((M_HEADS, SUBLANES, CHUNK), lambda c, row=row: (0, 0, row(c))),
        ]
        per_prompt_args += [z, kt, z, gch, grh]
    return pl.pallas_call(
        functools.partial(_mlstm_prompt_kernel, batch, len(cast_ws)),
        grid=(nc,),
        in_specs=[
            pl.BlockSpec((CHUNK, M_WIDTH), lambda c: (0, 0)),
            pl.BlockSpec((M_WIDTH, CHUNK), lambda c: (0, 0)),
            pl.BlockSpec((CHUNK, M_WIDTH), lambda c: (0, 1)),
            pl.BlockSpec((M_HEADS, 2, CHUNK, LANES), lambda c: (0, 0, 0, 0)),
            pl.BlockSpec((M_HEADS, SUBLANES, CHUNK), lambda c: (0, 0, 0)),
        ] + per_prompt_specs + [cs[0] for cs in casts],
        out_specs=[
            pl.BlockSpec((batch, CHUNK, M_WIDTH), lambda c: (0, c, 0)),
            pl.BlockSpec((batch, M_HEADS, M_DK, M_DV), lambda c: (0, 0, 0, 0)),
            pl.BlockSpec((batch, M_HEADS, M_DK), lambda c: (0, 0, 0)),
            pl.BlockSpec((batch, M_HEADS, LANES), lambda c: (0, 0, 0)),
        ] + [cs[1] for cs in casts],
        out_shape=[
            jax.ShapeDtypeStruct((batch, seq, M_WIDTH), BF16),
            jax.ShapeDtypeStruct((batch, M_HEADS, M_DK, M_DV), F32),
            jax.ShapeDtypeStruct((batch, M_HEADS, M_DK), F32),
            jax.ShapeDtypeStruct((batch, M_HEADS, LANES), F32),
        ] + [cs[2] for cs in casts],
        scratch_shapes=[pltpu.VMEM((batch, M_HEADS, M_DK, AUG), F32)],
        compiler_params=_params(("arbitrary",)),
        name="mlstm_prompt",
    )(zm, ktm, zm, gcm, grm, *per_prompt_args, *cast_ws)


def _group_max(x, size):
    n = x.shape[-1]
    lane = lax.broadcasted_iota(jnp.int32, x.shape, x.ndim - 1)
    k = 1
    while k < size:
        partner = jnp.where((lane & k) == 0, pltpu.roll(x, n - k, x.ndim - 1), pltpu.roll(x, k, x.ndim - 1))
        x = jnp.maximum(x, partner)
        k *= 2
    return x


def _mlstm_sample_kernel(dec_seq, q_ref, kt_ref, v_ref, gc_ref, gr_ref, mcol_ref, mrow_ref, cin_ref, nin_ref,
                         h_ref, c_ref, n_ref, m_ref):
    l = q_ref.shape[0]
    n_seq = l // dec_seq
    shift = dec_seq.bit_length() - 1
    q, kt, v = q_ref[...], kt_ref[...], v_ref[...]
    assert l == LANES
    gr, gc = gr_ref[0], gc_ref[0]
    ig_row, b_row, r_row = gr[G_IG:G_IG + 1, :], gr[G_B:G_B + 1, :], gr[G_R:G_R + 1, :]
    b_col = gc[0]
    m_col, m_row = mcol_ref[0], mrow_ref[0]

    t_idx = lax.broadcasted_iota(jnp.int32, (l, l), 0)
    s_idx = lax.broadcasted_iota(jnp.int32, (l, l), 1)
    t_seq = t_idx >> shift
    logw = jnp.where((t_seq == (s_idx >> shift)) & (s_idx <= t_idx), b_col - b_row + ig_row, NEG)
    inter = b_col + m_col
    m_t = jnp.maximum(inter, gc[1])
    w_inter = jnp.exp(inter - m_t)
    s = _dot(q, kt) * jnp.exp(logw - m_t)

    e_row = r_row + ig_row
    b_last = b_row + r_row
    m_new = jnp.maximum(b_last + m_row, _group_max(e_row, dec_seq))
    decay = jnp.exp(b_last + m_row - m_new)
    kwt = (kt.astype(F32) * jnp.exp(e_row - m_new)).astype(BF16)

    seq_lane = s_idx == t_seq
    v_aug = jnp.concatenate([v, jnp.where(seq_lane, 1.0, 0.0).astype(BF16)], axis=1)
    pa = _dot(s.astype(BF16), v_aug)

    n_rows = jnp.concatenate([nin_ref[...], jnp.zeros((LANES - n_seq, M_DK), F32)], axis=0)
    n_t = jnp.concatenate([n_rows[:, 0:LANES].T, n_rows[:, LANES:2 * LANES].T], axis=0)
    lane_k = lax.broadcasted_iota(jnp.int32, (M_DK, LANES), 1)
    qf = q.astype(F32)
    qca_rows = []
    n_t_new = jnp.zeros((M_DK, LANES), F32)
    for i in range(n_seq):
        ca = jnp.concatenate([cin_ref[i, 0], jnp.where(lane_k == i, n_t, 0.0)], axis=1)
        rows = slice(i * dec_seq, (i + 1) * dec_seq)
        qca_rows.append(_dot(qf[rows, :].astype(BF16), ca.astype(BF16)))
        kw_i = jnp.where((lane_k >> shift) == i, kwt, jnp.zeros_like(kwt))
        ca_new = decay[:, i * dec_seq:i * dec_seq + 1] * ca + _dot(kw_i, v_aug)
        c_ref[i, 0] = ca_new[:, 0:M_DV]
        n_t_new = n_t_new + ca_new[:, M_DV:AUG]
    nd = jnp.concatenate([w_inter] * (AUG // LANES), axis=1) * jnp.concatenate(qca_rows, axis=0) + pa
    den = jnp.sum(jnp.where(seq_lane, nd[:, M_DV:AUG], 0.0), axis=-1, keepdims=True)
    rcp = 1.0 / jnp.maximum(jnp.abs(den), jnp.exp(-m_t[:, 0:1]))
    h_ref[...] = (nd[:, 0:M_DV] * rcp).astype(h_ref.dtype)
    n_ref[...] = jnp.concatenate([n_t_new[0:LANES, :].T, n_t_new[LANES:2 * LANES, :].T], axis=1)[0:n_seq, :]
    m_ref[0] = jnp.broadcast_to(m_new, (SUBLANES, l))


def _gates_sample_kernel(dec_seq, xn_ref, wt_ref, b_ref, *refs):
    sample_in, z_ref, sample_out, w_ref = refs[:9], refs[9], refs[10:14], refs[14]

    @pl.when(pl.program_id(1) == 0)
    def _():
        w_ref[...] = wt_ref[...].astype(BF16)

    z_ref[...] = _sigmoid(_dot_nt(xn_ref[...], w_ref[...]) + b_ref[...]).astype(BF16)
    _mlstm_sample_kernel(dec_seq, *sample_in, *sample_out)


def _gates_sample_call(xn, wt, b1, zqv, kt, gch, grh, m_col, m_row, c_in, n_in, row0, dec_seq, tm):
    t = xn.shape[0]
    n_m = t // tm
    nb = c_in.shape[0]
    l = SEQ_PER_STEP * dec_seq
    blk0 = row0 // l
    n_steps = 2 * D_MODEL // TN
    n_guest = (nb // SEQ_PER_STEP) * M_HEADS
    assert n_guest <= n_steps * n_m
    w_off = lambda s: OFF_GA + s * TN
    b_blk = lambda s: (OFF_GA - N_GATES) // TN + s
    z_col = lambda s: s
    guest = lambda s, m: jnp.minimum(s * n_m + m, n_guest - 1)
    gi = lambda s, m: guest(s, m) // M_HEADS
    gh = lambda s, m: guest(s, m) % M_HEADS
    return pl.pallas_call(
        functools.partial(_gates_sample_kernel, dec_seq),
        grid=(n_steps, n_m),
        in_specs=[
            pl.BlockSpec((tm, D_MODEL), lambda s, m: (m, 0)),
            pl.BlockSpec((pl.Element(TN), pl.Element(D_MODEL)),
                         lambda s, m: (pl.multiple_of(w_off(s), SUBLANES), 0)),
            pl.BlockSpec((1, TN), lambda s, m: (0, b_blk(s))),
            pl.BlockSpec((l, M_DK), lambda s, m: (blk0 + gi(s, m), Z_Q * M_HEADS + gh(s, m))),
            pl.BlockSpec((M_DK, l), lambda s, m: (gh(s, m), blk0 + gi(s, m))),
            pl.BlockSpec((l, M_DV), lambda s, m: (blk0 + gi(s, m), Z_V * M_HEADS + gh(s, m))),
            pl.BlockSpec((1, 2, l, LANES), lambda s, m: (gh(s, m), 0, blk0 + gi(s, m), 0)),
            pl.BlockSpec((1, SUBLANES, l), lambda s, m: (gh(s, m), 0, blk0 + gi(s, m))),
            pl.BlockSpec((1, l, LANES), lambda s, m: (gh(s, m), gi(s, m), 0)),
            pl.BlockSpec((1, 1, l), lambda s, m: (gh(s, m), 0, gi(s, m))),
            pl.BlockSpec((SEQ_PER_STEP, 1, M_DK, M_DV), lambda s, m: (gi(s, m), gh(s, m), 0, 0)),
            pl.BlockSpec((SEQ_PER_STEP, M_DK), lambda s, m: (gi(s, m), gh(s, m))),
        ],
        out_specs=[
            pl.BlockSpec((tm, TN), lambda s, m: (m, z_col(s))),
            pl.BlockSpec((l, M_DV), lambda s, m: (gi(s, m), gh(s, m))),
            pl.BlockSpec((SEQ_PER_STEP, 1, M_DK, M_DV), lambda s, m: (gi(s, m), gh(s, m), 0, 0)),
            pl.BlockSpec((SEQ_PER_STEP, M_DK), lambda s, m: (gi(s, m), gh(s, m))),
            pl.BlockSpec((1, SUBLANES, l), lambda s, m: (gh(s, m), 0, gi(s, m))),
        ],
        out_shape=[
            jax.ShapeDtypeStruct((t, ZG_COLS), BF16),
            jax.ShapeDtypeStruct((nb * dec_seq, M_WIDTH), BF16),
            jax.ShapeDtypeStruct(c_in.shape, F32),
            jax.ShapeDtypeStruct(n_in.shape, F32),
            jax.ShapeDtypeStruct((M_HEADS, SUBLANES, nb * dec_seq), F32),
        ],
        scratch_shapes=[pltpu.VMEM((TN, D_MODEL), BF16)],
        compiler_params=_params(("arbitrary", "arbitrary")),
        name="inproj_gates_mlstm_sample",
    )(xn, wt, b1, zqv, kt, zqv, gch, grh, m_col, m_row, c_in, n_in)


def _mix_kernel(n_prompt_tiles, hp_ref, hs_ref, so_ref, yb0_ref, yb1_ref, ga_ref, gb_ref, xp_ref, xs_ref,
                pa_ref, pb_ref, wo_ref, gffn_ref, o_ref, on_ref):
    i = pl.program_id(0)

    def body(h_ref, x_ref):
        h_a = (so_ref[...].astype(F32) * h_ref[...].astype(F32)).astype(BF16)
        a = _dot(h_a, pa_ref[...])
        b = _dot(jnp.concatenate([yb0_ref[...], yb1_ref[...]], axis=1), pb_ref[...])
        merged = ga_ref[...].astype(F32) * a + gb_ref[...].astype(F32) * b
        x1 = x_ref[...] + _dot(merged.astype(BF16), wo_ref[...])
        o_ref[...] = x1
        on_ref[...] = _rmsnorm(x1, gffn_ref[...]).astype(BF16)

    @pl.when(i < n_prompt_tiles)
    def _():
        body(hp_ref, xp_ref)

    @pl.when(i >= n_prompt_tiles)
    def _():
        body(hs_ref, xs_ref)


def _mix_call(hp, hs, zqvo, zg, yb0, yb1, xp, xs, pa, pb, wo, gffn, tm):
    tp, ts = xp.shape[0], xs.shape[0]
    npt, nst = tp // tm, ts // tm
    pi = lambda i: jnp.minimum(i, npt - 1)
    si = lambda i: jnp.maximum(i - npt, 0)
    const = lambda i: (0, 0)
    return pl.pallas_call(
        functools.partial(_mix_kernel, npt),
        grid=(npt + nst,),
        in_specs=[
            pl.BlockSpec((tm, M_WIDTH), lambda i: (pi(i), 0)),
            pl.BlockSpec((tm, M_WIDTH), lambda i: (si(i), 0)),
            pl.BlockSpec((tm, M_WIDTH), lambda i: (i, Z_SO)),
            pl.BlockSpec((tm, yb0.shape[1]), lambda i: (i, 0)),
            pl.BlockSpec((tm, yb1.shape[1]), lambda i: (i, 0)),
            pl.BlockSpec((tm, D_MODEL), lambda i: (i, Z_GA // 2)),
            pl.BlockSpec((tm, D_MODEL), lambda i: (i, Z_GB // 2)),
            pl.BlockSpec((tm, D_MODEL), lambda i: (pi(i), 0)),
            pl.BlockSpec((tm, D_MODEL), lambda i: (si(i), 0)),
            pl.BlockSpec((M_WIDTH, D_MODEL), const, pipeline_mode=pl.Buffered(1)),
            pl.BlockSpec((C_WIDTH, D_MODEL), const, pipeline_mode=pl.Buffered(1)),
            pl.BlockSpec((D_MODEL, D_MODEL), const, pipeline_mode=pl.Buffered(1)),
            pl.BlockSpec((1, D_MODEL), const),
        ],
        out_specs=[pl.BlockSpec((tm, D_MODEL), lambda i: (i, 0)), pl.BlockSpec((tm, D_MODEL), lambda i: (i, 0))],
        out_shape=[jax.ShapeDtypeStruct((tp + ts, D_MODEL), F32), jax.ShapeDtypeStruct((tp + ts, D_MODEL), BF16)],
        compiler_params=_params(("arbitrary",)),
        name="mix",
    )(hp, hs, zqvo, yb0, yb1, zg, zg, xp, xs, pa, pb, wo, gffn)


def _ffn_kernel(x1_ref, xn_ref, wup_ref, wdn_ref, gfin_ref, y_ref):
    f = pl.program_id(1)
    last = pl.num_programs(1) - 1

    def mlp_part():
        u = jnp.maximum(_dot(xn_ref[...], wup_ref[...]), 0.0)
        return _dot((u * u).astype(BF16), wdn_ref[...])

    @pl.when(f == 0)
    def _():
        y_ref[...] = x1_ref[...] + mlp_part()

    @pl.when((f > 0) & (f < last))
    def _():
        y_ref[...] += mlp_part()

    @pl.when(f == last)
    def _():
        y_ref[...] = _rmsnorm(y_ref[...] + mlp_part(), gfin_ref[...])


def _ffn_call(x1, xn, wup, wdn, gfin, row0, rows, tm, tf):
    assert D_FF // tf >= 2 and row0 % tm == 0 and rows % tm == 0
    blk0 = row0 // tm
    return pl.pallas_call(
        _ffn_kernel,
        grid=(rows // tm, D_FF // tf),
        in_specs=[
            pl.BlockSpec((tm, D_MODEL), lambda i, f: (blk0 + i, 0)),
            pl.BlockSpec((tm, D_MODEL), lambda i, f: (blk0 + i, 0)),
            pl.BlockSpec((D_MODEL, tf), lambda i, f: (0, f)),
            pl.BlockSpec((tf, D_MODEL), lambda i, f: (f, 0)),
            pl.BlockSpec((1, D_MODEL), lambda i, f: (0, 0)),
        ],
        out_specs=pl.BlockSpec((tm, D_MODEL), lambda i, f: (i, 0)),
        out_shape=jax.ShapeDtypeStruct((rows, D_MODEL), F32),
        compiler_params=_params(("arbitrary", "arbitrary")),
        name="ffn",
    )(x1, xn, wup, wdn, gfin)


def kernel(x_prompt, x_sample, state_mlstm_C, state_mlstm_n, state_mlstm_m, state_conv, meta_tokens,
           g_mix, w_in, b_in, w_conv, p_a, p_b, w_o, g_ffn, w_up, w_down, g_final):
    assert w_in.shape[0] == 1, "single-layer trunk"
    batch, seq, _ = x_prompt.shape
    dec_batch, dec_seq, _ = x_sample.shape
    assert dec_seq == SUBLANES and seq % 1024 == 0 and dec_batch % SEQ_PER_STEP == 0
    n_p, n_s = batch * seq, dec_batch * dec_seq

    wt = w_in[0].T
    bias = b_in[0]
    b1 = jnp.concatenate([bias[:OFF_GATES], bias[OFF_O:]])
    bg = bias[OFF_GATES:OFF_O][:, None]
    gmix = g_mix[0][None, :]

    xp = x_prompt.reshape(n_p, D_MODEL)
    xs = x_sample.reshape(n_s, D_MODEL)
    n_meta = meta_tokens.shape[0]

    sconv = state_conv[0].reshape(dec_batch, (CONV_W - 1) * C_WIDTH)
    xn, gch, grh, xnm, gcm, grm, yb0, cu6a, cu7a = _rms_conv_call(
        xp, xs, meta_tokens.astype(F32), gmix, wt, bg, b1[None, :], w_conv[0], sconv, dec_seq, seq, 512)
    zqv, kt, zm, ktm = _qkvo_call(xn, xnm, wt, b1[None, :], (), 1024)
    m_tok = jnp.repeat(state_mlstm_m[0], dec_seq, axis=0).T
    zg, hs, c_s, n_sm, m_s = _gates_sample_call(
        xn, wt, b1[None, :], zqv, kt, gch, grh,
        jnp.broadcast_to(m_tok[:, :, None], m_tok.shape + (LANES,)), m_tok[:, None, :],
        state_mlstm_C[0], state_mlstm_n[0].reshape(dec_batch, M_HEADS * M_DK), n_p, dec_seq, 1024)
    yb1, cu6b, cu7b, p_a16, p_b16, w_o16 = _conv_call(
        xn, xnm, n_meta, wt, b1[None, :], w_conv[0], sconv, (p_a[0], p_b[0], w_o[0]), n_p, seq, 1024)

    hp, c_p, n_pr, m_p, w_up16, w_down16 = _mlstm_prompt_call(zm, ktm, gcm, grm, zqv, kt, gch, grh,
                                                              (w_up[0], w_down[0]), batch, seq)

    x1, xn2 = _mix_call(hp.reshape(n_p, M_WIDTH), hs, zqv, zg, yb0, yb1, xp, xs, p_a16, p_b16, w_o16,
                        g_ffn[0][None, :], 256)
    y_p = _ffn_call(x1, xn2, w_up16, w_down16, g_final[None, :], 0, n_p, 512, 2048)
    y_s = _ffn_call(x1, xn2, w_up16, w_down16, g_final[None, :], n_p, n_s, 512, 2048)

    per_seq = seq // SUBLANES
    first_s = n_p // SUBLANES
    pick = lambda rows: jnp.stack([jnp.concatenate([cu6a[rows], cu6b[rows]], axis=1),
                                   jnp.concatenate([cu7a[rows], cu7b[rows]], axis=1)], axis=1)
    cv_p = pick(slice(per_seq - 1, batch * per_seq, per_seq))
    cv_s = pick(slice(first_s, None))
    m_s = m_s[:, 0, dec_seq - 1::dec_seq].T
    return (y_p.reshape(batch, seq, D_MODEL), y_s.reshape(dec_batch, dec_seq, D_MODEL),
            c_p[None], n_pr[None], m_p[None, :, :, 0], cv_p[None],
            c_s[None], n_sm.reshape(dec_batch, M_HEADS, M_DK)[None], m_s[None], cv_s[None])
```

```python
import functools

import jax
import jax.numpy as jnp
from jax import lax
from jax.experimental import pallas as pl
from jax.experimental.pallas import tpu as pltpu

F32 = jnp.float32
BF16 = jnp.bfloat16

D_MODEL = 2048
N_META = 16
CHUNK = 128
M_HEADS = 4
M_DK = 256
M_DV = 256
M_WIDTH = M_HEADS * M_DV
C_WIDTH = 1024
CONV_W = 3
D_FF = 4 * D_MODEL
EPS = 1e-6
N_GATES = 2 * M_HEADS
OFF_Q = 0
OFF_K = OFF_Q + M_HEADS * M_DK
OFF_V = OFF_K + M_HEADS * M_DK
OFF_GATES = OFF_V + M_WIDTH
OFF_O = OFF_GATES + N_GATES
OFF_U = OFF_O + M_WIDTH
OFF_C = OFF_U + C_WIDTH
OFF_B = OFF_C + C_WIDTH
OFF_GA = OFF_B + C_WIDTH
OFF_GB = OFF_GA + D_MODEL

LANES = 128
SUBLANES = 8
VMEM_LIMIT_BYTES = 60000 * 1024

TN = 1024
Z_Q, Z_V, Z_SO = 0, 1, 2
Z_GA, Z_GB = 0, 2
ZG_COLS = 4 * TN
CQ = 256
AUG = M_DV + LANES
SEQ_PER_STEP = CHUNK // SUBLANES
G_IG, G_B, G_R = 0, 1, 2

NEG = -1e30
NT_DIMS = (((1,), (1,)), ((), ()))


def _params(semantics):
    return pltpu.CompilerParams(dimension_semantics=semantics, vmem_limit_bytes=VMEM_LIMIT_BYTES)


def _rmsnorm(x, g):
    y = x * lax.rsqrt(jnp.mean(x * x, axis=-1, keepdims=True) + EPS)
    return y * g


def _log_sigmoid(x):
    return jnp.minimum(x, 0.0) - jnp.log1p(jnp.exp(-jnp.abs(x)))


def _dot(a, b):
    return jnp.dot(a, b, preferred_element_type=F32)


def _dot_nt(a, b):
    return lax.dot_general(a, b, NT_DIMS, preferred_element_type=F32)


def _gate_prep(xn, wg_ref, bg_ref, blk, n_valid, gch_ref, grh_ref):
    tm = xn.shape[0]
    wg = wg_ref[...].astype(BF16)
    wg = jnp.concatenate([wg, jnp.zeros((LANES - N_GATES, wg.shape[1]), BF16)], axis=0)
    g = _dot_nt(wg, xn)[0:SUBLANES, :] + bg_ref[...]
    row = lax.broadcasted_iota(jnp.int32, (SUBLANES, tm), 0)
    lane = lax.broadcasted_iota(jnp.int32, (SUBLANES, tm), 1)
    a = jnp.where(row < M_HEADS, g, _log_sigmoid(g))
    if n_valid < tm:
        a = jnp.where(lane < n_valid, a, jnp.where(row < M_HEADS, NEG, 0.0))
    pos = lane & (blk - 1)
    n_steps_blk = blk if isinstance(blk, int) else LANES

    def scan(x, op, fill, reverse=False):
        shift = 1
        while shift < n_steps_blk:
            if reverse:
                x = op(x, jnp.where(pos < blk - shift, pltpu.roll(x, tm - shift, 1), fill))
            else:
                x = op(x, jnp.where(pos >= shift, pltpu.roll(x, shift, 1), fill))
            shift *= 2
        return x

    pre = scan(a, jnp.add, 0.0)
    suf = scan(a, jnp.add, 0.0, reverse=True) - a
    b_up = pltpu.roll(pre, M_HEADS, 0)
    m_in = b_up + scan(a - b_up, jnp.maximum, -3e38)
    for h in range(M_HEADS):
        grh_ref[h] = jnp.where(
            row == G_IG, pltpu.roll(a, (G_IG - h) % SUBLANES, 0),
            jnp.where(row == G_B, pltpu.roll(pre, (G_B - M_HEADS - h) % SUBLANES, 0),
                      jnp.where(row == G_R, pltpu.roll(suf, (G_R - M_HEADS - h) % SUBLANES, 0), 0.0)))
        for c in range(tm // LANES):
            cs = slice(c * LANES, (c + 1) * LANES)
            gch_ref[h, 0, cs, :] = jnp.broadcast_to(b_up[h:h + 1, cs], (LANES, LANES)).T
            gch_ref[h, 1, cs, :] = jnp.broadcast_to(m_in[h:h + 1, cs], (LANES, LANES)).T


S_Q, S_K, S_V, S_O = 0, 1, 2, 3


def _cast_blocks(pairs):
    for src_ref, dst_ref in pairs:
        dst_ref[...] = src_ref[...].astype(BF16)


def _cast_specs(w, n_blocks, step):
    rows, cols = w.shape
    blk = lambda *g: (jnp.minimum(step(*g), n_blocks - 1), 0)
    spec = pl.BlockSpec((rows // n_blocks, cols), blk)
    return spec, spec, jax.ShapeDtypeStruct(w.shape, BF16)


def _sigmoid(z):
    return 0.5 * jnp.tanh(0.5 * z) + 0.5


def _qkvo_kernel(n_cast, xn_ref, xnm_ref, wt_ref, b_ref, *refs):
    cast_in, (z_ref, kt_ref, zm_ref, ktm_ref) = refs[:n_cast], refs[n_cast:n_cast + 4]
    cast_out, (w_ref, bcol_ref) = refs[n_cast + 4:2 * n_cast + 4], refs[2 * n_cast + 4:]
    s = pl.program_id(0)
    casts = tuple(zip(cast_in, cast_out))
    k_scale = M_DK ** -0.5
    lane_tiles = lambda x, n: jnp.concatenate([x] * (n // LANES), axis=1)

    @pl.when(pl.program_id(1) == 0)
    def _():
        w_ref[...] = wt_ref[...].astype(BF16)

        @pl.when((s == S_Q) | (s == S_V))
        def _():
            zm_ref[...] = (_dot_nt(xnm_ref[...], w_ref[...]) + b_ref[...]).astype(BF16)

        @pl.when(s == S_K)
        def _():
            for c in range(TN // LANES):
                cs = slice(c * LANES, (c + 1) * LANES)
                bcol_ref[cs, :] = jnp.broadcast_to(b_ref[:, cs], (LANES, LANES)).T
            ktm = _dot_nt(w_ref[...], xnm_ref[...]) + lane_tiles(bcol_ref[...], xnm_ref.shape[0])
            ktm_ref[...] = (ktm * k_scale).astype(BF16)

    def z():
        return _dot_nt(xn_ref[...], w_ref[...]) + b_ref[...]

    @pl.when((s == S_Q) | (s == S_V))
    def _():
        _cast_blocks(casts)
        z_ref[...] = z().astype(BF16)

    @pl.when(s == S_K)
    def _():
        _cast_blocks(casts)
        kt = _dot_nt(w_ref[...], xn_ref[...]) + lane_tiles(bcol_ref[...], xn_ref.shape[0])
        kt_ref[...] = (kt * k_scale).astype(BF16)

    @pl.when(s == S_O)
    def _():
        _cast_blocks(casts)
        z_ref[...] = _sigmoid(z()).astype(BF16)


def _qkvo_call(xn, xnm, wt, b1, cast_ws, tm):
    t = xn.shape[0]
    assert t % tm == 0
    n_m = t // tm
    rows_m = xnm.shape[0]
    n_cast = 32
    assert n_cast <= 4 * n_m
    casts = [_cast_specs(w, n_cast, lambda s, m: s * n_m + m) for w in cast_ws]
    w_off = lambda s: jnp.where(s == S_O, OFF_O, s * TN)
    z_col = lambda s: jnp.where(s <= S_K, Z_Q, s - 1)
    z_row = lambda s, m: jnp.where(s == S_K, n_m - 1, m)
    kt_blk = lambda s, m: jnp.where(s == S_K, m, jnp.where(s < S_K, 0, n_m - 1))
    return pl.pallas_call(
        functools.partial(_qkvo_kernel, len(cast_ws)),
        grid=(4, n_m),
        in_specs=[
            pl.BlockSpec((tm, D_MODEL), lambda s, m: (m, 0)),
            pl.BlockSpec((rows_m, D_MODEL), lambda s, m: (0, 0)),
            pl.BlockSpec((pl.Element(TN), pl.Element(D_MODEL)),
                         lambda s, m: (pl.multiple_of(w_off(s), SUBLANES), 0)),
            pl.BlockSpec((1, TN), lambda s, m: (0, s)),
        ] + [cs[0] for cs in casts],
        out_specs=[
            pl.BlockSpec((tm, TN), lambda s, m: (z_row(s, m), z_col(s))),
            pl.BlockSpec((TN, tm), lambda s, m: (0, kt_blk(s, m))),
            pl.BlockSpec((rows_m, TN), lambda s, m: (0, (s >= S_V).astype(jnp.int32))),
            pl.BlockSpec((TN, rows_m), lambda s, m: (0, 0)),
        ] + [cs[1] for cs in casts],
        out_shape=[
            jax.ShapeDtypeStruct((t, 3 * TN), BF16),
            jax.ShapeDtypeStruct((TN, t), BF16),
            jax.ShapeDtypeStruct((rows_m, 2 * TN), BF16),
            jax.ShapeDtypeStruct((TN, rows_m), BF16),
        ] + [cs[2] for cs in casts],
        scratch_shapes=[pltpu.VMEM((TN, D_MODEL), BF16), pltpu.VMEM((TN, LANES), F32)],
        compiler_params=_params(("arbitrary", "arbitrary")),
        name="inproj_qkvo",
    )(xn, xnm, wt, b1, *cast_ws)


def _conv_prologue(m, is_prompt, tiles_per_seq, n_meta, xnm_ref, w_refs, b_refs, s0_ref, s1_ref, scratch,
                   make_xnm=None):
    wu_ref, wc_ref, wb_ref = w_refs
    bu_ref, bc_ref, _ = b_refs
    w3_ref, _, h1_ref, h2_ref, carry_ref, mtail_ref = scratch
    n_slab, tm, _ = h1_ref.shape
    n_seq = tm // SUBLANES
    slabs = [(k, slice(k * LANES, (k + 1) * LANES)) for k in range(n_slab)]
    seq_row = lambda r: pl.ds(r, n_seq, stride=SUBLANES)

    @pl.when(m == 0)
    def _():
        if make_xnm is not None:
            make_xnm()
        w3_ref[0] = wu_ref[...].astype(BF16)
        w3_ref[1] = wc_ref[...].astype(BF16)
        w3_ref[2] = wb_ref[...].astype(BF16)
        h1_ref[...] = jnp.zeros_like(h1_ref)
        h2_ref[...] = jnp.zeros_like(h2_ref)
        xnm = xnm_ref[...]
        cu_m = (_dot_nt(xnm, w3_ref[1]) + bc_ref[...]) * (_dot_nt(xnm, w3_ref[0]) + bu_ref[...])
        mtail_ref[...] = cu_m[n_meta - SUBLANES:n_meta, :]

    @pl.when(is_prompt)
    def _():
        first = (m % tiles_per_seq) == 0
        p6 = jnp.where(first, mtail_ref[6:7, :], carry_ref[6:7, :])
        p7 = jnp.where(first, mtail_ref[7:8, :], carry_ref[7:8, :])
        for k, ks in slabs:
            h2_ref[k, 0:1, :] = p6[:, ks]
            h2_ref[k, 1:2, :] = p7[:, ks]
            h1_ref[k, 0:1, :] = p7[:, ks]

    @pl.when(jnp.logical_not(is_prompt))
    def _():
        for k, ks in slabs:
            h2_ref[k, seq_row(0), :] = s0_ref[:, ks]
            h2_ref[k, seq_row(1), :] = s1_ref[:, ks]
            h1_ref[k, seq_row(0), :] = s1_ref[:, ks]


def _conv_main(xn, is_prompt, b_refs, wconv_ref, yb_ref, cu6_ref, cu7_ref, scratch):
    bu_ref, bc_ref, bb_ref = b_refs
    w3_ref, cu_ref, h1_ref, h2_ref, carry_ref, _ = scratch
    tm, cq = yb_ref.shape
    n_seq = tm // SUBLANES
    slabs = [(k, slice(k * LANES, (k + 1) * LANES)) for k in range(cq // LANES)]
    seq_row = lambda r: pl.ds(r, n_seq, stride=SUBLANES)
    proj = lambda k, b_ref: jnp.concatenate([_dot_nt(x, w3_ref[k]) for x in xn], axis=0) + b_ref[...]
    zu, zc, zb = proj(0, bu_ref), proj(1, bc_ref), proj(2, bb_ref)
    cu = zc * zu
    pos = lax.broadcasted_iota(jnp.int32, (tm, cq), 0) & jnp.where(is_prompt, tm - 1, SUBLANES - 1)
    h1 = jnp.concatenate([h1_ref[k] for k, _ in slabs], axis=1)
    h2 = jnp.concatenate([h2_ref[k] for k, _ in slabs], axis=1)
    x1 = jnp.where(pos >= 1, pltpu.roll(cu, 1, 0), h1)
    x2 = jnp.where(pos >= 2, pltpu.roll(cu, 2, 0), h2)
    w0, w1, w2 = wconv_ref[0:1, :], wconv_ref[1:2, :], wconv_ref[2:3, :]
    yb_ref[...] = (zb * ((w0 * x2 + w1 * x1) + w2 * cu)).astype(BF16)

    carry_ref[...] = cu[tm - SUBLANES:tm, :]
    for k, ks in slabs:
        cu_ref[k] = cu[:, ks]
        cu6_ref[:, ks] = cu_ref[k, seq_row(6), :]
        cu7_ref[:, ks] = cu_ref[k, seq_row(7), :]


def _conv_kernel(n_prompt_tiles, tiles_per_seq, n_meta, n_cast, xn_ref, xnm_ref, wu_ref, wc_ref, wb_ref,
                 bu_ref, bc_ref, bb_ref, wconv_ref, s0_ref, s1_ref, *refs):
    cast_in, (yb_ref, cu6_ref, cu7_ref) = refs[:n_cast], refs[n_cast:n_cast + 3]
    cast_out, scratch = refs[n_cast + 3:2 * n_cast + 3], refs[2 * n_cast + 3:]
    m = pl.program_id(1)
    is_prompt = m < n_prompt_tiles
    b_refs = (bu_ref, bc_ref, bb_ref)
    _conv_prologue(m, is_prompt, tiles_per_seq, n_meta, xnm_ref, (wu_ref, wc_ref, wb_ref), b_refs,
                   s0_ref, s1_ref, scratch)
    _cast_blocks(tuple(zip(cast_in, cast_out)))
    _conv_main([xn_ref[...]], is_prompt, b_refs, wconv_ref, yb_ref, cu6_ref, cu7_ref, scratch)


def _rms_conv_kernel(n_prompt_tiles, tiles_per_seq, dec_seq, xp_ref, xs_ref, xm_ref, g_ref, wg_ref, bg_ref,
                     wu_ref, wc_ref, wb_ref, bu_ref, bc_ref, bb_ref, wconv_ref, s0_ref, s1_ref,
                     xn_ref, gch_ref, grh_ref, xnm_ref, gchm_ref, grhm_ref, yb_ref, cu6_ref, cu7_ref, *scratch):
    m = pl.program_id(0)
    is_prompt = m < n_prompt_tiles
    n_meta = xm_ref.shape[0]
    b_refs = (bu_ref, bc_ref, bb_ref)

    def make_xnm():
        xm = jnp.concatenate([xm_ref[...], jnp.zeros((CHUNK - n_meta, D_MODEL), F32)], axis=0)
        xnm = _rmsnorm(xm, g_ref[...]).astype(BF16)
        xnm_ref[...] = xnm
        _gate_prep(xnm, wg_ref, bg_ref, CHUNK, n_meta, gchm_ref, grhm_ref)

    _conv_prologue(m, is_prompt, tiles_per_seq, n_meta, xnm_ref, (wu_ref, wc_ref, wb_ref), b_refs,
                   s0_ref, s1_ref, scratch, make_xnm)
    tm = xn_ref.shape[0]
    half = tm // 2
    xn_blocks = []
    for r in (slice(0, half), slice(half, tm)):
        x = jnp.where(is_prompt, xp_ref[r, :], xs_ref[r, :])
        xn_blocks.append(_rmsnorm(x, g_ref[...]).astype(BF16))
        xn_ref[r, :] = xn_blocks[-1]
    _gate_prep(jnp.concatenate(xn_blocks, axis=0), wg_ref, bg_ref, jnp.where(is_prompt, CHUNK, dec_seq), tm,
               gch_ref, grh_ref)
    _conv_main(xn_blocks, is_prompt, b_refs, wconv_ref, yb_ref, cu6_ref, cu7_ref, scratch)


def _conv_specs(c0, npt, tm, cm):
    n_seq = tm // SUBLANES
    n_slab = CQ // LANES
    chan = lambda *g: c0 + cm(*g)[0]
    tile = lambda *g: cm(*g)[1]
    w_spec = lambda off: pl.BlockSpec((pl.Element(CQ), pl.Element(D_MODEL)),
                                      lambda *g: (pl.multiple_of(off + chan(*g) * CQ, SUBLANES), 0))
    b_spec = lambda off: pl.BlockSpec((1, CQ), lambda *g: (0, (off - N_GATES) // CQ + chan(*g)))
    s_spec = lambda tok: pl.BlockSpec(
        (n_seq, CQ), lambda *g: (jnp.maximum(tile(*g) - npt, 0), tok * (C_WIDTH // CQ) + chan(*g)))
    in_specs = [w_spec(OFF_U), w_spec(OFF_C), w_spec(OFF_B), b_spec(OFF_U), b_spec(OFF_C), b_spec(OFF_B),
                pl.BlockSpec((CONV_W, CQ), lambda *g: (0, chan(*g))), s_spec(0), s_spec(1)]
    out_specs = [pl.BlockSpec((tm, CQ), lambda *g: (tile(*g), cm(*g)[0])),
                 pl.BlockSpec((n_seq, CQ), lambda *g: (tile(*g), cm(*g)[0])),
                 pl.BlockSpec((n_seq, CQ), lambda *g: (tile(*g), cm(*g)[0]))]
    scratch = [pltpu.VMEM((3, CQ, D_MODEL), BF16),
               pltpu.VMEM((n_slab, tm, LANES), F32),
               pltpu.VMEM((n_slab, tm, LANES), F32),
               pltpu.VMEM((n_slab, tm, LANES), F32),
               pltpu.VMEM((SUBLANES, CQ), F32),
               pltpu.VMEM((SUBLANES, CQ), F32)]
    return in_specs, out_specs, scratch


def _conv_out_shapes(t, n_blocks):
    return [jax.ShapeDtypeStruct((t, n_blocks * CQ), BF16),
            jax.ShapeDtypeStruct((t // SUBLANES, n_blocks * CQ), F32),
            jax.ShapeDtypeStruct((t // SUBLANES, n_blocks * CQ), F32)]


def _rms_conv_call(xp, xs, x_meta, g, wt, bg, b1, wconv, sconv, dec_seq, seq_len, tm):
    tp, ts = xp.shape[0], xs.shape[0]
    t = tp + ts
    assert tm & (tm - 1) == 0 and seq_len % tm == 0 and x_meta.shape[0] >= SUBLANES
    npt, nst = tp // tm, ts // tm
    conv_in, conv_out, scratch = _conv_specs(0, npt, tm, lambda m: (0, m))
    return pl.pallas_call(
        functools.partial(_rms_conv_kernel, npt, seq_len // tm, dec_seq),
        grid=(npt + nst,),
        in_specs=[
            pl.BlockSpec((tm, D_MODEL), lambda m: (jnp.minimum(m, npt - 1), 0)),
            pl.BlockSpec((tm, D_MODEL), lambda m: (jnp.maximum(m - npt, 0), 0)),
            pl.BlockSpec(x_meta.shape, lambda m: (0, 0)),
            pl.BlockSpec((1, D_MODEL), lambda m: (0, 0)),
            pl.BlockSpec((N_GATES, D_MODEL), lambda m: (OFF_GATES // N_GATES, 0)),
            pl.BlockSpec((N_GATES, 1), lambda m: (0, 0)),
        ] + conv_in,
        out_specs=[
            pl.BlockSpec((tm, D_MODEL), lambda m: (m, 0)),
            pl.BlockSpec((M_HEADS, 2, tm, LANES), lambda m: (0, 0, m, 0)),
            pl.BlockSpec((M_HEADS, SUBLANES, tm), lambda m: (0, 0, m)),
            pl.BlockSpec((CHUNK, D_MODEL), lambda m: (0, 0)),
            pl.BlockSpec((M_HEADS, 2, CHUNK, LANES), lambda m: (0, 0, 0, 0)),
            pl.BlockSpec((M_HEADS, SUBLANES, CHUNK), lambda m: (0, 0, 0)),
        ] + conv_out,
        out_shape=[
            jax.ShapeDtypeStruct((t, D_MODEL), BF16),
            jax.ShapeDtypeStruct((M_HEADS, 2, t, LANES), F32),
            jax.ShapeDtypeStruct((M_HEADS, SUBLANES, t), F32),
            jax.ShapeDtypeStruct((CHUNK, D_MODEL), BF16),
            jax.ShapeDtypeStruct((M_HEADS, 2, CHUNK, LANES), F32),
            jax.ShapeDtypeStruct((M_HEADS, SUBLANES, CHUNK), F32),
        ] + _conv_out_shapes(t, 1),
        scratch_shapes=scratch,
        compiler_params=_params(("arbitrary",)),
        name="rms_conv",
    )(xp, xs, x_meta, g, wt, bg, wt, wt, wt, b1, b1, b1, wconv, sconv, sconv)


def _conv_call(xn, xnm, n_meta, wt, b1, wconv, sconv, cast_ws, n_prompt_tokens, seq_len, tm):
    t = xn.shape[0]
    assert tm & (tm - 1) == 0 and seq_len % tm == 0 and n_meta >= SUBLANES
    npt = n_prompt_tokens // tm
    n_m = t // tm
    n_blocks = C_WIDTH // CQ - 1
    n_cast = 16
    assert n_cast <= n_blocks * n_m
    casts = [_cast_specs(w, n_cast, lambda c, m: c * n_m + m) for w in cast_ws]
    conv_in, conv_out, scratch = _conv_specs(1, npt, tm, lambda c, m: (c, m))
    return pl.pallas_call(
        functools.partial(_conv_kernel, npt, seq_len // tm, n_meta, len(cast_ws)),
        grid=(n_blocks, n_m),
        in_specs=[
            pl.BlockSpec((tm, D_MODEL), lambda c, m: (m, 0)),
            pl.BlockSpec(xnm.shape, lambda c, m: (0, 0)),
        ] + conv_in + [cs[0] for cs in casts],
        out_specs=conv_out + [cs[1] for cs in casts],
        out_shape=_conv_out_shapes(t, n_blocks) + [cs[2] for cs in casts],
        scratch_shapes=scratch,
        compiler_params=_params(("arbitrary", "arbitrary")),
        name="inproj_conv",
    )(xn, xnm, wt, wt, wt, b1, b1, b1, wconv, sconv, sconv, *cast_ws)


def _prompt_head(q, kt, v, gr, gc, ca, m_st):
    l = q.shape[0]
    assert l == LANES
    ig_row, b_row, r_row = gr[G_IG:G_IG + 1, :], gr[G_B:G_B + 1, :], gr[G_R:G_R + 1, :]
    b_col = gc[0]
    v_aug = jnp.concatenate([v, jnp.ones((l, LANES), BF16)], axis=1)
    t_idx = lax.broadcasted_iota(jnp.int32, (l, l), 0)
    s_idx = lax.broadcasted_iota(jnp.int32, (l, l), 1)
    logw = jnp.where(s_idx <= t_idx, b_col - b_row + ig_row, NEG)
    inter = b_col + m_st
    m_t = jnp.maximum(inter, gc[1])
    w_inter = jnp.exp(inter - m_t)
    s = _dot(q, kt) * jnp.exp(logw - m_t)
    nd = (jnp.concatenate([w_inter] * (AUG // LANES), axis=1) * _dot(q, ca.astype(BF16))
          + _dot(s.astype(BF16), v_aug))
    rcp = 1.0 / jnp.maximum(jnp.abs(nd[:, M_DV:AUG]), jnp.exp(-m_t))
    h = nd[:, 0:M_DV] * jnp.concatenate([rcp] * (M_DV // LANES), axis=1)
    m_new = m_t[l - 1:l, 0:1]
    decay = jnp.exp(b_col[l - 1:l, 0:1] + m_st - m_new)
    kwt = (kt.astype(F32) * jnp.exp(r_row + ig_row - m_new)).astype(BF16)
    return h, decay * ca + _dot(kwt, v_aug), m_new


def _mlstm_prompt_kernel(batch, n_cast, qm_ref, ktm_ref, vm_ref, gcm_ref, grm_ref, *refs):
    ins, refs = refs[:5 * batch], refs[5 * batch:]
    cast_in, (h_ref, c_ref, n_ref, m_ref) = refs[:n_cast], refs[n_cast:n_cast + 4]
    cast_out, (ca_ref,) = refs[n_cast + 4:2 * n_cast + 4], refs[2 * n_cast + 4:]
    heads = [(hd, slice(hd * M_DK, (hd + 1) * M_DK)) for hd in range(M_HEADS)]

    @pl.when(pl.program_id(0) == 0)
    def _():
        for hd, sl in heads:
            _, ca_new, m_new = _prompt_head(qm_ref[:, sl], ktm_ref[sl, :], vm_ref[:, sl], grm_ref[hd],
                                            gcm_ref[hd], jnp.zeros((M_DK, AUG), F32), jnp.zeros((1, 1), F32))
            for b in range(batch):
                ca_ref[b, hd] = ca_new
                m_ref[b, hd:hd + 1, :] = jnp.broadcast_to(m_new, (1, LANES))

    _cast_blocks(tuple(zip(cast_in, cast_out)))
    for b in range(batch):
        q_ref, kt_ref, v_ref, gc_ref, gr_ref = ins[5 * b:5 * b + 5]
        for hd, sl in heads:
            h, ca_new, m_new = _prompt_head(q_ref[:, sl], kt_ref[sl, :], v_ref[:, sl], gr_ref[hd], gc_ref[hd],
                                            ca_ref[b, hd], m_ref[b, hd:hd + 1, 0:1])
            ca_ref[b, hd] = ca_new
            m_ref[b, hd:hd + 1, :] = jnp.broadcast_to(m_new, (1, LANES))
            h_ref[b, :, sl] = h.astype(h_ref.dtype)

    @pl.when(pl.program_id(0) == pl.num_programs(0) - 1)
    def _():
        for b in range(batch):
            for hd, _ in heads:
                ca = ca_ref[b, hd]
                c_ref[b, hd] = ca[:, 0:M_DV]
                n_t = ca[:, M_DV:AUG]
                n_ref[b, hd:hd + 1, :] = jnp.concatenate(
                    [n_t[k * LANES:(k + 1) * LANES, :].T[0:1, :] for k in range(M_DK // LANES)], axis=1)


def _mlstm_prompt_call(zm, ktm, gcm, grm, z, kt, gch, grh, cast_ws, batch, seq):
    nc = seq // CHUNK
    casts = [_cast_specs(w, nc, lambda c: c) for w in cast_ws]
    per_prompt_specs, per_prompt_args = [], []
    for b in range(batch):
        row = functools.partial(lambda b, c: b * nc + c, b)
        per_prompt_specs += [
            pl.BlockSpec((CHUNK, M_WIDTH), lambda c, row=row: (row(c), Z_Q)),
            pl.BlockSpec((M_WIDTH, CHUNK), lambda c, row=row: (0, row(c))),
            pl.BlockSpec((CHUNK, M_WIDTH), lambda c, row=row: (row(c), Z_V)),
            pl.BlockSpec((M_HEADS, 2, CHUNK, LANES), lambda c, row=row: (0, 0, row(c), 0)),
            pl.BlockSpec((M_HEADS, SUBLANES, CHUNK), lambda c, row=row: (0, 0, row(c))),
        ]
        per_prompt_args += [z, kt, z, gch, grh]
    return pl.pallas_call(
        functools.partial(_mlstm_prompt_kernel, batch, len(cast_ws)),
        grid=(nc,),
        in_specs=[
            pl.BlockSpec((CHUNK, M_WIDTH), lambda c: (0, 0)),
            pl.BlockSpec((M_WIDTH, CHUNK), lambda c: (0, 0)),
            pl.BlockSpec((CHUNK, M_WIDTH), lambda c: (0, 1)),
            pl.BlockSpec((M_HEADS, 2, CHUNK, LANES), lambda c: (0, 0, 0, 0)),
            pl.BlockSpec((M_HEADS, SUBLANES, CHUNK), lambda c: (0, 0, 0)),
        ] + per_prompt_specs + [cs[0] for cs in casts],
        out_specs=[
            pl.BlockSpec((batch, CHUNK, M_WIDTH), lambda c: (0, c, 0)),
            pl.BlockSpec((batch, M_HEADS, M_DK, M_DV), lambda c: (0, 0, 0, 0)),
            pl.BlockSpec((batch, M_HEADS, M_DK), lambda c: (0, 0, 0)),
            pl.BlockSpec((batch, M_HEADS, LANES), lambda c: (0, 0, 0)),
        ] + [cs[1] for cs in casts],
        out_shape=[
            jax.ShapeDtypeStruct((batch, seq, M_WIDTH), BF16),
            jax.ShapeDtypeStruct((batch, M_HEADS, M_DK, M_DV), F32),
            jax.ShapeDtypeStruct((batch, M_HEADS, M_DK), F32),
            jax.ShapeDtypeStruct((batch, M_HEADS, LANES), F32),
        ] + [cs[2] for cs in casts],
        scratch_shapes=[pltpu.VMEM((batch, M_HEADS, M_DK, AUG), F32)],
        compiler_params=_params(("arbitrary",)),
        name="mlstm_prompt",
    )(zm, ktm, zm, gcm, grm, *per_prompt_args, *cast_ws)


def _group_max(x, size):
    n = x.shape[-1]
    lane = lax.broadcasted_iota(jnp.int32, x.shape, x.ndim - 1)
    k = 1
    while k < size:
        partner = jnp.where((lane & k) == 0, pltpu.roll(x, n - k, x.ndim - 1), pltpu.roll(x, k, x.ndim - 1))
        x = jnp.maximum(x, partner)
        k *= 2
    return x


def _mlstm_sample_kernel(dec_seq, q_ref, kt_ref, v_ref, gc_ref, gr_ref, mcol_ref, mrow_ref, cin_ref, nin_ref,
                         h_ref, c_ref, n_ref, m_ref):
    l = q_ref.shape[0]
    n_seq = l // dec_seq
    shift = dec_seq.bit_length() - 1
    q, kt, v = q_ref[...], kt_ref[...], v_ref[...]
    assert l == LANES
    gr, gc = gr_ref[0], gc_ref[0]
    ig_row, b_row, r_row = gr[G_IG:G_IG + 1, :], gr[G_B:G_B + 1, :], gr[G_R:G_R + 1, :]
    b_col = gc[0]
    m_col, m_row = mcol_ref[0], mrow_ref[0]

    t_idx = lax.broadcasted_iota(jnp.int32, (l, l), 0)
    s_idx = lax.broadcasted_iota(jnp.int32, (l, l), 1)
    t_seq = t_idx >> shift
    logw = jnp.where((t_seq == (s_idx >> shift)) & (s_idx <= t_idx), b_col - b_row + ig_row, NEG)
    inter = b_col + m_col
    m_t = jnp.maximum(inter, gc[1])
    w_inter = jnp.exp(inter - m_t)
    s = _dot(q, kt) * jnp.exp(logw - m_t)

    e_row = r_row + ig_row
    b_last = b_row + r_row
    m_new = jnp.maximum(b_last + m_row, _group_max(e_row, dec_seq))
    decay = jnp.exp(b_last + m_row - m_new)
    kwt = (kt.astype(F32) * jnp.exp(e_row - m_new)).astype(BF16)

    seq_lane = s_idx == t_seq
    seq_onehot = jnp.where(seq_lane, 1.0, 0.0).astype(BF16)
    pa = _dot(s.astype(BF16), jnp.concatenate([v, seq_onehot], axis=1))

    n_rows = jnp.concatenate([nin_ref[...], jnp.zeros((LANES - n_seq, M_DK), F32)], axis=0)
    n_t = jnp.concatenate([n_rows[:, 0:LANES].T, n_rows[:, LANES:2 * LANES].T], axis=0)
    lane_k = lax.broadcasted_iota(jnp.int32, (M_DK, LANES), 1)
    lane_1 = lax.broadcasted_iota(jnp.int32, (1, LANES), 1)
    qf = q.astype(F32)
    qca_rows = []
    decay_seq = jnp.zeros((1, LANES), F32)
    for i in range(n_seq):
        decay_i = decay[:, i * dec_seq:i * dec_seq + 1]
        decay_seq = jnp.where(lane_1 == i, decay_i, decay_seq)
        c_i = cin_ref[i, 0]
        ca = jnp.concatenate([c_i, jnp.where(lane_k == i, n_t, 0.0)], axis=1)
        rows = slice(i * dec_seq, (i + 1) * dec_seq)
        qca_rows.append(_dot(qf[rows, :].astype(BF16), ca.astype(BF16)))
        kw_i = jnp.where((lane_k >> shift) == i, kwt, jnp.zeros_like(kwt))
        c_ref[i, 0] = decay_i * c_i + _dot(kw_i, v)
    n_t_new = decay_seq * n_t + _dot(kwt, seq_onehot)
    nd = jnp.concatenate([w_inter] * (AUG // LANES), axis=1) * jnp.concatenate(qca_rows, axis=0) + pa
    den = jnp.sum(jnp.where(seq_lane, nd[:, M_DV:AUG], 0.0), axis=-1, keepdims=True)
    rcp = 1.0 / jnp.maximum(jnp.abs(den), jnp.exp(-m_t[:, 0:1]))
    h_ref[...] = (nd[:, 0:M_DV] * rcp).astype(h_ref.dtype)
    n_ref[...] = jnp.concatenate([n_t_new[0:LANES, :].T, n_t_new[LANES:2 * LANES, :].T], axis=1)[0:n_seq, :]
    m_ref[0] = jnp.broadcast_to(m_new, (SUBLANES, l))


def _gates_sample_kernel(dec_seq, xn_ref, wt_ref, b_ref, *refs):
    sample_in, z_ref, sample_out, w_ref = refs[:9], refs[9], refs[10:14], refs[14]

    @pl.when(pl.program_id(1) == 0)
    def _():
        w_ref[...] = wt_ref[...].astype(BF16)

    _mlstm_sample_kernel(dec_seq, *sample_in, *sample_out)
    z_ref[...] = _sigmoid(_dot_nt(xn_ref[...], w_ref[...]) + b_ref[...]).astype(BF16)


def _gates_sample_call(xn, wt, b1, zqv, kt, gch, grh, m_col, m_row, c_in, n_in, row0, dec_seq, tm):
    t = xn.shape[0]
    n_m = t // tm
    nb = c_in.shape[0]
    l = SEQ_PER_STEP * dec_seq
    blk0 = row0 // l
    n_steps = 2 * D_MODEL // TN
    n_guest = (nb // SEQ_PER_STEP) * M_HEADS
    assert n_guest <= n_steps * n_m
    w_off = lambda s: OFF_GA + s * TN
    b_blk = lambda s: (OFF_GA - N_GATES) // TN + s
    z_col = lambda s: s
    guest = lambda s, m: jnp.minimum(s * n_m + m, n_guest - 1)
    gi = lambda s, m: guest(s, m) // M_HEADS
    gh = lambda s, m: guest(s, m) % M_HEADS
    return pl.pallas_call(
        functools.partial(_gates_sample_kernel, dec_seq),
        grid=(n_steps, n_m),
        in_specs=[
            pl.BlockSpec((tm, D_MODEL), lambda s, m: (m, 0)),
            pl.BlockSpec((pl.Element(TN), pl.Element(D_MODEL)),
                         lambda s, m: (pl.multiple_of(w_off(s), SUBLANES), 0)),
            pl.BlockSpec((1, TN), lambda s, m: (0, b_blk(s))),
            pl.BlockSpec((l, M_DK), lambda s, m: (blk0 + gi(s, m), Z_Q * M_HEADS + gh(s, m))),
            pl.BlockSpec((M_DK, l), lambda s, m: (gh(s, m), blk0 + gi(s, m))),
            pl.BlockSpec((l, M_DV), lambda s, m: (blk0 + gi(s, m), Z_V * M_HEADS + gh(s, m))),
            pl.BlockSpec((1, 2, l, LANES), lambda s, m: (gh(s, m), 0, blk0 + gi(s, m), 0)),
            pl.BlockSpec((1, SUBLANES, l), lambda s, m: (gh(s, m), 0, blk0 + gi(s, m))),
            pl.BlockSpec((1, l, LANES), lambda s, m: (gh(s, m), gi(s, m), 0)),
            pl.BlockSpec((1, 1, l), lambda s, m: (gh(s, m), 0, gi(s, m))),
            pl.BlockSpec((SEQ_PER_STEP, 1, M_DK, M_DV), lambda s, m: (gi(s, m), gh(s, m), 0, 0)),
            pl.BlockSpec((SEQ_PER_STEP, M_DK), lambda s, m: (gi(s, m), gh(s, m))),
        ],
        out_specs=[
            pl.BlockSpec((tm, TN), lambda s, m: (m, z_col(s))),
            pl.BlockSpec((l, M_DV), lambda s, m: (gi(s, m), gh(s, m))),
            pl.BlockSpec((SEQ_PER_STEP, 1, M_DK, M_DV), lambda s, m: (gi(s, m), gh(s, m), 0, 0)),
            pl.BlockSpec((SEQ_PER_STEP, M_DK), lambda s, m: (gi(s, m), gh(s, m))),
            pl.BlockSpec((1, SUBLANES, l), lambda s, m: (gh(s, m), 0, gi(s, m))),
        ],
        out_shape=[
            jax.ShapeDtypeStruct((t, ZG_COLS), BF16),
            jax.ShapeDtypeStruct((nb * dec_seq, M_WIDTH), BF16),
            jax.ShapeDtypeStruct(c_in.shape, F32),
            jax.ShapeDtypeStruct(n_in.shape, F32),
            jax.ShapeDtypeStruct((M_HEADS, SUBLANES, nb * dec_seq), F32),
        ],
        scratch_shapes=[pltpu.VMEM((TN, D_MODEL), BF16)],
        compiler_params=_params(("arbitrary", "arbitrary")),
        name="inproj_gates_mlstm_sample",
    )(xn, wt, b1, zqv, kt, zqv, gch, grh, m_col, m_row, c_in, n_in)


def _mix_kernel(h_ref, so_ref, yb0_ref, yb1_ref, ga_ref, gb_ref, x_ref, pa_ref, pb_ref, wo_ref, gffn_ref,
                o_ref, on_ref):
    h_a = (so_ref[...].astype(F32) * h_ref[...].astype(F32)).astype(BF16)
    a = _dot(h_a, pa_ref[...])
    b = _dot(jnp.concatenate([yb0_ref[...], yb1_ref[...]], axis=1), pb_ref[...])
    merged = ga_ref[...].astype(F32) * a + gb_ref[...].astype(F32) * b
    x1 = x_ref[...] + _dot(merged.astype(BF16), wo_ref[...])
    o_ref[...] = x1
    on_ref[...] = _rmsnorm(x1, gffn_ref[...]).astype(BF16)


def _mix_call(h, x, zqvo, zg, yb0, yb1, pa, pb, wo, gffn, row0, tm):
    rows = x.shape[0]
    assert row0 % tm == 0 and rows % tm == 0
    blk0 = row0 // tm
    const = lambda i: (0, 0)
    return pl.pallas_call(
        _mix_kernel,
        grid=(rows // tm,),
        in_specs=[
            pl.BlockSpec((tm, M_WIDTH), lambda i: (i, 0)),
            pl.BlockSpec((tm, M_WIDTH), lambda i: (blk0 + i, Z_SO)),
            pl.BlockSpec((tm, yb0.shape[1]), lambda i: (blk0 + i, 0)),
            pl.BlockSpec((tm, yb1.shape[1]), lambda i: (blk0 + i, 0)),
            pl.BlockSpec((tm, D_MODEL), lambda i: (blk0 + i, Z_GA // 2)),
            pl.BlockSpec((tm, D_MODEL), lambda i: (blk0 + i, Z_GB // 2)),
            pl.BlockSpec((tm, D_MODEL), lambda i: (i, 0)),
            pl.BlockSpec((M_WIDTH, D_MODEL), const, pipeline_mode=pl.Buffered(1)),
            pl.BlockSpec((C_WIDTH, D_MODEL), const, pipeline_mode=pl.Buffered(1)),
            pl.BlockSpec((D_MODEL, D_MODEL), const, pipeline_mode=pl.Buffered(1)),
            pl.BlockSpec((1, D_MODEL), const),
        ],
        out_specs=[pl.BlockSpec((tm, D_MODEL), lambda i: (i, 0)), pl.BlockSpec((tm, D_MODEL), lambda i: (i, 0))],
        out_shape=[jax.ShapeDtypeStruct((rows, D_MODEL), F32), jax.ShapeDtypeStruct((rows, D_MODEL), BF16)],
        compiler_params=_params(("arbitrary",)),
        name="mix",
    )(h, zqvo, yb0, yb1, zg, zg, x, pa, pb, wo, gffn)


def _ffn_kernel(x1_ref, xn_ref, wup_ref, wdn_ref, gfin_ref, y_ref):
    f = pl.program_id(1)
    last = pl.num_programs(1) - 1

    def mlp_part():
        u = jnp.maximum(_dot(xn_ref[...], wup_ref[...]), 0.0)
        return _dot((u * u).astype(BF16), wdn_ref[...])

    @pl.when(f == 0)
    def _():
        y_ref[...] = x1_ref[...] + mlp_part()

    @pl.when((f > 0) & (f < last))
    def _():
        y_ref[...] += mlp_part()

    @pl.when(f == last)
    def _():
        y_ref[...] = _rmsnorm(y_ref[...] + mlp_part(), gfin_ref[...])


def _ffn_call(x1, xn, wup, wdn, gfin, tm, tf):
    rows = x1.shape[0]
    assert D_FF // tf >= 2 and rows % tm == 0
    return pl.pallas_call(
        _ffn_kernel,
        grid=(rows // tm, D_FF // tf),
        in_specs=[
            pl.BlockSpec((tm, D_MODEL), lambda i, f: (i, 0)),
            pl.BlockSpec((tm, D_MODEL), lambda i, f: (i, 0)),
            pl.BlockSpec((D_MODEL, tf), lambda i, f: (0, f)),
            pl.BlockSpec((tf, D_MODEL), lambda i, f: (f, 0)),
            pl.BlockSpec((1, D_MODEL), lambda i, f: (0, 0)),
        ],
        out_specs=pl.BlockSpec((tm, D_MODEL), lambda i, f: (i, 0)),
        out_shape=jax.ShapeDtypeStruct((rows, D_MODEL), F32),
        compiler_params=_params(("arbitrary", "arbitrary")),
        name="ffn",
    )(x1, xn, wup, wdn, gfin)


def kernel(x_prompt, x_sample, state_mlstm_C, state_mlstm_n, state_mlstm_m, state_conv, meta_tokens,
           g_mix, w_in, b_in, w_conv, p_a, p_b, w_o, g_ffn, w_up, w_down, g_final):
    assert w_in.shape[0] == 1, "single-layer trunk"
    batch, seq, _ = x_prompt.shape
    dec_batch, dec_seq, _ = x_sample.shape
    assert dec_seq == SUBLANES and seq % 1024 == 0 and dec_batch % SEQ_PER_STEP == 0
    n_p, n_s = batch * seq, dec_batch * dec_seq

    wt = w_in[0].T
    bias = b_in[0]
    b1 = jnp.concatenate([bias[:OFF_GATES], bias[OFF_O:]])
    bg = bias[OFF_GATES:OFF_O][:, None]
    gmix = g_mix[0][None, :]

    xp = x_prompt.reshape(n_p, D_MODEL)
    xs = x_sample.reshape(n_s, D_MODEL)
    n_meta = meta_tokens.shape[0]

    sconv = state_conv[0].reshape(dec_batch, (CONV_W - 1) * C_WIDTH)
    xn, gch, grh, xnm, gcm, grm, yb0, cu6a, cu7a = _rms_conv_call(
        xp, xs, meta_tokens.astype(F32), gmix, wt, bg, b1[None, :], w_conv[0], sconv, dec_seq, seq, 512)
    zqv, kt, zm, ktm, w_up16, w_down16 = _qkvo_call(xn, xnm, wt, b1[None, :], (w_up[0], w_down[0]), 1152)
    m_tok = jnp.repeat(state_mlstm_m[0], dec_seq, axis=0).T
    zg, hs, c_s, n_sm, m_s = _gates_sample_call(
        xn, wt, b1[None, :], zqv, kt, gch, grh,
        jnp.broadcast_to(m_tok[:, :, None], m_tok.shape + (LANES,)), m_tok[:, None, :],
        state_mlstm_C[0], state_mlstm_n[0].reshape(dec_batch, M_HEADS * M_DK), n_p, dec_seq, 1152)
    yb1, cu6b, cu7b, p_a16, p_b16, w_o16 = _conv_call(
        xn, xnm, n_meta, wt, b1[None, :], w_conv[0], sconv, (p_a[0], p_b[0], w_o[0]), n_p, seq, 1024)

    hp, c_p, n_pr, m_p = _mlstm_prompt_call(zm, ktm, gcm, grm, zqv, kt, gch, grh, (), batch, seq)

    y_p, y_s = [
        _ffn_call(*_mix_call(h, x, zqv, zg, yb0, yb1, p_a16, p_b16, w_o16, g_ffn[0][None, :], row0, 512),
                  w_up16, w_down16, g_final[None, :], 512, 2048)
        for h, x, row0 in ((hp.reshape(n_p, M_WIDTH), xp, 0), (hs, xs, n_p))]

    per_seq = seq // SUBLANES
    first_s = n_p // SUBLANES
    pick = lambda rows: jnp.stack([jnp.concatenate([cu6a[rows], cu6b[rows]], axis=1),
                                   jnp.concatenate([cu7a[rows], cu7b[rows]], axis=1)], axis=1)
    cv_p = pick(slice(per_seq - 1, batch * per_seq, per_seq))
    cv_s = pick(slice(first_s, None))
    m_s = m_s[:, 0, dec_seq - 1::dec_seq].T
    return (y_p.reshape(batch, seq, D_MODEL), y_s.reshape(dec_batch, dec_seq, D_MODEL),
            c_p[None], n_pr[None], m_p[None, :, :, 0], cv_p[None],
            c_s[None], n_sm.reshape(dec_batch, M_HEADS, M_DK)[None], m_s[None], cv_s[None])
```

```python
import functools

import jax
import jax.numpy as jnp
from jax import lax
from jax.experimental import pallas as pl
from jax.experimental.pallas import tpu as pltpu

F32 = jnp.float32
BF16 = jnp.bfloat16

D_MODEL = 2048
CHUNK = 128
M_HEADS = 4
M_DK = 256
M_DV = 256
M_WIDTH = M_HEADS * M_DV
C_WIDTH = 1024
CONV_W = 3
D_FF = 4 * D_MODEL
EPS = 1e-6
N_GATES = 2 * M_HEADS
OFF_Q = 0
OFF_K = OFF_Q + M_HEADS * M_DK
OFF_V = OFF_K + M_HEADS * M_DK
OFF_GATES = OFF_V + M_WIDTH
OFF_O = OFF_GATES + N_GATES
OFF_U = OFF_O + M_WIDTH
OFF_C = OFF_U + C_WIDTH
OFF_B = OFF_C + C_WIDTH
OFF_GA = OFF_B + C_WIDTH

LANES = 128
SUBLANES = 8
VMEM_LIMIT_BYTES = 60000 * 1024

TN = 1024
Z_Q, Z_V, Z_SO = 0, 1, 2
Z_GA, Z_GB = 0, 2
ZG_COLS = 4 * TN
CQ = 256
AUG = M_DV + LANES
SEQ_PER_STEP = CHUNK // SUBLANES
G_IG, G_B, G_R = 0, 1, 2

NEG = -1e30
NT_DIMS = (((1,), (1,)), ((), ()))


def _params(semantics):
    return pltpu.CompilerParams(dimension_semantics=semantics, vmem_limit_bytes=VMEM_LIMIT_BYTES)


def _rmsnorm(x, g):
    y = x * lax.rsqrt(jnp.mean(x * x, axis=-1, keepdims=True) + EPS)
    return y * g


def _log_sigmoid(x):
    return jnp.minimum(x, 0.0) - jnp.log1p(jnp.exp(-jnp.abs(x)))


def _dot(a, b):
    return jnp.dot(a, b, preferred_element_type=F32)


def _dot_nt(a, b):
    return lax.dot_general(a, b, NT_DIMS, preferred_element_type=F32)


def _gate_prep(xn, wg_ref, bg_ref, blk, n_valid, gch_ref, grh_ref):
    tm = xn.shape[0]
    wg = wg_ref[...].astype(BF16)
    wg = jnp.concatenate([wg, jnp.zeros((LANES - N_GATES, wg.shape[1]), BF16)], axis=0)
    g = _dot_nt(wg, xn)[0:SUBLANES, :] + bg_ref[...]
    row = lax.broadcasted_iota(jnp.int32, (SUBLANES, tm), 0)
    lane = lax.broadcasted_iota(jnp.int32, (SUBLANES, tm), 1)
    a = jnp.where(row < M_HEADS, g, _log_sigmoid(g))
    if n_valid < tm:
        a = jnp.where(lane < n_valid, a, jnp.where(row < M_HEADS, NEG, 0.0))
    pos = lane & (blk - 1)
    n_steps_blk = blk if isinstance(blk, int) else LANES

    def scan(x, op, fill, reverse=False):
        shift = 1
        while shift < n_steps_blk:
            if reverse:
                x = op(x, jnp.where(pos < blk - shift, pltpu.roll(x, tm - shift, 1), fill))
            else:
                x = op(x, jnp.where(pos >= shift, pltpu.roll(x, shift, 1), fill))
            shift *= 2
        return x

    pre = scan(a, jnp.add, 0.0)
    suf = scan(a, jnp.add, 0.0, reverse=True) - a
    b_up = pltpu.roll(pre, M_HEADS, 0)
    m_in = b_up + scan(a - b_up, jnp.maximum, -3e38)
    for h in range(M_HEADS):
        grh_ref[h] = jnp.where(
            row == G_IG, pltpu.roll(a, (G_IG - h) % SUBLANES, 0),
            jnp.where(row == G_B, pltpu.roll(pre, (G_B - M_HEADS - h) % SUBLANES, 0),
                      jnp.where(row == G_R, pltpu.roll(suf, (G_R - M_HEADS - h) % SUBLANES, 0), 0.0)))
        for c in range(tm // LANES):
            cs = slice(c * LANES, (c + 1) * LANES)
            gch_ref[h, 0, cs, :] = jnp.broadcast_to(b_up[h:h + 1, cs], (LANES, LANES)).T
            gch_ref[h, 1, cs, :] = jnp.broadcast_to(m_in[h:h + 1, cs], (LANES, LANES)).T


S_Q, S_K, S_V, S_O = 0, 1, 2, 3


def _cast_blocks(pairs):
    for src_ref, dst_ref in pairs:
        dst_ref[...] = src_ref[...].astype(BF16)


def _cast_specs(w, n_blocks, step):
    rows, cols = w.shape
    blk = lambda *g: (jnp.minimum(step(*g), n_blocks - 1), 0)
    spec = pl.BlockSpec((rows // n_blocks, cols), blk)
    return spec, spec, jax.ShapeDtypeStruct(w.shape, BF16)


def _sigmoid(z):
    return 0.5 * jnp.tanh(0.5 * z) + 0.5


def _qkvo_kernel(n_cast, xn_ref, xnm_ref, wt_ref, b_ref, *refs):
    cast_in, (z_ref, kt_ref, zm_ref, ktm_ref) = refs[:n_cast], refs[n_cast:n_cast + 4]
    cast_out, (w_ref, bcol_ref) = refs[n_cast + 4:2 * n_cast + 4], refs[2 * n_cast + 4:]
    s = pl.program_id(0)
    casts = tuple(zip(cast_in, cast_out))
    k_scale = M_DK ** -0.5
    lane_tiles = lambda x, n: jnp.concatenate([x] * (n // LANES), axis=1)

    @pl.when(pl.program_id(1) == 0)
    def _():
        w_ref[...] = wt_ref[...].astype(BF16)

        @pl.when((s == S_Q) | (s == S_V))
        def _():
            zm_ref[...] = (_dot_nt(xnm_ref[...], w_ref[...]) + b_ref[...]).astype(BF16)

        @pl.when(s == S_K)
        def _():
            for c in range(TN // LANES):
                cs = slice(c * LANES, (c + 1) * LANES)
                bcol_ref[cs, :] = jnp.broadcast_to(b_ref[:, cs], (LANES, LANES)).T
            ktm = _dot_nt(w_ref[...], xnm_ref[...]) + lane_tiles(bcol_ref[...], xnm_ref.shape[0])
            ktm_ref[...] = (ktm * k_scale).astype(BF16)

    def z():
        return _dot_nt(xn_ref[...], w_ref[...]) + b_ref[...]

    @pl.when((s == S_Q) | (s == S_V))
    def _():
        _cast_blocks(casts)
        z_ref[...] = z().astype(BF16)

    @pl.when(s == S_K)
    def _():
        _cast_blocks(casts)
        kt = _dot_nt(w_ref[...], xn_ref[...]) + lane_tiles(bcol_ref[...], xn_ref.shape[0])
        kt_ref[...] = (kt * k_scale).astype(BF16)

    @pl.when(s == S_O)
    def _():
        _cast_blocks(casts)
        z_ref[...] = _sigmoid(z()).astype(BF16)


def _qkvo_call(xn, xnm, wt, b1, cast_ws, tm):
    t = xn.shape[0]
    assert t % tm == 0
    n_m = t // tm
    rows_m = xnm.shape[0]
    n_cast = 32
    assert n_cast <= 4 * n_m
    casts = [_cast_specs(w, n_cast, lambda s, m: s * n_m + m) for w in cast_ws]
    w_off = lambda s: jnp.where(s == S_O, OFF_O, s * TN)
    z_col = lambda s: jnp.where(s <= S_K, Z_Q, s - 1)
    z_row = lambda s, m: jnp.where(s == S_K, n_m - 1, m)
    kt_blk = lambda s, m: jnp.where(s == S_K, m, jnp.where(s < S_K, 0, n_m - 1))
    return pl.pallas_call(
        functools.partial(_qkvo_kernel, len(cast_ws)),
        grid=(4, n_m),
        in_specs=[
            pl.BlockSpec((tm, D_MODEL), lambda s, m: (m, 0)),
            pl.BlockSpec((rows_m, D_MODEL), lambda s, m: (0, 0)),
            pl.BlockSpec((pl.Element(TN), pl.Element(D_MODEL)),
                         lambda s, m: (pl.multiple_of(w_off(s), SUBLANES), 0)),
            pl.BlockSpec((1, TN), lambda s, m: (0, s)),
        ] + [cs[0] for cs in casts],
        out_specs=[
            pl.BlockSpec((tm, TN), lambda s, m: (z_row(s, m), z_col(s))),
            pl.BlockSpec((TN, tm), lambda s, m: (0, kt_blk(s, m))),
            pl.BlockSpec((rows_m, TN), lambda s, m: (0, (s >= S_V).astype(jnp.int32))),
            pl.BlockSpec((TN, rows_m), lambda s, m: (0, 0)),
        ] + [cs[1] for cs in casts],
        out_shape=[
            jax.ShapeDtypeStruct((t, 3 * TN), BF16),
            jax.ShapeDtypeStruct((TN, t), BF16),
            jax.ShapeDtypeStruct((rows_m, 2 * TN), BF16),
            jax.ShapeDtypeStruct((TN, rows_m), BF16),
        ] + [cs[2] for cs in casts],
        scratch_shapes=[pltpu.VMEM((TN, D_MODEL), BF16), pltpu.VMEM((TN, LANES), F32)],
        compiler_params=_params(("arbitrary", "arbitrary")),
        name="inproj_qkvo",
    )(xn, xnm, wt, b1, *cast_ws)


def _conv_prologue(m, is_prompt, tiles_per_seq, n_meta, xnm_ref, w_refs, b_refs, s0_ref, s1_ref, scratch,
                   make_xnm=None):
    wu_ref, wc_ref, wb_ref = w_refs
    bu_ref, bc_ref, _ = b_refs
    w3_ref, _, h1_ref, h2_ref, carry_ref, mtail_ref = scratch
    n_slab, tm, _ = h1_ref.shape
    n_seq = tm // SUBLANES
    slabs = [(k, slice(k * LANES, (k + 1) * LANES)) for k in range(n_slab)]
    seq_row = lambda r: pl.ds(r, n_seq, stride=SUBLANES)

    @pl.when(m == 0)
    def _():
        if make_xnm is not None:
            make_xnm()
        w3_ref[0] = wu_ref[...].astype(BF16)
        w3_ref[1] = wc_ref[...].astype(BF16)
        w3_ref[2] = wb_ref[...].astype(BF16)
        h1_ref[...] = jnp.zeros_like(h1_ref)
        h2_ref[...] = jnp.zeros_like(h2_ref)
        xnm = xnm_ref[...]
        cu_m = (_dot_nt(xnm, w3_ref[1]) + bc_ref[...]) * (_dot_nt(xnm, w3_ref[0]) + bu_ref[...])
        mtail_ref[...] = cu_m[n_meta - SUBLANES:n_meta, :]

    @pl.when(is_prompt)
    def _():
        first = (m % tiles_per_seq) == 0
        p6 = jnp.where(first, mtail_ref[6:7, :], carry_ref[6:7, :])
        p7 = jnp.where(first, mtail_ref[7:8, :], carry_ref[7:8, :])
        for k, ks in slabs:
            h2_ref[k, 0:1, :] = p6[:, ks]
            h2_ref[k, 1:2, :] = p7[:, ks]
            h1_ref[k, 0:1, :] = p7[:, ks]

    @pl.when(jnp.logical_not(is_prompt))
    def _():
        for k, ks in slabs:
            h2_ref[k, seq_row(0), :] = s0_ref[:, ks]
            h2_ref[k, seq_row(1), :] = s1_ref[:, ks]
            h1_ref[k, seq_row(0), :] = s1_ref[:, ks]


def _conv_main(xn, is_prompt, b_refs, wconv_ref, yb_ref, cu6_ref, cu7_ref, scratch):
    bu_ref, bc_ref, bb_ref = b_refs
    w3_ref, cu_ref, h1_ref, h2_ref, carry_ref, _ = scratch
    tm, cq = yb_ref.shape
    n_seq = tm // SUBLANES
    slabs = [(k, slice(k * LANES, (k + 1) * LANES)) for k in range(cq // LANES)]
    seq_row = lambda r: pl.ds(r, n_seq, stride=SUBLANES)
    proj = lambda k, b_ref: jnp.concatenate([_dot_nt(x, w3_ref[k]) for x in xn], axis=0) + b_ref[...]
    zu, zc, zb = proj(0, bu_ref), proj(1, bc_ref), proj(2, bb_ref)
    cu = zc * zu
    pos = lax.broadcasted_iota(jnp.int32, (tm, cq), 0) & jnp.where(is_prompt, tm - 1, SUBLANES - 1)
    h1 = jnp.concatenate([h1_ref[k] for k, _ in slabs], axis=1)
    h2 = jnp.concatenate([h2_ref[k] for k, _ in slabs], axis=1)
    x1 = jnp.where(pos >= 1, pltpu.roll(cu, 1, 0), h1)
    x2 = jnp.where(pos >= 2, pltpu.roll(cu, 2, 0), h2)
    w0, w1, w2 = wconv_ref[0:1, :], wconv_ref[1:2, :], wconv_ref[2:3, :]
    yb_ref[...] = (zb * ((w0 * x2 + w1 * x1) + w2 * cu)).astype(BF16)

    carry_ref[...] = cu[tm - SUBLANES:tm, :]
    for k, ks in slabs:
        cu_ref[k] = cu[:, ks]
        cu6_ref[:, ks] = cu_ref[k, seq_row(6), :]
        cu7_ref[:, ks] = cu_ref[k, seq_row(7), :]


def _conv_kernel(n_prompt_tiles, tiles_per_seq, n_meta, n_cast, xn_ref, xnm_ref, wu_ref, wc_ref, wb_ref,
                 bu_ref, bc_ref, bb_ref, wconv_ref, s0_ref, s1_ref, *refs):
    cast_in, (yb_ref, cu6_ref, cu7_ref) = refs[:n_cast], refs[n_cast:n_cast + 3]
    cast_out, scratch = refs[n_cast + 3:2 * n_cast + 3], refs[2 * n_cast + 3:]
    m = pl.program_id(1)
    is_prompt = m < n_prompt_tiles
    b_refs = (bu_ref, bc_ref, bb_ref)
    _conv_prologue(m, is_prompt, tiles_per_seq, n_meta, xnm_ref, (wu_ref, wc_ref, wb_ref), b_refs,
                   s0_ref, s1_ref, scratch)
    _cast_blocks(tuple(zip(cast_in, cast_out)))
    _conv_main([xn_ref[...]], is_prompt, b_refs, wconv_ref, yb_ref, cu6_ref, cu7_ref, scratch)


def _rms_conv_kernel(n_prompt_tiles, tiles_per_seq, dec_seq, xp_ref, xs_ref, xm_ref, g_ref, wg_ref, bg_ref,
                     wu_ref, wc_ref, wb_ref, bu_ref, bc_ref, bb_ref, wconv_ref, s0_ref, s1_ref,
                     xn_ref, gch_ref, grh_ref, xnm_ref, gchm_ref, grhm_ref, yb_ref, cu6_ref, cu7_ref, *scratch):
    m = pl.program_id(0)
    is_prompt = m < n_prompt_tiles
    n_meta = xm_ref.shape[0]
    b_refs = (bu_ref, bc_ref, bb_ref)

    def make_xnm():
        xm = jnp.concatenate([xm_ref[...], jnp.zeros((CHUNK - n_meta, D_MODEL), F32)], axis=0)
        xnm = _rmsnorm(xm, g_ref[...]).astype(BF16)
        xnm_ref[...] = xnm
        _gate_prep(xnm, wg_ref, bg_ref, CHUNK, n_meta, gchm_ref, grhm_ref)

    _conv_prologue(m, is_prompt, tiles_per_seq, n_meta, xnm_ref, (wu_ref, wc_ref, wb_ref), b_refs,
                   s0_ref, s1_ref, scratch, make_xnm)
    tm = xn_ref.shape[0]
    half = tm // 2
    xn_blocks = []
    for r in (slice(0, half), slice(half, tm)):
        x = jnp.where(is_prompt, xp_ref[r, :], xs_ref[r, :])
        xn_blocks.append(_rmsnorm(x, g_ref[...]).astype(BF16))
        xn_ref[r, :] = xn_blocks[-1]
    _gate_prep(jnp.concatenate(xn_blocks, axis=0), wg_ref, bg_ref, jnp.where(is_prompt, CHUNK, dec_seq), tm,
               gch_ref, grh_ref)
    _conv_main(xn_blocks, is_prompt, b_refs, wconv_ref, yb_ref, cu6_ref, cu7_ref, scratch)


def _conv_specs(c0, npt, tm, cm):
    n_seq = tm // SUBLANES
    n_slab = CQ // LANES
    chan = lambda *g: c0 + cm(*g)[0]
    tile = lambda *g: cm(*g)[1]
    w_spec = lambda off: pl.BlockSpec((pl.Element(CQ), pl.Element(D_MODEL)),
                                      lambda *g: (pl.multiple_of(off + chan(*g) * CQ, SUBLANES), 0))
    b_spec = lambda off: pl.BlockSpec((1, CQ), lambda *g: (0, (off - N_GATES) // CQ + chan(*g)))
    s_spec = lambda tok: pl.BlockSpec(
        (n_seq, CQ), lambda *g: (jnp.maximum(tile(*g) - npt, 0), tok * (C_WIDTH // CQ) + chan(*g)))
    in_specs = [w_spec(OFF_U), w_spec(OFF_C), w_spec(OFF_B), b_spec(OFF_U), b_spec(OFF_C), b_spec(OFF_B),
                pl.BlockSpec((CONV_W, CQ), lambda *g: (0, chan(*g))), s_spec(0), s_spec(1)]
    out_specs = [pl.BlockSpec((tm, CQ), lambda *g: (tile(*g), cm(*g)[0])),
                 pl.BlockSpec((n_seq, CQ), lambda *g: (tile(*g), cm(*g)[0])),
                 pl.BlockSpec((n_seq, CQ), lambda *g: (tile(*g), cm(*g)[0]))]
    scratch = [pltpu.VMEM((3, CQ, D_MODEL), BF16),
               pltpu.VMEM((n_slab, tm, LANES), F32),
               pltpu.VMEM((n_slab, tm, LANES), F32),
               pltpu.VMEM((n_slab, tm, LANES), F32),
               pltpu.VMEM((SUBLANES, CQ), F32),
               pltpu.VMEM((SUBLANES, CQ), F32)]
    return in_specs, out_specs, scratch


def _conv_out_shapes(t, n_blocks):
    return [jax.ShapeDtypeStruct((t, n_blocks * CQ), BF16),
            jax.ShapeDtypeStruct((t // SUBLANES, n_blocks * CQ), F32),
            jax.ShapeDtypeStruct((t // SUBLANES, n_blocks * CQ), F32)]


def _rms_conv_call(xp, xs, x_meta, g, wt, bg, b1, wconv, sconv, dec_seq, seq_len, tm):
    tp, ts = xp.shape[0], xs.shape[0]
    t = tp + ts
    assert tm & (tm - 1) == 0 and seq_len % tm == 0 and x_meta.shape[0] >= SUBLANES
    npt, nst = tp // tm, ts // tm
    conv_in, conv_out, scratch = _conv_specs(0, npt, tm, lambda m: (0, m))
    return pl.pallas_call(
        functools.partial(_rms_conv_kernel, npt, seq_len // tm, dec_seq),
        grid=(npt + nst,),
        in_specs=[
            pl.BlockSpec((tm, D_MODEL), lambda m: (jnp.minimum(m, npt - 1), 0)),
            pl.BlockSpec((tm, D_MODEL), lambda m: (jnp.maximum(m - npt, 0), 0)),
            pl.BlockSpec(x_meta.shape, lambda m: (0, 0)),
            pl.BlockSpec((1, D_MODEL), lambda m: (0, 0)),
            pl.BlockSpec((N_GATES, D_MODEL), lambda m: (OFF_GATES // N_GATES, 0)),
            pl.BlockSpec((N_GATES, 1), lambda m: (0, 0)),
        ] + conv_in,
        out_specs=[
            pl.BlockSpec((tm, D_MODEL), lambda m: (m, 0)),
            pl.BlockSpec((M_HEADS, 2, tm, LANES), lambda m: (0, 0, m, 0)),
            pl.BlockSpec((M_HEADS, SUBLANES, tm), lambda m: (0, 0, m)),
            pl.BlockSpec((CHUNK, D_MODEL), lambda m: (0, 0)),
            pl.BlockSpec((M_HEADS, 2, CHUNK, LANES), lambda m: (0, 0, 0, 0)),
            pl.BlockSpec((M_HEADS, SUBLANES, CHUNK), lambda m: (0, 0, 0)),
        ] + conv_out,
        out_shape=[
            jax.ShapeDtypeStruct((t, D_MODEL), BF16),
            jax.ShapeDtypeStruct((M_HEADS, 2, t, LANES), F32),
            jax.ShapeDtypeStruct((M_HEADS, SUBLANES, t), F32),
            jax.ShapeDtypeStruct((CHUNK, D_MODEL), BF16),
            jax.ShapeDtypeStruct((M_HEADS, 2, CHUNK, LANES), F32),
            jax.ShapeDtypeStruct((M_HEADS, SUBLANES, CHUNK), F32),
        ] + _conv_out_shapes(t, 1),
        scratch_shapes=scratch,
        compiler_params=_params(("arbitrary",)),
        name="rms_conv",
    )(xp, xs, x_meta, g, wt, bg, wt, wt, wt, b1, b1, b1, wconv, sconv, sconv)


def _conv_call(xn, xnm, n_meta, wt, b1, wconv, sconv, cast_ws, n_prompt_tokens, seq_len, tm):
    t = xn.shape[0]
    assert tm & (tm - 1) == 0 and seq_len % tm == 0 and n_meta >= SUBLANES
    npt = n_prompt_tokens // tm
    n_m = t // tm
    n_blocks = C_WIDTH // CQ - 1
    n_cast = 16
    assert n_cast <= n_blocks * n_m
    casts = [_cast_specs(w, n_cast, lambda c, m: c * n_m + m) for w in cast_ws]
    conv_in, conv_out, scratch = _conv_specs(1, npt, tm, lambda c, m: (c, m))
    return pl.pallas_call(
        functools.partial(_conv_kernel, npt, seq_len // tm, n_meta, len(cast_ws)),
        grid=(n_blocks, n_m),
        in_specs=[
            pl.BlockSpec((tm, D_MODEL), lambda c, m: (m, 0)),
            pl.BlockSpec(xnm.shape, lambda c, m: (0, 0)),
        ] + conv_in + [cs[0] for cs in casts],
        out_specs=conv_out + [cs[1] for cs in casts],
        out_shape=_conv_out_shapes(t, n_blocks) + [cs[2] for cs in casts],
        scratch_shapes=scratch,
        compiler_params=_params(("arbitrary", "arbitrary")),
        name="inproj_conv",
    )(xn, xnm, wt, wt, wt, b1, b1, b1, wconv, sconv, sconv, *cast_ws)


def _prompt_head(q, kt, v, gr, gc, ca, m_st):
    l = q.shape[0]
    assert l == LANES
    ig_row, b_row, r_row = gr[G_IG:G_IG + 1, :], gr[G_B:G_B + 1, :], gr[G_R:G_R + 1, :]
    b_col = gc[0]
    v_aug = jnp.concatenate([v, jnp.ones((l, LANES), BF16)], axis=1)
    t_idx = lax.broadcasted_iota(jnp.int32, (l, l), 0)
    s_idx = lax.broadcasted_iota(jnp.int32, (l, l), 1)
    logw = jnp.where(s_idx <= t_idx, b_col - b_row + ig_row, NEG)
    inter = b_col + m_st
    m_t = jnp.maximum(inter, gc[1])
    w_inter = jnp.exp(inter - m_t)
    s = _dot(q, kt) * jnp.exp(logw - m_t)
    nd = (jnp.concatenate([w_inter] * (AUG // LANES), axis=1) * _dot(q, ca.astype(BF16))
          + _dot(s.astype(BF16), v_aug))
    rcp = 1.0 / jnp.maximum(jnp.abs(nd[:, M_DV:AUG]), jnp.exp(-m_t))
    h = nd[:, 0:M_DV] * jnp.concatenate([rcp] * (M_DV // LANES), axis=1)
    m_new = m_t[l - 1:l, 0:1]
    decay = jnp.exp(b_col[l - 1:l, 0:1] + m_st - m_new)
    kwt = (kt.astype(F32) * jnp.exp(r_row + ig_row - m_new)).astype(BF16)
    return h, decay * ca + _dot(kwt, v_aug), m_new


def _mlstm_prompt_kernel(batch, qm_ref, ktm_ref, vm_ref, gcm_ref, grm_ref, *refs):
    ins, (h_ref, c_ref, n_ref, m_ref, ca_ref) = refs[:5 * batch], refs[5 * batch:]
    heads = [(hd, slice(hd * M_DK, (hd + 1) * M_DK)) for hd in range(M_HEADS)]

    @pl.when(pl.program_id(0) == 0)
    def _():
        for hd, sl in heads:
            _, ca_new, m_new = _prompt_head(qm_ref[:, sl], ktm_ref[sl, :], vm_ref[:, sl], grm_ref[hd],
                                            gcm_ref[hd], jnp.zeros((M_DK, AUG), F32), jnp.zeros((1, 1), F32))
            for b in range(batch):
                ca_ref[b, hd] = ca_new
                m_ref[b, hd:hd + 1, :] = jnp.broadcast_to(m_new, (1, LANES))

    for b in range(batch):
        q_ref, kt_ref, v_ref, gc_ref, gr_ref = ins[5 * b:5 * b + 5]
        for hd, sl in heads:
            h, ca_new, m_new = _prompt_head(q_ref[:, sl], kt_ref[sl, :], v_ref[:, sl], gr_ref[hd], gc_ref[hd],
                                            ca_ref[b, hd], m_ref[b, hd:hd + 1, 0:1])
            ca_ref[b, hd] = ca_new
            m_ref[b, hd:hd + 1, :] = jnp.broadcast_to(m_new, (1, LANES))
            h_ref[b, :, sl] = h.astype(h_ref.dtype)

    @pl.when(pl.program_id(0) == pl.num_programs(0) - 1)
    def _():
        for b in range(batch):
            for hd, _ in heads:
                ca = ca_ref[b, hd]
                c_ref[b, hd] = ca[:, 0:M_DV]
                n_t = ca[:, M_DV:AUG]
                n_ref[b, hd:hd + 1, :] = jnp.concatenate(
                    [n_t[k * LANES:(k + 1) * LANES, :].T[0:1, :] for k in range(M_DK // LANES)], axis=1)


def _mlstm_prompt_call(zm, ktm, gcm, grm, z, kt, gch, grh, batch, seq):
    nc = seq // CHUNK
    per_prompt_specs, per_prompt_args = [], []
    for b in range(batch):
        row = functools.partial(lambda b, c: b * nc + c, b)
        per_prompt_specs += [
            pl.BlockSpec((CHUNK, M_WIDTH), lambda c, row=row: (row(c), Z_Q)),
            pl.BlockSpec((M_WIDTH, CHUNK), lambda c, row=row: (0, row(c))),
            pl.BlockSpec((CHUNK, M_WIDTH), lambda c, row=row: (row(c), Z_V)),
            pl.BlockSpec((M_HEADS, 2, CHUNK, LANES), lambda c, row=row: (0, 0, row(c), 0)),
            pl.BlockSpec((M_HEADS, SUBLANES, CHUNK), lambda c, row=row: (0, 0, row(c))),
        ]
        per_prompt_args += [z, kt, z, gch, grh]
    return pl.pallas_call(
        functools.partial(_mlstm_prompt_kernel, batch),
        grid=(nc,),
        in_specs=[
            pl.BlockSpec((CHUNK, M_WIDTH), lambda c: (0, 0)),
            pl.BlockSpec((M_WIDTH, CHUNK), lambda c: (0, 0)),
            pl.BlockSpec((CHUNK, M_WIDTH), lambda c: (0, 1)),
            pl.BlockSpec((M_HEADS, 2, CHUNK, LANES), lambda c: (0, 0, 0, 0)),
            pl.BlockSpec((M_HEADS, SUBLANES, CHUNK), lambda c: (0, 0, 0)),
        ] + per_prompt_specs,
        out_specs=[
            pl.BlockSpec((batch, CHUNK, M_WIDTH), lambda c: (0, c, 0)),
            pl.BlockSpec((batch, M_HEADS, M_DK, M_DV), lambda c: (0, 0, 0, 0)),
            pl.BlockSpec((batch, M_HEADS, M_DK), lambda c: (0, 0, 0)),
            pl.BlockSpec((batch, M_HEADS, LANES), lambda c: (0, 0, 0)),
        ],
        out_shape=[
            jax.ShapeDtypeStruct((batch, seq, M_WIDTH), BF16),
            jax.ShapeDtypeStruct((batch, M_HEADS, M_DK, M_DV), F32),
            jax.ShapeDtypeStruct((batch, M_HEADS, M_DK), F32),
            jax.ShapeDtypeStruct((batch, M_HEADS, LANES), F32),
        ],
        scratch_shapes=[pltpu.VMEM((batch, M_HEADS, M_DK, AUG), F32)],
        compiler_params=_params(("arbitrary",)),
        name="mlstm_prompt",
    )(zm, ktm, zm, gcm, grm, *per_prompt_args)


def _group_max(x, size):
    n = x.shape[-1]
    lane = lax.broadcasted_iota(jnp.int32, x.shape, x.ndim - 1)
    k = 1
    while k < size:
        partner = jnp.where((lane & k) == 0, pltpu.roll(x, n - k, x.ndim - 1), pltpu.roll(x, k, x.ndim - 1))
        x = jnp.maximum(x, partner)
        k *= 2
    return x


def _mlstm_sample_kernel(dec_seq, q_ref, kt_ref, v_ref, gc_ref, gr_ref, mcol_ref, mrow_ref, cin_ref, nin_ref,
                         h_ref, c_ref, n_ref, m_ref):
    l = q_ref.shape[0]
    n_seq = l // dec_seq
    shift = dec_seq.bit_length() - 1
    q, kt, v = q_ref[...], kt_ref[...], v_ref[...]
    assert l == LANES
    gr, gc = gr_ref[0], gc_ref[0]
    ig_row, b_row, r_row = gr[G_IG:G_IG + 1, :], gr[G_B:G_B + 1, :], gr[G_R:G_R + 1, :]
    b_col = gc[0]
    m_col, m_row = mcol_ref[0], mrow_ref[0]

    t_idx = lax.broadcasted_iota(jnp.int32, (l, l), 0)
    s_idx = lax.broadcasted_iota(jnp.int32, (l, l), 1)
    t_seq = t_idx >> shift
    logw = jnp.where((t_seq == (s_idx >> shift)) & (s_idx <= t_idx), b_col - b_row + ig_row, NEG)
    inter = b_col + m_col
    m_t = jnp.maximum(inter, gc[1])
    w_inter = jnp.exp(inter - m_t)
    s = _dot(q, kt) * jnp.exp(logw - m_t)

    e_row = r_row + ig_row
    b_last = b_row + r_row
    m_new = jnp.maximum(b_last + m_row, _group_max(e_row, dec_seq))
    decay = jnp.exp(b_last + m_row - m_new)
    kwt = (kt.astype(F32) * jnp.exp(e_row - m_new)).astype(BF16)

    seq_lane = s_idx == t_seq
    seq_onehot = jnp.where(seq_lane, 1.0, 0.0).astype(BF16)
    pa = _dot(s.astype(BF16), jnp.concatenate([v, seq_onehot], axis=1))

    n_rows = jnp.concatenate([nin_ref[...], jnp.zeros((LANES - n_seq, M_DK), F32)], axis=0)
    n_t = jnp.concatenate([n_rows[:, 0:LANES].T, n_rows[:, LANES:2 * LANES].T], axis=0)
    lane_k = lax.broadcasted_iota(jnp.int32, (M_DK, LANES), 1)
    lane_1 = lax.broadcasted_iota(jnp.int32, (1, LANES), 1)
    qf = q.astype(F32)
    qca_rows = []
    decay_seq = jnp.zeros((1, LANES), F32)
    for i in range(n_seq):
        decay_i = decay[:, i * dec_seq:i * dec_seq + 1]
        decay_seq = jnp.where(lane_1 == i, decay_i, decay_seq)
        c_i = cin_ref[i, 0]
        ca = jnp.concatenate([c_i, jnp.where(lane_k == i, n_t, 0.0)], axis=1)
        rows = slice(i * dec_seq, (i + 1) * dec_seq)
        qca_rows.append(_dot(qf[rows, :].astype(BF16), ca.astype(BF16)))
        kw_i = jnp.where((lane_k >> shift) == i, kwt, jnp.zeros_like(kwt))
        c_ref[i, 0] = decay_i * c_i + _dot(kw_i, v)
    n_t_new = decay_seq * n_t + _dot(kwt, seq_onehot)
    nd = jnp.concatenate([w_inter] * (AUG // LANES), axis=1) * jnp.concatenate(qca_rows, axis=0) + pa
    den = jnp.sum(jnp.where(seq_lane, nd[:, M_DV:AUG], 0.0), axis=-1, keepdims=True)
    rcp = 1.0 / jnp.maximum(jnp.abs(den), jnp.exp(-m_t[:, 0:1]))
    h_ref[...] = (nd[:, 0:M_DV] * rcp).astype(h_ref.dtype)
    n_ref[...] = jnp.concatenate([n_t_new[0:LANES, :].T, n_t_new[LANES:2 * LANES, :].T], axis=1)[0:n_seq, :]
    m_ref[0] = jnp.broadcast_to(m_new, (SUBLANES, l))


def _gates_sample_kernel(dec_seq, xn_ref, wt_ref, b_ref, *refs):
    sample_in, z_ref, sample_out, w_ref = refs[:9], refs[9], refs[10:14], refs[14]

    @pl.when(pl.program_id(1) == 0)
    def _():
        w_ref[...] = wt_ref[...].astype(BF16)

    _mlstm_sample_kernel(dec_seq, *sample_in, *sample_out)
    z_ref[...] = _sigmoid(_dot_nt(xn_ref[...], w_ref[...]) + b_ref[...]).astype(BF16)


def _gates_sample_call(xn, wt, b1, zqv, kt, gch, grh, m_col, m_row, c_in, n_in, row0, dec_seq, tm):
    t = xn.shape[0]
    n_m = t // tm
    nb = c_in.shape[0]
    l = SEQ_PER_STEP * dec_seq
    blk0 = row0 // l
    n_steps = 2 * D_MODEL // TN
    n_guest = (nb // SEQ_PER_STEP) * M_HEADS
    assert n_guest <= n_steps * n_m
    w_off = lambda s: OFF_GA + s * TN
    b_blk = lambda s: (OFF_GA - N_GATES) // TN + s
    z_col = lambda s: s
    guest = lambda s, m: jnp.minimum(s * n_m + m, n_guest - 1)
    gi = lambda s, m: guest(s, m) // M_HEADS
    gh = lambda s, m: guest(s, m) % M_HEADS
    return pl.pallas_call(
        functools.partial(_gates_sample_kernel, dec_seq),
        grid=(n_steps, n_m),
        in_specs=[
            pl.BlockSpec((tm, D_MODEL), lambda s, m: (m, 0)),
            pl.BlockSpec((pl.Element(TN), pl.Element(D_MODEL)),
                         lambda s, m: (pl.multiple_of(w_off(s), SUBLANES), 0)),
            pl.BlockSpec((1, TN), lambda s, m: (0, b_blk(s))),
            pl.BlockSpec((l, M_DK), lambda s, m: (blk0 + gi(s, m), Z_Q * M_HEADS + gh(s, m))),
            pl.BlockSpec((M_DK, l), lambda s, m: (gh(s, m), blk0 + gi(s, m))),
            pl.BlockSpec((l, M_DV), lambda s, m: (blk0 + gi(s, m), Z_V * M_HEADS + gh(s, m))),
            pl.BlockSpec((1, 2, l, LANES), lambda s, m: (gh(s, m), 0, blk0 + gi(s, m), 0)),
            pl.BlockSpec((1, SUBLANES, l), lambda s, m: (gh(s, m), 0, blk0 + gi(s, m))),
            pl.BlockSpec((1, l, LANES), lambda s, m: (gh(s, m), gi(s, m), 0)),
            pl.BlockSpec((1, 1, l), lambda s, m: (gh(s, m), 0, gi(s, m))),
            pl.BlockSpec((SEQ_PER_STEP, 1, M_DK, M_DV), lambda s, m: (gi(s, m), gh(s, m), 0, 0)),
            pl.BlockSpec((SEQ_PER_STEP, M_DK), lambda s, m: (gi(s, m), gh(s, m))),
        ],
        out_specs=[
            pl.BlockSpec((tm, TN), lambda s, m: (m, z_col(s))),
            pl.BlockSpec((l, M_DV), lambda s, m: (gi(s, m), gh(s, m))),
            pl.BlockSpec((SEQ_PER_STEP, 1, M_DK, M_DV), lambda s, m: (gi(s, m), gh(s, m), 0, 0)),
            pl.BlockSpec((SEQ_PER_STEP, M_DK), lambda s, m: (gi(s, m), gh(s, m))),
            pl.BlockSpec((1, SUBLANES, l), lambda s, m: (gh(s, m), 0, gi(s, m))),
        ],
        out_shape=[
            jax.ShapeDtypeStruct((t, ZG_COLS), BF16),
            jax.ShapeDtypeStruct((nb * dec_seq, M_WIDTH), BF16),
            jax.ShapeDtypeStruct(c_in.shape, F32),
            jax.ShapeDtypeStruct(n_in.shape, F32),
            jax.ShapeDtypeStruct((M_HEADS, SUBLANES, nb * dec_seq), F32),
        ],
        scratch_shapes=[pltpu.VMEM((TN, D_MODEL), BF16)],
        compiler_params=_params(("arbitrary", "arbitrary")),
        name="inproj_gates_mlstm_sample",
    )(xn, wt, b1, zqv, kt, zqv, gch, grh, m_col, m_row, c_in, n_in)


def _mix_kernel(h_ref, so_ref, yb0_ref, yb1_ref, ga_ref, gb_ref, x_ref, pa_ref, pb_ref, wo_ref, gffn_ref,
                o_ref, on_ref):
    h_a = (so_ref[...].astype(F32) * h_ref[...].astype(F32)).astype(BF16)
    a = _dot(h_a, pa_ref[...])
    b = _dot(jnp.concatenate([yb0_ref[...], yb1_ref[...]], axis=1), pb_ref[...])
    merged = ga_ref[...].astype(F32) * a + gb_ref[...].astype(F32) * b
    x1 = x_ref[...] + _dot(merged.astype(BF16), wo_ref[...])
    o_ref[...] = x1
    on_ref[...] = _rmsnorm(x1, gffn_ref[...]).astype(BF16)


def _mix_call(h, x, zqvo, zg, yb0, yb1, pa, pb, wo, gffn, row0, tm):
    rows = x.shape[0]
    assert row0 % tm == 0 and rows % tm == 0
    blk0 = row0 // tm
    const = lambda i: (0, 0)
    return pl.pallas_call(
        _mix_kernel,
        grid=(rows // tm,),
        in_specs=[
            pl.BlockSpec((tm, M_WIDTH), lambda i: (i, 0)),
            pl.BlockSpec((tm, M_WIDTH), lambda i: (blk0 + i, Z_SO)),
            pl.BlockSpec((tm, yb0.shape[1]), lambda i: (blk0 + i, 0)),
            pl.BlockSpec((tm, yb1.shape[1]), lambda i: (blk0 + i, 0)),
            pl.BlockSpec((tm, D_MODEL), lambda i: (blk0 + i, Z_GA // 2)),
            pl.BlockSpec((tm, D_MODEL), lambda i: (blk0 + i, Z_GB // 2)),
            pl.BlockSpec((tm, D_MODEL), lambda i: (i, 0)),
            pl.BlockSpec((M_WIDTH, D_MODEL), const, pipeline_mode=pl.Buffered(1)),
            pl.BlockSpec((C_WIDTH, D_MODEL), const, pipeline_mode=pl.Buffered(1)),
            pl.BlockSpec((D_MODEL, D_MODEL), const, pipeline_mode=pl.Buffered(1)),
            pl.BlockSpec((1, D_MODEL), const),
        ],
        out_specs=[pl.BlockSpec((tm, D_MODEL), lambda i: (i, 0)), pl.BlockSpec((tm, D_MODEL), lambda i: (i, 0))],
        out_shape=[jax.ShapeDtypeStruct((rows, D_MODEL), F32), jax.ShapeDtypeStruct((rows, D_MODEL), BF16)],
        compiler_params=_params(("arbitrary",)),
        name="mix",
    )(h, zqvo, yb0, yb1, zg, zg, x, pa, pb, wo, gffn)


def _ffn_kernel(x1_ref, xn_ref, wup_ref, wdn_ref, gfin_ref, y_ref):
    f = pl.program_id(1)
    last = pl.num_programs(1) - 1

    def mlp_part():
        u = jnp.maximum(_dot(xn_ref[...], wup_ref[...]), 0.0)
        return _dot((u * u).astype(BF16), wdn_ref[...])

    @pl.when(f == 0)
    def _():
        y_ref[...] = x1_ref[...] + mlp_part()

    @pl.when((f > 0) & (f < last))
    def _():
        y_ref[...] += mlp_part()

    @pl.when(f == last)
    def _():
        y_ref[...] = _rmsnorm(y_ref[...] + mlp_part(), gfin_ref[...])


def _ffn_call(x1, xn, wup, wdn, gfin, tm, tf):
    rows = x1.shape[0]
    assert D_FF // tf >= 2 and rows % tm == 0
    return pl.pallas_call(
        _ffn_kernel,
        grid=(rows // tm, D_FF // tf),
        in_specs=[
            pl.BlockSpec((tm, D_MODEL), lambda i, f: (i, 0)),
            pl.BlockSpec((tm, D_MODEL), lambda i, f: (i, 0)),
            pl.BlockSpec((D_MODEL, tf), lambda i, f: (0, f)),
            pl.BlockSpec((tf, D_MODEL), lambda i, f: (f, 0)),
            pl.BlockSpec((1, D_MODEL), lambda i, f: (0, 0)),
        ],
        out_specs=pl.BlockSpec((tm, D_MODEL), lambda i, f: (i, 0)),
        out_shape=jax.ShapeDtypeStruct((rows, D_MODEL), F32),
        compiler_params=_params(("arbitrary", "arbitrary")),
        name="ffn",
    )(x1, xn, wup, wdn, gfin)


def kernel(x_prompt, x_sample, state_mlstm_C, state_mlstm_n, state_mlstm_m, state_conv, meta_tokens,
           g_mix, w_in, b_in, w_conv, p_a, p_b, w_o, g_ffn, w_up, w_down, g_final):
    assert w_in.shape[0] == 1, "single-layer trunk"
    batch, seq, _ = x_prompt.shape
    dec_batch, dec_seq, _ = x_sample.shape
    assert dec_seq == SUBLANES and seq % 1024 == 0 and dec_batch % SEQ_PER_STEP == 0
    n_p, n_s = batch * seq, dec_batch * dec_seq

    wt = w_in[0].T
    bias = b_in[0]
    b1 = jnp.concatenate([bias[:OFF_GATES], bias[OFF_O:]])
    bg = bias[OFF_GATES:OFF_O][:, None]
    gmix = g_mix[0][None, :]

    xp = x_prompt.reshape(n_p, D_MODEL)
    xs = x_sample.reshape(n_s, D_MODEL)
    n_meta = meta_tokens.shape[0]

    sconv = state_conv[0].reshape(dec_batch, (CONV_W - 1) * C_WIDTH)
    xn, gch, grh, xnm, gcm, grm, yb0, cu6a, cu7a = _rms_conv_call(
        xp, xs, meta_tokens.astype(F32), gmix, wt, bg, b1[None, :], w_conv[0], sconv, dec_seq, seq, 512)
    zqv, kt, zm, ktm, w_up16, w_down16 = _qkvo_call(xn, xnm, wt, b1[None, :], (w_up[0], w_down[0]), 1024)
    m_tok = jnp.repeat(state_mlstm_m[0], dec_seq, axis=0).T
    zg, hs, c_s, n_sm, m_s = _gates_sample_call(
        xn, wt, b1[None, :], zqv, kt, gch, grh,
        jnp.broadcast_to(m_tok[:, :, None], m_tok.shape + (LANES,)), m_tok[:, None, :],
        state_mlstm_C[0], state_mlstm_n[0].reshape(dec_batch, M_HEADS * M_DK), n_p, dec_seq, 1152)
    yb1, cu6b, cu7b, p_a16, p_b16, w_o16 = _conv_call(
        xn, xnm, n_meta, wt, b1[None, :], w_conv[0], sconv, (p_a[0], p_b[0], w_o[0]), n_p, seq, 1024)

    hp, c_p, n_pr, m_p = _mlstm_prompt_call(zm, ktm, gcm, grm, zqv, kt, gch, grh, batch, seq)

    y_p, y_s = [
        _ffn_call(*_mix_call(h, x, zqv, zg, yb0, yb1, p_a16, p_b16, w_o16, g_ffn[0][None, :], row0, 512),
                  w_up16, w_down16, g_final[None, :], 512, 2048)
        for h, x, row0 in ((hp.reshape(n_p, M_WIDTH), xp, 0), (hs, xs, n_p))]

    per_seq = seq // SUBLANES
    first_s = n_p // SUBLANES
    pick = lambda rows: jnp.stack([jnp.concatenate([cu6a[rows], cu6b[rows]], axis=1),
                                   jnp.concatenate([cu7a[rows], cu7b[rows]], axis=1)], axis=1)
    cv_p = pick(slice(per_seq - 1, batch * per_seq, per_seq))
    cv_s = pick(slice(first_s, None))
    m_s = m_s[:, 0, dec_seq - 1::dec_seq].T
    return (y_p.reshape(batch, seq, D_MODEL), y_s.reshape(dec_batch, dec_seq, D_MODEL),
            c_p[None], n_pr[None], m_p[None, :, :, 0], cv_p[None],
            c_s[None], n_sm.reshape(dec_batch, M_HEADS, M_DK)[None], m_s[None], cv_s[None])
```

```python
import functools

import jax
import jax.numpy as jnp
from jax import lax
from jax.experimental import pallas as pl
from jax.experimental.pallas import tpu as pltpu

F32 = jnp.float32
BF16 = jnp.bfloat16

D_MODEL = 2048
CHUNK = 128
M_HEADS = 4
M_DK = 256
M_DV = 256
M_WIDTH = M_HEADS * M_DV
C_WIDTH = 1024
CONV_W = 3
D_FF = 4 * D_MODEL
EPS = 1e-6
N_GATES = 2 * M_HEADS
OFF_Q = 0
OFF_K = OFF_Q + M_HEADS * M_DK
OFF_V = OFF_K + M_HEADS * M_DK
OFF_GATES = OFF_V + M_WIDTH
OFF_O = OFF_GATES + N_GATES
OFF_U = OFF_O + M_WIDTH
OFF_C = OFF_U + C_WIDTH
OFF_B = OFF_C + C_WIDTH
OFF_GA = OFF_B + C_WIDTH

LANES = 128
SUBLANES = 8
VMEM_LIMIT_BYTES = 60000 * 1024

TN = 1024
Z_Q, Z_V, Z_SO = 0, 1, 2
Z_GA, Z_GB = 0, 2
ZG_COLS = 4 * TN
CQ = 256
AUG = M_DV + LANES
SEQ_PER_STEP = CHUNK // SUBLANES
G_IG, G_B, G_R = 0, 1, 2

NEG = -1e30
NT_DIMS = (((1,), (1,)), ((), ()))


def _params(semantics):
    return pltpu.CompilerParams(dimension_semantics=semantics, vmem_limit_bytes=VMEM_LIMIT_BYTES)


def _rmsnorm(x, g):
    y = x * lax.rsqrt(jnp.mean(x * x, axis=-1, keepdims=True) + EPS)
    return y * g


def _log_sigmoid(x):
    return jnp.minimum(x, 0.0) - jnp.log1p(jnp.exp(-jnp.abs(x)))


def _dot(a, b):
    return jnp.dot(a, b, preferred_element_type=F32)


def _dot_nt(a, b):
    return lax.dot_general(a, b, NT_DIMS, preferred_element_type=F32)


def _gate_prep(xn, wg_ref, bg_ref, blk, n_valid, gch_ref, grh_ref):
    tm = xn.shape[0]
    wg = wg_ref[...].astype(BF16)
    wg = jnp.concatenate([wg, jnp.zeros((LANES - N_GATES, wg.shape[1]), BF16)], axis=0)
    g = _dot_nt(wg, xn)[0:SUBLANES, :] + bg_ref[...]
    row = lax.broadcasted_iota(jnp.int32, (SUBLANES, tm), 0)
    lane = lax.broadcasted_iota(jnp.int32, (SUBLANES, tm), 1)
    a = jnp.where(row < M_HEADS, g, _log_sigmoid(g))
    if n_valid < tm:
        a = jnp.where(lane < n_valid, a, jnp.where(row < M_HEADS, NEG, 0.0))
    pos = lane & (blk - 1)
    n_steps_blk = blk if isinstance(blk, int) else LANES

    def scan(x, op, fill, reverse=False):
        shift = 1
        while shift < n_steps_blk:
            if reverse:
                x = op(x, jnp.where(pos < blk - shift, pltpu.roll(x, tm - shift, 1), fill))
            else:
                x = op(x, jnp.where(pos >= shift, pltpu.roll(x, shift, 1), fill))
            shift *= 2
        return x

    pre = scan(a, jnp.add, 0.0)
    suf = scan(a, jnp.add, 0.0, reverse=True) - a
    b_up = pltpu.roll(pre, M_HEADS, 0)
    m_in = b_up + scan(a - b_up, jnp.maximum, -3e38)
    for h in range(M_HEADS):
        grh_ref[h] = jnp.where(
            row == G_IG, pltpu.roll(a, (G_IG - h) % SUBLANES, 0),
            jnp.where(row == G_B, pltpu.roll(pre, (G_B - M_HEADS - h) % SUBLANES, 0),
                      jnp.where(row == G_R, pltpu.roll(suf, (G_R - M_HEADS - h) % SUBLANES, 0), 0.0)))
        for c in range(tm // LANES):
            cs = slice(c * LANES, (c + 1) * LANES)
            gch_ref[h, 0, cs, :] = jnp.broadcast_to(b_up[h:h + 1, cs], (LANES, LANES)).T
            gch_ref[h, 1, cs, :] = jnp.broadcast_to(m_in[h:h + 1, cs], (LANES, LANES)).T


S_Q, S_K, S_V, S_O = 0, 1, 2, 3


def _cast_blocks(pairs):
    for src_ref, dst_ref in pairs:
        dst_ref[...] = src_ref[...].astype(BF16)


def _cast_specs(w, n_blocks, step):
    rows, cols = w.shape
    blk = lambda *g: (jnp.minimum(step(*g), n_blocks - 1), 0)
    spec = pl.BlockSpec((rows // n_blocks, cols), blk)
    return spec, spec, jax.ShapeDtypeStruct(w.shape, BF16)


def _sigmoid(z):
    return 0.5 * jnp.tanh(0.5 * z) + 0.5


def _qkvo_kernel(n_cast, xn_ref, xnm_ref, wt_ref, b_ref, *refs):
    cast_in, (z_ref, kt_ref, zm_ref, ktm_ref) = refs[:n_cast], refs[n_cast:n_cast + 4]
    cast_out, (w_ref, bcol_ref) = refs[n_cast + 4:2 * n_cast + 4], refs[2 * n_cast + 4:]
    s = pl.program_id(0)
    casts = tuple(zip(cast_in, cast_out))
    k_scale = M_DK ** -0.5
    lane_tiles = lambda x, n: jnp.concatenate([x] * (n // LANES), axis=1)

    @pl.when(pl.program_id(1) == 0)
    def _():
        w_ref[...] = wt_ref[...].astype(BF16)

        @pl.when((s == S_Q) | (s == S_V))
        def _():
            zm_ref[...] = (_dot_nt(xnm_ref[...], w_ref[...]) + b_ref[...]).astype(BF16)

        @pl.when(s == S_K)
        def _():
            for c in range(TN // LANES):
                cs = slice(c * LANES, (c + 1) * LANES)
                bcol_ref[cs, :] = jnp.broadcast_to(b_ref[:, cs], (LANES, LANES)).T
            ktm = _dot_nt(w_ref[...], xnm_ref[...]) + lane_tiles(bcol_ref[...], xnm_ref.shape[0])
            ktm_ref[...] = (ktm * k_scale).astype(BF16)

    def z():
        return _dot_nt(xn_ref[...], w_ref[...]) + b_ref[...]

    @pl.when((s == S_Q) | (s == S_V))
    def _():
        _cast_blocks(casts)
        z_ref[...] = z().astype(BF16)

    @pl.when(s == S_K)
    def _():
        _cast_blocks(casts)
        kt = _dot_nt(w_ref[...], xn_ref[...]) + lane_tiles(bcol_ref[...], xn_ref.shape[0])
        kt_ref[...] = (kt * k_scale).astype(BF16)

    @pl.when(s == S_O)
    def _():
        _cast_blocks(casts)
        z_ref[...] = _sigmoid(z()).astype(BF16)


def _qkvo_call(xn, xnm, wt, b1, cast_ws, tm):
    t = xn.shape[0]
    assert t % tm == 0
    n_m = t // tm
    rows_m = xnm.shape[0]
    n_cast = 32
    assert n_cast <= 4 * n_m
    casts = [_cast_specs(w, n_cast, lambda s, m: s * n_m + m) for w in cast_ws]
    w_off = lambda s: jnp.where(s == S_O, OFF_O, s * TN)
    z_col = lambda s: jnp.where(s <= S_K, Z_Q, s - 1)
    z_row = lambda s, m: jnp.where(s == S_K, n_m - 1, m)
    kt_blk = lambda s, m: jnp.where(s == S_K, m, jnp.where(s < S_K, 0, n_m - 1))
    return pl.pallas_call(
        functools.partial(_qkvo_kernel, len(cast_ws)),
        grid=(4, n_m),
        in_specs=[
            pl.BlockSpec((tm, D_MODEL), lambda s, m: (m, 0)),
            pl.BlockSpec((rows_m, D_MODEL), lambda s, m: (0, 0)),
            pl.BlockSpec((pl.Element(TN), pl.Element(D_MODEL)),
                         lambda s, m: (pl.multiple_of(w_off(s), SUBLANES), 0)),
            pl.BlockSpec((1, TN), lambda s, m: (0, s)),
        ] + [cs[0] for cs in casts],
        out_specs=[
            pl.BlockSpec((tm, TN), lambda s, m: (z_row(s, m), z_col(s))),
            pl.BlockSpec((TN, tm), lambda s, m: (0, kt_blk(s, m))),
            pl.BlockSpec((rows_m, TN), lambda s, m: (0, (s >= S_V).astype(jnp.int32))),
            pl.BlockSpec((TN, rows_m), lambda s, m: (0, 0)),
        ] + [cs[1] for cs in casts],
        out_shape=[
            jax.ShapeDtypeStruct((t, 3 * TN), BF16),
            jax.ShapeDtypeStruct((TN, t), BF16),
            jax.ShapeDtypeStruct((rows_m, 2 * TN), BF16),
            jax.ShapeDtypeStruct((TN, rows_m), BF16),
        ] + [cs[2] for cs in casts],
        scratch_shapes=[pltpu.VMEM((TN, D_MODEL), BF16), pltpu.VMEM((TN, LANES), F32)],
        compiler_params=_params(("arbitrary", "arbitrary")),
        name="inproj_qkvo",
    )(xn, xnm, wt, b1, *cast_ws)


def _conv_prologue(m, is_prompt, tiles_per_seq, n_meta, xnm_ref, w_refs, b_refs, s0_ref, s1_ref, scratch,
                   make_xnm=None):
    wu_ref, wc_ref, wb_ref = w_refs
    bu_ref, bc_ref, _ = b_refs
    w3_ref, _, h1_ref, h2_ref, carry_ref, mtail_ref = scratch
    n_slab, tm, _ = h1_ref.shape
    n_seq = tm // SUBLANES
    slabs = [(k, slice(k * LANES, (k + 1) * LANES)) for k in range(n_slab)]
    seq_row = lambda r: pl.ds(r, n_seq, stride=SUBLANES)

    @pl.when(m == 0)
    def _():
        if make_xnm is not None:
            make_xnm()
        w3_ref[0] = wu_ref[...].astype(BF16)
        w3_ref[1] = wc_ref[...].astype(BF16)
        w3_ref[2] = wb_ref[...].astype(BF16)
        h1_ref[...] = jnp.zeros_like(h1_ref)
        h2_ref[...] = jnp.zeros_like(h2_ref)
        xnm = xnm_ref[...]
        cu_m = (_dot_nt(xnm, w3_ref[1]) + bc_ref[...]) * (_dot_nt(xnm, w3_ref[0]) + bu_ref[...])
        mtail_ref[...] = cu_m[n_meta - SUBLANES:n_meta, :]

    @pl.when(is_prompt)
    def _():
        first = (m % tiles_per_seq) == 0
        p6 = jnp.where(first, mtail_ref[6:7, :], carry_ref[6:7, :])
        p7 = jnp.where(first, mtail_ref[7:8, :], carry_ref[7:8, :])
        for k, ks in slabs:
            h2_ref[k, 0:1, :] = p6[:, ks]
            h2_ref[k, 1:2, :] = p7[:, ks]
            h1_ref[k, 0:1, :] = p7[:, ks]

    @pl.when(jnp.logical_not(is_prompt))
    def _():
        for k, ks in slabs:
            h2_ref[k, seq_row(0), :] = s0_ref[:, ks]
            h2_ref[k, seq_row(1), :] = s1_ref[:, ks]
            h1_ref[k, seq_row(0), :] = s1_ref[:, ks]


def _conv_main(xn, is_prompt, b_refs, wconv_ref, yb_ref, cu6_ref, cu7_ref, scratch):
    bu_ref, bc_ref, bb_ref = b_refs
    w3_ref, cu_ref, h1_ref, h2_ref, carry_ref, _ = scratch
    tm, cq = yb_ref.shape
    n_seq = tm // SUBLANES
    slabs = [(k, slice(k * LANES, (k + 1) * LANES)) for k in range(cq // LANES)]
    seq_row = lambda r: pl.ds(r, n_seq, stride=SUBLANES)
    proj = lambda k, b_ref: jnp.concatenate([_dot_nt(x, w3_ref[k]) for x in xn], axis=0) + b_ref[...]
    zu, zc, zb = proj(0, bu_ref), proj(1, bc_ref), proj(2, bb_ref)
    cu = zc * zu
    pos = lax.broadcasted_iota(jnp.int32, (tm, cq), 0) & jnp.where(is_prompt, tm - 1, SUBLANES - 1)
    h1 = jnp.concatenate([h1_ref[k] for k, _ in slabs], axis=1)
    h2 = jnp.concatenate([h2_ref[k] for k, _ in slabs], axis=1)
    x1 = jnp.where(pos >= 1, pltpu.roll(cu, 1, 0), h1)
    x2 = jnp.where(pos >= 2, pltpu.roll(cu, 2, 0), h2)
    w0, w1, w2 = wconv_ref[0:1, :], wconv_ref[1:2, :], wconv_ref[2:3, :]
    yb_ref[...] = (zb * ((w0 * x2 + w1 * x1) + w2 * cu)).astype(BF16)

    carry_ref[...] = cu[tm - SUBLANES:tm, :]
    for k, ks in slabs:
        cu_ref[k] = cu[:, ks]
        cu6_ref[:, ks] = cu_ref[k, seq_row(6), :]
        cu7_ref[:, ks] = cu_ref[k, seq_row(7), :]


def _conv_kernel(n_prompt_tiles, tiles_per_seq, n_meta, n_cast, xn_ref, xnm_ref, wu_ref, wc_ref, wb_ref,
                 bu_ref, bc_ref, bb_ref, wconv_ref, s0_ref, s1_ref, *refs):
    cast_in, (yb_ref, cu6_ref, cu7_ref) = refs[:n_cast], refs[n_cast:n_cast + 3]
    cast_out, scratch = refs[n_cast + 3:2 * n_cast + 3], refs[2 * n_cast + 3:]
    m = pl.program_id(1)
    is_prompt = m < n_prompt_tiles
    b_refs = (bu_ref, bc_ref, bb_ref)
    _conv_prologue(m, is_prompt, tiles_per_seq, n_meta, xnm_ref, (wu_ref, wc_ref, wb_ref), b_refs,
                   s0_ref, s1_ref, scratch)
    _cast_blocks(tuple(zip(cast_in, cast_out)))
    _conv_main([xn_ref[...]], is_prompt, b_refs, wconv_ref, yb_ref, cu6_ref, cu7_ref, scratch)


def _rms_conv_kernel(n_prompt_tiles, tiles_per_seq, dec_seq, xp_ref, xs_ref, xm_ref, g_ref, wg_ref, bg_ref,
                     wu_ref, wc_ref, wb_ref, bu_ref, bc_ref, bb_ref, wconv_ref, s0_ref, s1_ref,
                     xn_ref, gch_ref, grh_ref, xnm_ref, gchm_ref, grhm_ref, yb_ref, cu6_ref, cu7_ref, *scratch):
    m = pl.program_id(0)
    is_prompt = m < n_prompt_tiles
    n_meta = xm_ref.shape[0]
    b_refs = (bu_ref, bc_ref, bb_ref)

    def make_xnm():
        xm = jnp.concatenate([xm_ref[...], jnp.zeros((CHUNK - n_meta, D_MODEL), F32)], axis=0)
        xnm = _rmsnorm(xm, g_ref[...]).astype(BF16)
        xnm_ref[...] = xnm
        _gate_prep(xnm, wg_ref, bg_ref, CHUNK, n_meta, gchm_ref, grhm_ref)

    _conv_prologue(m, is_prompt, tiles_per_seq, n_meta, xnm_ref, (wu_ref, wc_ref, wb_ref), b_refs,
                   s0_ref, s1_ref, scratch, make_xnm)
    tm = xn_ref.shape[0]
    half = tm // 2
    xn_blocks = []
    for r in (slice(0, half), slice(half, tm)):
        x = jnp.where(is_prompt, xp_ref[r, :], xs_ref[r, :])
        xn_blocks.append(_rmsnorm(x, g_ref[...]).astype(BF16))
        xn_ref[r, :] = xn_blocks[-1]
    _gate_prep(jnp.concatenate(xn_blocks, axis=0), wg_ref, bg_ref, jnp.where(is_prompt, CHUNK, dec_seq), tm,
               gch_ref, grh_ref)
    _conv_main(xn_blocks, is_prompt, b_refs, wconv_ref, yb_ref, cu6_ref, cu7_ref, scratch)


def _conv_specs(c0, npt, tm, cm):
    n_seq = tm // SUBLANES
    n_slab = CQ // LANES
    chan = lambda *g: c0 + cm(*g)[0]
    tile = lambda *g: cm(*g)[1]
    w_spec = lambda off: pl.BlockSpec((pl.Element(CQ), pl.Element(D_MODEL)),
                                      lambda *g: (pl.multiple_of(off + chan(*g) * CQ, SUBLANES), 0))
    b_spec = lambda off: pl.BlockSpec((1, CQ), lambda *g: (0, (off - N_GATES) // CQ + chan(*g)))
    s_spec = lambda tok: pl.BlockSpec(
        (n_seq, CQ), lambda *g: (jnp.maximum(tile(*g) - npt, 0), tok * (C_WIDTH // CQ) + chan(*g)))
    in_specs = [w_spec(OFF_U), w_spec(OFF_C), w_spec(OFF_B), b_spec(OFF_U), b_spec(OFF_C), b_spec(OFF_B),
                pl.BlockSpec((CONV_W, CQ), lambda *g: (0, chan(*g))), s_spec(0), s_spec(1)]
    out_specs = [pl.BlockSpec((tm, CQ), lambda *g: (tile(*g), cm(*g)[0])),
                 pl.BlockSpec((n_seq, CQ), lambda *g: (tile(*g), cm(*g)[0])),
                 pl.BlockSpec((n_seq, CQ), lambda *g: (tile(*g), cm(*g)[0]))]
    scratch = [pltpu.VMEM((3, CQ, D_MODEL), BF16),
               pltpu.VMEM((n_slab, tm, LANES), F32),
               pltpu.VMEM((n_slab, tm, LANES), F32),
               pltpu.VMEM((n_slab, tm, LANES), F32),
               pltpu.VMEM((SUBLANES, CQ), F32),
               pltpu.VMEM((SUBLANES, CQ), F32)]
    return in_specs, out_specs, scratch


def _conv_out_shapes(t, n_blocks):
    return [jax.ShapeDtypeStruct((t, n_blocks * CQ), BF16),
            jax.ShapeDtypeStruct((t // SUBLANES, n_blocks * CQ), F32),
            jax.ShapeDtypeStruct((t // SUBLANES, n_blocks * CQ), F32)]


def _rms_conv_call(xp, xs, x_meta, g, wt, bg, b1, wconv, sconv, dec_seq, seq_len, tm):
    tp, ts = xp.shape[0], xs.shape[0]
    t = tp + ts
    assert tm & (tm - 1) == 0 and seq_len % tm == 0 and x_meta.shape[0] >= SUBLANES
    npt, nst = tp // tm, ts // tm
    conv_in, conv_out, scratch = _conv_specs(0, npt, tm, lambda m: (0, m))
    return pl.pallas_call(
        functools.partial(_rms_conv_kernel, npt, seq_len // tm, dec_seq),
        grid=(npt + nst,),
        in_specs=[
            pl.BlockSpec((tm, D_MODEL), lambda m: (jnp.minimum(m, npt - 1), 0)),
            pl.BlockSpec((tm, D_MODEL), lambda m: (jnp.maximum(m - npt, 0), 0)),
            pl.BlockSpec(x_meta.shape, lambda m: (0, 0)),
            pl.BlockSpec((1, D_MODEL), lambda m: (0, 0)),
            pl.BlockSpec((N_GATES, D_MODEL), lambda m: (OFF_GATES // N_GATES, 0)),
            pl.BlockSpec((N_GATES, 1), lambda m: (0, 0)),
        ] + conv_in,
        out_specs=[
            pl.BlockSpec((tm, D_MODEL), lambda m: (m, 0)),
            pl.BlockSpec((M_HEADS, 2, tm, LANES), lambda m: (0, 0, m, 0)),
            pl.BlockSpec((M_HEADS, SUBLANES, tm), lambda m: (0, 0, m)),
            pl.BlockSpec((CHUNK, D_MODEL), lambda m: (0, 0)),
            pl.BlockSpec((M_HEADS, 2, CHUNK, LANES), lambda m: (0, 0, 0, 0)),
            pl.BlockSpec((M_HEADS, SUBLANES, CHUNK), lambda m: (0, 0, 0)),
        ] + conv_out,
        out_shape=[
            jax.ShapeDtypeStruct((t, D_MODEL), BF16),
            jax.ShapeDtypeStruct((M_HEADS, 2, t, LANES), F32),
            jax.ShapeDtypeStruct((M_HEADS, SUBLANES, t), F32),
            jax.ShapeDtypeStruct((CHUNK, D_MODEL), BF16),
            jax.ShapeDtypeStruct((M_HEADS, 2, CHUNK, LANES), F32),
            jax.ShapeDtypeStruct((M_HEADS, SUBLANES, CHUNK), F32),
        ] + _conv_out_shapes(t, 1),
        scratch_shapes=scratch,
        compiler_params=_params(("arbitrary",)),
        name="rms_conv",
    )(xp, xs, x_meta, g, wt, bg, wt, wt, wt, b1, b1, b1, wconv, sconv, sconv)


def _conv_call(xn, xnm, n_meta, wt, b1, wconv, sconv, cast_ws, n_prompt_tokens, seq_len, tm):
    t = xn.shape[0]
    assert tm & (tm - 1) == 0 and seq_len % tm == 0 and n_meta >= SUBLANES
    npt = n_prompt_tokens // tm
    n_m = t // tm
    n_blocks = C_WIDTH // CQ - 1
    n_cast = 16
    assert n_cast <= n_blocks * n_m
    casts = [_cast_specs(w, n_cast, lambda c, m: c * n_m + m) for w in cast_ws]
    conv_in, conv_out, scratch = _conv_specs(1, npt, tm, lambda c, m: (c, m))
    return pl.pallas_call(
        functools.partial(_conv_kernel, npt, seq_len // tm, n_meta, len(cast_ws)),
        grid=(n_blocks, n_m),
        in_specs=[
            pl.BlockSpec((tm, D_MODEL), lambda c, m: (m, 0)),
            pl.BlockSpec(xnm.shape, lambda c, m: (0, 0)),
        ] + conv_in + [cs[0] for cs in casts],
        out_specs=conv_out + [cs[1] for cs in casts],
        out_shape=_conv_out_shapes(t, n_blocks) + [cs[2] for cs in casts],
        scratch_shapes=scratch,
        compiler_params=_params(("arbitrary", "arbitrary")),
        name="inproj_conv",
    )(xn, xnm, wt, wt, wt, b1, b1, b1, wconv, sconv, sconv, *cast_ws)


def _prompt_head(q, kt, v, gr, gc, ca, m_st):
    l = q.shape[0]
    assert l == LANES
    ig_row, b_row, r_row = gr[G_IG:G_IG + 1, :], gr[G_B:G_B + 1, :], gr[G_R:G_R + 1, :]
    b_col = gc[0]
    v_aug = jnp.concatenate([v, jnp.ones((l, LANES), BF16)], axis=1)
    t_idx = lax.broadcasted_iota(jnp.int32, (l, l), 0)
    s_idx = lax.broadcasted_iota(jnp.int32, (l, l), 1)
    logw = jnp.where(s_idx <= t_idx, b_col - b_row + ig_row, NEG)
    inter = b_col + m_st
    m_t = jnp.maximum(inter, gc[1])
    w_inter = jnp.exp(inter - m_t)
    s = _dot(q, kt) * jnp.exp(logw - m_t)
    nd = (jnp.concatenate([w_inter] * (AUG // LANES), axis=1) * _dot(q, ca.astype(BF16))
          + _dot(s.astype(BF16), v_aug))
    rcp = 1.0 / jnp.maximum(jnp.abs(nd[:, M_DV:AUG]), jnp.exp(-m_t))
    h = nd[:, 0:M_DV] * jnp.concatenate([rcp] * (M_DV // LANES), axis=1)
    m_new = m_t[l - 1:l, 0:1]
    decay = jnp.exp(b_col[l - 1:l, 0:1] + m_st - m_new)
    kwt = (kt.astype(F32) * jnp.exp(r_row + ig_row - m_new)).astype(BF16)
    return h, decay * ca + _dot(kwt, v_aug), m_new


def _mlstm_prompt_kernel(batch, qm_ref, ktm_ref, vm_ref, gcm_ref, grm_ref, *refs):
    ins, (h_ref, c_ref, n_ref, m_ref, ca_ref) = refs[:5 * batch], refs[5 * batch:]
    heads = [(hd, slice(hd * M_DK, (hd + 1) * M_DK)) for hd in range(M_HEADS)]

    @pl.when(pl.program_id(0) == 0)
    def _():
        for hd, sl in heads:
            _, ca_new, m_new = _prompt_head(qm_ref[:, sl], ktm_ref[sl, :], vm_ref[:, sl], grm_ref[hd],
                                            gcm_ref[hd], jnp.zeros((M_DK, AUG), F32), jnp.zeros((1, 1), F32))
            for b in range(batch):
                ca_ref[b, hd] = ca_new
                m_ref[b, hd:hd + 1, :] = jnp.broadcast_to(m_new, (1, LANES))

    for k in range(h_ref.shape[1] // CHUNK):
        tok = slice(k * CHUNK, (k + 1) * CHUNK)
        for b in range(batch):
            q_ref, kt_ref, v_ref, gc_ref, gr_ref = ins[5 * b:5 * b + 5]
            for hd, sl in heads:
                h, ca_new, m_new = _prompt_head(q_ref[tok, sl], kt_ref[sl, tok], v_ref[tok, sl],
                                                gr_ref[hd, :, tok], gc_ref[hd, :, tok, :],
                                                ca_ref[b, hd], m_ref[b, hd:hd + 1, 0:1])
                ca_ref[b, hd] = ca_new
                m_ref[b, hd:hd + 1, :] = jnp.broadcast_to(m_new, (1, LANES))
                h_ref[b, tok, sl] = h.astype(h_ref.dtype)

    @pl.when(pl.program_id(0) == pl.num_programs(0) - 1)
    def _():
        for b in range(batch):
            for hd, _ in heads:
                ca = ca_ref[b, hd]
                c_ref[b, hd] = ca[:, 0:M_DV]
                n_t = ca[:, M_DV:AUG]
                n_ref[b, hd:hd + 1, :] = jnp.concatenate(
                    [n_t[k * LANES:(k + 1) * LANES, :].T[0:1, :] for k in range(M_DK // LANES)], axis=1)


def _mlstm_prompt_call(zm, ktm, gcm, grm, z, kt, gch, grh, batch, seq, tok):
    assert tok % CHUNK == 0 and seq % tok == 0
    nc = seq // tok
    per_prompt_specs, per_prompt_args = [], []
    for b in range(batch):
        row = functools.partial(lambda b, c: b * nc + c, b)
        per_prompt_specs += [
            pl.BlockSpec((tok, M_WIDTH), lambda c, row=row: (row(c), Z_Q)),
            pl.BlockSpec((M_WIDTH, tok), lambda c, row=row: (0, row(c))),
            pl.BlockSpec((tok, M_WIDTH), lambda c, row=row: (row(c), Z_V)),
            pl.BlockSpec((M_HEADS, 2, tok, LANES), lambda c, row=row: (0, 0, row(c), 0)),
            pl.BlockSpec((M_HEADS, SUBLANES, tok), lambda c, row=row: (0, 0, row(c))),
        ]
        per_prompt_args += [z, kt, z, gch, grh]
    return pl.pallas_call(
        functools.partial(_mlstm_prompt_kernel, batch),
        grid=(nc,),
        in_specs=[
            pl.BlockSpec((CHUNK, M_WIDTH), lambda c: (0, 0)),
            pl.BlockSpec((M_WIDTH, CHUNK), lambda c: (0, 0)),
            pl.BlockSpec((CHUNK, M_WIDTH), lambda c: (0, 1)),
            pl.BlockSpec((M_HEADS, 2, CHUNK, LANES), lambda c: (0, 0, 0, 0)),
            pl.BlockSpec((M_HEADS, SUBLANES, CHUNK), lambda c: (0, 0, 0)),
        ] + per_prompt_specs,
        out_specs=[
            pl.BlockSpec((batch, tok, M_WIDTH), lambda c: (0, c, 0)),
            pl.BlockSpec((batch, M_HEADS, M_DK, M_DV), lambda c: (0, 0, 0, 0)),
            pl.BlockSpec((batch, M_HEADS, M_DK), lambda c: (0, 0, 0)),
            pl.BlockSpec((batch, M_HEADS, LANES), lambda c: (0, 0, 0)),
        ],
        out_shape=[
            jax.ShapeDtypeStruct((batch, seq, M_WIDTH), BF16),
            jax.ShapeDtypeStruct((batch, M_HEADS, M_DK, M_DV), F32),
            jax.ShapeDtypeStruct((batch, M_HEADS, M_DK), F32),
            jax.ShapeDtypeStruct((batch, M_HEADS, LANES), F32),
        ],
        scratch_shapes=[pltpu.VMEM((batch, M_HEADS, M_DK, AUG), F32)],
        compiler_params=_params(("arbitrary",)),
        name="mlstm_prompt",
    )(zm, ktm, zm, gcm, grm, *per_prompt_args)


def _group_max(x, size):
    n = x.shape[-1]
    lane = lax.broadcasted_iota(jnp.int32, x.shape, x.ndim - 1)
    k = 1
    while k < size:
        partner = jnp.where((lane & k) == 0, pltpu.roll(x, n - k, x.ndim - 1), pltpu.roll(x, k, x.ndim - 1))
        x = jnp.maximum(x, partner)
        k *= 2
    return x


def _mlstm_sample_kernel(dec_seq, q_ref, kt_ref, v_ref, gc_ref, gr_ref, mcol_ref, mrow_ref, cin_ref, nin_ref,
                         h_ref, c_ref, n_ref, m_ref):
    l = q_ref.shape[0]
    n_seq = l // dec_seq
    shift = dec_seq.bit_length() - 1
    q, kt, v = q_ref[...], kt_ref[...], v_ref[...]
    assert l == LANES
    gr, gc = gr_ref[0], gc_ref[0]
    ig_row, b_row, r_row = gr[G_IG:G_IG + 1, :], gr[G_B:G_B + 1, :], gr[G_R:G_R + 1, :]
    b_col = gc[0]
    m_col, m_row = mcol_ref[0], mrow_ref[0]

    t_idx = lax.broadcasted_iota(jnp.int32, (l, l), 0)
    s_idx = lax.broadcasted_iota(jnp.int32, (l, l), 1)
    t_seq = t_idx >> shift
    logw = jnp.where((t_seq == (s_idx >> shift)) & (s_idx <= t_idx), b_col - b_row + ig_row, NEG)
    inter = b_col + m_col
    m_t = jnp.maximum(inter, gc[1])
    w_inter = jnp.exp(inter - m_t)
    s = _dot(q, kt) * jnp.exp(logw - m_t)

    e_row = r_row + ig_row
    b_last = b_row + r_row
    m_new = jnp.maximum(b_last + m_row, _group_max(e_row, dec_seq))
    decay = jnp.exp(b_last + m_row - m_new)
    kwt = (kt.astype(F32) * jnp.exp(e_row - m_new)).astype(BF16)

    seq_lane = s_idx == t_seq
    seq_onehot = jnp.where(seq_lane, 1.0, 0.0).astype(BF16)
    pa = _dot(s.astype(BF16), jnp.concatenate([v, seq_onehot], axis=1))

    n_rows = jnp.concatenate([nin_ref[...], jnp.zeros((LANES - n_seq, M_DK), F32)], axis=0)
    n_t = jnp.concatenate([n_rows[:, 0:LANES].T, n_rows[:, LANES:2 * LANES].T], axis=0)
    lane_k = lax.broadcasted_iota(jnp.int32, (M_DK, LANES), 1)
    lane_1 = lax.broadcasted_iota(jnp.int32, (1, LANES), 1)
    qf = q.astype(F32)
    qca_rows = []
    decay_seq = jnp.zeros((1, LANES), F32)
    for i in range(n_seq):
        decay_i = decay[:, i * dec_seq:i * dec_seq + 1]
        decay_seq = jnp.where(lane_1 == i, decay_i, decay_seq)
        c_i = cin_ref[i, 0]
        ca = jnp.concatenate([c_i, jnp.where(lane_k == i, n_t, 0.0)], axis=1)
        rows = slice(i * dec_seq, (i + 1) * dec_seq)
        qca_rows.append(_dot(qf[rows, :].astype(BF16), ca.astype(BF16)))
        kw_i = jnp.where((lane_k >> shift) == i, kwt, jnp.zeros_like(kwt))
        c_ref[i, 0] = decay_i * c_i + _dot(kw_i, v)
    n_t_new = decay_seq * n_t + _dot(kwt, seq_onehot)
    nd = jnp.concatenate([w_inter] * (AUG // LANES), axis=1) * jnp.concatenate(qca_rows, axis=0) + pa
    den = jnp.sum(jnp.where(seq_lane, nd[:, M_DV:AUG], 0.0), axis=-1, keepdims=True)
    rcp = 1.0 / jnp.maximum(jnp.abs(den), jnp.exp(-m_t[:, 0:1]))
    h_ref[...] = (nd[:, 0:M_DV] * rcp).astype(h_ref.dtype)
    n_ref[...] = jnp.concatenate([n_t_new[0:LANES, :].T, n_t_new[LANES:2 * LANES, :].T], axis=1)[0:n_seq, :]
    m_ref[0] = jnp.broadcast_to(m_new, (SUBLANES, l))


def _gates_sample_kernel(dec_seq, xn_ref, wt_ref, b_ref, *refs):
    sample_in, z_ref, sample_out, w_ref = refs[:9], refs[9], refs[10:14], refs[14]

    @pl.when(pl.program_id(1) == 0)
    def _():
        w_ref[...] = wt_ref[...].astype(BF16)

    _mlstm_sample_kernel(dec_seq, *sample_in, *sample_out)
    z_ref[...] = _sigmoid(_dot_nt(xn_ref[...], w_ref[...]) + b_ref[...]).astype(BF16)


def _gates_sample_call(xn, wt, b1, zqv, kt, gch, grh, m_col, m_row, c_in, n_in, row0, dec_seq, tm):
    t = xn.shape[0]
    n_m = t // tm
    nb = c_in.shape[0]
    l = SEQ_PER_STEP * dec_seq
    blk0 = row0 // l
    n_steps = 2 * D_MODEL // TN
    n_guest = (nb // SEQ_PER_STEP) * M_HEADS
    assert n_guest <= n_steps * n_m
    w_off = lambda s: OFF_GA + s * TN
    b_blk = lambda s: (OFF_GA - N_GATES) // TN + s
    z_col = lambda s: s
    guest = lambda s, m: jnp.minimum(s * n_m + m, n_guest - 1)
    gi = lambda s, m: guest(s, m) // M_HEADS
    gh = lambda s, m: guest(s, m) % M_HEADS
    return pl.pallas_call(
        functools.partial(_gates_sample_kernel, dec_seq),
        grid=(n_steps, n_m),
        in_specs=[
            pl.BlockSpec((tm, D_MODEL), lambda s, m: (m, 0)),
            pl.BlockSpec((pl.Element(TN), pl.Element(D_MODEL)),
                         lambda s, m: (pl.multiple_of(w_off(s), SUBLANES), 0)),
            pl.BlockSpec((1, TN), lambda s, m: (0, b_blk(s))),
            pl.BlockSpec((l, M_DK), lambda s, m: (blk0 + gi(s, m), Z_Q * M_HEADS + gh(s, m))),
            pl.BlockSpec((M_DK, l), lambda s, m: (gh(s, m), blk0 + gi(s, m))),
            pl.BlockSpec((l, M_DV), lambda s, m: (blk0 + gi(s, m), Z_V * M_HEADS + gh(s, m))),
            pl.BlockSpec((1, 2, l, LANES), lambda s, m: (gh(s, m), 0, blk0 + gi(s, m), 0)),
            pl.BlockSpec((1, SUBLANES, l), lambda s, m: (gh(s, m), 0, blk0 + gi(s, m))),
            pl.BlockSpec((1, l, LANES), lambda s, m: (gh(s, m), gi(s, m), 0)),
            pl.BlockSpec((1, 1, l), lambda s, m: (gh(s, m), 0, gi(s, m))),
            pl.BlockSpec((SEQ_PER_STEP, 1, M_DK, M_DV), lambda s, m: (gi(s, m), gh(s, m), 0, 0)),
            pl.BlockSpec((SEQ_PER_STEP, M_DK), lambda s, m: (gi(s, m), gh(s, m))),
        ],
        out_specs=[
            pl.BlockSpec((tm, TN), lambda s, m: (m, z_col(s))),
            pl.BlockSpec((l, M_DV), lambda s, m: (gi(s, m), gh(s, m))),
            pl.BlockSpec((SEQ_PER_STEP, 1, M_DK, M_DV), lambda s, m: (gi(s, m), gh(s, m), 0, 0)),
            pl.BlockSpec((SEQ_PER_STEP, M_DK), lambda s, m: (gi(s, m), gh(s, m))),
            pl.BlockSpec((1, SUBLANES, l), lambda s, m: (gh(s, m), 0, gi(s, m))),
        ],
        out_shape=[
            jax.ShapeDtypeStruct((t, ZG_COLS), BF16),
            jax.ShapeDtypeStruct((nb * dec_seq, M_WIDTH), BF16),
            jax.ShapeDtypeStruct(c_in.shape, F32),
            jax.ShapeDtypeStruct(n_in.shape, F32),
            jax.ShapeDtypeStruct((M_HEADS, SUBLANES, nb * dec_seq), F32),
        ],
        scratch_shapes=[pltpu.VMEM((TN, D_MODEL), BF16)],
        compiler_params=_params(("arbitrary", "arbitrary")),
        name="inproj_gates_mlstm_sample",
    )(xn, wt, b1, zqv, kt, zqv, gch, grh, m_col, m_row, c_in, n_in)


def _mix_kernel(h_ref, so_ref, yb0_ref, yb1_ref, ga_ref, gb_ref, x_ref, pa_ref, pb_ref, wo_ref, gffn_ref,
                o_ref, on_ref):
    h_a = (so_ref[...].astype(F32) * h_ref[...].astype(F32)).astype(BF16)
    a = _dot(h_a, pa_ref[...])
    b = _dot(jnp.concatenate([yb0_ref[...], yb1_ref[...]], axis=1), pb_ref[...])
    merged = ga_ref[...].astype(F32) * a + gb_ref[...].astype(F32) * b
    x1 = x_ref[...] + _dot(merged.astype(BF16), wo_ref[...])
    o_ref[...] = x1
    on_ref[...] = _rmsnorm(x1, gffn_ref[...]).astype(BF16)


def _mix_call(h, x, zqvo, zg, yb0, yb1, pa, pb, wo, gffn, row0, tm):
    rows = x.shape[0]
    assert row0 % tm == 0 and rows % tm == 0
    blk0 = row0 // tm
    const = lambda i: (0, 0)
    return pl.pallas_call(
        _mix_kernel,
        grid=(rows // tm,),
        in_specs=[
            pl.BlockSpec((tm, M_WIDTH), lambda i: (i, 0)),
            pl.BlockSpec((tm, M_WIDTH), lambda i: (blk0 + i, Z_SO)),
            pl.BlockSpec((tm, yb0.shape[1]), lambda i: (blk0 + i, 0)),
            pl.BlockSpec((tm, yb1.shape[1]), lambda i: (blk0 + i, 0)),
            pl.BlockSpec((tm, D_MODEL), lambda i: (blk0 + i, Z_GA // 2)),
            pl.BlockSpec((tm, D_MODEL), lambda i: (blk0 + i, Z_GB // 2)),
            pl.BlockSpec((tm, D_MODEL), lambda i: (i, 0)),
            pl.BlockSpec((M_WIDTH, D_MODEL), const, pipeline_mode=pl.Buffered(1)),
            pl.BlockSpec((C_WIDTH, D_MODEL), const, pipeline_mode=pl.Buffered(1)),
            pl.BlockSpec((D_MODEL, D_MODEL), const, pipeline_mode=pl.Buffered(1)),
            pl.BlockSpec((1, D_MODEL), const),
        ],
        out_specs=[pl.BlockSpec((tm, D_MODEL), lambda i: (i, 0)), pl.BlockSpec((tm, D_MODEL), lambda i: (i, 0))],
        out_shape=[jax.ShapeDtypeStruct((rows, D_MODEL), F32), jax.ShapeDtypeStruct((rows, D_MODEL), BF16)],
        compiler_params=_params(("arbitrary",)),
        name="mix",
    )(h, zqvo, yb0, yb1, zg, zg, x, pa, pb, wo, gffn)


def _ffn_kernel(x1_ref, xn_ref, wup_ref, wdn_ref, gfin_ref, y_ref):
    f = pl.program_id(1)
    last = pl.num_programs(1) - 1

    def mlp_part():
        u = jnp.maximum(_dot(xn_ref[...], wup_ref[...]), 0.0)
        return _dot((u * u).astype(BF16), wdn_ref[...])

    @pl.when(f == 0)
    def _():
        y_ref[...] = x1_ref[...] + mlp_part()

    @pl.when((f > 0) & (f < last))
    def _():
        y_ref[...] += mlp_part()

    @pl.when(f == last)
    def _():
        y_ref[...] = _rmsnorm(y_ref[...] + mlp_part(), gfin_ref[...])


def _ffn_call(x1, xn, wup, wdn, gfin, tm, tf):
    rows = x1.shape[0]
    assert D_FF // tf >= 2 and rows % tm == 0
    return pl.pallas_call(
        _ffn_kernel,
        grid=(rows // tm, D_FF // tf),
        in_specs=[
            pl.BlockSpec((tm, D_MODEL), lambda i, f: (i, 0)),
            pl.BlockSpec((tm, D_MODEL), lambda i, f: (i, 0)),
            pl.BlockSpec((D_MODEL, tf), lambda i, f: (0, f)),
            pl.BlockSpec((tf, D_MODEL), lambda i, f: (f, 0)),
            pl.BlockSpec((1, D_MODEL), lambda i, f: (0, 0)),
        ],
        out_specs=pl.BlockSpec((tm, D_MODEL), lambda i, f: (i, 0)),
        out_shape=jax.ShapeDtypeStruct((rows, D_MODEL), F32),
        compiler_params=_params(("arbitrary", "arbitrary")),
        name="ffn",
    )(x1, xn, wup, wdn, gfin)


def kernel(x_prompt, x_sample, state_mlstm_C, state_mlstm_n, state_mlstm_m, state_conv, meta_tokens,
           g_mix, w_in, b_in, w_conv, p_a, p_b, w_o, g_ffn, w_up, w_down, g_final):
    assert w_in.shape[0] == 1, "single-layer trunk"
    batch, seq, _ = x_prompt.shape
    dec_batch, dec_seq, _ = x_sample.shape
    assert dec_seq == SUBLANES and seq % 1024 == 0 and dec_batch % SEQ_PER_STEP == 0
    n_p, n_s = batch * seq, dec_batch * dec_seq

    wt = w_in[0].T
    bias = b_in[0]
    b1 = jnp.concatenate([bias[:OFF_GATES], bias[OFF_O:]])
    bg = bias[OFF_GATES:OFF_O][:, None]
    gmix = g_mix[0][None, :]

    xp = x_prompt.reshape(n_p, D_MODEL)
    xs = x_sample.reshape(n_s, D_MODEL)
    n_meta = meta_tokens.shape[0]

    sconv = state_conv[0].reshape(dec_batch, (CONV_W - 1) * C_WIDTH)
    xn, gch, grh, xnm, gcm, grm, yb0, cu6a, cu7a = _rms_conv_call(
        xp, xs, meta_tokens.astype(F32), gmix, wt, bg, b1[None, :], w_conv[0], sconv, dec_seq, seq, 512)
    zqv, kt, zm, ktm, w_up16, w_down16 = _qkvo_call(xn, xnm, wt, b1[None, :], (w_up[0], w_down[0]), 1024)
    m_tok = jnp.repeat(state_mlstm_m[0], dec_seq, axis=0).T
    zg, hs, c_s, n_sm, m_s = _gates_sample_call(
        xn, wt, b1[None, :], zqv, kt, gch, grh,
        jnp.broadcast_to(m_tok[:, :, None], m_tok.shape + (LANES,)), m_tok[:, None, :],
        state_mlstm_C[0], state_mlstm_n[0].reshape(dec_batch, M_HEADS * M_DK), n_p, dec_seq, 1152)
    yb1, cu6b, cu7b, p_a16, p_b16, w_o16 = _conv_call(
        xn, xnm, n_meta, wt, b1[None, :], w_conv[0], sconv, (p_a[0], p_b[0], w_o[0]), n_p, seq, 1024)

    hp, c_p, n_pr, m_p = _mlstm_prompt_call(zm, ktm, gcm, grm, zqv, kt, gch, grh, batch, seq, 2 * CHUNK)

    y_p, y_s = [
        _ffn_call(*_mix_call(h, x, zqv, zg, yb0, yb1, p_a16, p_b16, w_o16, g_ffn[0][None, :], row0, 512),
                  w_up16, w_down16, g_final[None, :], 512, 2048)
        for h, x, row0 in ((hp.reshape(n_p, M_WIDTH), xp, 0), (hs, xs, n_p))]

    per_seq = seq // SUBLANES
    first_s = n_p // SUBLANES
    pick = lambda rows: jnp.stack([jnp.concatenate([cu6a[rows], cu6b[rows]], axis=1),
                                   jnp.concatenate([cu7a[rows], cu7b[rows]], axis=1)], axis=1)
    cv_p = pick(slice(per_seq - 1, batch * per_seq, per_seq))
    cv_s = pick(slice(first_s, None))
    m_s = m_s[:, 0, dec_seq - 1::dec_seq].T
    return (y_p.reshape(batch, seq, D_MODEL), y_s.reshape(dec_batch, dec_seq, D_MODEL),
            c_p[None], n_pr[None], m_p[None, :, :, 0], cv_p[None],
            c_s[None], n_sm.reshape(dec_batch, M_HEADS, M_DK)[None], m_s[None], cv_s[None])
```

```python
import functools

import jax
import jax.numpy as jnp
from jax import lax
from jax.experimental import pallas as pl
from jax.experimental.pallas import tpu as pltpu

F32 = jnp.float32
BF16 = jnp.bfloat16

D_MODEL = 2048
CHUNK = 128
M_HEADS = 4
M_DK = 256
M_DV = 256
M_WIDTH = M_HEADS * M_DV
C_WIDTH = 1024
CONV_W = 3
D_FF = 4 * D_MODEL
EPS = 1e-6
N_GATES = 2 * M_HEADS
OFF_Q = 0
OFF_K = OFF_Q + M_HEADS * M_DK
OFF_V = OFF_K + M_HEADS * M_DK
OFF_GATES = OFF_V + M_WIDTH
OFF_O = OFF_GATES + N_GATES
OFF_U = OFF_O + M_WIDTH
OFF_C = OFF_U + C_WIDTH
OFF_B = OFF_C + C_WIDTH
OFF_GA = OFF_B + C_WIDTH

LANES = 128
SUBLANES = 8
VMEM_LIMIT_BYTES = 60000 * 1024

TN = 1024
Z_Q, Z_V, Z_SO = 0, 1, 2
Z_GA, Z_GB = 0, 2
ZG_COLS = 4 * TN
CQ = 256
AUG = M_DV + LANES
SEQ_PER_STEP = CHUNK // SUBLANES
G_IG, G_B, G_R = 0, 1, 2

NEG = -1e30
F32_LOWEST = -3e38
NT_DIMS = (((1,), (1,)), ((), ()))


def _params(semantics):
    return pltpu.CompilerParams(dimension_semantics=semantics, vmem_limit_bytes=VMEM_LIMIT_BYTES)


def _rmsnorm(x, g):
    y = x * lax.rsqrt(jnp.mean(x * x, axis=-1, keepdims=True) + EPS)
    return y * g


def _log_sigmoid(x):
    return jnp.minimum(x, 0.0) - jnp.log1p(jnp.exp(-jnp.abs(x)))


def _dot(a, b):
    return jnp.dot(a, b, preferred_element_type=F32)


def _dot_nt(a, b):
    return lax.dot_general(a, b, NT_DIMS, preferred_element_type=F32)


def _gate_prep(xn, wg_ref, bg_ref, blk, n_valid, gch_ref, grh_ref):
    tm = xn.shape[0]
    wg = wg_ref[...].astype(BF16)
    wg = jnp.concatenate([wg, jnp.zeros((LANES - N_GATES, wg.shape[1]), BF16)], axis=0)
    g = _dot_nt(wg, xn)[0:SUBLANES, :] + bg_ref[...]
    row = lax.broadcasted_iota(jnp.int32, (SUBLANES, tm), 0)
    lane = lax.broadcasted_iota(jnp.int32, (SUBLANES, tm), 1)
    a = jnp.where(row < M_HEADS, g, _log_sigmoid(g))
    if n_valid < tm:
        a = jnp.where(lane < n_valid, a, jnp.where(row < M_HEADS, NEG, 0.0))
    pos = lane & (blk - 1)
    n_steps_blk = blk if isinstance(blk, int) else LANES

    def scan(x, op, fill, reverse=False):
        shift = 1
        while shift < n_steps_blk:
            if reverse:
                x = op(x, jnp.where(pos < blk - shift, pltpu.roll(x, tm - shift, 1), fill))
            else:
                x = op(x, jnp.where(pos >= shift, pltpu.roll(x, shift, 1), fill))
            shift *= 2
        return x

    pre = scan(a, jnp.add, 0.0)
    suf = scan(a, jnp.add, 0.0, reverse=True) - a
    b_up = pltpu.roll(pre, M_HEADS, 0)
    m_in = b_up + scan(a - b_up, jnp.maximum, F32_LOWEST)
    for h in range(M_HEADS):
        grh_ref[h] = jnp.where(
            row == G_IG, pltpu.roll(a, (G_IG - h) % SUBLANES, 0),
            jnp.where(row == G_B, pltpu.roll(pre, (G_B - M_HEADS - h) % SUBLANES, 0),
                      jnp.where(row == G_R, pltpu.roll(suf, (G_R - M_HEADS - h) % SUBLANES, 0), 0.0)))
        for c in range(tm // LANES):
            cs = slice(c * LANES, (c + 1) * LANES)
            gch_ref[h, 0, cs, :] = jnp.broadcast_to(b_up[h:h + 1, cs], (LANES, LANES)).T
            gch_ref[h, 1, cs, :] = jnp.broadcast_to(m_in[h:h + 1, cs], (LANES, LANES)).T


S_Q, S_K, S_V, S_O = 0, 1, 2, 3


def _cast_blocks(pairs):
    for src_ref, dst_ref in pairs:
        dst_ref[...] = src_ref[...].astype(BF16)


def _cast_specs(w, n_blocks, step):
    rows, cols = w.shape
    blk = lambda *g: (jnp.minimum(step(*g), n_blocks - 1), 0)
    spec = pl.BlockSpec((rows // n_blocks, cols), blk)
    return spec, spec, jax.ShapeDtypeStruct(w.shape, BF16)


def _sigmoid(z):
    return 0.5 * jnp.tanh(0.5 * z) + 0.5


def _qkvo_kernel(n_cast, xn_ref, xnm_ref, wt_ref, b_ref, *refs):
    cast_in, (z_ref, kt_ref, zm_ref, ktm_ref) = refs[:n_cast], refs[n_cast:n_cast + 4]
    cast_out, (w_ref, bcol_ref) = refs[n_cast + 4:2 * n_cast + 4], refs[2 * n_cast + 4:]
    s = pl.program_id(0)
    casts = tuple(zip(cast_in, cast_out))
    k_scale = M_DK ** -0.5
    lane_tiles = lambda x, n: jnp.concatenate([x] * (n // LANES), axis=1)

    @pl.when(pl.program_id(1) == 0)
    def _():
        w_ref[...] = wt_ref[...].astype(BF16)

        @pl.when((s == S_Q) | (s == S_V))
        def _():
            zm_ref[...] = (_dot_nt(xnm_ref[...], w_ref[...]) + b_ref[...]).astype(BF16)

        @pl.when(s == S_K)
        def _():
            for c in range(TN // LANES):
                cs = slice(c * LANES, (c + 1) * LANES)
                bcol_ref[cs, :] = jnp.broadcast_to(b_ref[:, cs], (LANES, LANES)).T
            ktm = _dot_nt(w_ref[...], xnm_ref[...]) + lane_tiles(bcol_ref[...], xnm_ref.shape[0])
            ktm_ref[...] = (ktm * k_scale).astype(BF16)

    def z():
        return _dot_nt(xn_ref[...], w_ref[...]) + b_ref[...]

    @pl.when((s == S_Q) | (s == S_V))
    def _():
        _cast_blocks(casts)
        z_ref[...] = z().astype(BF16)

    @pl.when(s == S_K)
    def _():
        _cast_blocks(casts)
        kt = _dot_nt(w_ref[...], xn_ref[...]) + lane_tiles(bcol_ref[...], xn_ref.shape[0])
        kt_ref[...] = (kt * k_scale).astype(BF16)

    @pl.when(s == S_O)
    def _():
        _cast_blocks(casts)
        z_ref[...] = _sigmoid(z()).astype(BF16)


def _qkvo_call(xn, xnm, wt, b1, cast_ws, tm):
    t = xn.shape[0]
    assert t % tm == 0
    n_m = t // tm
    rows_m = xnm.shape[0]
    n_cast = 32
    assert n_cast <= 4 * n_m
    casts = [_cast_specs(w, n_cast, lambda s, m: s * n_m + m) for w in cast_ws]
    w_off = lambda s: jnp.where(s == S_O, OFF_O, s * TN)
    z_col = lambda s: jnp.where(s <= S_K, Z_Q, s - 1)
    z_row = lambda s, m: jnp.where(s == S_K, n_m - 1, m)
    kt_blk = lambda s, m: jnp.where(s == S_K, m, jnp.where(s < S_K, 0, n_m - 1))
    return pl.pallas_call(
        functools.partial(_qkvo_kernel, len(cast_ws)),
        grid=(4, n_m),
        in_specs=[
            pl.BlockSpec((tm, D_MODEL), lambda s, m: (m, 0)),
            pl.BlockSpec((rows_m, D_MODEL), lambda s, m: (0, 0)),
            pl.BlockSpec((pl.Element(TN), pl.Element(D_MODEL)),
                         lambda s, m: (pl.multiple_of(w_off(s), SUBLANES), 0)),
            pl.BlockSpec((1, TN), lambda s, m: (0, s)),
        ] + [cs[0] for cs in casts],
        out_specs=[
            pl.BlockSpec((tm, TN), lambda s, m: (z_row(s, m), z_col(s))),
            pl.BlockSpec((TN, tm), lambda s, m: (0, kt_blk(s, m))),
            pl.BlockSpec((rows_m, TN), lambda s, m: (0, (s >= S_V).astype(jnp.int32))),
            pl.BlockSpec((TN, rows_m), lambda s, m: (0, 0)),
        ] + [cs[1] for cs in casts],
        out_shape=[
            jax.ShapeDtypeStruct((t, 3 * TN), BF16),
            jax.ShapeDtypeStruct((TN, t), BF16),
            jax.ShapeDtypeStruct((rows_m, 2 * TN), BF16),
            jax.ShapeDtypeStruct((TN, rows_m), BF16),
        ] + [cs[2] for cs in casts],
        scratch_shapes=[pltpu.VMEM((TN, D_MODEL), BF16), pltpu.VMEM((TN, LANES), F32)],
        compiler_params=_params(("arbitrary", "arbitrary")),
        name="inproj_qkvo",
    )(xn, xnm, wt, b1, *cast_ws)


def _conv_prologue(m, is_prompt, tiles_per_seq, n_meta, xnm_ref, w_refs, b_refs, s0_ref, s1_ref, scratch,
                   make_xnm=None):
    wu_ref, wc_ref, wb_ref = w_refs
    bu_ref, bc_ref, _ = b_refs
    w3_ref, _, h1_ref, h2_ref, carry_ref, mtail_ref = scratch
    n_slab, tm, _ = h1_ref.shape
    n_seq = tm // SUBLANES
    slabs = [(k, slice(k * LANES, (k + 1) * LANES)) for k in range(n_slab)]
    seq_row = lambda r: pl.ds(r, n_seq, stride=SUBLANES)

    @pl.when(m == 0)
    def _():
        if make_xnm is not None:
            make_xnm()
        w3_ref[0] = wu_ref[...].astype(BF16)
        w3_ref[1] = wc_ref[...].astype(BF16)
        w3_ref[2] = wb_ref[...].astype(BF16)
        h1_ref[...] = jnp.zeros_like(h1_ref)
        h2_ref[...] = jnp.zeros_like(h2_ref)
        xnm = xnm_ref[...]
        cu_m = (_dot_nt(xnm, w3_ref[1]) + bc_ref[...]) * (_dot_nt(xnm, w3_ref[0]) + bu_ref[...])
        mtail_ref[...] = cu_m[n_meta - SUBLANES:n_meta, :]

    @pl.when(is_prompt)
    def _():
        first = (m % tiles_per_seq) == 0
        p6 = jnp.where(first, mtail_ref[6:7, :], carry_ref[6:7, :])
        p7 = jnp.where(first, mtail_ref[7:8, :], carry_ref[7:8, :])
        for k, ks in slabs:
            h2_ref[k, 0:1, :] = p6[:, ks]
            h2_ref[k, 1:2, :] = p7[:, ks]
            h1_ref[k, 0:1, :] = p7[:, ks]

    @pl.when(jnp.logical_not(is_prompt))
    def _():
        for k, ks in slabs:
            h2_ref[k, seq_row(0), :] = s0_ref[:, ks]
            h2_ref[k, seq_row(1), :] = s1_ref[:, ks]
            h1_ref[k, seq_row(0), :] = s1_ref[:, ks]


def _conv_main(xn, is_prompt, b_refs, wconv_ref, yb_ref, cu6_ref, cu7_ref, scratch):
    bu_ref, bc_ref, bb_ref = b_refs
    w3_ref, cu_ref, h1_ref, h2_ref, carry_ref, _ = scratch
    tm, cq = yb_ref.shape
    n_seq = tm // SUBLANES
    slabs = [(k, slice(k * LANES, (k + 1) * LANES)) for k in range(cq // LANES)]
    seq_row = lambda r: pl.ds(r, n_seq, stride=SUBLANES)
    proj = lambda k, b_ref: jnp.concatenate([_dot_nt(x, w3_ref[k]) for x in xn], axis=0) + b_ref[...]
    zu, zc, zb = proj(0, bu_ref), proj(1, bc_ref), proj(2, bb_ref)
    cu = zc * zu
    pos = lax.broadcasted_iota(jnp.int32, (tm, cq), 0) & jnp.where(is_prompt, tm - 1, SUBLANES - 1)
    h1 = jnp.concatenate([h1_ref[k] for k, _ in slabs], axis=1)
    h2 = jnp.concatenate([h2_ref[k] for k, _ in slabs], axis=1)
    x1 = jnp.where(pos >= 1, pltpu.roll(cu, 1, 0), h1)
    x2 = jnp.where(pos >= 2, pltpu.roll(cu, 2, 0), h2)
    w0, w1, w2 = wconv_ref[0:1, :], wconv_ref[1:2, :], wconv_ref[2:3, :]
    yb_ref[...] = (zb * ((w0 * x2 + w1 * x1) + w2 * cu)).astype(BF16)

    carry_ref[...] = cu[tm - SUBLANES:tm, :]
    for k, ks in slabs:
        cu_ref[k] = cu[:, ks]
        cu6_ref[:, ks] = cu_ref[k, seq_row(6), :]
        cu7_ref[:, ks] = cu_ref[k, seq_row(7), :]


def _conv_kernel(n_prompt_tiles, tiles_per_seq, n_meta, n_cast, xn_ref, xnm_ref, wu_ref, wc_ref, wb_ref,
                 bu_ref, bc_ref, bb_ref, wconv_ref, s0_ref, s1_ref, *refs):
    cast_in, (yb_ref, cu6_ref, cu7_ref) = refs[:n_cast], refs[n_cast:n_cast + 3]
    cast_out, scratch = refs[n_cast + 3:2 * n_cast + 3], refs[2 * n_cast + 3:]
    m = pl.program_id(1)
    is_prompt = m < n_prompt_tiles
    b_refs = (bu_ref, bc_ref, bb_ref)
    _conv_prologue(m, is_prompt, tiles_per_seq, n_meta, xnm_ref, (wu_ref, wc_ref, wb_ref), b_refs,
                   s0_ref, s1_ref, scratch)
    _cast_blocks(tuple(zip(cast_in, cast_out)))
    _conv_main([xn_ref[...]], is_prompt, b_refs, wconv_ref, yb_ref, cu6_ref, cu7_ref, scratch)


def _rms_conv_kernel(n_prompt_tiles, tiles_per_seq, dec_seq, xp_ref, xs_ref, xm_ref, g_ref, wg_ref, bg_ref,
                     wu_ref, wc_ref, wb_ref, bu_ref, bc_ref, bb_ref, wconv_ref, s0_ref, s1_ref,
                     xn_ref, gch_ref, grh_ref, xnm_ref, gchm_ref, grhm_ref, yb_ref, cu6_ref, cu7_ref, *scratch):
    m = pl.program_id(0)
    is_prompt = m < n_prompt_tiles
    n_meta = xm_ref.shape[0]
    b_refs = (bu_ref, bc_ref, bb_ref)

    def make_xnm():
        xm = jnp.concatenate([xm_ref[...], jnp.zeros((CHUNK - n_meta, D_MODEL), F32)], axis=0)
        xnm = _rmsnorm(xm, g_ref[...]).astype(BF16)
        xnm_ref[...] = xnm
        _gate_prep(xnm, wg_ref, bg_ref, CHUNK, n_meta, gchm_ref, grhm_ref)

    _conv_prologue(m, is_prompt, tiles_per_seq, n_meta, xnm_ref, (wu_ref, wc_ref, wb_ref), b_refs,
                   s0_ref, s1_ref, scratch, make_xnm)
    tm = xn_ref.shape[0]
    half = tm // 2
    xn_blocks = []
    for r in (slice(0, half), slice(half, tm)):
        x = jnp.where(is_prompt, xp_ref[r, :], xs_ref[r, :])
        xn_blocks.append(_rmsnorm(x, g_ref[...]).astype(BF16))
        xn_ref[r, :] = xn_blocks[-1]
    _gate_prep(jnp.concatenate(xn_blocks, axis=0), wg_ref, bg_ref, jnp.where(is_prompt, CHUNK, dec_seq), tm,
               gch_ref, grh_ref)
    _conv_main(xn_blocks, is_prompt, b_refs, wconv_ref, yb_ref, cu6_ref, cu7_ref, scratch)


def _conv_specs(c0, npt, tm, cm):
    n_seq = tm // SUBLANES
    n_slab = CQ // LANES
    chan = lambda *g: c0 + cm(*g)[0]
    tile = lambda *g: cm(*g)[1]
    w_spec = lambda off: pl.BlockSpec((pl.Element(CQ), pl.Element(D_MODEL)),
                                      lambda *g: (pl.multiple_of(off + chan(*g) * CQ, SUBLANES), 0))
    b_spec = lambda off: pl.BlockSpec((1, CQ), lambda *g: (0, (off - N_GATES) // CQ + chan(*g)))
    s_spec = lambda tok: pl.BlockSpec(
        (n_seq, CQ), lambda *g: (jnp.maximum(tile(*g) - npt, 0), tok * (C_WIDTH // CQ) + chan(*g)))
    in_specs = [w_spec(OFF_U), w_spec(OFF_C), w_spec(OFF_B), b_spec(OFF_U), b_spec(OFF_C), b_spec(OFF_B),
                pl.BlockSpec((CONV_W, CQ), lambda *g: (0, chan(*g))), s_spec(0), s_spec(1)]
    out_specs = [pl.BlockSpec((tm, CQ), lambda *g: (tile(*g), cm(*g)[0])),
                 pl.BlockSpec((n_seq, CQ), lambda *g: (tile(*g), cm(*g)[0])),
                 pl.BlockSpec((n_seq, CQ), lambda *g: (tile(*g), cm(*g)[0]))]
    scratch = [pltpu.VMEM((3, CQ, D_MODEL), BF16),
               pltpu.VMEM((n_slab, tm, LANES), F32),
               pltpu.VMEM((n_slab, tm, LANES), F32),
               pltpu.VMEM((n_slab, tm, LANES), F32),
               pltpu.VMEM((SUBLANES, CQ), F32),
               pltpu.VMEM((SUBLANES, CQ), F32)]
    return in_specs, out_specs, scratch


def _conv_out_shapes(t, n_blocks):
    return [jax.ShapeDtypeStruct((t, n_blocks * CQ), BF16),
            jax.ShapeDtypeStruct((t // SUBLANES, n_blocks * CQ), F32),
            jax.ShapeDtypeStruct((t // SUBLANES, n_blocks * CQ), F32)]


def _rms_conv_call(xp, xs, x_meta, g, wt, bg, b1, wconv, sconv, dec_seq, seq_len, tm):
    tp, ts = xp.shape[0], xs.shape[0]
    t = tp + ts
    assert tm & (tm - 1) == 0 and seq_len % tm == 0 and x_meta.shape[0] >= SUBLANES
    npt, nst = tp // tm, ts // tm
    conv_in, conv_out, scratch = _conv_specs(0, npt, tm, lambda m: (0, m))
    return pl.pallas_call(
        functools.partial(_rms_conv_kernel, npt, seq_len // tm, dec_seq),
        grid=(npt + nst,),
        in_specs=[
            pl.BlockSpec((tm, D_MODEL), lambda m: (jnp.minimum(m, npt - 1), 0)),
            pl.BlockSpec((tm, D_MODEL), lambda m: (jnp.maximum(m - npt, 0), 0)),
            pl.BlockSpec(x_meta.shape, lambda m: (0, 0)),
            pl.BlockSpec((1, D_MODEL), lambda m: (0, 0)),
            pl.BlockSpec((N_GATES, D_MODEL), lambda m: (OFF_GATES // N_GATES, 0)),
            pl.BlockSpec((N_GATES, 1), lambda m: (0, 0)),
        ] + conv_in,
        out_specs=[
            pl.BlockSpec((tm, D_MODEL), lambda m: (m, 0)),
            pl.BlockSpec((M_HEADS, 2, tm, LANES), lambda m: (0, 0, m, 0)),
            pl.BlockSpec((M_HEADS, SUBLANES, tm), lambda m: (0, 0, m)),
            pl.BlockSpec((CHUNK, D_MODEL), lambda m: (0, 0)),
            pl.BlockSpec((M_HEADS, 2, CHUNK, LANES), lambda m: (0, 0, 0, 0)),
            pl.BlockSpec((M_HEADS, SUBLANES, CHUNK), lambda m: (0, 0, 0)),
        ] + conv_out,
        out_shape=[
            jax.ShapeDtypeStruct((t, D_MODEL), BF16),
            jax.ShapeDtypeStruct((M_HEADS, 2, t, LANES), F32),
            jax.ShapeDtypeStruct((M_HEADS, SUBLANES, t), F32),
            jax.ShapeDtypeStruct((CHUNK, D_MODEL), BF16),
            jax.ShapeDtypeStruct((M_HEADS, 2, CHUNK, LANES), F32),
            jax.ShapeDtypeStruct((M_HEADS, SUBLANES, CHUNK), F32),
        ] + _conv_out_shapes(t, 1),
        scratch_shapes=scratch,
        compiler_params=_params(("arbitrary",)),
        name="rms_conv",
    )(xp, xs, x_meta, g, wt, bg, wt, wt, wt, b1, b1, b1, wconv, sconv, sconv)


def _conv_call(xn, xnm, n_meta, wt, b1, wconv, sconv, cast_ws, n_prompt_tokens, seq_len, tm):
    t = xn.shape[0]
    assert tm & (tm - 1) == 0 and seq_len % tm == 0 and n_meta >= SUBLANES
    npt = n_prompt_tokens // tm
    n_m = t // tm
    n_blocks = C_WIDTH // CQ - 1
    n_cast = 16
    assert n_cast <= n_blocks * n_m
    casts = [_cast_specs(w, n_cast, lambda c, m: c * n_m + m) for w in cast_ws]
    conv_in, conv_out, scratch = _conv_specs(1, npt, tm, lambda c, m: (c, m))
    return pl.pallas_call(
        functools.partial(_conv_kernel, npt, seq_len // tm, n_meta, len(cast_ws)),
        grid=(n_blocks, n_m),
        in_specs=[
            pl.BlockSpec((tm, D_MODEL), lambda c, m: (m, 0)),
            pl.BlockSpec(xnm.shape, lambda c, m: (0, 0)),
        ] + conv_in + [cs[0] for cs in casts],
        out_specs=conv_out + [cs[1] for cs in casts],
        out_shape=_conv_out_shapes(t, n_blocks) + [cs[2] for cs in casts],
        scratch_shapes=scratch,
        compiler_params=_params(("arbitrary", "arbitrary")),
        name="inproj_conv",
    )(xn, xnm, wt, wt, wt, b1, b1, b1, wconv, sconv, sconv, *cast_ws)


def _prompt_head(q, kt, v, gr, gc, ca, m_st):
    l = q.shape[0]
    assert l == LANES
    ig_row, b_row, r_row = gr[G_IG:G_IG + 1, :], gr[G_B:G_B + 1, :], gr[G_R:G_R + 1, :]
    b_col = gc[0]
    v_aug = jnp.concatenate([v, jnp.ones((l, LANES), BF16)], axis=1)
    t_idx = lax.broadcasted_iota(jnp.int32, (l, l), 0)
    s_idx = lax.broadcasted_iota(jnp.int32, (l, l), 1)
    logw = jnp.where(s_idx <= t_idx, b_col - b_row + ig_row, NEG)
    inter = b_col + m_st
    m_t = jnp.maximum(inter, gc[1])
    w_inter = jnp.exp(inter - m_t)
    s = _dot(q, kt) * jnp.exp(logw - m_t)
    nd = (jnp.concatenate([w_inter] * (AUG // LANES), axis=1) * _dot(q, ca.astype(BF16))
          + _dot(s.astype(BF16), v_aug))
    rcp = 1.0 / jnp.maximum(jnp.abs(nd[:, M_DV:AUG]), jnp.exp(-m_t))
    h = nd[:, 0:M_DV] * jnp.concatenate([rcp] * (M_DV // LANES), axis=1)
    m_new = m_t[l - 1:l, 0:1]
    decay = jnp.exp(b_col[l - 1:l, 0:1] + m_st - m_new)
    kwt = (kt.astype(F32) * jnp.exp(r_row + ig_row - m_new)).astype(BF16)
    return h, decay * ca + _dot(kwt, v_aug), m_new


def _mlstm_prompt_kernel(batch, qm_ref, ktm_ref, vm_ref, gcm_ref, grm_ref, *refs):
    ins, (h_ref, c_ref, n_ref, m_ref, ca_ref) = refs[:5 * batch], refs[5 * batch:]
    heads = [(hd, slice(hd * M_DK, (hd + 1) * M_DK)) for hd in range(M_HEADS)]

    @pl.when(pl.program_id(0) == 0)
    def _():
        for hd, sl in heads:
            _, ca_new, m_new = _prompt_head(qm_ref[:, sl], ktm_ref[sl, :], vm_ref[:, sl], grm_ref[hd],
                                            gcm_ref[hd], jnp.zeros((M_DK, AUG), F32), jnp.zeros((1, 1), F32))
            for b in range(batch):
                ca_ref[b, hd] = ca_new
                m_ref[b, hd:hd + 1, :] = jnp.broadcast_to(m_new, (1, LANES))

    for k in range(h_ref.shape[1] // CHUNK):
        tok = slice(k * CHUNK, (k + 1) * CHUNK)
        for b in range(batch):
            q_ref, kt_ref, v_ref, gc_ref, gr_ref = ins[5 * b:5 * b + 5]
            for hd, sl in heads:
                h, ca_new, m_new = _prompt_head(q_ref[tok, sl], kt_ref[sl, tok], v_ref[tok, sl],
                                                gr_ref[hd, :, tok], gc_ref[hd, :, tok, :],
                                                ca_ref[b, hd], m_ref[b, hd:hd + 1, 0:1])
                ca_ref[b, hd] = ca_new
                m_ref[b, hd:hd + 1, :] = jnp.broadcast_to(m_new, (1, LANES))
                h_ref[b, tok, sl] = h.astype(h_ref.dtype)

    @pl.when(pl.program_id(0) == pl.num_programs(0) - 1)
    def _():
        for b in range(batch):
            for hd, _ in heads:
                ca = ca_ref[b, hd]
                c_ref[b, hd] = ca[:, 0:M_DV]
                n_t = ca[:, M_DV:AUG]
                n_ref[b, hd:hd + 1, :] = jnp.concatenate(
                    [n_t[k * LANES:(k + 1) * LANES, :].T[0:1, :] for k in range(M_DK // LANES)], axis=1)


def _mlstm_prompt_call(zm, ktm, gcm, grm, z, kt, gch, grh, batch, seq, tok):
    assert tok % CHUNK == 0 and seq % tok == 0
    nc = seq // tok
    per_prompt_specs, per_prompt_args = [], []
    for b in range(batch):
        row = functools.partial(lambda b, c: b * nc + c, b)
        per_prompt_specs += [
            pl.BlockSpec((tok, M_WIDTH), lambda c, row=row: (row(c), Z_Q)),
            pl.BlockSpec((M_WIDTH, tok), lambda c, row=row: (0, row(c))),
            pl.BlockSpec((tok, M_WIDTH), lambda c, row=row: (row(c), Z_V)),
            pl.BlockSpec((M_HEADS, 2, tok, LANES), lambda c, row=row: (0, 0, row(c), 0)),
            pl.BlockSpec((M_HEADS, SUBLANES, tok), lambda c, row=row: (0, 0, row(c))),
        ]
        per_prompt_args += [z, kt, z, gch, grh]
    return pl.pallas_call(
        functools.partial(_mlstm_prompt_kernel, batch),
        grid=(nc,),
        in_specs=[
            pl.BlockSpec((CHUNK, M_WIDTH), lambda c: (0, 0)),
            pl.BlockSpec((M_WIDTH, CHUNK), lambda c: (0, 0)),
            pl.BlockSpec((CHUNK, M_WIDTH), lambda c: (0, 1)),
            pl.BlockSpec((M_HEADS, 2, CHUNK, LANES), lambda c: (0, 0, 0, 0)),
            pl.BlockSpec((M_HEADS, SUBLANES, CHUNK), lambda c: (0, 0, 0)),
        ] + per_prompt_specs,
        out_specs=[
            pl.BlockSpec((batch, tok, M_WIDTH), lambda c: (0, c, 0)),
            pl.BlockSpec((batch, M_HEADS, M_DK, M_DV), lambda c: (0, 0, 0, 0)),
            pl.BlockSpec((batch, M_HEADS, M_DK), lambda c: (0, 0, 0)),
            pl.BlockSpec((batch, M_HEADS, LANES), lambda c: (0, 0, 0)),
        ],
        out_shape=[
            jax.ShapeDtypeStruct((batch, seq, M_WIDTH), BF16),
            jax.ShapeDtypeStruct((batch, M_HEADS, M_DK, M_DV), F32),
            jax.ShapeDtypeStruct((batch, M_HEADS, M_DK), F32),
            jax.ShapeDtypeStruct((batch, M_HEADS, LANES), F32),
        ],
        scratch_shapes=[pltpu.VMEM((batch, M_HEADS, M_DK, AUG), F32)],
        compiler_params=_params(("arbitrary",)),
        name="mlstm_prompt",
    )(zm, ktm, zm, gcm, grm, *per_prompt_args)


def _group_max(x, size):
    n = x.shape[-1]
    lane = lax.broadcasted_iota(jnp.int32, x.shape, x.ndim - 1)
    k = 1
    while k < size:
        partner = jnp.where((lane & k) == 0, pltpu.roll(x, n - k, x.ndim - 1), pltpu.roll(x, k, x.ndim - 1))
        x = jnp.maximum(x, partner)
        k *= 2
    return x


def _mlstm_sample_kernel(dec_seq, q_ref, kt_ref, v_ref, gc_ref, gr_ref, mcol_ref, mrow_ref, cin_ref, nin_ref,
                         h_ref, c_ref, n_ref, m_ref):
    l = q_ref.shape[0]
    n_seq = l // dec_seq
    shift = dec_seq.bit_length() - 1
    q, kt, v = q_ref[...], kt_ref[...], v_ref[...]
    assert l == LANES
    gr, gc = gr_ref[0], gc_ref[0]
    ig_row, b_row, r_row = gr[G_IG:G_IG + 1, :], gr[G_B:G_B + 1, :], gr[G_R:G_R + 1, :]
    b_col = gc[0]
    m_col, m_row = mcol_ref[0], mrow_ref[0]

    t_idx = lax.broadcasted_iota(jnp.int32, (l, l), 0)
    s_idx = lax.broadcasted_iota(jnp.int32, (l, l), 1)
    t_seq = t_idx >> shift
    logw = jnp.where((t_seq == (s_idx >> shift)) & (s_idx <= t_idx), b_col - b_row + ig_row, NEG)
    inter = b_col + m_col
    m_t = jnp.maximum(inter, gc[1])
    w_inter = jnp.exp(inter - m_t)
    s = _dot(q, kt) * jnp.exp(logw - m_t)

    e_row = r_row + ig_row
    b_last = b_row + r_row
    m_new = jnp.maximum(b_last + m_row, _group_max(e_row, dec_seq))
    decay = jnp.exp(b_last + m_row - m_new)
    kwt = (kt.astype(F32) * jnp.exp(e_row - m_new)).astype(BF16)

    seq_lane = s_idx == t_seq
    seq_onehot = jnp.where(seq_lane, 1.0, 0.0).astype(BF16)
    pa = _dot(s.astype(BF16), jnp.concatenate([v, seq_onehot], axis=1))

    n_rows = jnp.concatenate([nin_ref[...], jnp.zeros((LANES - n_seq, M_DK), F32)], axis=0)
    n_t = jnp.concatenate([n_rows[:, 0:LANES].T, n_rows[:, LANES:2 * LANES].T], axis=0)
    lane_k = lax.broadcasted_iota(jnp.int32, (M_DK, LANES), 1)
    lane_1 = lax.broadcasted_iota(jnp.int32, (1, LANES), 1)
    qf = q.astype(F32)
    qca_rows = []
    decay_seq = jnp.zeros((1, LANES), F32)
    for i in range(n_seq):
        decay_i = decay[:, i * dec_seq:i * dec_seq + 1]
        decay_seq = jnp.where(lane_1 == i, decay_i, decay_seq)
        c_i = cin_ref[i, 0]
        ca = jnp.concatenate([c_i, jnp.where(lane_k == i, n_t, 0.0)], axis=1)
        rows = slice(i * dec_seq, (i + 1) * dec_seq)
        qca_rows.append(_dot(qf[rows, :].astype(BF16), ca.astype(BF16)))
        kw_i = jnp.where((lane_k >> shift) == i, kwt, jnp.zeros_like(kwt))
        c_ref[i, 0] = decay_i * c_i + _dot(kw_i, v)
    n_t_new = decay_seq * n_t + _dot(kwt, seq_onehot)
    nd = jnp.concatenate([w_inter] * (AUG // LANES), axis=1) * jnp.concatenate(qca_rows, axis=0) + pa
    den = jnp.sum(jnp.where(seq_lane, nd[:, M_DV:AUG], 0.0), axis=-1, keepdims=True)
    rcp = 1.0 / jnp.maximum(jnp.abs(den), jnp.exp(-m_t[:, 0:1]))
    h_ref[...] = (nd[:, 0:M_DV] * rcp).astype(h_ref.dtype)
    n_ref[...] = jnp.concatenate([n_t_new[0:LANES, :].T, n_t_new[LANES:2 * LANES, :].T], axis=1)[0:n_seq, :]
    m_ref[0] = jnp.broadcast_to(m_new, (SUBLANES, l))


def _gates_sample_kernel(dec_seq, xn_ref, wt_ref, b_ref, *refs):
    sample_in, z_ref, sample_out, w_ref = refs[:9], refs[9], refs[10:14], refs[14]

    @pl.when(pl.program_id(1) == 0)
    def _():
        w_ref[...] = wt_ref[...].astype(BF16)

    _mlstm_sample_kernel(dec_seq, *sample_in, *sample_out)
    z_ref[...] = _sigmoid(_dot_nt(xn_ref[...], w_ref[...]) + b_ref[...]).astype(BF16)


def _gates_sample_call(xn, wt, b1, zqv, kt, gch, grh, m_col, m_row, c_in, n_in, row0, dec_seq, tm):
    t = xn.shape[0]
    n_m = t // tm
    nb = c_in.shape[0]
    l = SEQ_PER_STEP * dec_seq
    blk0 = row0 // l
    n_steps = 2 * D_MODEL // TN
    n_guest = (nb // SEQ_PER_STEP) * M_HEADS
    assert n_guest <= n_steps * n_m
    w_off = lambda s: OFF_GA + s * TN
    b_blk = lambda s: (OFF_GA - N_GATES) // TN + s
    z_col = lambda s: s
    guest = lambda s, m: jnp.minimum(s * n_m + m, n_guest - 1)
    gi = lambda s, m: guest(s, m) // M_HEADS
    gh = lambda s, m: guest(s, m) % M_HEADS
    return pl.pallas_call(
        functools.partial(_gates_sample_kernel, dec_seq),
        grid=(n_steps, n_m),
        in_specs=[
            pl.BlockSpec((tm, D_MODEL), lambda s, m: (m, 0)),
            pl.BlockSpec((pl.Element(TN), pl.Element(D_MODEL)),
                         lambda s, m: (pl.multiple_of(w_off(s), SUBLANES), 0)),
            pl.BlockSpec((1, TN), lambda s, m: (0, b_blk(s))),
            pl.BlockSpec((l, M_DK), lambda s, m: (blk0 + gi(s, m), Z_Q * M_HEADS + gh(s, m))),
            pl.BlockSpec((M_DK, l), lambda s, m: (gh(s, m), blk0 + gi(s, m))),
            pl.BlockSpec((l, M_DV), lambda s, m: (blk0 + gi(s, m), Z_V * M_HEADS + gh(s, m))),
            pl.BlockSpec((1, 2, l, LANES), lambda s, m: (gh(s, m), 0, blk0 + gi(s, m), 0)),
            pl.BlockSpec((1, SUBLANES, l), lambda s, m: (gh(s, m), 0, blk0 + gi(s, m))),
            pl.BlockSpec((1, l, LANES), lambda s, m: (gh(s, m), gi(s, m), 0)),
            pl.BlockSpec((1, 1, l), lambda s, m: (gh(s, m), 0, gi(s, m))),
            pl.BlockSpec((SEQ_PER_STEP, 1, M_DK, M_DV), lambda s, m: (gi(s, m), gh(s, m), 0, 0)),
            pl.BlockSpec((SEQ_PER_STEP, M_DK), lambda s, m: (gi(s, m), gh(s, m))),
        ],
        out_specs=[
            pl.BlockSpec((tm, TN), lambda s, m: (m, z_col(s))),
            pl.BlockSpec((l, M_DV), lambda s, m: (gi(s, m), gh(s, m))),
            pl.BlockSpec((SEQ_PER_STEP, 1, M_DK, M_DV), lambda s, m: (gi(s, m), gh(s, m), 0, 0)),
            pl.BlockSpec((SEQ_PER_STEP, M_DK), lambda s, m: (gi(s, m), gh(s, m))),
            pl.BlockSpec((1, SUBLANES, l), lambda s, m: (gh(s, m), 0, gi(s, m))),
        ],
        out_shape=[
            jax.ShapeDtypeStruct((t, ZG_COLS), BF16),
            jax.ShapeDtypeStruct((nb * dec_seq, M_WIDTH), BF16),
            jax.ShapeDtypeStruct(c_in.shape, F32),
            jax.ShapeDtypeStruct(n_in.shape, F32),
            jax.ShapeDtypeStruct((M_HEADS, SUBLANES, nb * dec_seq), F32),
        ],
        scratch_shapes=[pltpu.VMEM((TN, D_MODEL), BF16)],
        compiler_params=_params(("arbitrary", "arbitrary")),
        name="inproj_gates_mlstm_sample",
    )(xn, wt, b1, zqv, kt, zqv, gch, grh, m_col, m_row, c_in, n_in)


def _mix_kernel(h_ref, so_ref, yb0_ref, yb1_ref, ga_ref, gb_ref, x_ref, pa_ref, pb_ref, wo_ref, gffn_ref,
                o_ref, on_ref):
    h_a = (so_ref[...].astype(F32) * h_ref[...].astype(F32)).astype(BF16)
    a = _dot(h_a, pa_ref[...])
    b = _dot(jnp.concatenate([yb0_ref[...], yb1_ref[...]], axis=1), pb_ref[...])
    merged = ga_ref[...].astype(F32) * a + gb_ref[...].astype(F32) * b
    x1 = x_ref[...] + _dot(merged.astype(BF16), wo_ref[...])
    o_ref[...] = x1
    on_ref[...] = _rmsnorm(x1, gffn_ref[...]).astype(BF16)


def _mix_call(h, x, zqvo, zg, yb0, yb1, pa, pb, wo, gffn, row0, tm):
    rows = x.shape[0]
    assert row0 % tm == 0 and rows % tm == 0
    blk0 = row0 // tm
    const = lambda i: (0, 0)
    return pl.pallas_call(
        _mix_kernel,
        grid=(rows // tm,),
        in_specs=[
            pl.BlockSpec((tm, M_WIDTH), lambda i: (i, 0)),
            pl.BlockSpec((tm, M_WIDTH), lambda i: (blk0 + i, Z_SO)),
            pl.BlockSpec((tm, yb0.shape[1]), lambda i: (blk0 + i, 0)),
            pl.BlockSpec((tm, yb1.shape[1]), lambda i: (blk0 + i, 0)),
            pl.BlockSpec((tm, D_MODEL), lambda i: (blk0 + i, Z_GA // 2)),
            pl.BlockSpec((tm, D_MODEL), lambda i: (blk0 + i, Z_GB // 2)),
            pl.BlockSpec((tm, D_MODEL), lambda i: (i, 0)),
            pl.BlockSpec((M_WIDTH, D_MODEL), const, pipeline_mode=pl.Buffered(1)),
            pl.BlockSpec((C_WIDTH, D_MODEL), const, pipeline_mode=pl.Buffered(1)),
            pl.BlockSpec((D_MODEL, D_MODEL), const, pipeline_mode=pl.Buffered(1)),
            pl.BlockSpec((1, D_MODEL), const),
        ],
        out_specs=[pl.BlockSpec((tm, D_MODEL), lambda i: (i, 0)), pl.BlockSpec((tm, D_MODEL), lambda i: (i, 0))],
        out_shape=[jax.ShapeDtypeStruct((rows, D_MODEL), F32), jax.ShapeDtypeStruct((rows, D_MODEL), BF16)],
        compiler_params=_params(("arbitrary",)),
        name="mix",
    )(h, zqvo, yb0, yb1, zg, zg, x, pa, pb, wo, gffn)


def _ffn_kernel(x1_ref, xn_ref, wup_ref, wdn_ref, gfin_ref, y_ref):
    f = pl.program_id(1)
    last = pl.num_programs(1) - 1

    def mlp_part():
        u = jnp.maximum(_dot(xn_ref[...], wup_ref[...]), 0.0)
        return _dot((u * u).astype(BF16), wdn_ref[...])

    @pl.when(f == 0)
    def _():
        y_ref[...] = x1_ref[...] + mlp_part()

    @pl.when((f > 0) & (f < last))
    def _():
        y_ref[...] += mlp_part()

    @pl.when(f == last)
    def _():
        y_ref[...] = _rmsnorm(y_ref[...] + mlp_part(), gfin_ref[...])


def _ffn_call(x1, xn, wup, wdn, gfin, tm, tf):
    rows = x1.shape[0]
    assert D_FF // tf >= 2 and rows % tm == 0
    return pl.pallas_call(
        _ffn_kernel,
        grid=(rows // tm, D_FF // tf),
        in_specs=[
            pl.BlockSpec((tm, D_MODEL), lambda i, f: (i, 0)),
            pl.BlockSpec((tm, D_MODEL), lambda i, f: (i, 0)),
            pl.BlockSpec((D_MODEL, tf), lambda i, f: (0, f)),
            pl.BlockSpec((tf, D_MODEL), lambda i, f: (f, 0)),
            pl.BlockSpec((1, D_MODEL), lambda i, f: (0, 0)),
        ],
        out_specs=pl.BlockSpec((tm, D_MODEL), lambda i, f: (i, 0)),
        out_shape=jax.ShapeDtypeStruct((rows, D_MODEL), F32),
        compiler_params=_params(("arbitrary", "arbitrary")),
        name="ffn",
    )(x1, xn, wup, wdn, gfin)


def kernel(x_prompt, x_sample, state_mlstm_C, state_mlstm_n, state_mlstm_m, state_conv, meta_tokens,
           g_mix, w_in, b_in, w_conv, p_a, p_b, w_o, g_ffn, w_up, w_down, g_final):
    assert w_in.shape[0] == 1, "single-layer trunk"
    batch, seq, _ = x_prompt.shape
    dec_batch, dec_seq, _ = x_sample.shape
    assert dec_seq == SUBLANES and seq % 1024 == 0 and dec_batch % SEQ_PER_STEP == 0
    n_p, n_s = batch * seq, dec_batch * dec_seq

    wt = w_in[0].T
    bias = b_in[0]
    b1 = jnp.concatenate([bias[:OFF_GATES], bias[OFF_O:]])
    bg = bias[OFF_GATES:OFF_O][:, None]
    gmix = g_mix[0][None, :]

    xp = x_prompt.reshape(n_p, D_MODEL)
    xs = x_sample.reshape(n_s, D_MODEL)
    n_meta = meta_tokens.shape[0]

    sconv = state_conv[0].reshape(dec_batch, (CONV_W - 1) * C_WIDTH)
    xn, gch, grh, xnm, gcm, grm, yb0, cu6a, cu7a = _rms_conv_call(
        xp, xs, meta_tokens.astype(F32), gmix, wt, bg, b1[None, :], w_conv[0], sconv, dec_seq, seq, 512)
    zqv, kt, zm, ktm, w_up16, w_down16 = _qkvo_call(xn, xnm, wt, b1[None, :], (w_up[0], w_down[0]), 1024)
    m_tok = jnp.repeat(state_mlstm_m[0], dec_seq, axis=0).T
    zg, hs, c_s, n_sm, m_s = _gates_sample_call(
        xn, wt, b1[None, :], zqv, kt, gch, grh,
        jnp.broadcast_to(m_tok[:, :, None], m_tok.shape + (LANES,)), m_tok[:, None, :],
        state_mlstm_C[0], state_mlstm_n[0].reshape(dec_batch, M_HEADS * M_DK), n_p, dec_seq, 1152)
    yb1, cu6b, cu7b, p_a16, p_b16, w_o16 = _conv_call(
        xn, xnm, n_meta, wt, b1[None, :], w_conv[0], sconv, (p_a[0], p_b[0], w_o[0]), n_p, seq, 1024)

    hp, c_p, n_pr, m_p = _mlstm_prompt_call(zm, ktm, gcm, grm, zqv, kt, gch, grh, batch, seq, 2 * CHUNK)

    y_p, y_s = [
        _ffn_call(*_mix_call(h, x, zqv, zg, yb0, yb1, p_a16, p_b16, w_o16, g_ffn[0][None, :], row0, 512),
                  w_up16, w_down16, g_final[None, :], 512, 2048)
        for h, x, row0 in ((hp.reshape(n_p, M_WIDTH), xp, 0), (hs, xs, n_p))]

    per_seq = seq // SUBLANES
    first_s = n_p // SUBLANES
    pick = lambda rows: jnp.stack([jnp.concatenate([cu6a[rows], cu6b[rows]], axis=1),
                                   jnp.concatenate([cu7a[rows], cu7b[rows]], axis=1)], axis=1)
    cv_p = pick(slice(per_seq - 1, batch * per_seq, per_seq))
    cv_s = pick(slice(first_s, None))
    m_s = m_s[:, 0, dec_seq - 1::dec_seq].T
    return (y_p.reshape(batch, seq, D_MODEL), y_s.reshape(dec_batch, dec_seq, D_MODEL),
            c_p[None], n_pr[None], m_p[None, :, :, 0], cv_p[None],
            c_s[None], n_sm.reshape(dec_batch, M_HEADS, M_DK)[None], m_s[None], cv_s[None])
```

```python
import functools

import jax
import jax.numpy as jnp
from jax import lax
from jax.experimental import pallas as pl
from jax.experimental.pallas import tpu as pltpu

F32 = jnp.float32
BF16 = jnp.bfloat16

D_MODEL = 2048
CHUNK = 128
M_HEADS = 4
M_DK = 256
M_DV = 256
M_WIDTH = M_HEADS * M_DV
C_WIDTH = 1024
CONV_W = 3
D_FF = 4 * D_MODEL
EPS = 1e-6
N_GATES = 2 * M_HEADS
OFF_Q = 0
OFF_K = OFF_Q + M_HEADS * M_DK
OFF_V = OFF_K + M_HEADS * M_DK
OFF_GATES = OFF_V + M_WIDTH
OFF_O = OFF_GATES + N_GATES
OFF_U = OFF_O + M_WIDTH
OFF_C = OFF_U + C_WIDTH
OFF_B = OFF_C + C_WIDTH
OFF_GA = OFF_B + C_WIDTH

LANES = 128
SUBLANES = 8
VMEM_LIMIT_BYTES = 60000 * 1024

TN = 1024
Z_Q, Z_V, Z_SO = 0, 1, 2
Z_GA, Z_GB = 0, 2
ZG_COLS = 4 * TN
CQ = 256
AUG = M_DV + LANES
SEQ_PER_STEP = CHUNK // SUBLANES
G_IG, G_B, G_R = 0, 1, 2

NEG = -1e30
F32_LOWEST = -3e38
NT_DIMS = (((1,), (1,)), ((), ()))


def _params(semantics):
    return pltpu.CompilerParams(dimension_semantics=semantics, vmem_limit_bytes=VMEM_LIMIT_BYTES)


def _rmsnorm(x, g):
    y = x * lax.rsqrt(jnp.mean(x * x, axis=-1, keepdims=True) + EPS)
    return y * g


def _log_sigmoid(x):
    return jnp.minimum(x, 0.0) - jnp.log1p(jnp.exp(-jnp.abs(x)))


def _dot(a, b):
    return jnp.dot(a, b, preferred_element_type=F32)


def _dot_nt(a, b):
    return lax.dot_general(a, b, NT_DIMS, preferred_element_type=F32)


def _gate_prep(xn, wg_ref, bg_ref, blk, n_valid, gch_ref, grh_ref):
    tm = xn.shape[0]
    wg = wg_ref[...].astype(BF16)
    wg = jnp.concatenate([wg, jnp.zeros((LANES - N_GATES, wg.shape[1]), BF16)], axis=0)
    g = _dot_nt(wg, xn)[0:SUBLANES, :] + bg_ref[...]
    row = lax.broadcasted_iota(jnp.int32, (SUBLANES, tm), 0)
    lane = lax.broadcasted_iota(jnp.int32, (SUBLANES, tm), 1)
    a = jnp.where(row < M_HEADS, g, _log_sigmoid(g))
    if n_valid < tm:
        a = jnp.where(lane < n_valid, a, jnp.where(row < M_HEADS, NEG, 0.0))
    pos = lane & (blk - 1)
    n_steps_blk = blk if isinstance(blk, int) else LANES

    def scan(x, op, fill, reverse=False):
        shift = 1
        while shift < n_steps_blk:
            if reverse:
                x = op(x, jnp.where(pos < blk - shift, pltpu.roll(x, tm - shift, 1), fill))
            else:
                x = op(x, jnp.where(pos >= shift, pltpu.roll(x, shift, 1), fill))
            shift *= 2
        return x

    pre = scan(a, jnp.add, 0.0)
    suf = scan(a, jnp.add, 0.0, reverse=True) - a
    b_up = pltpu.roll(pre, M_HEADS, 0)
    m_in = b_up + scan(a - b_up, jnp.maximum, F32_LOWEST)
    for h in range(M_HEADS):
        grh_ref[h] = jnp.where(
            row == G_IG, pltpu.roll(a, (G_IG - h) % SUBLANES, 0),
            jnp.where(row == G_B, pltpu.roll(pre, (G_B - M_HEADS - h) % SUBLANES, 0),
                      jnp.where(row == G_R, pltpu.roll(suf, (G_R - M_HEADS - h) % SUBLANES, 0), 0.0)))
        for c in range(tm // LANES):
            cs = slice(c * LANES, (c + 1) * LANES)
            gch_ref[h, 0, cs, :] = jnp.broadcast_to(b_up[h:h + 1, cs], (LANES, LANES)).T
            gch_ref[h, 1, cs, :] = jnp.broadcast_to(m_in[h:h + 1, cs], (LANES, LANES)).T


S_Q, S_K, S_V, S_O = 0, 1, 2, 3


def _cast_blocks(pairs):
    for src_ref, dst_ref in pairs:
        dst_ref[...] = src_ref[...].astype(BF16)


def _cast_specs(w, n_blocks, step):
    rows, cols = w.shape
    blk = lambda *g: (jnp.minimum(step(*g), n_blocks - 1), 0)
    spec = pl.BlockSpec((rows // n_blocks, cols), blk)
    return spec, spec, jax.ShapeDtypeStruct(w.shape, BF16)


def _sigmoid(z):
    return 0.5 * jnp.tanh(0.5 * z) + 0.5


def _qkvo_kernel(n_cast, xn_ref, xnm_ref, wt_ref, b_ref, *refs):
    cast_in, (z_ref, kt_ref, zm_ref, ktm_ref) = refs[:n_cast], refs[n_cast:n_cast + 4]
    cast_out, (w_ref, bcol_ref) = refs[n_cast + 4:2 * n_cast + 4], refs[2 * n_cast + 4:]
    s = pl.program_id(0)
    casts = tuple(zip(cast_in, cast_out))
    k_scale = M_DK ** -0.5
    lane_tiles = lambda x, n: jnp.concatenate([x] * (n // LANES), axis=1)

    @pl.when(pl.program_id(1) == 0)
    def _():
        w_ref[...] = wt_ref[...].astype(BF16)

        @pl.when((s == S_Q) | (s == S_V))
        def _():
            zm_ref[...] = (_dot_nt(xnm_ref[...], w_ref[...]) + b_ref[...]).astype(BF16)

        @pl.when(s == S_K)
        def _():
            for c in range(TN // LANES):
                cs = slice(c * LANES, (c + 1) * LANES)
                bcol_ref[cs, :] = jnp.broadcast_to(b_ref[:, cs], (LANES, LANES)).T
            ktm = _dot_nt(w_ref[...], xnm_ref[...]) + lane_tiles(bcol_ref[...], xnm_ref.shape[0])
            ktm_ref[...] = (ktm * k_scale).astype(BF16)

    def z():
        return _dot_nt(xn_ref[...], w_ref[...]) + b_ref[...]

    @pl.when((s == S_Q) | (s == S_V))
    def _():
        _cast_blocks(casts)
        z_ref[...] = z().astype(BF16)

    @pl.when(s == S_K)
    def _():
        _cast_blocks(casts)
        kt = _dot_nt(w_ref[...], xn_ref[...]) + lane_tiles(bcol_ref[...], xn_ref.shape[0])
        kt_ref[...] = (kt * k_scale).astype(BF16)

    @pl.when(s == S_O)
    def _():
        _cast_blocks(casts)
        z_ref[...] = _sigmoid(z()).astype(BF16)


def _qkvo_call(xn, xnm, wt, b1, cast_ws, tm):
    t = xn.shape[0]
    assert t % tm == 0
    n_m = t // tm
    rows_m = xnm.shape[0]
    n_cast = 32
    assert n_cast <= 4 * n_m
    casts = [_cast_specs(w, n_cast, lambda s, m: s * n_m + m) for w in cast_ws]
    w_off = lambda s: jnp.where(s == S_O, OFF_O, s * TN)
    z_col = lambda s: jnp.where(s <= S_K, Z_Q, s - 1)
    z_row = lambda s, m: jnp.where(s == S_K, n_m - 1, m)
    kt_blk = lambda s, m: jnp.where(s == S_K, m, jnp.where(s < S_K, 0, n_m - 1))
    return pl.pallas_call(
        functools.partial(_qkvo_kernel, len(cast_ws)),
        grid=(4, n_m),
        in_specs=[
            pl.BlockSpec((tm, D_MODEL), lambda s, m: (m, 0)),
            pl.BlockSpec((rows_m, D_MODEL), lambda s, m: (0, 0)),
            pl.BlockSpec((pl.Element(TN), pl.Element(D_MODEL)),
                         lambda s, m: (pl.multiple_of(w_off(s), SUBLANES), 0)),
            pl.BlockSpec((1, TN), lambda s, m: (0, s)),
        ] + [cs[0] for cs in casts],
        out_specs=[
            pl.BlockSpec((tm, TN), lambda s, m: (z_row(s, m), z_col(s))),
            pl.BlockSpec((TN, tm), lambda s, m: (0, kt_blk(s, m))),
            pl.BlockSpec((rows_m, TN), lambda s, m: (0, (s >= S_V).astype(jnp.int32))),
            pl.BlockSpec((TN, rows_m), lambda s, m: (0, 0)),
        ] + [cs[1] for cs in casts],
        out_shape=[
            jax.ShapeDtypeStruct((t, 3 * TN), BF16),
            jax.ShapeDtypeStruct((TN, t), BF16),
            jax.ShapeDtypeStruct((rows_m, 2 * TN), BF16),
            jax.ShapeDtypeStruct((TN, rows_m), BF16),
        ] + [cs[2] for cs in casts],
        scratch_shapes=[pltpu.VMEM((TN, D_MODEL), BF16), pltpu.VMEM((TN, LANES), F32)],
        compiler_params=_params(("arbitrary", "arbitrary")),
        name="inproj_qkvo",
    )(xn, xnm, wt, b1, *cast_ws)


def _conv_prologue(m, is_prompt, tiles_per_seq, n_meta, xnm_ref, w_refs, b_refs, s0_ref, s1_ref, scratch,
                   make_xnm=None):
    wu_ref, wc_ref, wb_ref = w_refs
    bu_ref, bc_ref, _ = b_refs
    w3_ref, _, h1_ref, h2_ref, carry_ref, mtail_ref = scratch
    n_slab, tm, _ = h1_ref.shape
    n_seq = tm // SUBLANES
    slabs = [(k, slice(k * LANES, (k + 1) * LANES)) for k in range(n_slab)]
    seq_row = lambda r: pl.ds(r, n_seq, stride=SUBLANES)

    @pl.when(m == 0)
    def _():
        if make_xnm is not None:
            make_xnm()
        w3_ref[0] = wu_ref[...].astype(BF16)
        w3_ref[1] = wc_ref[...].astype(BF16)
        w3_ref[2] = wb_ref[...].astype(BF16)
        h1_ref[...] = jnp.zeros_like(h1_ref)
        h2_ref[...] = jnp.zeros_like(h2_ref)
        xnm = xnm_ref[...]
        cu_m = (_dot_nt(xnm, w3_ref[1]) + bc_ref[...]) * (_dot_nt(xnm, w3_ref[0]) + bu_ref[...])
        mtail_ref[...] = cu_m[n_meta - SUBLANES:n_meta, :]

    @pl.when(is_prompt)
    def _():
        first = (m % tiles_per_seq) == 0
        p6 = jnp.where(first, mtail_ref[6:7, :], carry_ref[6:7, :])
        p7 = jnp.where(first, mtail_ref[7:8, :], carry_ref[7:8, :])
        for k, ks in slabs:
            h2_ref[k, 0:1, :] = p6[:, ks]
            h2_ref[k, 1:2, :] = p7[:, ks]
            h1_ref[k, 0:1, :] = p7[:, ks]

    @pl.when(jnp.logical_not(is_prompt))
    def _():
        for k, ks in slabs:
            h2_ref[k, seq_row(0), :] = s0_ref[:, ks]
            h2_ref[k, seq_row(1), :] = s1_ref[:, ks]
            h1_ref[k, seq_row(0), :] = s1_ref[:, ks]


def _conv_main(xn, is_prompt, b_refs, wconv_ref, yb_ref, cu6_ref, cu7_ref, scratch):
    bu_ref, bc_ref, bb_ref = b_refs
    w3_ref, cu_ref, h1_ref, h2_ref, carry_ref, _ = scratch
    tm, cq = yb_ref.shape
    n_seq = tm // SUBLANES
    slabs = [(k, slice(k * LANES, (k + 1) * LANES)) for k in range(cq // LANES)]
    seq_row = lambda r: pl.ds(r, n_seq, stride=SUBLANES)
    proj = lambda k, b_ref: jnp.concatenate([_dot_nt(x, w3_ref[k]) for x in xn], axis=0) + b_ref[...]
    zu, zc, zb = proj(0, bu_ref), proj(1, bc_ref), proj(2, bb_ref)
    cu = zc * zu
    pos = lax.broadcasted_iota(jnp.int32, (tm, cq), 0) & jnp.where(is_prompt, tm - 1, SUBLANES - 1)
    h1 = jnp.concatenate([h1_ref[k] for k, _ in slabs], axis=1)
    h2 = jnp.concatenate([h2_ref[k] for k, _ in slabs], axis=1)
    x1 = jnp.where(pos >= 1, pltpu.roll(cu, 1, 0), h1)
    x2 = jnp.where(pos >= 2, pltpu.roll(cu, 2, 0), h2)
    w0, w1, w2 = wconv_ref[0:1, :], wconv_ref[1:2, :], wconv_ref[2:3, :]
    yb_ref[...] = (zb * ((w0 * x2 + w1 * x1) + w2 * cu)).astype(BF16)

    carry_ref[...] = cu[tm - SUBLANES:tm, :]
    for k, ks in slabs:
        cu_ref[k] = cu[:, ks]
        cu6_ref[:, ks] = cu_ref[k, seq_row(6), :]
        cu7_ref[:, ks] = cu_ref[k, seq_row(7), :]


def _conv_kernel(n_prompt_tiles, tiles_per_seq, n_meta, n_cast, xn_ref, xnm_ref, wu_ref, wc_ref, wb_ref,
                 bu_ref, bc_ref, bb_ref, wconv_ref, s0_ref, s1_ref, *refs):
    cast_in, (yb_ref, cu6_ref, cu7_ref) = refs[:n_cast], refs[n_cast:n_cast + 3]
    cast_out, scratch = refs[n_cast + 3:2 * n_cast + 3], refs[2 * n_cast + 3:]
    m = pl.program_id(1)
    is_prompt = m < n_prompt_tiles
    b_refs = (bu_ref, bc_ref, bb_ref)
    _conv_prologue(m, is_prompt, tiles_per_seq, n_meta, xnm_ref, (wu_ref, wc_ref, wb_ref), b_refs,
                   s0_ref, s1_ref, scratch)
    _cast_blocks(tuple(zip(cast_in, cast_out)))
    _conv_main([xn_ref[...]], is_prompt, b_refs, wconv_ref, yb_ref, cu6_ref, cu7_ref, scratch)


def _rms_conv_kernel(n_prompt_tiles, tiles_per_seq, dec_seq, xp_ref, xs_ref, xm_ref, g_ref, wg_ref, bg_ref,
                     wu_ref, wc_ref, wb_ref, bu_ref, bc_ref, bb_ref, wconv_ref, s0_ref, s1_ref,
                     xn_ref, gch_ref, grh_ref, xnm_ref, gchm_ref, grhm_ref, yb_ref, cu6_ref, cu7_ref, *scratch):
    m = pl.program_id(0)
    is_prompt = m < n_prompt_tiles
    n_meta = xm_ref.shape[0]
    b_refs = (bu_ref, bc_ref, bb_ref)

    def make_xnm():
        xm = jnp.concatenate([xm_ref[...], jnp.zeros((CHUNK - n_meta, D_MODEL), F32)], axis=0)
        xnm = _rmsnorm(xm, g_ref[...]).astype(BF16)
        xnm_ref[...] = xnm
        _gate_prep(xnm, wg_ref, bg_ref, CHUNK, n_meta, gchm_ref, grhm_ref)

    _conv_prologue(m, is_prompt, tiles_per_seq, n_meta, xnm_ref, (wu_ref, wc_ref, wb_ref), b_refs,
                   s0_ref, s1_ref, scratch, make_xnm)
    tm = xn_ref.shape[0]
    half = tm // 2
    xn_blocks = []
    for r in (slice(0, half), slice(half, tm)):
        x = jnp.where(is_prompt, xp_ref[r, :], xs_ref[r, :])
        xn_blocks.append(_rmsnorm(x, g_ref[...]).astype(BF16))
        xn_ref[r, :] = xn_blocks[-1]
    _gate_prep(jnp.concatenate(xn_blocks, axis=0), wg_ref, bg_ref, jnp.where(is_prompt, CHUNK, dec_seq), tm,
               gch_ref, grh_ref)
    _conv_main(xn_blocks, is_prompt, b_refs, wconv_ref, yb_ref, cu6_ref, cu7_ref, scratch)


def _conv_specs(c0, npt, tm, cm):
    n_seq = tm // SUBLANES
    n_slab = CQ // LANES
    chan = lambda *g: c0 + cm(*g)[0]
    tile = lambda *g: cm(*g)[1]
    w_spec = lambda off: pl.BlockSpec((pl.Element(CQ), pl.Element(D_MODEL)),
                                      lambda *g: (pl.multiple_of(off + chan(*g) * CQ, SUBLANES), 0))
    b_spec = lambda off: pl.BlockSpec((1, CQ), lambda *g: (0, (off - N_GATES) // CQ + chan(*g)))
    s_spec = lambda tok: pl.BlockSpec(
        (n_seq, CQ), lambda *g: (jnp.maximum(tile(*g) - npt, 0), tok * (C_WIDTH // CQ) + chan(*g)))
    in_specs = [w_spec(OFF_U), w_spec(OFF_C), w_spec(OFF_B), b_spec(OFF_U), b_spec(OFF_C), b_spec(OFF_B),
                pl.BlockSpec((CONV_W, CQ), lambda *g: (0, chan(*g))), s_spec(0), s_spec(1)]
    out_specs = [pl.BlockSpec((tm, CQ), lambda *g: (tile(*g), cm(*g)[0])),
                 pl.BlockSpec((n_seq, CQ), lambda *g: (tile(*g), cm(*g)[0])),
                 pl.BlockSpec((n_seq, CQ), lambda *g: (tile(*g), cm(*g)[0]))]
    scratch = [pltpu.VMEM((3, CQ, D_MODEL), BF16),
               pltpu.VMEM((n_slab, tm, LANES), F32),
               pltpu.VMEM((n_slab, tm, LANES), F32),
               pltpu.VMEM((n_slab, tm, LANES), F32),
               pltpu.VMEM((SUBLANES, CQ), F32),
               pltpu.VMEM((SUBLANES, CQ), F32)]
    return in_specs, out_specs, scratch


def _conv_out_shapes(t, n_blocks):
    return [jax.ShapeDtypeStruct((t, n_blocks * CQ), BF16),
            jax.ShapeDtypeStruct((t // SUBLANES, n_blocks * CQ), F32),
            jax.ShapeDtypeStruct((t // SUBLANES, n_blocks * CQ), F32)]


def _rms_conv_call(xp, xs, x_meta, g, wt, bg, b1, wconv, sconv, dec_seq, seq_len, tm):
    tp, ts = xp.shape[0], xs.shape[0]
    t = tp + ts
    assert tm & (tm - 1) == 0 and seq_len % tm == 0 and x_meta.shape[0] >= SUBLANES
    npt, nst = tp // tm, ts // tm
    conv_in, conv_out, scratch = _conv_specs(0, npt, tm, lambda m: (0, m))
    return pl.pallas_call(
        functools.partial(_rms_conv_kernel, npt, seq_len // tm, dec_seq),
        grid=(npt + nst,),
        in_specs=[
            pl.BlockSpec((tm, D_MODEL), lambda m: (jnp.minimum(m, npt - 1), 0)),
            pl.BlockSpec((tm, D_MODEL), lambda m: (jnp.maximum(m - npt, 0), 0)),
            pl.BlockSpec(x_meta.shape, lambda m: (0, 0)),
            pl.BlockSpec((1, D_MODEL), lambda m: (0, 0)),
            pl.BlockSpec((N_GATES, D_MODEL), lambda m: (OFF_GATES // N_GATES, 0)),
            pl.BlockSpec((N_GATES, 1), lambda m: (0, 0)),
        ] + conv_in,
        out_specs=[
            pl.BlockSpec((tm, D_MODEL), lambda m: (m, 0)),
            pl.BlockSpec((M_HEADS, 2, tm, LANES), lambda m: (0, 0, m, 0)),
            pl.BlockSpec((M_HEADS, SUBLANES, tm), lambda m: (0, 0, m)),
            pl.BlockSpec((CHUNK, D_MODEL), lambda m: (0, 0)),
            pl.BlockSpec((M_HEADS, 2, CHUNK, LANES), lambda m: (0, 0, 0, 0)),
            pl.BlockSpec((M_HEADS, SUBLANES, CHUNK), lambda m: (0, 0, 0)),
        ] + conv_out,
        out_shape=[
            jax.ShapeDtypeStruct((t, D_MODEL), BF16),
            jax.ShapeDtypeStruct((M_HEADS, 2, t, LANES), F32),
            jax.ShapeDtypeStruct((M_HEADS, SUBLANES, t), F32),
            jax.ShapeDtypeStruct((CHUNK, D_MODEL), BF16),
            jax.ShapeDtypeStruct((M_HEADS, 2, CHUNK, LANES), F32),
            jax.ShapeDtypeStruct((M_HEADS, SUBLANES, CHUNK), F32),
        ] + _conv_out_shapes(t, 1),
        scratch_shapes=scratch,
        compiler_params=_params(("arbitrary",)),
        name="rms_conv",
    )(xp, xs, x_meta, g, wt, bg, wt, wt, wt, b1, b1, b1, wconv, sconv, sconv)


def _conv_call(xn, xnm, n_meta, wt, b1, wconv, sconv, cast_ws, n_prompt_tokens, seq_len, tm):
    t = xn.shape[0]
    assert tm & (tm - 1) == 0 and seq_len % tm == 0 and n_meta >= SUBLANES
    npt = n_prompt_tokens // tm
    n_m = t // tm
    n_blocks = C_WIDTH // CQ - 1
    n_cast = 16
    assert n_cast <= n_blocks * n_m
    casts = [_cast_specs(w, n_cast, lambda c, m: c * n_m + m) for w in cast_ws]
    conv_in, conv_out, scratch = _conv_specs(1, npt, tm, lambda c, m: (c, m))
    return pl.pallas_call(
        functools.partial(_conv_kernel, npt, seq_len // tm, n_meta, len(cast_ws)),
        grid=(n_blocks, n_m),
        in_specs=[
            pl.BlockSpec((tm, D_MODEL), lambda c, m: (m, 0)),
            pl.BlockSpec(xnm.shape, lambda c, m: (0, 0)),
        ] + conv_in + [cs[0] for cs in casts],
        out_specs=conv_out + [cs[1] for cs in casts],
        out_shape=_conv_out_shapes(t, n_blocks) + [cs[2] for cs in casts],
        scratch_shapes=scratch,
        compiler_params=_params(("arbitrary", "arbitrary")),
        name="inproj_conv",
    )(xn, xnm, wt, wt, wt, b1, b1, b1, wconv, sconv, sconv, *cast_ws)


def _prompt_head(q, kt, v, gr, gc, ca, m_st):
    l = q.shape[0]
    assert l == LANES
    ig_row, b_row, r_row = gr[G_IG:G_IG + 1, :], gr[G_B:G_B + 1, :], gr[G_R:G_R + 1, :]
    b_col = gc[0]
    v_aug = jnp.concatenate([v, jnp.ones((l, LANES), BF16)], axis=1)
    t_idx = lax.broadcasted_iota(jnp.int32, (l, l), 0)
    s_idx = lax.broadcasted_iota(jnp.int32, (l, l), 1)
    logw = jnp.where(s_idx <= t_idx, b_col - b_row + ig_row, NEG)
    inter = b_col + m_st
    m_t = jnp.maximum(inter, gc[1])
    w_inter = jnp.exp(inter - m_t)
    s = _dot(q, kt) * jnp.exp(logw - m_t)
    nd = (jnp.concatenate([w_inter] * (AUG // LANES), axis=1) * _dot(q, ca.astype(BF16))
          + _dot(s.astype(BF16), v_aug))
    rcp = 1.0 / jnp.maximum(jnp.abs(nd[:, M_DV:AUG]), jnp.exp(-m_t))
    h = nd[:, 0:M_DV] * jnp.concatenate([rcp] * (M_DV // LANES), axis=1)
    m_new = m_t[l - 1:l, 0:1]
    decay = jnp.exp(b_col[l - 1:l, 0:1] + m_st - m_new)
    kwt = (kt.astype(F32) * jnp.exp(r_row + ig_row - m_new)).astype(BF16)
    return h, decay * ca + _dot(kwt, v_aug), m_new


def _mlstm_prompt_kernel(batch, qm_ref, ktm_ref, vm_ref, gcm_ref, grm_ref, *refs):
    ins, (h_ref, c_ref, n_ref, m_ref, ca_ref) = refs[:5 * batch], refs[5 * batch:]
    heads = [(hd, slice(hd * M_DK, (hd + 1) * M_DK)) for hd in range(M_HEADS)]

    @pl.when(pl.program_id(0) == 0)
    def _():
        for hd, sl in heads:
            _, ca_new, m_new = _prompt_head(qm_ref[:, sl], ktm_ref[sl, :], vm_ref[:, sl], grm_ref[hd],
                                            gcm_ref[hd], jnp.zeros((M_DK, AUG), F32), jnp.zeros((1, 1), F32))
            for b in range(batch):
                ca_ref[b, hd] = ca_new
                m_ref[b, hd:hd + 1, :] = jnp.broadcast_to(m_new, (1, LANES))

    for k in range(h_ref.shape[1] // CHUNK):
        tok = slice(k * CHUNK, (k + 1) * CHUNK)
        for b in range(batch):
            q_ref, kt_ref, v_ref, gc_ref, gr_ref = ins[5 * b:5 * b + 5]
            for hd, sl in heads:
                h, ca_new, m_new = _prompt_head(q_ref[tok, sl], kt_ref[sl, tok], v_ref[tok, sl],
                                                gr_ref[hd, :, tok], gc_ref[hd, :, tok, :],
                                                ca_ref[b, hd], m_ref[b, hd:hd + 1, 0:1])
                ca_ref[b, hd] = ca_new
                m_ref[b, hd:hd + 1, :] = jnp.broadcast_to(m_new, (1, LANES))
                h_ref[b, tok, sl] = h.astype(h_ref.dtype)

    @pl.when(pl.program_id(0) == pl.num_programs(0) - 1)
    def _():
        for b in range(batch):
            for hd, _ in heads:
                ca = ca_ref[b, hd]
                c_ref[b, hd] = ca[:, 0:M_DV]
                n_t = ca[:, M_DV:AUG]
                n_ref[b, hd:hd + 1, :] = jnp.concatenate(
                    [n_t[k * LANES:(k + 1) * LANES, :].T[0:1, :] for k in range(M_DK // LANES)], axis=1)


def _mlstm_prompt_call(zm, ktm, gcm, grm, z, kt, gch, grh, batch, seq, tok):
    assert tok % CHUNK == 0 and seq % tok == 0
    nc = seq // tok
    per_prompt_specs, per_prompt_args = [], []
    for b in range(batch):
        row = functools.partial(lambda b, c: b * nc + c, b)
        per_prompt_specs += [
            pl.BlockSpec((tok, M_WIDTH), lambda c, row=row: (row(c), Z_Q)),
            pl.BlockSpec((M_WIDTH, tok), lambda c, row=row: (0, row(c))),
            pl.BlockSpec((tok, M_WIDTH), lambda c, row=row: (row(c), Z_V)),
            pl.BlockSpec((M_HEADS, 2, tok, LANES), lambda c, row=row: (0, 0, row(c), 0)),
            pl.BlockSpec((M_HEADS, SUBLANES, tok), lambda c, row=row: (0, 0, row(c))),
        ]
        per_prompt_args += [z, kt, z, gch, grh]
    return pl.pallas_call(
        functools.partial(_mlstm_prompt_kernel, batch),
        grid=(nc,),
        in_specs=[
            pl.BlockSpec((CHUNK, M_WIDTH), lambda c: (0, 0)),
            pl.BlockSpec((M_WIDTH, CHUNK), lambda c: (0, 0)),
            pl.BlockSpec((CHUNK, M_WIDTH), lambda c: (0, 1)),
            pl.BlockSpec((M_HEADS, 2, CHUNK, LANES), lambda c: (0, 0, 0, 0)),
            pl.BlockSpec((M_HEADS, SUBLANES, CHUNK), lambda c: (0, 0, 0)),
        ] + per_prompt_specs,
        out_specs=[
            pl.BlockSpec((batch, tok, M_WIDTH), lambda c: (0, c, 0)),
            pl.BlockSpec((batch, M_HEADS, M_DK, M_DV), lambda c: (0, 0, 0, 0)),
            pl.BlockSpec((batch, M_HEADS, M_DK), lambda c: (0, 0, 0)),
            pl.BlockSpec((batch, M_HEADS, LANES), lambda c: (0, 0, 0)),
        ],
        out_shape=[
            jax.ShapeDtypeStruct((batch, seq, M_WIDTH), BF16),
            jax.ShapeDtypeStruct((batch, M_HEADS, M_DK, M_DV), F32),
            jax.ShapeDtypeStruct((batch, M_HEADS, M_DK), F32),
            jax.ShapeDtypeStruct((batch, M_HEADS, LANES), F32),
        ],
        scratch_shapes=[pltpu.VMEM((batch, M_HEADS, M_DK, AUG), F32)],
        compiler_params=_params(("arbitrary",)),
        name="mlstm_prompt",
    )(zm, ktm, zm, gcm, grm, *per_prompt_args)


def _group_max(x, size):
    n = x.shape[-1]
    lane = lax.broadcasted_iota(jnp.int32, x.shape, x.ndim - 1)
    k = 1
    while k < size:
        partner = jnp.where((lane & k) == 0, pltpu.roll(x, n - k, x.ndim - 1), pltpu.roll(x, k, x.ndim - 1))
        x = jnp.maximum(x, partner)
        k *= 2
    return x


def _mlstm_sample_kernel(dec_seq, q_ref, kt_ref, v_ref, gc_ref, gr_ref, mcol_ref, mrow_ref, cin_ref, nin_ref,
                         h_ref, c_ref, n_ref, m_ref):
    l = q_ref.shape[0]
    n_seq = l // dec_seq
    shift = dec_seq.bit_length() - 1
    q, kt, v = q_ref[...], kt_ref[...], v_ref[...]
    assert l == LANES
    gr, gc = gr_ref[0], gc_ref[0]
    ig_row, b_row, r_row = gr[G_IG:G_IG + 1, :], gr[G_B:G_B + 1, :], gr[G_R:G_R + 1, :]
    b_col = gc[0]
    m_col, m_row = mcol_ref[0], mrow_ref[0]

    t_idx = lax.broadcasted_iota(jnp.int32, (l, l), 0)
    s_idx = lax.broadcasted_iota(jnp.int32, (l, l), 1)
    t_seq = t_idx >> shift
    logw = jnp.where((t_seq == (s_idx >> shift)) & (s_idx <= t_idx), b_col - b_row + ig_row, NEG)
    inter = b_col + m_col
    m_t = jnp.maximum(inter, gc[1])
    w_inter = jnp.exp(inter - m_t)
    s = _dot(q, kt) * jnp.exp(logw - m_t)

    e_row = r_row + ig_row
    b_last = b_row + r_row
    m_new = jnp.maximum(b_last + m_row, _group_max(e_row, dec_seq))
    decay = jnp.exp(b_last + m_row - m_new)
    kwt = (kt.astype(F32) * jnp.exp(e_row - m_new)).astype(BF16)

    seq_lane = s_idx == t_seq
    seq_onehot = jnp.where(seq_lane, 1.0, 0.0).astype(BF16)
    pa = _dot(s.astype(BF16), jnp.concatenate([v, seq_onehot], axis=1))

    n_rows = jnp.concatenate([nin_ref[...], jnp.zeros((LANES - n_seq, M_DK), F32)], axis=0)
    n_t = jnp.concatenate([n_rows[:, 0:LANES].T, n_rows[:, LANES:2 * LANES].T], axis=0)
    lane_k = lax.broadcasted_iota(jnp.int32, (M_DK, LANES), 1)
    lane_1 = lax.broadcasted_iota(jnp.int32, (1, LANES), 1)
    qf = q.astype(F32)
    qca_rows = []
    decay_seq = jnp.zeros((1, LANES), F32)
    for i in range(n_seq):
        decay_i = decay[:, i * dec_seq:i * dec_seq + 1]
        decay_seq = jnp.where(lane_1 == i, decay_i, decay_seq)
        c_i = cin_ref[i, 0]
        ca = jnp.concatenate([c_i, jnp.where(lane_k == i, n_t, 0.0)], axis=1)
        rows = slice(i * dec_seq, (i + 1) * dec_seq)
        qca_rows.append(_dot(qf[rows, :].astype(BF16), ca.astype(BF16)))
        kw_i = jnp.where((lane_k >> shift) == i, kwt, jnp.zeros_like(kwt))
        c_ref[i, 0] = decay_i * c_i + _dot(kw_i, v)
    n_t_new = decay_seq * n_t + _dot(kwt, seq_onehot)
    nd = jnp.concatenate([w_inter] * (AUG // LANES), axis=1) * jnp.concatenate(qca_rows, axis=0) + pa
    den = jnp.sum(jnp.where(seq_lane, nd[:, M_DV:AUG], 0.0), axis=-1, keepdims=True)
    rcp = 1.0 / jnp.maximum(jnp.abs(den), jnp.exp(-m_t[:, 0:1]))
    h_ref[...] = (nd[:, 0:M_DV] * rcp).astype(h_ref.dtype)
    n_ref[...] = jnp.concatenate([n_t_new[0:LANES, :].T, n_t_new[LANES:2 * LANES, :].T], axis=1)[0:n_seq, :]
    m_ref[0] = jnp.broadcast_to(m_new, (SUBLANES, l))


def _gates_sample_kernel(dec_seq, xn_ref, wt_ref, b_ref, *refs):
    sample_in, z_ref, sample_out, w_ref = refs[:9], refs[9], refs[10:14], refs[14]

    @pl.when(pl.program_id(1) == 0)
    def _():
        w_ref[...] = wt_ref[...].astype(BF16)

    _mlstm_sample_kernel(dec_seq, *sample_in, *sample_out)
    z_ref[...] = _sigmoid(_dot_nt(xn_ref[...], w_ref[...]) + b_ref[...]).astype(BF16)


def _gates_sample_call(xn, wt, b1, zqv, kt, gch, grh, m_col, m_row, c_in, n_in, row0, dec_seq, tm):
    t = xn.shape[0]
    assert t % tm == 0
    n_m = t // tm
    nb = c_in.shape[0]
    l = SEQ_PER_STEP * dec_seq
    blk0 = row0 // l
    n_steps = 2 * D_MODEL // TN
    n_guest = (nb // SEQ_PER_STEP) * M_HEADS
    assert n_guest <= n_steps * n_m
    w_off = lambda s: OFF_GA + s * TN
    b_blk = lambda s: (OFF_GA - N_GATES) // TN + s
    z_col = lambda s: s
    guest = lambda s, m: jnp.minimum(s * n_m + m, n_guest - 1)
    gi = lambda s, m: guest(s, m) // M_HEADS
    gh = lambda s, m: guest(s, m) % M_HEADS
    return pl.pallas_call(
        functools.partial(_gates_sample_kernel, dec_seq),
        grid=(n_steps, n_m),
        in_specs=[
            pl.BlockSpec((tm, D_MODEL), lambda s, m: (m, 0)),
            pl.BlockSpec((pl.Element(TN), pl.Element(D_MODEL)),
                         lambda s, m: (pl.multiple_of(w_off(s), SUBLANES), 0)),
            pl.BlockSpec((1, TN), lambda s, m: (0, b_blk(s))),
            pl.BlockSpec((l, M_DK), lambda s, m: (blk0 + gi(s, m), Z_Q * M_HEADS + gh(s, m))),
            pl.BlockSpec((M_DK, l), lambda s, m: (gh(s, m), blk0 + gi(s, m))),
            pl.BlockSpec((l, M_DV), lambda s, m: (blk0 + gi(s, m), Z_V * M_HEADS + gh(s, m))),
            pl.BlockSpec((1, 2, l, LANES), lambda s, m: (gh(s, m), 0, blk0 + gi(s, m), 0)),
            pl.BlockSpec((1, SUBLANES, l), lambda s, m: (gh(s, m), 0, blk0 + gi(s, m))),
            pl.BlockSpec((1, l, LANES), lambda s, m: (gh(s, m), gi(s, m), 0)),
            pl.BlockSpec((1, 1, l), lambda s, m: (gh(s, m), 0, gi(s, m))),
            pl.BlockSpec((SEQ_PER_STEP, 1, M_DK, M_DV), lambda s, m: (gi(s, m), gh(s, m), 0, 0)),
            pl.BlockSpec((SEQ_PER_STEP, M_DK), lambda s, m: (gi(s, m), gh(s, m))),
        ],
        out_specs=[
            pl.BlockSpec((tm, TN), lambda s, m: (m, z_col(s))),
            pl.BlockSpec((l, M_DV), lambda s, m: (gi(s, m), gh(s, m))),
            pl.BlockSpec((SEQ_PER_STEP, 1, M_DK, M_DV), lambda s, m: (gi(s, m), gh(s, m), 0, 0)),
            pl.BlockSpec((SEQ_PER_STEP, M_DK), lambda s, m: (gi(s, m), gh(s, m))),
            pl.BlockSpec((1, SUBLANES, l), lambda s, m: (gh(s, m), 0, gi(s, m))),
        ],
        out_shape=[
            jax.ShapeDtypeStruct((t, ZG_COLS), BF16),
            jax.ShapeDtypeStruct((nb * dec_seq, M_WIDTH), BF16),
            jax.ShapeDtypeStruct(c_in.shape, F32),
            jax.ShapeDtypeStruct(n_in.shape, F32),
            jax.ShapeDtypeStruct((M_HEADS, SUBLANES, nb * dec_seq), F32),
        ],
        scratch_shapes=[pltpu.VMEM((TN, D_MODEL), BF16)],
        compiler_params=_params(("arbitrary", "arbitrary")),
        name="inproj_gates_mlstm_sample",
    )(xn, wt, b1, zqv, kt, zqv, gch, grh, m_col, m_row, c_in, n_in)


def _mix_kernel(h_ref, so_ref, yb0_ref, yb1_ref, ga_ref, gb_ref, x_ref, pa_ref, pb_ref, wo_ref, gffn_ref,
                o_ref, on_ref):
    h_a = (so_ref[...].astype(F32) * h_ref[...].astype(F32)).astype(BF16)
    a = _dot(h_a, pa_ref[...])
    b = _dot(jnp.concatenate([yb0_ref[...], yb1_ref[...]], axis=1), pb_ref[...])
    merged = ga_ref[...].astype(F32) * a + gb_ref[...].astype(F32) * b
    x1 = x_ref[...] + _dot(merged.astype(BF16), wo_ref[...])
    o_ref[...] = x1
    on_ref[...] = _rmsnorm(x1, gffn_ref[...]).astype(BF16)


def _mix_call(h, x, zqvo, zg, yb0, yb1, pa, pb, wo, gffn, row0, tm):
    rows = x.shape[0]
    assert row0 % tm == 0 and rows % tm == 0
    blk0 = row0 // tm
    const = lambda i: (0, 0)
    return pl.pallas_call(
        _mix_kernel,
        grid=(rows // tm,),
        in_specs=[
            pl.BlockSpec((tm, M_WIDTH), lambda i: (i, 0)),
            pl.BlockSpec((tm, M_WIDTH), lambda i: (blk0 + i, Z_SO)),
            pl.BlockSpec((tm, yb0.shape[1]), lambda i: (blk0 + i, 0)),
            pl.BlockSpec((tm, yb1.shape[1]), lambda i: (blk0 + i, 0)),
            pl.BlockSpec((tm, D_MODEL), lambda i: (blk0 + i, Z_GA // 2)),
            pl.BlockSpec((tm, D_MODEL), lambda i: (blk0 + i, Z_GB // 2)),
            pl.BlockSpec((tm, D_MODEL), lambda i: (i, 0)),
            pl.BlockSpec((M_WIDTH, D_MODEL), const, pipeline_mode=pl.Buffered(1)),
            pl.BlockSpec((C_WIDTH, D_MODEL), const, pipeline_mode=pl.Buffered(1)),
            pl.BlockSpec((D_MODEL, D_MODEL), const, pipeline_mode=pl.Buffered(1)),
            pl.BlockSpec((1, D_MODEL), const),
        ],
        out_specs=[pl.BlockSpec((tm, D_MODEL), lambda i: (i, 0)), pl.BlockSpec((tm, D_MODEL), lambda i: (i, 0))],
        out_shape=[jax.ShapeDtypeStruct((rows, D_MODEL), F32), jax.ShapeDtypeStruct((rows, D_MODEL), BF16)],
        compiler_params=_params(("arbitrary",)),
        name="mix",
    )(h, zqvo, yb0, yb1, zg, zg, x, pa, pb, wo, gffn)


def _ffn_kernel(x1_ref, xn_ref, wup_ref, wdn_ref, gfin_ref, y_ref):
    f = pl.program_id(1)
    last = pl.num_programs(1) - 1

    def mlp_part():
        u = jnp.maximum(_dot(xn_ref[...], wup_ref[...]), 0.0)
        return _dot((u * u).astype(BF16), wdn_ref[...])

    @pl.when(f == 0)
    def _():
        y_ref[...] = x1_ref[...] + mlp_part()

    @pl.when((f > 0) & (f < last))
    def _():
        y_ref[...] += mlp_part()

    @pl.when(f == last)
    def _():
        y_ref[...] = _rmsnorm(y_ref[...] + mlp_part(), gfin_ref[...])


def _ffn_call(x1, xn, wup, wdn, gfin, tm, tf):
    rows = x1.shape[0]
    assert D_FF // tf >= 2 and rows % tm == 0
    return pl.pallas_call(
        _ffn_kernel,
        grid=(rows // tm, D_FF // tf),
        in_specs=[
            pl.BlockSpec((tm, D_MODEL), lambda i, f: (i, 0)),
            pl.BlockSpec((tm, D_MODEL), lambda i, f: (i, 0)),
            pl.BlockSpec((D_MODEL, tf), lambda i, f: (0, f)),
            pl.BlockSpec((tf, D_MODEL), lambda i, f: (f, 0)),
            pl.BlockSpec((1, D_MODEL), lambda i, f: (0, 0)),
        ],
        out_specs=pl.BlockSpec((tm, D_MODEL), lambda i, f: (i, 0)),
        out_shape=jax.ShapeDtypeStruct((rows, D_MODEL), F32),
        compiler_params=_params(("arbitrary", "arbitrary")),
        name="ffn",
    )(x1, xn, wup, wdn, gfin)


def kernel(x_prompt, x_sample, state_mlstm_C, state_mlstm_n, state_mlstm_m, state_conv, meta_tokens,
           g_mix, w_in, b_in, w_conv, p_a, p_b, w_o, g_ffn, w_up, w_down, g_final):
    assert w_in.shape[0] == 1, "single-layer trunk"
    batch, seq, _ = x_prompt.shape
    dec_batch, dec_seq, _ = x_sample.shape
    assert dec_seq == SUBLANES and seq % 1024 == 0 and dec_batch % SEQ_PER_STEP == 0
    n_p, n_s = batch * seq, dec_batch * dec_seq

    wt = w_in[0].T
    bias = b_in[0]
    b1 = jnp.concatenate([bias[:OFF_GATES], bias[OFF_O:]])
    bg = bias[OFF_GATES:OFF_O][:, None]
    gmix = g_mix[0][None, :]

    xp = x_prompt.reshape(n_p, D_MODEL)
    xs = x_sample.reshape(n_s, D_MODEL)
    n_meta = meta_tokens.shape[0]

    sconv = state_conv[0].reshape(dec_batch, (CONV_W - 1) * C_WIDTH)
    xn, gch, grh, xnm, gcm, grm, yb0, cu6a, cu7a = _rms_conv_call(
        xp, xs, meta_tokens.astype(F32), gmix, wt, bg, b1[None, :], w_conv[0], sconv, dec_seq, seq, 512)
    zqv, kt, zm, ktm, w_up16, w_down16 = _qkvo_call(xn, xnm, wt, b1[None, :], (w_up[0], w_down[0]), 1024)
    m_tok = jnp.repeat(state_mlstm_m[0], dec_seq, axis=0).T
    zg, hs, c_s, n_sm, m_s = _gates_sample_call(
        xn, wt, b1[None, :], zqv, kt, gch, grh,
        jnp.broadcast_to(m_tok[:, :, None], m_tok.shape + (LANES,)), m_tok[:, None, :],
        state_mlstm_C[0], state_mlstm_n[0].reshape(dec_batch, M_HEADS * M_DK), n_p, dec_seq, 1152)
    yb1, cu6b, cu7b, p_a16, p_b16, w_o16 = _conv_call(
        xn, xnm, n_meta, wt, b1[None, :], w_conv[0], sconv, (p_a[0], p_b[0], w_o[0]), n_p, seq, 1024)

    hp, c_p, n_pr, m_p = _mlstm_prompt_call(zm, ktm, gcm, grm, zqv, kt, gch, grh, batch, seq, 2 * CHUNK)

    y_p, y_s = [
        _ffn_call(*_mix_call(h, x, zqv, zg, yb0, yb1, p_a16, p_b16, w_o16, g_ffn[0][None, :], row0, 512),
                  w_up16, w_down16, g_final[None, :], 512, 2048)
        for h, x, row0 in ((hp.reshape(n_p, M_WIDTH), xp, 0), (hs, xs, n_p))]

    per_seq = seq // SUBLANES
    first_s = n_p // SUBLANES
    pick = lambda rows: jnp.stack([jnp.concatenate([cu6a[rows], cu6b[rows]], axis=1),
                                   jnp.concatenate([cu7a[rows], cu7b[rows]], axis=1)], axis=1)
    cv_p = pick(slice(per_seq - 1, batch * per_seq, per_seq))
    cv_s = pick(slice(first_s, None))
    m_s = m_s[:, 0, dec_seq - 1::dec_seq].T
    return (y_p.reshape(batch, seq, D_MODEL), y_s.reshape(dec_batch, dec_seq, D_MODEL),
            c_p[None], n_pr[None], m_p[None, :, :, 0], cv_p[None],
            c_s[None], n_sm.reshape(dec_batch, M_HEADS, M_DK)[None], m_s[None], cv_s[None])
```

```python
import functools

import jax
import jax.numpy as jnp
from jax import lax
from jax.experimental import pallas as pl
from jax.experimental.pallas import tpu as pltpu

F32 = jnp.float32
BF16 = jnp.bfloat16

D_MODEL = 2048
CHUNK = 128
M_HEADS = 4
M_DK = 256
M_DV = 256
M_WIDTH = M_HEADS * M_DV
C_WIDTH = 1024
CONV_W = 3
D_FF = 4 * D_MODEL
EPS = 1e-6
N_GATES = 2 * M_HEADS
OFF_Q = 0
OFF_K = OFF_Q + M_HEADS * M_DK
OFF_V = OFF_K + M_HEADS * M_DK
OFF_GATES = OFF_V + M_WIDTH
OFF_O = OFF_GATES + N_GATES
OFF_U = OFF_O + M_WIDTH
OFF_C = OFF_U + C_WIDTH
OFF_B = OFF_C + C_WIDTH
OFF_GA = OFF_B + C_WIDTH

LANES = 128
SUBLANES = 8
VMEM_LIMIT_BYTES = 60000 * 1024

TN = 1024
Z_Q, Z_V, Z_SO = 0, 1, 2
Z_GA, Z_GB = 0, 2
ZG_COLS = 4 * TN
CQ = 256
AUG = M_DV + LANES
SEQ_PER_STEP = CHUNK // SUBLANES
G_IG, G_B, G_R = 0, 1, 2

NEG = -1e30
F32_LOWEST = -3e38
NT_DIMS = (((1,), (1,)), ((), ()))


def _params(semantics):
    return pltpu.CompilerParams(dimension_semantics=semantics, vmem_limit_bytes=VMEM_LIMIT_BYTES)


def _rmsnorm(x, g):
    y = x * lax.rsqrt(jnp.mean(x * x, axis=-1, keepdims=True) + EPS)
    return y * g


def _log_sigmoid(x):
    return jnp.minimum(x, 0.0) - jnp.log1p(jnp.exp(-jnp.abs(x)))


def _dot(a, b):
    return jnp.dot(a, b, preferred_element_type=F32)


def _dot_nt(a, b):
    return lax.dot_general(a, b, NT_DIMS, preferred_element_type=F32)


def _gate_prep(xn, wg_ref, bg_ref, blk, n_valid, gch_ref, grh_ref):
    tm = xn.shape[0]
    wg = wg_ref[...].astype(BF16)
    wg = jnp.concatenate([wg, jnp.zeros((LANES - N_GATES, wg.shape[1]), BF16)], axis=0)
    g = _dot_nt(wg, xn)[0:SUBLANES, :] + bg_ref[...]
    row = lax.broadcasted_iota(jnp.int32, (SUBLANES, tm), 0)
    lane = lax.broadcasted_iota(jnp.int32, (SUBLANES, tm), 1)
    a = jnp.where(row < M_HEADS, g, _log_sigmoid(g))
    if n_valid < tm:
        a = jnp.where(lane < n_valid, a, jnp.where(row < M_HEADS, NEG, 0.0))
    pos = lane & (blk - 1)
    n_steps_blk = blk if isinstance(blk, int) else LANES

    def scan(x, op, fill, reverse=False):
        shift = 1
        while shift < n_steps_blk:
            if reverse:
                x = op(x, jnp.where(pos < blk - shift, pltpu.roll(x, tm - shift, 1), fill))
            else:
                x = op(x, jnp.where(pos >= shift, pltpu.roll(x, shift, 1), fill))
            shift *= 2
        return x

    pre = scan(a, jnp.add, 0.0)
    suf = scan(a, jnp.add, 0.0, reverse=True) - a
    b_up = pltpu.roll(pre, M_HEADS, 0)
    m_in = b_up + scan(a - b_up, jnp.maximum, F32_LOWEST)
    for h in range(M_HEADS):
        grh_ref[h] = jnp.where(
            row == G_IG, pltpu.roll(a, (G_IG - h) % SUBLANES, 0),
            jnp.where(row == G_B, pltpu.roll(pre, (G_B - M_HEADS - h) % SUBLANES, 0),
                      jnp.where(row == G_R, pltpu.roll(suf, (G_R - M_HEADS - h) % SUBLANES, 0), 0.0)))
        for c in range(tm // LANES):
            cs = slice(c * LANES, (c + 1) * LANES)
            gch_ref[h, 0, cs, :] = jnp.broadcast_to(b_up[h:h + 1, cs], (LANES, LANES)).T
            gch_ref[h, 1, cs, :] = jnp.broadcast_to(m_in[h:h + 1, cs], (LANES, LANES)).T


S_Q, S_K, S_V, S_O = 0, 1, 2, 3


def _cast_blocks(pairs):
    for src_ref, dst_ref in pairs:
        dst_ref[...] = src_ref[...].astype(BF16)


def _cast_specs(w, n_blocks, step):
    rows, cols = w.shape
    blk = lambda *g: (jnp.minimum(step(*g), n_blocks - 1), 0)
    spec = pl.BlockSpec((rows // n_blocks, cols), blk)
    return spec, spec, jax.ShapeDtypeStruct(w.shape, BF16)


def _sigmoid(z):
    return 0.5 * jnp.tanh(0.5 * z) + 0.5


def _qkvo_kernel(n_cast, xn_ref, xnm_ref, wt_ref, b_ref, *refs):
    cast_in, (z_ref, kt_ref, zm_ref, ktm_ref) = refs[:n_cast], refs[n_cast:n_cast + 4]
    cast_out, (w_ref, bcol_ref) = refs[n_cast + 4:2 * n_cast + 4], refs[2 * n_cast + 4:]
    s = pl.program_id(0)
    casts = tuple(zip(cast_in, cast_out))
    k_scale = M_DK ** -0.5
    lane_tiles = lambda x, n: jnp.concatenate([x] * (n // LANES), axis=1)

    @pl.when(pl.program_id(1) == 0)
    def _():
        w_ref[...] = wt_ref[...].astype(BF16)

        @pl.when((s == S_Q) | (s == S_V))
        def _():
            zm_ref[...] = (_dot_nt(xnm_ref[...], w_ref[...]) + b_ref[...]).astype(BF16)

        @pl.when(s == S_K)
        def _():
            for c in range(TN // LANES):
                cs = slice(c * LANES, (c + 1) * LANES)
                bcol_ref[cs, :] = jnp.broadcast_to(b_ref[:, cs], (LANES, LANES)).T
            ktm = _dot_nt(w_ref[...], xnm_ref[...]) + lane_tiles(bcol_ref[...], xnm_ref.shape[0])
            ktm_ref[...] = (ktm * k_scale).astype(BF16)

    def z():
        return _dot_nt(xn_ref[...], w_ref[...]) + b_ref[...]

    @pl.when((s == S_Q) | (s == S_V))
    def _():
        _cast_blocks(casts)
        z_ref[...] = z().astype(BF16)

    @pl.when(s == S_K)
    def _():
        _cast_blocks(casts)
        kt = _dot_nt(w_ref[...], xn_ref[...]) + lane_tiles(bcol_ref[...], xn_ref.shape[0])
        kt_ref[...] = (kt * k_scale).astype(BF16)

    @pl.when(s == S_O)
    def _():
        _cast_blocks(casts)
        z_ref[...] = _sigmoid(z()).astype(BF16)


def _qkvo_call(xn, xnm, wt, b1, cast_ws, tm):
    t = xn.shape[0]
    assert t % tm == 0
    n_m = t // tm
    rows_m = xnm.shape[0]
    n_cast = 32
    assert n_cast <= 4 * n_m
    casts = [_cast_specs(w, n_cast, lambda s, m: s * n_m + m) for w in cast_ws]
    w_off = lambda s: jnp.where(s == S_O, OFF_O, s * TN)
    z_col = lambda s: jnp.where(s <= S_K, Z_Q, s - 1)
    z_row = lambda s, m: jnp.where(s == S_K, n_m - 1, m)
    kt_blk = lambda s, m: jnp.where(s == S_K, m, jnp.where(s < S_K, 0, n_m - 1))
    return pl.pallas_call(
        functools.partial(_qkvo_kernel, len(cast_ws)),
        grid=(4, n_m),
        in_specs=[
            pl.BlockSpec((tm, D_MODEL), lambda s, m: (m, 0)),
            pl.BlockSpec((rows_m, D_MODEL), lambda s, m: (0, 0)),
            pl.BlockSpec((pl.Element(TN), pl.Element(D_MODEL)),
                         lambda s, m: (pl.multiple_of(w_off(s), SUBLANES), 0)),
            pl.BlockSpec((1, TN), lambda s, m: (0, s)),
        ] + [cs[0] for cs in casts],
        out_specs=[
            pl.BlockSpec((tm, TN), lambda s, m: (z_row(s, m), z_col(s))),
            pl.BlockSpec((TN, tm), lambda s, m: (0, kt_blk(s, m))),
            pl.BlockSpec((rows_m, TN), lambda s, m: (0, (s >= S_V).astype(jnp.int32))),
            pl.BlockSpec((TN, rows_m), lambda s, m: (0, 0)),
        ] + [cs[1] for cs in casts],
        out_shape=[
            jax.ShapeDtypeStruct((t, 3 * TN), BF16),
            jax.ShapeDtypeStruct((TN, t), BF16),
            jax.ShapeDtypeStruct((rows_m, 2 * TN), BF16),
            jax.ShapeDtypeStruct((TN, rows_m), BF16),
        ] + [cs[2] for cs in casts],
        scratch_shapes=[pltpu.VMEM((TN, D_MODEL), BF16), pltpu.VMEM((TN, LANES), F32)],
        compiler_params=_params(("arbitrary", "arbitrary")),
        name="inproj_qkvo",
    )(xn, xnm, wt, b1, *cast_ws)


def _conv_prologue(m, is_prompt, tiles_per_seq, n_meta, xnm_ref, w_refs, b_refs, s0_ref, s1_ref, scratch,
                   make_xnm=None):
    wu_ref, wc_ref, wb_ref = w_refs
    bu_ref, bc_ref, _ = b_refs
    w3_ref, _, h1_ref, h2_ref, carry_ref, mtail_ref = scratch
    n_slab, tm, _ = h1_ref.shape
    n_seq = tm // SUBLANES
    slabs = [(k, slice(k * LANES, (k + 1) * LANES)) for k in range(n_slab)]
    seq_row = lambda r: pl.ds(r, n_seq, stride=SUBLANES)

    @pl.when(m == 0)
    def _():
        if make_xnm is not None:
            make_xnm()
        w3_ref[0] = wu_ref[...].astype(BF16)
        w3_ref[1] = wc_ref[...].astype(BF16)
        w3_ref[2] = wb_ref[...].astype(BF16)
        h1_ref[...] = jnp.zeros_like(h1_ref)
        h2_ref[...] = jnp.zeros_like(h2_ref)
        xnm = xnm_ref[...]
        cu_m = (_dot_nt(xnm, w3_ref[1]) + bc_ref[...]) * (_dot_nt(xnm, w3_ref[0]) + bu_ref[...])
        mtail_ref[...] = cu_m[n_meta - SUBLANES:n_meta, :]

    @pl.when(is_prompt)
    def _():
        first = (m % tiles_per_seq) == 0
        p6 = jnp.where(first, mtail_ref[6:7, :], carry_ref[6:7, :])
        p7 = jnp.where(first, mtail_ref[7:8, :], carry_ref[7:8, :])
        for k, ks in slabs:
            h2_ref[k, 0:1, :] = p6[:, ks]
            h2_ref[k, 1:2, :] = p7[:, ks]
            h1_ref[k, 0:1, :] = p7[:, ks]

    @pl.when(jnp.logical_not(is_prompt))
    def _():
        for k, ks in slabs:
            h2_ref[k, seq_row(0), :] = s0_ref[:, ks]
            h2_ref[k, seq_row(1), :] = s1_ref[:, ks]
            h1_ref[k, seq_row(0), :] = s1_ref[:, ks]


def _conv_main(xn, is_prompt, b_refs, wconv_ref, yb_ref, cu6_ref, cu7_ref, scratch):
    bu_ref, bc_ref, bb_ref = b_refs
    w3_ref, cu_ref, h1_ref, h2_ref, carry_ref, _ = scratch
    tm, cq = yb_ref.shape
    n_seq = tm // SUBLANES
    slabs = [(k, slice(k * LANES, (k + 1) * LANES)) for k in range(cq // LANES)]
    seq_row = lambda r: pl.ds(r, n_seq, stride=SUBLANES)
    proj = lambda k, b_ref: jnp.concatenate([_dot_nt(x, w3_ref[k]) for x in xn], axis=0) + b_ref[...]
    zu, zc, zb = proj(0, bu_ref), proj(1, bc_ref), proj(2, bb_ref)
    cu = zc * zu
    pos = lax.broadcasted_iota(jnp.int32, (tm, cq), 0) & jnp.where(is_prompt, tm - 1, SUBLANES - 1)
    h1 = jnp.concatenate([h1_ref[k] for k, _ in slabs], axis=1)
    h2 = jnp.concatenate([h2_ref[k] for k, _ in slabs], axis=1)
    x1 = jnp.where(pos >= 1, pltpu.roll(cu, 1, 0), h1)
    x2 = jnp.where(pos >= 2, pltpu.roll(cu, 2, 0), h2)
    w0, w1, w2 = wconv_ref[0:1, :], wconv_ref[1:2, :], wconv_ref[2:3, :]
    yb_ref[...] = (zb * ((w0 * x2 + w1 * x1) + w2 * cu)).astype(BF16)

    carry_ref[...] = cu[tm - SUBLANES:tm, :]
    for k, ks in slabs:
        cu_ref[k] = cu[:, ks]
        cu6_ref[:, ks] = cu_ref[k, seq_row(6), :]
        cu7_ref[:, ks] = cu_ref[k, seq_row(7), :]


def _conv_kernel(n_prompt_tiles, tiles_per_seq, n_meta, n_cast, xn_ref, xnm_ref, wu_ref, wc_ref, wb_ref,
                 bu_ref, bc_ref, bb_ref, wconv_ref, s0_ref, s1_ref, *refs):
    cast_in, (yb_ref, cu6_ref, cu7_ref) = refs[:n_cast], refs[n_cast:n_cast + 3]
    cast_out, scratch = refs[n_cast + 3:2 * n_cast + 3], refs[2 * n_cast + 3:]
    m = pl.program_id(1)
    is_prompt = m < n_prompt_tiles
    b_refs = (bu_ref, bc_ref, bb_ref)
    _conv_prologue(m, is_prompt, tiles_per_seq, n_meta, xnm_ref, (wu_ref, wc_ref, wb_ref), b_refs,
                   s0_ref, s1_ref, scratch)
    _cast_blocks(tuple(zip(cast_in, cast_out)))
    _conv_main([xn_ref[...]], is_prompt, b_refs, wconv_ref, yb_ref, cu6_ref, cu7_ref, scratch)


def _rms_conv_kernel(n_prompt_tiles, tiles_per_seq, dec_seq, xp_ref, xs_ref, xm_ref, g_ref, wg_ref, bg_ref,
                     wu_ref, wc_ref, wb_ref, bu_ref, bc_ref, bb_ref, wconv_ref, s0_ref, s1_ref,
                     xn_ref, gch_ref, grh_ref, xnm_ref, gchm_ref, grhm_ref, yb_ref, cu6_ref, cu7_ref, *scratch):
    m = pl.program_id(0)
    is_prompt = m < n_prompt_tiles
    n_meta = xm_ref.shape[0]
    b_refs = (bu_ref, bc_ref, bb_ref)

    def make_xnm():
        xm = jnp.concatenate([xm_ref[...], jnp.zeros((CHUNK - n_meta, D_MODEL), F32)], axis=0)
        xnm = _rmsnorm(xm, g_ref[...]).astype(BF16)
        xnm_ref[...] = xnm
        _gate_prep(xnm, wg_ref, bg_ref, CHUNK, n_meta, gchm_ref, grhm_ref)

    _conv_prologue(m, is_prompt, tiles_per_seq, n_meta, xnm_ref, (wu_ref, wc_ref, wb_ref), b_refs,
                   s0_ref, s1_ref, scratch, make_xnm)
    tm = xn_ref.shape[0]
    half = tm // 2
    xn_blocks = []
    for r in (slice(0, half), slice(half, tm)):
        x = jnp.where(is_prompt, xp_ref[r, :], xs_ref[r, :])
        xn_blocks.append(_rmsnorm(x, g_ref[...]).astype(BF16))
        xn_ref[r, :] = xn_blocks[-1]
    _gate_prep(jnp.concatenate(xn_blocks, axis=0), wg_ref, bg_ref, jnp.where(is_prompt, CHUNK, dec_seq), tm,
               gch_ref, grh_ref)
    _conv_main(xn_blocks, is_prompt, b_refs, wconv_ref, yb_ref, cu6_ref, cu7_ref, scratch)


def _conv_specs(c0, npt, tm, cm):
    n_seq = tm // SUBLANES
    n_slab = CQ // LANES
    chan = lambda *g: c0 + cm(*g)[0]
    tile = lambda *g: cm(*g)[1]
    w_spec = lambda off: pl.BlockSpec((pl.Element(CQ), pl.Element(D_MODEL)),
                                      lambda *g: (pl.multiple_of(off + chan(*g) * CQ, SUBLANES), 0))
    b_spec = lambda off: pl.BlockSpec((1, CQ), lambda *g: (0, (off - N_GATES) // CQ + chan(*g)))
    s_spec = lambda tok: pl.BlockSpec(
        (n_seq, CQ), lambda *g: (jnp.maximum(tile(*g) - npt, 0), tok * (C_WIDTH // CQ) + chan(*g)))
    in_specs = [w_spec(OFF_U), w_spec(OFF_C), w_spec(OFF_B), b_spec(OFF_U), b_spec(OFF_C), b_spec(OFF_B),
                pl.BlockSpec((CONV_W, CQ), lambda *g: (0, chan(*g))), s_spec(0), s_spec(1)]
    out_specs = [pl.BlockSpec((tm, CQ), lambda *g: (tile(*g), cm(*g)[0])),
                 pl.BlockSpec((n_seq, CQ), lambda *g: (tile(*g), cm(*g)[0])),
                 pl.BlockSpec((n_seq, CQ), lambda *g: (tile(*g), cm(*g)[0]))]
    scratch = [pltpu.VMEM((3, CQ, D_MODEL), BF16),
               pltpu.VMEM((n_slab, tm, LANES), F32),
               pltpu.VMEM((n_slab, tm, LANES), F32),
               pltpu.VMEM((n_slab, tm, LANES), F32),
               pltpu.VMEM((SUBLANES, CQ), F32),
               pltpu.VMEM((SUBLANES, CQ), F32)]
    return in_specs, out_specs, scratch


def _conv_out_shapes(t, n_blocks):
    return [jax.ShapeDtypeStruct((t, n_blocks * CQ), BF16),
            jax.ShapeDtypeStruct((t // SUBLANES, n_blocks * CQ), F32),
            jax.ShapeDtypeStruct((t // SUBLANES, n_blocks * CQ), F32)]


def _rms_conv_call(xp, xs, x_meta, g, wt, bg, b1, wconv, sconv, dec_seq, seq_len, tm):
    tp, ts = xp.shape[0], xs.shape[0]
    t = tp + ts
    assert tm & (tm - 1) == 0 and seq_len % tm == 0 and x_meta.shape[0] >= SUBLANES
    npt, nst = tp // tm, ts // tm
    conv_in, conv_out, scratch = _conv_specs(0, npt, tm, lambda m: (0, m))
    return pl.pallas_call(
        functools.partial(_rms_conv_kernel, npt, seq_len // tm, dec_seq),
        grid=(npt + nst,),
        in_specs=[
            pl.BlockSpec((tm, D_MODEL), lambda m: (jnp.minimum(m, npt - 1), 0)),
            pl.BlockSpec((tm, D_MODEL), lambda m: (jnp.maximum(m - npt, 0), 0)),
            pl.BlockSpec(x_meta.shape, lambda m: (0, 0)),
            pl.BlockSpec((1, D_MODEL), lambda m: (0, 0)),
            pl.BlockSpec((N_GATES, D_MODEL), lambda m: (OFF_GATES // N_GATES, 0)),
            pl.BlockSpec((N_GATES, 1), lambda m: (0, 0)),
        ] + conv_in,
        out_specs=[
            pl.BlockSpec((tm, D_MODEL), lambda m: (m, 0)),
            pl.BlockSpec((M_HEADS, 2, tm, LANES), lambda m: (0, 0, m, 0)),
            pl.BlockSpec((M_HEADS, SUBLANES, tm), lambda m: (0, 0, m)),
            pl.BlockSpec((CHUNK, D_MODEL), lambda m: (0, 0)),
            pl.BlockSpec((M_HEADS, 2, CHUNK, LANES), lambda m: (0, 0, 0, 0)),
            pl.BlockSpec((M_HEADS, SUBLANES, CHUNK), lambda m: (0, 0, 0)),
        ] + conv_out,
        out_shape=[
            jax.ShapeDtypeStruct((t, D_MODEL), BF16),
            jax.ShapeDtypeStruct((M_HEADS, 2, t, LANES), F32),
            jax.ShapeDtypeStruct((M_HEADS, SUBLANES, t), F32),
            jax.ShapeDtypeStruct((CHUNK, D_MODEL), BF16),
            jax.ShapeDtypeStruct((M_HEADS, 2, CHUNK, LANES), F32),
            jax.ShapeDtypeStruct((M_HEADS, SUBLANES, CHUNK), F32),
        ] + _conv_out_shapes(t, 1),
        scratch_shapes=scratch,
        compiler_params=_params(("arbitrary",)),
        name="rms_conv",
    )(xp, xs, x_meta, g, wt, bg, wt, wt, wt, b1, b1, b1, wconv, sconv, sconv)


def _conv_call(xn, xnm, n_meta, wt, b1, wconv, sconv, cast_ws, n_prompt_tokens, seq_len, tm):
    t = xn.shape[0]
    assert tm & (tm - 1) == 0 and seq_len % tm == 0 and n_meta >= SUBLANES
    npt = n_prompt_tokens // tm
    n_m = t // tm
    n_blocks = C_WIDTH // CQ - 1
    n_cast = 16
    assert n_cast <= n_blocks * n_m
    casts = [_cast_specs(w, n_cast, lambda c, m: c * n_m + m) for w in cast_ws]
    conv_in, conv_out, scratch = _conv_specs(1, npt, tm, lambda c, m: (c, m))
    return pl.pallas_call(
        functools.partial(_conv_kernel, npt, seq_len // tm, n_meta, len(cast_ws)),
        grid=(n_blocks, n_m),
        in_specs=[
            pl.BlockSpec((tm, D_MODEL), lambda c, m: (m, 0)),
            pl.BlockSpec(xnm.shape, lambda c, m: (0, 0)),
        ] + conv_in + [cs[0] for cs in casts],
        out_specs=conv_out + [cs[1] for cs in casts],
        out_shape=_conv_out_shapes(t, n_blocks) + [cs[2] for cs in casts],
        scratch_shapes=scratch,
        compiler_params=_params(("arbitrary", "arbitrary")),
        name="inproj_conv",
    )(xn, xnm, wt, wt, wt, b1, b1, b1, wconv, sconv, sconv, *cast_ws)


def _prompt_head(q, kt, v, gr, gc, ca, m_st):
    l = q.shape[0]
    assert l == LANES
    ig_row, b_row, r_row = gr[G_IG:G_IG + 1, :], gr[G_B:G_B + 1, :], gr[G_R:G_R + 1, :]
    b_col = gc[0]
    v_aug = jnp.concatenate([v, jnp.ones((l, LANES), BF16)], axis=1)
    t_idx = lax.broadcasted_iota(jnp.int32, (l, l), 0)
    s_idx = lax.broadcasted_iota(jnp.int32, (l, l), 1)
    logw = jnp.where(s_idx <= t_idx, b_col - b_row + ig_row, NEG)
    inter = b_col + m_st
    m_t = jnp.maximum(inter, gc[1])
    w_inter = jnp.exp(inter - m_t)
    s = _dot(q, kt) * jnp.exp(logw - m_t)
    nd = (jnp.concatenate([w_inter] * (AUG // LANES), axis=1) * _dot(q, ca.astype(BF16))
          + _dot(s.astype(BF16), v_aug))
    rcp = 1.0 / jnp.maximum(jnp.abs(nd[:, M_DV:AUG]), jnp.exp(-m_t))
    h = nd[:, 0:M_DV] * jnp.concatenate([rcp] * (M_DV // LANES), axis=1)
    m_new = m_t[l - 1:l, 0:1]
    decay = jnp.exp(b_col[l - 1:l, 0:1] + m_st - m_new)
    kwt = (kt.astype(F32) * jnp.exp(r_row + ig_row - m_new)).astype(BF16)
    return h, decay * ca + _dot(kwt, v_aug), m_new


def _mlstm_prompt_kernel(batch, qm_ref, ktm_ref, vm_ref, gcm_ref, grm_ref, *refs):
    ins, (h_ref, c_ref, n_ref, m_ref, ca_ref) = refs[:5 * batch], refs[5 * batch:]
    heads = [(hd, slice(hd * M_DK, (hd + 1) * M_DK)) for hd in range(M_HEADS)]

    @pl.when(pl.program_id(0) == 0)
    def _():
        for hd, sl in heads:
            _, ca_new, m_new = _prompt_head(qm_ref[:, sl], ktm_ref[sl, :], vm_ref[:, sl], grm_ref[hd],
                                            gcm_ref[hd], jnp.zeros((M_DK, AUG), F32), jnp.zeros((1, 1), F32))
            for b in range(batch):
                ca_ref[b, hd] = ca_new
                m_ref[b, hd:hd + 1, :] = jnp.broadcast_to(m_new, (1, LANES))

    for k in range(h_ref.shape[1] // CHUNK):
        tok = slice(k * CHUNK, (k + 1) * CHUNK)
        for b in range(batch):
            q_ref, kt_ref, v_ref, gc_ref, gr_ref = ins[5 * b:5 * b + 5]
            for hd, sl in heads:
                h, ca_new, m_new = _prompt_head(q_ref[tok, sl], kt_ref[sl, tok], v_ref[tok, sl],
                                                gr_ref[hd, :, tok], gc_ref[hd, :, tok, :],
                                                ca_ref[b, hd], m_ref[b, hd:hd + 1, 0:1])
                ca_ref[b, hd] = ca_new
                m_ref[b, hd:hd + 1, :] = jnp.broadcast_to(m_new, (1, LANES))
                h_ref[b, tok, sl] = h.astype(h_ref.dtype)

    @pl.when(pl.program_id(0) == pl.num_programs(0) - 1)
    def _():
        for b in range(batch):
            for hd, _ in heads:
                ca = ca_ref[b, hd]
                c_ref[b, hd] = ca[:, 0:M_DV]
                n_t = ca[:, M_DV:AUG]
                n_ref[b, hd:hd + 1, :] = jnp.concatenate(
                    [n_t[k * LANES:(k + 1) * LANES, :].T[0:1, :] for k in range(M_DK // LANES)], axis=1)


def _mlstm_prompt_call(zm, ktm, gcm, grm, z, kt, gch, grh, batch, seq, tok):
    assert tok % CHUNK == 0 and seq % tok == 0
    nc = seq // tok
    per_prompt_specs, per_prompt_args = [], []
    for b in range(batch):
        row = functools.partial(lambda b, c: b * nc + c, b)
        per_prompt_specs += [
            pl.BlockSpec((tok, M_WIDTH), lambda c, row=row: (row(c), Z_Q)),
            pl.BlockSpec((M_WIDTH, tok), lambda c, row=row: (0, row(c))),
            pl.BlockSpec((tok, M_WIDTH), lambda c, row=row: (row(c), Z_V)),
            pl.BlockSpec((M_HEADS, 2, tok, LANES), lambda c, row=row: (0, 0, row(c), 0)),
            pl.BlockSpec((M_HEADS, SUBLANES, tok), lambda c, row=row: (0, 0, row(c))),
        ]
        per_prompt_args += [z, kt, z, gch, grh]
    return pl.pallas_call(
        functools.partial(_mlstm_prompt_kernel, batch),
        grid=(nc,),
        in_specs=[
            pl.BlockSpec((CHUNK, M_WIDTH), lambda c: (0, 0)),
            pl.BlockSpec((M_WIDTH, CHUNK), lambda c: (0, 0)),
            pl.BlockSpec((CHUNK, M_WIDTH), lambda c: (0, 1)),
            pl.BlockSpec((M_HEADS, 2, CHUNK, LANES), lambda c: (0, 0, 0, 0)),
            pl.BlockSpec((M_HEADS, SUBLANES, CHUNK), lambda c: (0, 0, 0)),
        ] + per_prompt_specs,
        out_specs=[
            pl.BlockSpec((batch, tok, M_WIDTH), lambda c: (0, c, 0)),
            pl.BlockSpec((batch, M_HEADS, M_DK, M_DV), lambda c: (0, 0, 0, 0)),
            pl.BlockSpec((batch, M_HEADS, M_DK), lambda c: (0, 0, 0)),
            pl.BlockSpec((batch, M_HEADS, LANES), lambda c: (0, 0, 0)),
        ],
        out_shape=[
            jax.ShapeDtypeStruct((batch, seq, M_WIDTH), BF16),
            jax.ShapeDtypeStruct((batch, M_HEADS, M_DK, M_DV), F32),
            jax.ShapeDtypeStruct((batch, M_HEADS, M_DK), F32),
            jax.ShapeDtypeStruct((batch, M_HEADS, LANES), F32),
        ],
        scratch_shapes=[pltpu.VMEM((batch, M_HEADS, M_DK, AUG), F32)],
        compiler_params=_params(("arbitrary",)),
        name="mlstm_prompt",
    )(zm, ktm, zm, gcm, grm, *per_prompt_args)


def _group_max(x, size):
    n = x.shape[-1]
    lane = lax.broadcasted_iota(jnp.int32, x.shape, x.ndim - 1)
    k = 1
    while k < size:
        partner = jnp.where((lane & k) == 0, pltpu.roll(x, n - k, x.ndim - 1), pltpu.roll(x, k, x.ndim - 1))
        x = jnp.maximum(x, partner)
        k *= 2
    return x


def _mlstm_sample_kernel(dec_seq, q_ref, kt_ref, v_ref, gc_ref, gr_ref, mcol_ref, mrow_ref, cin_ref, nin_ref,
                         h_ref, c_ref, n_ref, m_ref):
    l = q_ref.shape[0]
    n_seq = l // dec_seq
    shift = dec_seq.bit_length() - 1
    q, kt, v = q_ref[...], kt_ref[...], v_ref[...]
    assert l == LANES
    gr, gc = gr_ref[0], gc_ref[0]
    ig_row, b_row, r_row = gr[G_IG:G_IG + 1, :], gr[G_B:G_B + 1, :], gr[G_R:G_R + 1, :]
    b_col = gc[0]
    m_col, m_row = mcol_ref[0], mrow_ref[0]

    t_idx = lax.broadcasted_iota(jnp.int32, (l, l), 0)
    s_idx = lax.broadcasted_iota(jnp.int32, (l, l), 1)
    t_seq = t_idx >> shift
    logw = jnp.where((t_seq == (s_idx >> shift)) & (s_idx <= t_idx), b_col - b_row + ig_row, NEG)
    inter = b_col + m_col
    m_t = jnp.maximum(inter, gc[1])
    w_inter = jnp.exp(inter - m_t)
    s = _dot(q, kt) * jnp.exp(logw - m_t)

    e_row = r_row + ig_row
    b_last = b_row + r_row
    m_new = jnp.maximum(b_last + m_row, _group_max(e_row, dec_seq))
    decay = jnp.exp(b_last + m_row - m_new)
    kwt = (kt.astype(F32) * jnp.exp(e_row - m_new)).astype(BF16)

    seq_lane = s_idx == t_seq
    seq_onehot = jnp.where(seq_lane, 1.0, 0.0).astype(BF16)
    pa = _dot(s.astype(BF16), jnp.concatenate([v, seq_onehot], axis=1))

    n_rows = jnp.concatenate([nin_ref[...], jnp.zeros((LANES - n_seq, M_DK), F32)], axis=0)
    n_t = jnp.concatenate([n_rows[:, 0:LANES].T, n_rows[:, LANES:2 * LANES].T], axis=0)
    lane_k = lax.broadcasted_iota(jnp.int32, (M_DK, LANES), 1)
    lane_1 = lax.broadcasted_iota(jnp.int32, (1, LANES), 1)
    qf = q.astype(F32)
    qca_rows = []
    decay_seq = jnp.zeros((1, LANES), F32)
    for i in range(n_seq):
        decay_i = decay[:, i * dec_seq:i * dec_seq + 1]
        decay_seq = jnp.where(lane_1 == i, decay_i, decay_seq)
        c_i = cin_ref[i, 0]
        ca = jnp.concatenate([c_i, jnp.where(lane_k == i, n_t, 0.0)], axis=1)
        rows = slice(i * dec_seq, (i + 1) * dec_seq)
        qca_rows.append(_dot(qf[rows, :].astype(BF16), ca.astype(BF16)))
        kw_i = jnp.where((lane_k >> shift) == i, kwt, jnp.zeros_like(kwt))
        c_ref[i, 0] = decay_i * c_i + _dot(kw_i, v)
    n_t_new = decay_seq * n_t + _dot(kwt, seq_onehot)
    nd = jnp.concatenate([w_inter] * (AUG // LANES), axis=1) * jnp.concatenate(qca_rows, axis=0) + pa
    den = jnp.sum(jnp.where(seq_lane, nd[:, M_DV:AUG], 0.0), axis=-1, keepdims=True)
    rcp = 1.0 / jnp.maximum(jnp.abs(den), jnp.exp(-m_t[:, 0:1]))
    h_ref[...] = (nd[:, 0:M_DV] * rcp).astype(h_ref.dtype)
    n_ref[...] = jnp.concatenate([n_t_new[0:LANES, :].T, n_t_new[LANES:2 * LANES, :].T], axis=1)[0:n_seq, :]
    m_ref[0] = jnp.broadcast_to(m_new, (SUBLANES, l))


def _gates_sample_kernel(dec_seq, xn_ref, wt_ref, b_ref, *refs):
    sample_in, z_ref, sample_out, w_ref = refs[:9], refs[9], refs[10:14], refs[14]

    @pl.when(pl.program_id(1) == 0)
    def _():
        w_ref[...] = wt_ref[...].astype(BF16)

    _mlstm_sample_kernel(dec_seq, *sample_in, *sample_out)
    z_ref[...] = _sigmoid(_dot_nt(xn_ref[...], w_ref[...]) + b_ref[...]).astype(BF16)


def _gates_sample_call(xn, wt, b1, zqv, kt, gch, grh, m_col, m_row, c_in, n_in, row0, dec_seq, tm):
    t = xn.shape[0]
    assert t % tm == 0
    n_m = t // tm
    nb = c_in.shape[0]
    l = SEQ_PER_STEP * dec_seq
    blk0 = row0 // l
    n_steps = 2 * D_MODEL // TN
    n_guest = (nb // SEQ_PER_STEP) * M_HEADS
    assert n_guest <= n_steps * n_m
    w_off = lambda s: OFF_GA + s * TN
    b_blk = lambda s: (OFF_GA - N_GATES) // TN + s
    z_col = lambda s: s
    guest = lambda s, m: jnp.minimum(s * n_m + m, n_guest - 1)
    gi = lambda s, m: guest(s, m) // M_HEADS
    gh = lambda s, m: guest(s, m) % M_HEADS
    return pl.pallas_call(
        functools.partial(_gates_sample_kernel, dec_seq),
        grid=(n_steps, n_m),
        in_specs=[
            pl.BlockSpec((tm, D_MODEL), lambda s, m: (m, 0)),
            pl.BlockSpec((pl.Element(TN), pl.Element(D_MODEL)),
                         lambda s, m: (pl.multiple_of(w_off(s), SUBLANES), 0)),
            pl.BlockSpec((1, TN), lambda s, m: (0, b_blk(s))),
            pl.BlockSpec((l, M_DK), lambda s, m: (blk0 + gi(s, m), Z_Q * M_HEADS + gh(s, m))),
            pl.BlockSpec((M_DK, l), lambda s, m: (gh(s, m), blk0 + gi(s, m))),
            pl.BlockSpec((l, M_DV), lambda s, m: (blk0 + gi(s, m), Z_V * M_HEADS + gh(s, m))),
            pl.BlockSpec((1, 2, l, LANES), lambda s, m: (gh(s, m), 0, blk0 + gi(s, m), 0)),
            pl.BlockSpec((1, SUBLANES, l), lambda s, m: (gh(s, m), 0, blk0 + gi(s, m))),
            pl.BlockSpec((1, l, LANES), lambda s, m: (gh(s, m), gi(s, m), 0)),
            pl.BlockSpec((1, 1, l), lambda s, m: (gh(s, m), 0, gi(s, m))),
            pl.BlockSpec((SEQ_PER_STEP, 1, M_DK, M_DV), lambda s, m: (gi(s, m), gh(s, m), 0, 0)),
            pl.BlockSpec((SEQ_PER_STEP, M_DK), lambda s, m: (gi(s, m), gh(s, m))),
        ],
        out_specs=[
            pl.BlockSpec((tm, TN), lambda s, m: (m, z_col(s))),
            pl.BlockSpec((l, M_DV), lambda s, m: (gi(s, m), gh(s, m))),
            pl.BlockSpec((SEQ_PER_STEP, 1, M_DK, M_DV), lambda s, m: (gi(s, m), gh(s, m), 0, 0)),
            pl.BlockSpec((SEQ_PER_STEP, M_DK), lambda s, m: (gi(s, m), gh(s, m))),
            pl.BlockSpec((1, SUBLANES, l), lambda s, m: (gh(s, m), 0, gi(s, m))),
        ],
        out_shape=[
            jax.ShapeDtypeStruct((t, ZG_COLS), BF16),
            jax.ShapeDtypeStruct((nb * dec_seq, M_WIDTH), BF16),
            jax.ShapeDtypeStruct(c_in.shape, F32),
            jax.ShapeDtypeStruct(n_in.shape, F32),
            jax.ShapeDtypeStruct((M_HEADS, SUBLANES, nb * dec_seq), F32),
        ],
        scratch_shapes=[pltpu.VMEM((TN, D_MODEL), BF16)],
        compiler_params=_params(("arbitrary", "arbitrary")),
        name="inproj_gates_mlstm_sample",
    )(xn, wt, b1, zqv, kt, zqv, gch, grh, m_col, m_row, c_in, n_in)


def _mix_kernel(h_ref, so_ref, yb0_ref, yb1_ref, ga_ref, gb_ref, x_ref, pa_ref, pb_ref, wo_ref, gffn_ref,
                o_ref, on_ref):
    h_a = (so_ref[...].astype(F32) * h_ref[...].astype(F32)).astype(BF16)
    a = _dot(h_a, pa_ref[...])
    b = _dot(jnp.concatenate([yb0_ref[...], yb1_ref[...]], axis=1), pb_ref[...])
    merged = ga_ref[...].astype(F32) * a + gb_ref[...].astype(F32) * b
    x1 = x_ref[...] + _dot(merged.astype(BF16), wo_ref[...])
    o_ref[...] = x1
    on_ref[...] = _rmsnorm(x1, gffn_ref[...]).astype(BF16)


def _mix_call(h, x, zqvo, zg, yb0, yb1, pa, pb, wo, gffn, row0, tm):
    rows = x.shape[0]
    assert row0 % tm == 0 and rows % tm == 0
    blk0 = row0 // tm
    deep = pl.Buffered(3)
    stream_specs = [
        pl.BlockSpec((tm, M_WIDTH), lambda i: (i, 0)),
        pl.BlockSpec((tm, M_WIDTH), lambda i: (blk0 + i, Z_SO)),
        pl.BlockSpec((tm, yb0.shape[1]), lambda i: (blk0 + i, 0)),
        pl.BlockSpec((tm, yb1.shape[1]), lambda i: (blk0 + i, 0)),
        pl.BlockSpec((tm, D_MODEL), lambda i: (blk0 + i, Z_GA // 2)),
        pl.BlockSpec((tm, D_MODEL), lambda i: (blk0 + i, Z_GB // 2)),
        pl.BlockSpec((tm, D_MODEL), lambda i: (i, 0), pipeline_mode=deep),
    ]
    out_specs = [pl.BlockSpec((tm, D_MODEL), lambda i: (i, 0)), pl.BlockSpec((tm, D_MODEL), lambda i: (i, 0))]

    def pipelined(h_hbm, so_hbm, yb0_hbm, yb1_hbm, ga_hbm, gb_hbm, x_hbm, pa_ref, pb_ref, wo_ref, gffn_ref,
                  o_hbm, on_hbm):
        def body(h_ref, so_ref, yb0_ref, yb1_ref, ga_ref, gb_ref, x_ref, o_ref, on_ref):
            _mix_kernel(h_ref, so_ref, yb0_ref, yb1_ref, ga_ref, gb_ref, x_ref, pa_ref, pb_ref, wo_ref, gffn_ref,
                        o_ref, on_ref)
        pltpu.emit_pipeline(body, grid=(rows // tm,), in_specs=stream_specs, out_specs=out_specs)(
            h_hbm, so_hbm, yb0_hbm, yb1_hbm, ga_hbm, gb_hbm, x_hbm, o_hbm, on_hbm)

    hbm = pl.BlockSpec(memory_space=pl.ANY)
    vmem = pl.BlockSpec(memory_space=pltpu.VMEM)
    return pl.pallas_call(
        pipelined,
        in_specs=[hbm] * 7 + [vmem] * 4,
        out_specs=[hbm, hbm],
        out_shape=[jax.ShapeDtypeStruct((rows, D_MODEL), F32), jax.ShapeDtypeStruct((rows, D_MODEL), BF16)],
        compiler_params=pltpu.CompilerParams(vmem_limit_bytes=VMEM_LIMIT_BYTES),
        name="mix",
    )(h, zqvo, yb0, yb1, zg, zg, x, pa, pb, wo, gffn)


def _ffn_kernel(x1_ref, xn_ref, wup_ref, wdn_ref, gfin_ref, y_ref):
    f = pl.program_id(1)
    last = pl.num_programs(1) - 1

    def mlp_part():
        u = jnp.maximum(_dot(xn_ref[...], wup_ref[...]), 0.0)
        return _dot((u * u).astype(BF16), wdn_ref[...])

    @pl.when(f == 0)
    def _():
        y_ref[...] = x1_ref[...] + mlp_part()

    @pl.when((f > 0) & (f < last))
    def _():
        y_ref[...] += mlp_part()

    @pl.when(f == last)
    def _():
        y_ref[...] = _rmsnorm(y_ref[...] + mlp_part(), gfin_ref[...])


def _ffn_call(x1, xn, wup, wdn, gfin, tm, tf):
    rows = x1.shape[0]
    assert D_FF // tf >= 2 and rows % tm == 0
    return pl.pallas_call(
        _ffn_kernel,
        grid=(rows // tm, D_FF // tf),
        in_specs=[
            pl.BlockSpec((tm, D_MODEL), lambda i, f: (i, 0)),
            pl.BlockSpec((tm, D_MODEL), lambda i, f: (i, 0)),
            pl.BlockSpec((D_MODEL, tf), lambda i, f: (0, f)),
            pl.BlockSpec((tf, D_MODEL), lambda i, f: (f, 0)),
            pl.BlockSpec((1, D_MODEL), lambda i, f: (0, 0)),
        ],
        out_specs=pl.BlockSpec((tm, D_MODEL), lambda i, f: (i, 0)),
        out_shape=jax.ShapeDtypeStruct((rows, D_MODEL), F32),
        compiler_params=_params(("arbitrary", "arbitrary")),
        name="ffn",
    )(x1, xn, wup, wdn, gfin)


def kernel(x_prompt, x_sample, state_mlstm_C, state_mlstm_n, state_mlstm_m, state_conv, meta_tokens,
           g_mix, w_in, b_in, w_conv, p_a, p_b, w_o, g_ffn, w_up, w_down, g_final):
    assert w_in.shape[0] == 1, "single-layer trunk"
    batch, seq, _ = x_prompt.shape
    dec_batch, dec_seq, _ = x_sample.shape
    assert dec_seq == SUBLANES and seq % 1024 == 0 and dec_batch % SEQ_PER_STEP == 0
    n_p, n_s = batch * seq, dec_batch * dec_seq

    wt = w_in[0].T
    bias = b_in[0]
    b1 = jnp.concatenate([bias[:OFF_GATES], bias[OFF_O:]])
    bg = bias[OFF_GATES:OFF_O][:, None]
    gmix = g_mix[0][None, :]

    xp = x_prompt.reshape(n_p, D_MODEL)
    xs = x_sample.reshape(n_s, D_MODEL)
    n_meta = meta_tokens.shape[0]

    sconv = state_conv[0].reshape(dec_batch, (CONV_W - 1) * C_WIDTH)
    xn, gch, grh, xnm, gcm, grm, yb0, cu6a, cu7a = _rms_conv_call(
        xp, xs, meta_tokens.astype(F32), gmix, wt, bg, b1[None, :], w_conv[0], sconv, dec_seq, seq, 512)
    zqv, kt, zm, ktm, w_up16, w_down16 = _qkvo_call(xn, xnm, wt, b1[None, :], (w_up[0], w_down[0]), 1024)
    m_tok = jnp.repeat(state_mlstm_m[0], dec_seq, axis=0).T
    zg, hs, c_s, n_sm, m_s = _gates_sample_call(
        xn, wt, b1[None, :], zqv, kt, gch, grh,
        jnp.broadcast_to(m_tok[:, :, None], m_tok.shape + (LANES,)), m_tok[:, None, :],
        state_mlstm_C[0], state_mlstm_n[0].reshape(dec_batch, M_HEADS * M_DK), n_p, dec_seq, 1152)
    yb1, cu6b, cu7b, p_a16, p_b16, w_o16 = _conv_call(
        xn, xnm, n_meta, wt, b1[None, :], w_conv[0], sconv, (p_a[0], p_b[0], w_o[0]), n_p, seq, 1024)

    hp, c_p, n_pr, m_p = _mlstm_prompt_call(zm, ktm, gcm, grm, zqv, kt, gch, grh, batch, seq, 2 * CHUNK)

    y_p, y_s = [
        _ffn_call(*_mix_call(h, x, zqv, zg, yb0, yb1, p_a16, p_b16, w_o16, g_ffn[0][None, :], row0, 512),
                  w_up16, w_down16, g_final[None, :], 512, 2048)
        for h, x, row0 in ((hp.reshape(n_p, M_WIDTH), xp, 0), (hs, xs, n_p))]

    per_seq = seq // SUBLANES
    first_s = n_p // SUBLANES
    pick = lambda rows: jnp.stack([jnp.concatenate([cu6a[rows], cu6b[rows]], axis=1),
                                   jnp.concatenate([cu7a[rows], cu7b[rows]], axis=1)], axis=1)
    cv_p = pick(slice(per_seq - 1, batch * per_seq, per_seq))
    cv_s = pick(slice(first_s, None))
    m_s = m_s[:, 0, dec_seq - 1::dec_seq].T
    return (y_p.reshape(batch, seq, D_MODEL), y_s.reshape(dec_batch, dec_seq, D_MODEL),
            c_p[None], n_pr[None], m_p[None, :, :, 0], cv_p[None],
            c_s[None], n_sm.reshape(dec_batch, M_HEADS, M_DK)[None], m_s[None], cv_s[None])
```

```python
import functools

import jax
import jax.numpy as jnp
from jax import lax
from jax.experimental import pallas as pl
from jax.experimental.pallas import tpu as pltpu

F32 = jnp.float32
BF16 = jnp.bfloat16

D_MODEL = 2048
CHUNK = 128
M_HEADS = 4
M_DK = 256
M_DV = 256
M_WIDTH = M_HEADS * M_DV
C_WIDTH = 1024
CONV_W = 3
D_FF = 4 * D_MODEL
EPS = 1e-6
N_GATES = 2 * M_HEADS
OFF_Q = 0
OFF_K = OFF_Q + M_HEADS * M_DK
OFF_V = OFF_K + M_HEADS * M_DK
OFF_GATES = OFF_V + M_WIDTH
OFF_O = OFF_GATES + N_GATES
OFF_U = OFF_O + M_WIDTH
OFF_C = OFF_U + C_WIDTH
OFF_B = OFF_C + C_WIDTH
OFF_GA = OFF_B + C_WIDTH

LANES = 128
SUBLANES = 8
VMEM_LIMIT_BYTES = 60000 * 1024

TN = 1024
Z_Q, Z_V, Z_SO = 0, 1, 2
Z_GA, Z_GB = 0, 2
ZG_COLS = 4 * TN
CQ = 256
AUG = M_DV + LANES
SEQ_PER_STEP = CHUNK // SUBLANES
G_IG, G_B, G_R = 0, 1, 2

NEG = -1e30
F32_LOWEST = -3e38
NT_DIMS = (((1,), (1,)), ((), ()))


def _params(semantics):
    return pltpu.CompilerParams(dimension_semantics=semantics, vmem_limit_bytes=VMEM_LIMIT_BYTES)


def _rmsnorm(x, g):
    y = x * lax.rsqrt(jnp.mean(x * x, axis=-1, keepdims=True) + EPS)
    return y * g


def _log_sigmoid(x):
    return jnp.minimum(x, 0.0) - jnp.log1p(jnp.exp(-jnp.abs(x)))


def _dot(a, b):
    return jnp.dot(a, b, preferred_element_type=F32)


def _dot_nt(a, b):
    return lax.dot_general(a, b, NT_DIMS, preferred_element_type=F32)


def _gate_prep(xn, wg_ref, bg_ref, blk, n_valid, gch_ref, grh_ref):
    tm = xn.shape[0]
    wg = wg_ref[...].astype(BF16)
    wg = jnp.concatenate([wg, jnp.zeros((LANES - N_GATES, wg.shape[1]), BF16)], axis=0)
    g = _dot_nt(wg, xn)[0:SUBLANES, :] + bg_ref[...]
    row = lax.broadcasted_iota(jnp.int32, (SUBLANES, tm), 0)
    lane = lax.broadcasted_iota(jnp.int32, (SUBLANES, tm), 1)
    a = jnp.where(row < M_HEADS, g, _log_sigmoid(g))
    if n_valid < tm:
        a = jnp.where(lane < n_valid, a, jnp.where(row < M_HEADS, NEG, 0.0))
    pos = lane & (blk - 1)
    n_steps_blk = blk if isinstance(blk, int) else LANES

    def scan(x, op, fill, reverse=False):
        shift = 1
        while shift < n_steps_blk:
            if reverse:
                x = op(x, jnp.where(pos < blk - shift, pltpu.roll(x, tm - shift, 1), fill))
            else:
                x = op(x, jnp.where(pos >= shift, pltpu.roll(x, shift, 1), fill))
            shift *= 2
        return x

    pre = scan(a, jnp.add, 0.0)
    suf = scan(a, jnp.add, 0.0, reverse=True) - a
    b_up = pltpu.roll(pre, M_HEADS, 0)
    m_in = b_up + scan(a - b_up, jnp.maximum, F32_LOWEST)
    for h in range(M_HEADS):
        grh_ref[h] = jnp.where(
            row == G_IG, pltpu.roll(a, (G_IG - h) % SUBLANES, 0),
            jnp.where(row == G_B, pltpu.roll(pre, (G_B - M_HEADS - h) % SUBLANES, 0),
                      jnp.where(row == G_R, pltpu.roll(suf, (G_R - M_HEADS - h) % SUBLANES, 0), 0.0)))
        for c in range(tm // LANES):
            cs = slice(c * LANES, (c + 1) * LANES)
            gch_ref[h, 0, cs, :] = jnp.broadcast_to(b_up[h:h + 1, cs], (LANES, LANES)).T
            gch_ref[h, 1, cs, :] = jnp.broadcast_to(m_in[h:h + 1, cs], (LANES, LANES)).T


S_Q, S_K, S_V, S_O = 0, 1, 2, 3


def _cast_blocks(pairs):
    for src_ref, dst_ref in pairs:
        dst_ref[...] = src_ref[...].astype(BF16)


def _cast_specs(w, n_blocks, step):
    rows, cols = w.shape
    blk = lambda *g: (jnp.minimum(step(*g), n_blocks - 1), 0)
    spec = pl.BlockSpec((rows // n_blocks, cols), blk)
    return spec, spec, jax.ShapeDtypeStruct(w.shape, BF16)


def _sigmoid(z):
    return 0.5 * jnp.tanh(0.5 * z) + 0.5


def _qkvo_kernel(n_cast, xn_ref, xnm_ref, wt_ref, b_ref, *refs):
    cast_in, (z_ref, kt_ref, zm_ref, ktm_ref) = refs[:n_cast], refs[n_cast:n_cast + 4]
    cast_out, (w_ref, bcol_ref) = refs[n_cast + 4:2 * n_cast + 4], refs[2 * n_cast + 4:]
    s = pl.program_id(0)
    casts = tuple(zip(cast_in, cast_out))
    k_scale = M_DK ** -0.5
    lane_tiles = lambda x, n: jnp.concatenate([x] * (n // LANES), axis=1)

    @pl.when(pl.program_id(1) == 0)
    def _():
        w_ref[...] = wt_ref[...].astype(BF16)

        @pl.when((s == S_Q) | (s == S_V))
        def _():
            zm_ref[...] = (_dot_nt(xnm_ref[...], w_ref[...]) + b_ref[...]).astype(BF16)

        @pl.when(s == S_K)
        def _():
            for c in range(TN // LANES):
                cs = slice(c * LANES, (c + 1) * LANES)
                bcol_ref[cs, :] = jnp.broadcast_to(b_ref[:, cs], (LANES, LANES)).T
            ktm = _dot_nt(w_ref[...], xnm_ref[...]) + lane_tiles(bcol_ref[...], xnm_ref.shape[0])
            ktm_ref[...] = (ktm * k_scale).astype(BF16)

    def z():
        return _dot_nt(xn_ref[...], w_ref[...]) + b_ref[...]

    @pl.when((s == S_Q) | (s == S_V))
    def _():
        _cast_blocks(casts)
        z_ref[...] = z().astype(BF16)

    @pl.when(s == S_K)
    def _():
        _cast_blocks(casts)
        kt = _dot_nt(w_ref[...], xn_ref[...]) + lane_tiles(bcol_ref[...], xn_ref.shape[0])
        kt_ref[...] = (kt * k_scale).astype(BF16)

    @pl.when(s == S_O)
    def _():
        _cast_blocks(casts)
        z_ref[...] = _sigmoid(z()).astype(BF16)


def _qkvo_call(xn, xnm, wt, b1, cast_ws, tm):
    t = xn.shape[0]
    assert t % tm == 0
    n_m = t // tm
    rows_m = xnm.shape[0]
    n_cast = 32
    assert n_cast <= 4 * n_m
    casts = [_cast_specs(w, n_cast, lambda s, m: s * n_m + m) for w in cast_ws]
    w_off = lambda s: jnp.where(s == S_O, OFF_O, s * TN)
    z_col = lambda s: jnp.where(s <= S_K, Z_Q, s - 1)
    z_row = lambda s, m: jnp.where(s == S_K, n_m - 1, m)
    kt_blk = lambda s, m: jnp.where(s == S_K, m, jnp.where(s < S_K, 0, n_m - 1))
    return pl.pallas_call(
        functools.partial(_qkvo_kernel, len(cast_ws)),
        grid=(4, n_m),
        in_specs=[
            pl.BlockSpec((tm, D_MODEL), lambda s, m: (m, 0)),
            pl.BlockSpec((rows_m, D_MODEL), lambda s, m: (0, 0)),
            pl.BlockSpec((pl.Element(TN), pl.Element(D_MODEL)),
                         lambda s, m: (pl.multiple_of(w_off(s), SUBLANES), 0)),
            pl.BlockSpec((1, TN), lambda s, m: (0, s)),
        ] + [cs[0] for cs in casts],
        out_specs=[
            pl.BlockSpec((tm, TN), lambda s, m: (z_row(s, m), z_col(s))),
            pl.BlockSpec((TN, tm), lambda s, m: (0, kt_blk(s, m))),
            pl.BlockSpec((rows_m, TN), lambda s, m: (0, (s >= S_V).astype(jnp.int32))),
            pl.BlockSpec((TN, rows_m), lambda s, m: (0, 0)),
        ] + [cs[1] for cs in casts],
        out_shape=[
            jax.ShapeDtypeStruct((t, 3 * TN), BF16),
            jax.ShapeDtypeStruct((TN, t), BF16),
            jax.ShapeDtypeStruct((rows_m, 2 * TN), BF16),
            jax.ShapeDtypeStruct((TN, rows_m), BF16),
        ] + [cs[2] for cs in casts],
        scratch_shapes=[pltpu.VMEM((TN, D_MODEL), BF16), pltpu.VMEM((TN, LANES), F32)],
        compiler_params=_params(("arbitrary", "arbitrary")),
        name="inproj_qkvo",
    )(xn, xnm, wt, b1, *cast_ws)


def _conv_prologue(m, is_prompt, tiles_per_seq, n_meta, xnm_ref, w_refs, b_refs, s0_ref, s1_ref, scratch,
                   make_xnm=None):
    wu_ref, wc_ref, wb_ref = w_refs
    bu_ref, bc_ref, _ = b_refs
    w3_ref, _, h1_ref, h2_ref, carry_ref, mtail_ref = scratch
    n_slab, tm, _ = h1_ref.shape
    n_seq = tm // SUBLANES
    slabs = [(k, slice(k * LANES, (k + 1) * LANES)) for k in range(n_slab)]
    seq_row = lambda r: pl.ds(r, n_seq, stride=SUBLANES)

    @pl.when(m == 0)
    def _():
        if make_xnm is not None:
            make_xnm()
        w3_ref[0] = wu_ref[...].astype(BF16)
        w3_ref[1] = wc_ref[...].astype(BF16)
        w3_ref[2] = wb_ref[...].astype(BF16)
        h1_ref[...] = jnp.zeros_like(h1_ref)
        h2_ref[...] = jnp.zeros_like(h2_ref)
        xnm = xnm_ref[...]
        cu_m = (_dot_nt(xnm, w3_ref[1]) + bc_ref[...]) * (_dot_nt(xnm, w3_ref[0]) + bu_ref[...])
        mtail_ref[...] = cu_m[n_meta - SUBLANES:n_meta, :]

    @pl.when(is_prompt)
    def _():
        first = (m % tiles_per_seq) == 0
        p6 = jnp.where(first, mtail_ref[6:7, :], carry_ref[6:7, :])
        p7 = jnp.where(first, mtail_ref[7:8, :], carry_ref[7:8, :])
        for k, ks in slabs:
            h2_ref[k, 0:1, :] = p6[:, ks]
            h2_ref[k, 1:2, :] = p7[:, ks]
            h1_ref[k, 0:1, :] = p7[:, ks]

    @pl.when(jnp.logical_not(is_prompt))
    def _():
        for k, ks in slabs:
            h2_ref[k, seq_row(0), :] = s0_ref[:, ks]
            h2_ref[k, seq_row(1), :] = s1_ref[:, ks]
            h1_ref[k, seq_row(0), :] = s1_ref[:, ks]


def _conv_main(xn, is_prompt, b_refs, wconv_ref, yb_ref, cu6_ref, cu7_ref, scratch):
    bu_ref, bc_ref, bb_ref = b_refs
    w3_ref, cu_ref, h1_ref, h2_ref, carry_ref, _ = scratch
    tm, cq = yb_ref.shape
    n_seq = tm // SUBLANES
    slabs = [(k, slice(k * LANES, (k + 1) * LANES)) for k in range(cq // LANES)]
    seq_row = lambda r: pl.ds(r, n_seq, stride=SUBLANES)
    proj = lambda k, b_ref: jnp.concatenate([_dot_nt(x, w3_ref[k]) for x in xn], axis=0) + b_ref[...]
    zu, zc, zb = proj(0, bu_ref), proj(1, bc_ref), proj(2, bb_ref)
    cu = zc * zu
    pos = lax.broadcasted_iota(jnp.int32, (tm, cq), 0) & jnp.where(is_prompt, tm - 1, SUBLANES - 1)
    h1 = jnp.concatenate([h1_ref[k] for k, _ in slabs], axis=1)
    h2 = jnp.concatenate([h2_ref[k] for k, _ in slabs], axis=1)
    x1 = jnp.where(pos >= 1, pltpu.roll(cu, 1, 0), h1)
    x2 = jnp.where(pos >= 2, pltpu.roll(cu, 2, 0), h2)
    w0, w1, w2 = wconv_ref[0:1, :], wconv_ref[1:2, :], wconv_ref[2:3, :]
    yb_ref[...] = (zb * ((w0 * x2 + w1 * x1) + w2 * cu)).astype(BF16)

    carry_ref[...] = cu[tm - SUBLANES:tm, :]
    for k, ks in slabs:
        cu_ref[k] = cu[:, ks]
        cu6_ref[:, ks] = cu_ref[k, seq_row(6), :]
        cu7_ref[:, ks] = cu_ref[k, seq_row(7), :]


def _conv_kernel(n_prompt_tiles, tiles_per_seq, n_meta, n_cast, xn_ref, xnm_ref, wu_ref, wc_ref, wb_ref,
                 bu_ref, bc_ref, bb_ref, wconv_ref, s0_ref, s1_ref, *refs):
    cast_in, (yb_ref, cu6_ref, cu7_ref) = refs[:n_cast], refs[n_cast:n_cast + 3]
    cast_out, scratch = refs[n_cast + 3:2 * n_cast + 3], refs[2 * n_cast + 3:]
    m = pl.program_id(1)
    is_prompt = m < n_prompt_tiles
    b_refs = (bu_ref, bc_ref, bb_ref)
    _conv_prologue(m, is_prompt, tiles_per_seq, n_meta, xnm_ref, (wu_ref, wc_ref, wb_ref), b_refs,
                   s0_ref, s1_ref, scratch)
    _cast_blocks(tuple(zip(cast_in, cast_out)))
    _conv_main([xn_ref[...]], is_prompt, b_refs, wconv_ref, yb_ref, cu6_ref, cu7_ref, scratch)


def _rms_conv_kernel(n_prompt_tiles, tiles_per_seq, dec_seq, xp_ref, xs_ref, xm_ref, g_ref, wg_ref, bg_ref,
                     wu_ref, wc_ref, wb_ref, bu_ref, bc_ref, bb_ref, wconv_ref, s0_ref, s1_ref,
                     xn_ref, gch_ref, grh_ref, xnm_ref, gchm_ref, grhm_ref, yb_ref, cu6_ref, cu7_ref, *scratch):
    m = pl.program_id(0)
    is_prompt = m < n_prompt_tiles
    n_meta = xm_ref.shape[0]
    b_refs = (bu_ref, bc_ref, bb_ref)

    def make_xnm():
        xm = jnp.concatenate([xm_ref[...], jnp.zeros((CHUNK - n_meta, D_MODEL), F32)], axis=0)
        xnm = _rmsnorm(xm, g_ref[...]).astype(BF16)
        xnm_ref[...] = xnm
        _gate_prep(xnm, wg_ref, bg_ref, CHUNK, n_meta, gchm_ref, grhm_ref)

    _conv_prologue(m, is_prompt, tiles_per_seq, n_meta, xnm_ref, (wu_ref, wc_ref, wb_ref), b_refs,
                   s0_ref, s1_ref, scratch, make_xnm)
    tm = xn_ref.shape[0]
    half = tm // 2
    xn_blocks = []
    for r in (slice(0, half), slice(half, tm)):
        x = jnp.where(is_prompt, xp_ref[r, :], xs_ref[r, :])
        xn_blocks.append(_rmsnorm(x, g_ref[...]).astype(BF16))
        xn_ref[r, :] = xn_blocks[-1]
    _gate_prep(jnp.concatenate(xn_blocks, axis=0), wg_ref, bg_ref, jnp.where(is_prompt, CHUNK, dec_seq), tm,
               gch_ref, grh_ref)
    _conv_main(xn_blocks, is_prompt, b_refs, wconv_ref, yb_ref, cu6_ref, cu7_ref, scratch)


def _conv_specs(c0, npt, tm, cm):
    n_seq = tm // SUBLANES
    n_slab = CQ // LANES
    chan = lambda *g: c0 + cm(*g)[0]
    tile = lambda *g: cm(*g)[1]
    w_spec = lambda off: pl.BlockSpec((pl.Element(CQ), pl.Element(D_MODEL)),
                                      lambda *g: (pl.multiple_of(off + chan(*g) * CQ, SUBLANES), 0))
    b_spec = lambda off: pl.BlockSpec((1, CQ), lambda *g: (0, (off - N_GATES) // CQ + chan(*g)))
    s_spec = lambda tok: pl.BlockSpec(
        (n_seq, CQ), lambda *g: (jnp.maximum(tile(*g) - npt, 0), tok * (C_WIDTH // CQ) + chan(*g)))
    in_specs = [w_spec(OFF_U), w_spec(OFF_C), w_spec(OFF_B), b_spec(OFF_U), b_spec(OFF_C), b_spec(OFF_B),
                pl.BlockSpec((CONV_W, CQ), lambda *g: (0, chan(*g))), s_spec(0), s_spec(1)]
    out_specs = [pl.BlockSpec((tm, CQ), lambda *g: (tile(*g), cm(*g)[0])),
                 pl.BlockSpec((n_seq, CQ), lambda *g: (tile(*g), cm(*g)[0])),
                 pl.BlockSpec((n_seq, CQ), lambda *g: (tile(*g), cm(*g)[0]))]
    scratch = [pltpu.VMEM((3, CQ, D_MODEL), BF16),
               pltpu.VMEM((n_slab, tm, LANES), F32),
               pltpu.VMEM((n_slab, tm, LANES), F32),
               pltpu.VMEM((n_slab, tm, LANES), F32),
               pltpu.VMEM((SUBLANES, CQ), F32),
               pltpu.VMEM((SUBLANES, CQ), F32)]
    return in_specs, out_specs, scratch


def _conv_out_shapes(t, n_blocks):
    return [jax.ShapeDtypeStruct((t, n_blocks * CQ), BF16),
            jax.ShapeDtypeStruct((t // SUBLANES, n_blocks * CQ), F32),
            jax.ShapeDtypeStruct((t // SUBLANES, n_blocks * CQ), F32)]


def _rms_conv_call(xp, xs, x_meta, g, wt, bg, b1, wconv, sconv, dec_seq, seq_len, tm):
    tp, ts = xp.shape[0], xs.shape[0]
    t = tp + ts
    assert tm & (tm - 1) == 0 and seq_len % tm == 0 and x_meta.shape[0] >= SUBLANES
    npt, nst = tp // tm, ts // tm
    conv_in, conv_out, scratch = _conv_specs(0, npt, tm, lambda m: (0, m))
    return pl.pallas_call(
        functools.partial(_rms_conv_kernel, npt, seq_len // tm, dec_seq),
        grid=(npt + nst,),
        in_specs=[
            pl.BlockSpec((tm, D_MODEL), lambda m: (jnp.minimum(m, npt - 1), 0)),
            pl.BlockSpec((tm, D_MODEL), lambda m: (jnp.maximum(m - npt, 0), 0)),
            pl.BlockSpec(x_meta.shape, lambda m: (0, 0)),
            pl.BlockSpec((1, D_MODEL), lambda m: (0, 0)),
            pl.BlockSpec((N_GATES, D_MODEL), lambda m: (OFF_GATES // N_GATES, 0)),
            pl.BlockSpec((N_GATES, 1), lambda m: (0, 0)),
        ] + conv_in,
        out_specs=[
            pl.BlockSpec((tm, D_MODEL), lambda m: (m, 0)),
            pl.BlockSpec((M_HEADS, 2, tm, LANES), lambda m: (0, 0, m, 0)),
            pl.BlockSpec((M_HEADS, SUBLANES, tm), lambda m: (0, 0, m)),
            pl.BlockSpec((CHUNK, D_MODEL), lambda m: (0, 0)),
            pl.BlockSpec((M_HEADS, 2, CHUNK, LANES), lambda m: (0, 0, 0, 0)),
            pl.BlockSpec((M_HEADS, SUBLANES, CHUNK), lambda m: (0, 0, 0)),
        ] + conv_out,
        out_shape=[
            jax.ShapeDtypeStruct((t, D_MODEL), BF16),
            jax.ShapeDtypeStruct((M_HEADS, 2, t, LANES), F32),
            jax.ShapeDtypeStruct((M_HEADS, SUBLANES, t), F32),
            jax.ShapeDtypeStruct((CHUNK, D_MODEL), BF16),
            jax.ShapeDtypeStruct((M_HEADS, 2, CHUNK, LANES), F32),
            jax.ShapeDtypeStruct((M_HEADS, SUBLANES, CHUNK), F32),
        ] + _conv_out_shapes(t, 1),
        scratch_shapes=scratch,
        compiler_params=_params(("arbitrary",)),
        name="rms_conv",
    )(xp, xs, x_meta, g, wt, bg, wt, wt, wt, b1, b1, b1, wconv, sconv, sconv)


def _conv_call(xn, xnm, n_meta, wt, b1, wconv, sconv, cast_ws, n_prompt_tokens, seq_len, tm):
    t = xn.shape[0]
    assert tm & (tm - 1) == 0 and seq_len % tm == 0 and n_meta >= SUBLANES
    npt = n_prompt_tokens // tm
    n_m = t // tm
    n_blocks = C_WIDTH // CQ - 1
    n_cast = 16
    assert n_cast <= n_blocks * n_m
    casts = [_cast_specs(w, n_cast, lambda c, m: c * n_m + m) for w in cast_ws]
    conv_in, conv_out, scratch = _conv_specs(1, npt, tm, lambda c, m: (c, m))
    return pl.pallas_call(
        functools.partial(_conv_kernel, npt, seq_len // tm, n_meta, len(cast_ws)),
        grid=(n_blocks, n_m),
        in_specs=[
            pl.BlockSpec((tm, D_MODEL), lambda c, m: (m, 0)),
            pl.BlockSpec(xnm.shape, lambda c, m: (0, 0)),
        ] + conv_in + [cs[0] for cs in casts],
        out_specs=conv_out + [cs[1] for cs in casts],
        out_shape=_conv_out_shapes(t, n_blocks) + [cs[2] for cs in casts],
        scratch_shapes=scratch,
        compiler_params=_params(("arbitrary", "arbitrary")),
        name="inproj_conv",
    )(xn, xnm, wt, wt, wt, b1, b1, b1, wconv, sconv, sconv, *cast_ws)


def _prompt_head(q, kt, v, gr, gc, ca, m_st):
    l = q.shape[0]
    assert l == LANES
    ig_row, b_row, r_row = gr[G_IG:G_IG + 1, :], gr[G_B:G_B + 1, :], gr[G_R:G_R + 1, :]
    b_col = gc[0]
    v_aug = jnp.concatenate([v, jnp.ones((l, LANES), BF16)], axis=1)
    t_idx = lax.broadcasted_iota(jnp.int32, (l, l), 0)
    s_idx = lax.broadcasted_iota(jnp.int32, (l, l), 1)
    logw = jnp.where(s_idx <= t_idx, b_col - b_row + ig_row, NEG)
    inter = b_col + m_st
    m_t = jnp.maximum(inter, gc[1])
    w_inter = jnp.exp(inter - m_t)
    s = _dot(q, kt) * jnp.exp(logw - m_t)
    nd = (jnp.concatenate([w_inter] * (AUG // LANES), axis=1) * _dot(q, ca.astype(BF16))
          + _dot(s.astype(BF16), v_aug))
    rcp = 1.0 / jnp.maximum(jnp.abs(nd[:, M_DV:AUG]), jnp.exp(-m_t))
    h = nd[:, 0:M_DV] * jnp.concatenate([rcp] * (M_DV // LANES), axis=1)
    m_new = m_t[l - 1:l, 0:1]
    decay = jnp.exp(b_col[l - 1:l, 0:1] + m_st - m_new)
    kwt = (kt.astype(F32) * jnp.exp(r_row + ig_row - m_new)).astype(BF16)
    return h, decay * ca + _dot(kwt, v_aug), m_new


def _mlstm_prompt_kernel(batch, qm_ref, ktm_ref, vm_ref, gcm_ref, grm_ref, *refs):
    ins, (h_ref, c_ref, n_ref, m_ref, ca_ref) = refs[:5 * batch], refs[5 * batch:]
    heads = [(hd, slice(hd * M_DK, (hd + 1) * M_DK)) for hd in range(M_HEADS)]

    @pl.when(pl.program_id(0) == 0)
    def _():
        for hd, sl in heads:
            _, ca_new, m_new = _prompt_head(qm_ref[:, sl], ktm_ref[sl, :], vm_ref[:, sl], grm_ref[hd],
                                            gcm_ref[hd], jnp.zeros((M_DK, AUG), F32), jnp.zeros((1, 1), F32))
            for b in range(batch):
                ca_ref[b, hd] = ca_new
                m_ref[b, hd:hd + 1, :] = jnp.broadcast_to(m_new, (1, LANES))

    for k in range(h_ref.shape[1] // CHUNK):
        tok = slice(k * CHUNK, (k + 1) * CHUNK)
        for b in range(batch):
            q_ref, kt_ref, v_ref, gc_ref, gr_ref = ins[5 * b:5 * b + 5]
            for hd, sl in heads:
                h, ca_new, m_new = _prompt_head(q_ref[tok, sl], kt_ref[sl, tok], v_ref[tok, sl],
                                                gr_ref[hd, :, tok], gc_ref[hd, :, tok, :],
                                                ca_ref[b, hd], m_ref[b, hd:hd + 1, 0:1])
                ca_ref[b, hd] = ca_new
                m_ref[b, hd:hd + 1, :] = jnp.broadcast_to(m_new, (1, LANES))
                h_ref[b, tok, sl] = h.astype(h_ref.dtype)

    @pl.when(pl.program_id(0) == pl.num_programs(0) - 1)
    def _():
        for b in range(batch):
            for hd, _ in heads:
                ca = ca_ref[b, hd]
                c_ref[b, hd] = ca[:, 0:M_DV]
                n_t = ca[:, M_DV:AUG]
                n_ref[b, hd:hd + 1, :] = jnp.concatenate(
                    [n_t[k * LANES:(k + 1) * LANES, :].T[0:1, :] for k in range(M_DK // LANES)], axis=1)


def _mlstm_prompt_call(zm, ktm, gcm, grm, z, kt, gch, grh, batch, seq, tok):
    assert tok % CHUNK == 0 and seq % tok == 0
    nc = seq // tok
    per_prompt_specs, per_prompt_args = [], []
    for b in range(batch):
        row = functools.partial(lambda b, c: b * nc + c, b)
        per_prompt_specs += [
            pl.BlockSpec((tok, M_WIDTH), lambda c, row=row: (row(c), Z_Q)),
            pl.BlockSpec((M_WIDTH, tok), lambda c, row=row: (0, row(c))),
            pl.BlockSpec((tok, M_WIDTH), lambda c, row=row: (row(c), Z_V)),
            pl.BlockSpec((M_HEADS, 2, tok, LANES), lambda c, row=row: (0, 0, row(c), 0)),
            pl.BlockSpec((M_HEADS, SUBLANES, tok), lambda c, row=row: (0, 0, row(c))),
        ]
        per_prompt_args += [z, kt, z, gch, grh]
    return pl.pallas_call(
        functools.partial(_mlstm_prompt_kernel, batch),
        grid=(nc,),
        in_specs=[
            pl.BlockSpec((CHUNK, M_WIDTH), lambda c: (0, 0)),
            pl.BlockSpec((M_WIDTH, CHUNK), lambda c: (0, 0)),
            pl.BlockSpec((CHUNK, M_WIDTH), lambda c: (0, 1)),
            pl.BlockSpec((M_HEADS, 2, CHUNK, LANES), lambda c: (0, 0, 0, 0)),
            pl.BlockSpec((M_HEADS, SUBLANES, CHUNK), lambda c: (0, 0, 0)),
        ] + per_prompt_specs,
        out_specs=[
            pl.BlockSpec((batch, tok, M_WIDTH), lambda c: (0, c, 0)),
            pl.BlockSpec((batch, M_HEADS, M_DK, M_DV), lambda c: (0, 0, 0, 0)),
            pl.BlockSpec((batch, M_HEADS, M_DK), lambda c: (0, 0, 0)),
            pl.BlockSpec((batch, M_HEADS, LANES), lambda c: (0, 0, 0)),
        ],
        out_shape=[
            jax.ShapeDtypeStruct((batch, seq, M_WIDTH), BF16),
            jax.ShapeDtypeStruct((batch, M_HEADS, M_DK, M_DV), F32),
            jax.ShapeDtypeStruct((batch, M_HEADS, M_DK), F32),
            jax.ShapeDtypeStruct((batch, M_HEADS, LANES), F32),
        ],
        scratch_shapes=[pltpu.VMEM((batch, M_HEADS, M_DK, AUG), F32)],
        compiler_params=_params(("arbitrary",)),
        name="mlstm_prompt",
    )(zm, ktm, zm, gcm, grm, *per_prompt_args)


def _group_max(x, size):
    n = x.shape[-1]
    lane = lax.broadcasted_iota(jnp.int32, x.shape, x.ndim - 1)
    k = 1
    while k < size:
        partner = jnp.where((lane & k) == 0, pltpu.roll(x, n - k, x.ndim - 1), pltpu.roll(x, k, x.ndim - 1))
        x = jnp.maximum(x, partner)
        k *= 2
    return x


def _mlstm_sample_kernel(dec_seq, q_ref, kt_ref, v_ref, gc_ref, gr_ref, mcol_ref, mrow_ref, cin_ref, nin_ref,
                         h_ref, c_ref, n_ref, m_ref):
    l = q_ref.shape[0]
    n_seq = l // dec_seq
    shift = dec_seq.bit_length() - 1
    q, kt, v = q_ref[...], kt_ref[...], v_ref[...]
    assert l == LANES
    gr, gc = gr_ref[0], gc_ref[0]
    ig_row, b_row, r_row = gr[G_IG:G_IG + 1, :], gr[G_B:G_B + 1, :], gr[G_R:G_R + 1, :]
    b_col = gc[0]
    m_col, m_row = mcol_ref[0], mrow_ref[0]

    t_idx = lax.broadcasted_iota(jnp.int32, (l, l), 0)
    s_idx = lax.broadcasted_iota(jnp.int32, (l, l), 1)
    t_seq = t_idx >> shift
    logw = jnp.where((t_seq == (s_idx >> shift)) & (s_idx <= t_idx), b_col - b_row + ig_row, NEG)
    inter = b_col + m_col
    m_t = jnp.maximum(inter, gc[1])
    w_inter = jnp.exp(inter - m_t)
    s = _dot(q, kt) * jnp.exp(logw - m_t)

    e_row = r_row + ig_row
    b_last = b_row + r_row
    m_new = jnp.maximum(b_last + m_row, _group_max(e_row, dec_seq))
    decay = jnp.exp(b_last + m_row - m_new)
    kwt = (kt.astype(F32) * jnp.exp(e_row - m_new)).astype(BF16)

    seq_lane = s_idx == t_seq
    seq_onehot = jnp.where(seq_lane, 1.0, 0.0).astype(BF16)
    pa = _dot(s.astype(BF16), jnp.concatenate([v, seq_onehot], axis=1))

    n_rows = jnp.concatenate([nin_ref[...], jnp.zeros((LANES - n_seq, M_DK), F32)], axis=0)
    n_t = jnp.concatenate([n_rows[:, 0:LANES].T, n_rows[:, LANES:2 * LANES].T], axis=0)
    lane_k = lax.broadcasted_iota(jnp.int32, (M_DK, LANES), 1)
    lane_1 = lax.broadcasted_iota(jnp.int32, (1, LANES), 1)
    qf = q.astype(F32)
    qca_rows = []
    decay_seq = jnp.zeros((1, LANES), F32)
    for i in range(n_seq):
        decay_i = decay[:, i * dec_seq:i * dec_seq + 1]
        decay_seq = jnp.where(lane_1 == i, decay_i, decay_seq)
        c_i = cin_ref[i, 0]
        ca = jnp.concatenate([c_i, jnp.where(lane_k == i, n_t, 0.0)], axis=1)
        rows = slice(i * dec_seq, (i + 1) * dec_seq)
        qca_rows.append(_dot(qf[rows, :].astype(BF16), ca.astype(BF16)))
        kw_i = jnp.where((lane_k >> shift) == i, kwt, jnp.zeros_like(kwt))
        c_ref[i, 0] = decay_i * c_i + _dot(kw_i, v)
    n_t_new = decay_seq * n_t + _dot(kwt, seq_onehot)
    nd = jnp.concatenate([w_inter] * (AUG // LANES), axis=1) * jnp.concatenate(qca_rows, axis=0) + pa
    den = jnp.sum(jnp.where(seq_lane, nd[:, M_DV:AUG], 0.0), axis=-1, keepdims=True)
    rcp = 1.0 / jnp.maximum(jnp.abs(den), jnp.exp(-m_t[:, 0:1]))
    h_ref[...] = (nd[:, 0:M_DV] * rcp).astype(h_ref.dtype)
    n_ref[...] = jnp.concatenate([n_t_new[0:LANES, :].T, n_t_new[LANES:2 * LANES, :].T], axis=1)[0:n_seq, :]
    m_ref[0] = jnp.broadcast_to(m_new, (SUBLANES, l))


def _gates_sample_kernel(dec_seq, xn_ref, wt_ref, b_ref, *refs):
    sample_in, z_ref, sample_out, w_ref = refs[:9], refs[9], refs[10:14], refs[14]

    @pl.when(pl.program_id(1) == 0)
    def _():
        w_ref[...] = wt_ref[...].astype(BF16)

    _mlstm_sample_kernel(dec_seq, *sample_in, *sample_out)
    z_ref[...] = _sigmoid(_dot_nt(xn_ref[...], w_ref[...]) + b_ref[...]).astype(BF16)


def _gates_sample_call(xn, wt, b1, zqv, kt, gch, grh, m_col, m_row, c_in, n_in, row0, dec_seq, tm):
    t = xn.shape[0]
    assert t % tm == 0
    n_m = t // tm
    nb = c_in.shape[0]
    l = SEQ_PER_STEP * dec_seq
    blk0 = row0 // l
    n_steps = 2 * D_MODEL // TN
    n_guest = (nb // SEQ_PER_STEP) * M_HEADS
    assert n_guest <= n_steps * n_m
    w_off = lambda s: OFF_GA + s * TN
    b_blk = lambda s: (OFF_GA - N_GATES) // TN + s
    z_col = lambda s: s
    guest = lambda s, m: jnp.minimum(s * n_m + m, n_guest - 1)
    gi = lambda s, m: guest(s, m) // M_HEADS
    gh = lambda s, m: guest(s, m) % M_HEADS
    return pl.pallas_call(
        functools.partial(_gates_sample_kernel, dec_seq),
        grid=(n_steps, n_m),
        in_specs=[
            pl.BlockSpec((tm, D_MODEL), lambda s, m: (m, 0)),
            pl.BlockSpec((pl.Element(TN), pl.Element(D_MODEL)),
                         lambda s, m: (pl.multiple_of(w_off(s), SUBLANES), 0)),
            pl.BlockSpec((1, TN), lambda s, m: (0, b_blk(s))),
            pl.BlockSpec((l, M_DK), lambda s, m: (blk0 + gi(s, m), Z_Q * M_HEADS + gh(s, m))),
            pl.BlockSpec((M_DK, l), lambda s, m: (gh(s, m), blk0 + gi(s, m))),
            pl.BlockSpec((l, M_DV), lambda s, m: (blk0 + gi(s, m), Z_V * M_HEADS + gh(s, m))),
            pl.BlockSpec((1, 2, l, LANES), lambda s, m: (gh(s, m), 0, blk0 + gi(s, m), 0)),
            pl.BlockSpec((1, SUBLANES, l), lambda s, m: (gh(s, m), 0, blk0 + gi(s, m))),
            pl.BlockSpec((1, l, LANES), lambda s, m: (gh(s, m), gi(s, m), 0)),
            pl.BlockSpec((1, 1, l), lambda s, m: (gh(s, m), 0, gi(s, m))),
            pl.BlockSpec((SEQ_PER_STEP, 1, M_DK, M_DV), lambda s, m: (gi(s, m), gh(s, m), 0, 0)),
            pl.BlockSpec((SEQ_PER_STEP, M_DK), lambda s, m: (gi(s, m), gh(s, m))),
        ],
        out_specs=[
            pl.BlockSpec((tm, TN), lambda s, m: (m, z_col(s))),
            pl.BlockSpec((l, M_DV), lambda s, m: (gi(s, m), gh(s, m))),
            pl.BlockSpec((SEQ_PER_STEP, 1, M_DK, M_DV), lambda s, m: (gi(s, m), gh(s, m), 0, 0)),
            pl.BlockSpec((SEQ_PER_STEP, M_DK), lambda s, m: (gi(s, m), gh(s, m))),
            pl.BlockSpec((1, SUBLANES, l), lambda s, m: (gh(s, m), 0, gi(s, m))),
        ],
        out_shape=[
            jax.ShapeDtypeStruct((t, ZG_COLS), BF16),
            jax.ShapeDtypeStruct((nb * dec_seq, M_WIDTH), BF16),
            jax.ShapeDtypeStruct(c_in.shape, F32),
            jax.ShapeDtypeStruct(n_in.shape, F32),
            jax.ShapeDtypeStruct((M_HEADS, SUBLANES, nb * dec_seq), F32),
        ],
        scratch_shapes=[pltpu.VMEM((TN, D_MODEL), BF16)],
        compiler_params=pltpu.CompilerParams(
            dimension_semantics=("arbitrary", "arbitrary"), vmem_limit_bytes=VMEM_LIMIT_BYTES,
            allow_input_fusion=[i in (8, 9) for i in range(12)]),
        name="inproj_gates_mlstm_sample",
    )(xn, wt, b1, zqv, kt, zqv, gch, grh, m_col, m_row, c_in, n_in)


def _mix_kernel(h_ref, so_ref, yb0_ref, yb1_ref, ga_ref, gb_ref, x_ref, pa_ref, pb_ref, wo_ref, gffn_ref,
                o_ref, on_ref):
    h_a = (so_ref[...].astype(F32) * h_ref[...].astype(F32)).astype(BF16)
    a = _dot(h_a, pa_ref[...])
    b = _dot(jnp.concatenate([yb0_ref[...], yb1_ref[...]], axis=1), pb_ref[...])
    merged = ga_ref[...].astype(F32) * a + gb_ref[...].astype(F32) * b
    x1 = x_ref[...] + _dot(merged.astype(BF16), wo_ref[...])
    o_ref[...] = x1
    on_ref[...] = _rmsnorm(x1, gffn_ref[...]).astype(BF16)


def _mix_call(h, x, zqvo, zg, yb0, yb1, pa, pb, wo, gffn, row0, tm):
    rows = x.shape[0]
    assert row0 % tm == 0 and rows % tm == 0
    blk0 = row0 // tm
    const = lambda i: (0, 0)
    return pl.pallas_call(
        _mix_kernel,
        grid=(rows // tm,),
        in_specs=[
            pl.BlockSpec((tm, M_WIDTH), lambda i: (i, 0)),
            pl.BlockSpec((tm, M_WIDTH), lambda i: (blk0 + i, Z_SO)),
            pl.BlockSpec((tm, yb0.shape[1]), lambda i: (blk0 + i, 0)),
            pl.BlockSpec((tm, yb1.shape[1]), lambda i: (blk0 + i, 0)),
            pl.BlockSpec((tm, D_MODEL), lambda i: (blk0 + i, Z_GA // 2)),
            pl.BlockSpec((tm, D_MODEL), lambda i: (blk0 + i, Z_GB // 2)),
            pl.BlockSpec((tm, D_MODEL), lambda i: (i, 0)),
            pl.BlockSpec((M_WIDTH, D_MODEL), const, pipeline_mode=pl.Buffered(1)),
            pl.BlockSpec((C_WIDTH, D_MODEL), const, pipeline_mode=pl.Buffered(1)),
            pl.BlockSpec((D_MODEL, D_MODEL), const, pipeline_mode=pl.Buffered(1)),
            pl.BlockSpec((1, D_MODEL), const),
        ],
        out_specs=[pl.BlockSpec((tm, D_MODEL), lambda i: (i, 0)), pl.BlockSpec((tm, D_MODEL), lambda i: (i, 0))],
        out_shape=[jax.ShapeDtypeStruct((rows, D_MODEL), F32), jax.ShapeDtypeStruct((rows, D_MODEL), BF16)],
        compiler_params=_params(("arbitrary",)),
        name="mix",
    )(h, zqvo, yb0, yb1, zg, zg, x, pa, pb, wo, gffn)


def _ffn_kernel(x1_ref, xn_ref, wup_ref, wdn_ref, gfin_ref, y_ref):
    f = pl.program_id(1)
    last = pl.num_programs(1) - 1

    def mlp_part():
        u = jnp.maximum(_dot(xn_ref[...], wup_ref[...]), 0.0)
        return _dot((u * u).astype(BF16), wdn_ref[...])

    @pl.when(f == 0)
    def _():
        y_ref[...] = x1_ref[...] + mlp_part()

    @pl.when((f > 0) & (f < last))
    def _():
        y_ref[...] += mlp_part()

    @pl.when(f == last)
    def _():
        y_ref[...] = _rmsnorm(y_ref[...] + mlp_part(), gfin_ref[...])


def _ffn_call(x1, xn, wup, wdn, gfin, tm, tf):
    rows = x1.shape[0]
    assert D_FF // tf >= 2 and rows % tm == 0
    return pl.pallas_call(
        _ffn_kernel,
        grid=(rows // tm, D_FF // tf),
        in_specs=[
            pl.BlockSpec((tm, D_MODEL), lambda i, f: (i, 0)),
            pl.BlockSpec((tm, D_MODEL), lambda i, f: (i, 0)),
            pl.BlockSpec((D_MODEL, tf), lambda i, f: (0, f)),
            pl.BlockSpec((tf, D_MODEL), lambda i, f: (f, 0)),
            pl.BlockSpec((1, D_MODEL), lambda i, f: (0, 0)),
        ],
        out_specs=pl.BlockSpec((tm, D_MODEL), lambda i, f: (i, 0)),
        out_shape=jax.ShapeDtypeStruct((rows, D_MODEL), F32),
        compiler_params=_params(("arbitrary", "arbitrary")),
        name="ffn",
    )(x1, xn, wup, wdn, gfin)


def kernel(x_prompt, x_sample, state_mlstm_C, state_mlstm_n, state_mlstm_m, state_conv, meta_tokens,
           g_mix, w_in, b_in, w_conv, p_a, p_b, w_o, g_ffn, w_up, w_down, g_final):
    assert w_in.shape[0] == 1, "single-layer trunk"
    batch, seq, _ = x_prompt.shape
    dec_batch, dec_seq, _ = x_sample.shape
    assert dec_seq == SUBLANES and seq % 1024 == 0 and dec_batch % SEQ_PER_STEP == 0
    n_p, n_s = batch * seq, dec_batch * dec_seq

    wt = w_in[0].T
    bias = b_in[0]
    b1 = jnp.concatenate([bias[:OFF_GATES], bias[OFF_O:]])
    bg = bias[OFF_GATES:OFF_O][:, None]
    gmix = g_mix[0][None, :]

    xp = x_prompt.reshape(n_p, D_MODEL)
    xs = x_sample.reshape(n_s, D_MODEL)
    n_meta = meta_tokens.shape[0]

    sconv = state_conv[0].reshape(dec_batch, (CONV_W - 1) * C_WIDTH)
    xn, gch, grh, xnm, gcm, grm, yb0, cu6a, cu7a = _rms_conv_call(
        xp, xs, meta_tokens.astype(F32), gmix, wt, bg, b1[None, :], w_conv[0], sconv, dec_seq, seq, 512)
    zqv, kt, zm, ktm, w_up16, w_down16 = _qkvo_call(xn, xnm, wt, b1[None, :], (w_up[0], w_down[0]), 1024)
    m_tok = jnp.repeat(state_mlstm_m[0], dec_seq, axis=0).T
    zg, hs, c_s, n_sm, m_s = _gates_sample_call(
        xn, wt, b1[None, :], zqv, kt, gch, grh,
        jnp.broadcast_to(m_tok[:, :, None], m_tok.shape + (LANES,)), m_tok[:, None, :],
        state_mlstm_C[0], state_mlstm_n[0].reshape(dec_batch, M_HEADS * M_DK), n_p, dec_seq, 1152)
    yb1, cu6b, cu7b, p_a16, p_b16, w_o16 = _conv_call(
        xn, xnm, n_meta, wt, b1[None, :], w_conv[0], sconv, (p_a[0], p_b[0], w_o[0]), n_p, seq, 1024)

    hp, c_p, n_pr, m_p = _mlstm_prompt_call(zm, ktm, gcm, grm, zqv, kt, gch, grh, batch, seq, 2 * CHUNK)

    y_p, y_s = [
        _ffn_call(*_mix_call(h, x, zqv, zg, yb0, yb1, p_a16, p_b16, w_o16, g_ffn[0][None, :], row0, 512),
                  w_up16, w_down16, g_final[None, :], 512, 2048)
        for h, x, row0 in ((hp.reshape(n_p, M_WIDTH), xp, 0), (hs, xs, n_p))]

    per_seq = seq // SUBLANES
    first_s = n_p // SUBLANES
    pick = lambda rows: jnp.stack([jnp.concatenate([cu6a[rows], cu6b[rows]], axis=1),
                                   jnp.concatenate([cu7a[rows], cu7b[rows]], axis=1)], axis=1)
    cv_p = pick(slice(per_seq - 1, batch * per_seq, per_seq))
    cv_s = pick(slice(first_s, None))
    m_s = m_s[:, 0, dec_seq - 1::dec_seq].T
    return (y_p.reshape(batch, seq, D_MODEL), y_s.reshape(dec_batch, dec_seq, D_MODEL),
            c_p[None], n_pr[None], m_p[None, :, :, 0], cv_p[None],
            c_s[None], n_sm.reshape(dec_batch, M_HEADS, M_DK)[None], m_s[None], cv_s[None])
```
